```python
import jax, jax.numpy as jnp
from jax import lax
import numpy as np

D_MODEL = 1024
BATCH = 16
SEQ = 4096
DEPTH = 2
DEC_BATCH = 16
DEC_SEQ = 32
PAST_LEN = 1024

CHUNK = 64
N_EVEN = (DEPTH + 1) // 2
N_ODD = DEPTH // 2
ALPHA = (2 * DEPTH) ** 0.25
BETA = (8 * DEPTH) ** -0.25
LN_EPS = 1e-5
D_MIX = D_MODEL

GLA_HEADS = 4
GLA_DV = D_MIX // 2 // GLA_HEADS
GLA_DK = GLA_DV // 2
GLA_GATE_RANK = 16
GLA_GATE_TAU = 16.0
POOL_GROUPS = 4
POOL_WIDTH = D_MIX // 2
POOL_CH = POOL_WIDTH // POOL_GROUPS
POOL_WINDOWS = (2, 4, 8, 16)
POOL_HIST = 15
D_IN_EVEN = 2 * GLA_HEADS * GLA_DK + 2 * GLA_HEADS * GLA_DV + GLA_GATE_RANK + POOL_WIDTH

MLA_HEADS = 8
MLA_NOPE = 64
MLA_ROPE = 32
MLA_V = 64
MLA_Q_RANK = 256
MLA_KV_RANK = 128
ROPE_THETA = 10000.0
ATTN_Q_BLOCK = 128
MLA_SCALE = (MLA_NOPE + MLA_ROPE) ** -0.5
GMLP_GROUPS = 4
GMLP_WIDTH = D_MIX - MLA_HEADS * MLA_V
GMLP_CH = GMLP_WIDTH // GMLP_GROUPS
GMLP_CHUNK = 128
D_IN_ODD = MLA_Q_RANK + MLA_KV_RANK + MLA_ROPE + 2 * GMLP_WIDTH

N_GROUPS = 4
EXPERTS_PER_GROUP = 8
N_EXPERTS = N_GROUPS * EXPERTS_PER_GROUP
TOP_K = 2
D_EXPERT = 256
MOE_BLOCK = 256

kernel_name = 'hybrid_gla_pool_mla_gmlp_hmoe_stream_step'


def layernorm(x, g, b):
    xf = x.astype(jnp.float32)
    mu = jnp.mean(xf, -1, keepdims=True)
    var = jnp.mean(jnp.square(xf - mu), -1, keepdims=True)
    return ((xf - mu) * lax.rsqrt(var + LN_EPS) * g + b).astype(x.dtype)


def rmsnorm(x, g):
    xf = x.astype(jnp.float32)
    return (xf * lax.rsqrt(jnp.mean(xf * xf, -1, keepdims=True) + LN_EPS) * g).astype(x.dtype)


def split_cols(x, widths):
    out, start = [], 0
    for w in widths:
        out.append(x[..., start:start + w])
        start += w
    return out


def rope(x, pos):
    half = MLA_ROPE // 2
    inv = ROPE_THETA ** (-jnp.arange(half, dtype=jnp.float32) * 2.0 / MLA_ROPE)
    ang = pos.astype(jnp.float32)[:, None] * inv[None, :]
    shape = (1, pos.shape[0]) + (1,) * (x.ndim - 3) + (half,)
    cos, sin = jnp.cos(ang).reshape(shape), jnp.sin(ang).reshape(shape)
    xf = x.astype(jnp.float32)
    x1, x2 = xf[..., :half], xf[..., half:]
    return jnp.concatenate([x1 * cos - x2 * sin, x1 * sin + x2 * cos], -1).astype(x.dtype)


def gla_recurrence(q, k, v, log_a, s0):
    B, L, H, _ = q.shape
    DV = v.shape[-1]
    cl = min(L, CHUNK)
    n = L // cl

    def to_blocks(a):
        return a.astype(jnp.float32).reshape(B, n, cl, H, a.shape[-1]).transpose(1, 0, 3, 2, 4)

    incl = jnp.tril(jnp.ones((cl, cl), bool))[None, None, :, :, None]

    def step(S, inp):
        qc, kc, vc, ac = inp
        b = jnp.cumsum(ac, axis=2)
        diff = b[:, :, :, None, :] - b[:, :, None, :, :]
        decay = jnp.exp(jnp.where(incl, diff, -jnp.inf))
        scores = jnp.einsum('bhijd,bhjd->bhij', qc[:, :, :, None, :] * decay, kc)
        o = jnp.einsum('bhij,bhje->bhie', scores, vc) + jnp.einsum('bhid,bhde->bhie', qc * jnp.exp(b), S)
        b_end = b[:, :, -1:, :]
        S = jnp.exp(b_end[:, :, 0, :])[..., None] * S + jnp.einsum('bhjd,bhje->bhde', kc * jnp.exp(b_end - b), vc)
        return S, o

    S, o = lax.scan(step, s0.astype(jnp.float32), (to_blocks(q), to_blocks(k), to_blocks(v), to_blocks(log_a)))
    return o.transpose(1, 0, 3, 2, 4).reshape(B, L, H, DV), S


def pool_mix(xp, hist, pos0, pool_w, pool_scale):
    B, L, C = xp.shape
    ext = jnp.concatenate([hist.astype(xp.dtype), xp], axis=1)
    cs = jnp.concatenate([jnp.zeros((B, 1, C), jnp.float32), jnp.cumsum(ext.astype(jnp.float32), axis=1)], axis=1)
    pos = pos0 + jnp.arange(L)
    means = []
    for g, w in enumerate(POOL_WINDOWS):
        sl = slice(g * POOL_CH, (g + 1) * POOL_CH)
        hi = cs[:, POOL_HIST + 1:POOL_HIST + 1 + L, sl]
        lo = cs[:, POOL_HIST + 1 - w:POOL_HIST + 1 - w + L, sl]
        cnt = jnp.minimum(pos + 1, w).astype(jnp.float32)[None, :, None]
        means.append((hi - lo) / cnt)
    mix = (jnp.concatenate(means, -1) - xp.astype(jnp.float32)).reshape(B, L, POOL_GROUPS, POOL_CH)
    out = jnp.einsum('blgc,gcd->blgd', mix, pool_w.astype(jnp.float32)).reshape(B, L, C) * pool_scale
    return out.astype(xp.dtype), ext[:, -POOL_HIST:]


def even_mixer(h, s_gla, pool_hist, pos0, w_in, w_gate_up, b_gate, gla_norm_g, pool_w, pool_scale, w_out):
    B, L, _ = h.shape
    q, k, v, r, g_lr, xp = split_cols(h @ w_in, (GLA_HEADS * GLA_DK, GLA_HEADS * GLA_DK, GLA_HEADS * GLA_DV,
                                                 GLA_HEADS * GLA_DV, GLA_GATE_RANK, POOL_WIDTH))
    q = q.reshape(B, L, GLA_HEADS, GLA_DK) * GLA_DK ** -0.5
    k = k.reshape(B, L, GLA_HEADS, GLA_DK)
    v = v.reshape(B, L, GLA_HEADS, GLA_DV)
    log_a = jax.nn.log_sigmoid((g_lr @ w_gate_up + b_gate).astype(jnp.float32)) / GLA_GATE_TAU
    log_a = log_a.reshape(B, L, GLA_HEADS, GLA_DK)
    o, s_new = gla_recurrence(q, k, v, log_a, s_gla)
    o = rmsnorm(o, gla_norm_g) * jax.nn.silu(r.reshape(B, L, GLA_HEADS, GLA_DV).astype(jnp.float32))
    o = o.astype(h.dtype).reshape(B, L, GLA_HEADS * GLA_DV)
    pooled, hist_new = pool_mix(xp, pool_hist, pos0, pool_w, pool_scale)
    y = jnp.concatenate([o, pooled], axis=-1) @ w_out
    return y, s_new.astype(h.dtype), hist_new


def mla_attention(q_nope, q_pe, q_pos, k_nope, k_pe, val, k_pos):
    B, L, H, _ = q_nope.shape
    qb = min(L, ATTN_Q_BLOCK)
    n = L // qb
    k_chunk = k_pos // CHUNK

    def block(args):
        qn, qp, qpos = args
        s = (jnp.einsum('bqhd,bkhd->bhqk', qn, k_nope) + jnp.einsum('bqhd,bkd->bhqk', qp, k_pe)).astype(jnp.float32) * MLA_SCALE
        mask = k_chunk[None, :] <= (qpos // CHUNK)[:, None]
        p = jax.nn.softmax(jnp.where(mask[None, None], s, -jnp.inf), axis=-1).astype(val.dtype)
        return jnp.einsum('bhqk,bkhd->bqhd', p, val)

    def to_blocks(a):
        return a.reshape((B, n, qb) + a.shape[2:]).transpose((1, 0, 2) + tuple(range(3, a.ndim + 1)))

    out = lax.map(block, (to_blocks(q_nope), to_blocks(q_pe), q_pos.reshape(n, qb)))
    return out.transpose(1, 0, 2, 3, 4).reshape(B, L, H, val.shape[-1])


def spatial_gate(vn, gmlp_ws, gmlp_bs):
    B, L, C = vn.shape
    cl = min(L, GMLP_CHUNK)
    n = L // cl
    w = jnp.tril(gmlp_ws[:, :cl, :cl])
    vg = vn.reshape(B, n, cl, GMLP_GROUPS, GMLP_CH)
    s = jnp.einsum('gts,bnsgc->bntgc', w, vg) + gmlp_bs[:, :cl].T[None, None, :, :, None]
    return s.reshape(B, L, C)


def odd_mixer(h, ckv_past, kpe_past, w_in, q_norm_g, w_uq, kv_norm_g, w_uk, w_uv,
              gmlp_norm_g, gmlp_norm_b, gmlp_ws, gmlp_bs, w_out):
    B, L, _ = h.shape
    n_past = ckv_past.shape[1]
    cq, ckv, kpe, u, v = split_cols(h @ w_in, (MLA_Q_RANK, MLA_KV_RANK, MLA_ROPE, GMLP_WIDTH, GMLP_WIDTH))
    q_pos = n_past + jnp.arange(L)
    k_pos = jnp.arange(n_past + L)
    q = (rmsnorm(cq, q_norm_g) @ w_uq).reshape(B, L, MLA_HEADS, MLA_NOPE + MLA_ROPE)
    q_nope = q[..., :MLA_NOPE]
    q_pe = rope(q[..., MLA_NOPE:], q_pos)
    ckv = rmsnorm(ckv, kv_norm_g)
    kpe = rope(kpe, q_pos)
    ckv_all = jnp.concatenate([ckv_past.astype(ckv.dtype), ckv], axis=1)
    kpe_all = jnp.concatenate([kpe_past.astype(kpe.dtype), kpe], axis=1)
    k_nope = jnp.einsum('bkr,rhd->bkhd', ckv_all, w_uk.reshape(MLA_KV_RANK, MLA_HEADS, MLA_NOPE))
    val = jnp.einsum('bkr,rhd->bkhd', ckv_all, w_uv.reshape(MLA_KV_RANK, MLA_HEADS, MLA_V))
    attn = mla_attention(q_nope, q_pe, q_pos, k_nope, kpe_all, val, k_pos).reshape(B, L, MLA_HEADS * MLA_V)
    u = jax.nn.gelu(u)
    vn = layernorm(jax.nn.gelu(v), gmlp_norm_g, gmlp_norm_b)
    gated = (u * spatial_gate(vn, gmlp_ws, gmlp_bs)).astype(attn.dtype)
    y = jnp.concatenate([attn, gated], axis=-1) @ w_out
    return y, ckv, kpe, vn


def grouped_experts(x, eid, gate, w1, w3, w2):
    T, D = x.shape
    A = T * TOP_K
    flat_e = eid.reshape(A)
    order = jnp.argsort(flat_e)
    sorted_e = flat_e[order]
    counts = jnp.zeros((N_EXPERTS,), jnp.int32).at[flat_e].add(1)
    padded = (counts + MOE_BLOCK - 1) // MOE_BLOCK * MOE_BLOCK
    pad_end = jnp.cumsum(padded)
    pad_start = pad_end - padded
    grp_start = jnp.cumsum(counts) - counts
    dest = pad_start[sorted_e] + jnp.arange(A) - grp_start[sorted_e]
    n_blocks = A // MOE_BLOCK + N_EXPERTS
    rows = n_blocks * MOE_BLOCK
    row_tok = jnp.full((rows,), T, jnp.int32).at[dest].set((order // TOP_K).astype(jnp.int32))
    row_w = jnp.zeros((rows,), jnp.float32).at[dest].set(gate.reshape(A)[order])
    block_e = jnp.minimum(jnp.searchsorted(pad_end, jnp.arange(n_blocks) * MOE_BLOCK, side='right'), N_EXPERTS - 1)
    xpad = jnp.concatenate([x, jnp.zeros((1, D), x.dtype)], axis=0)

    def step(acc, blk):
        tok, wrow, e = blk
        xb = xpad[tok]
        hid = jax.nn.silu(xb @ w1[e]) * (xb @ w3[e])
        y = (hid @ w2[e]).astype(jnp.float32)
        return acc.at[tok].add(y * wrow[:, None]), None

    acc, _ = lax.scan(step, jnp.zeros((T + 1, D), jnp.float32),
                      (row_tok.reshape(n_blocks, MOE_BLOCK), row_w.reshape(n_blocks, MOE_BLOCK), block_e))
    return acc[:T]


def hier_moe(h, wg, bg, we, be, w1, w3, w2):
    B, L, D = h.shape
    x = h.reshape(B * L, D)
    T = x.shape[0]
    xf = x.astype(jnp.float32)
    tok = jnp.arange(T)
    g_logits = xf @ wg.astype(jnp.float32) + bg.astype(jnp.float32)
    g_sel = jnp.argmax(g_logits, axis=-1).astype(jnp.int32)
    g_prob = jax.nn.softmax(g_logits, axis=-1)[tok, g_sel]
    e_logits = (xf @ we.astype(jnp.float32) + be.astype(jnp.float32)).reshape(T, N_GROUPS, EXPERTS_PER_GROUP)
    e_logits = e_logits[tok, g_sel]
    top_v, top_i = lax.top_k(e_logits, TOP_K)
    gate = jax.nn.softmax(top_v, axis=-1) * g_prob[:, None]
    eid = g_sel[:, None] * EXPERTS_PER_GROUP + top_i.astype(jnp.int32)
    y = grouped_experts(x, eid, gate, w1, w3, w2)
    return y.astype(h.dtype).reshape(B, L, D)


def setup_inputs(seed: int = 0) -> dict:
    key = jax.random.key(seed)
    ks = iter(jax.random.split(key, 40))

    def nrm(shape, scale=1.0):
        return jax.random.normal(next(ks), shape, jnp.float32) * scale

    def gain(shape):
        return 1.0 + nrm(shape, 0.1)

    return {
        'x_prompt': nrm((BATCH, SEQ, D_MODEL)),
        'x_sample': nrm((DEC_BATCH, DEC_SEQ, D_MODEL)),
        'state_gla': nrm((N_EVEN, DEC_BATCH, GLA_HEADS, GLA_DK, GLA_DV)),
        'state_pool': nrm((N_EVEN, DEC_BATCH, POOL_HIST, POOL_WIDTH)),
        'cache_mla_ckv': nrm((N_ODD, DEC_BATCH, PAST_LEN, MLA_KV_RANK)),
        'cache_mla_kpe': nrm((N_ODD, DEC_BATCH, PAST_LEN, MLA_ROPE)),
        'w_in_even': nrm((N_EVEN, D_MODEL, D_IN_EVEN), D_MODEL ** -0.5),
        'w_gate_up': nrm((N_EVEN, GLA_GATE_RANK, GLA_HEADS * GLA_DK), GLA_GATE_RANK ** -0.5),
        'b_gate': nrm((N_EVEN, GLA_HEADS * GLA_DK), 0.1),
        'gla_norm_g': gain((N_EVEN, GLA_DV)),
        'pool_w': nrm((N_EVEN, POOL_GROUPS, POOL_CH, POOL_CH), POOL_CH ** -0.5),
        'pool_scale': gain((N_EVEN, POOL_WIDTH)),
        'w_out_even': nrm((N_EVEN, D_MIX, D_MODEL), BETA * D_MIX ** -0.5),
        'w_in_odd': nrm((N_ODD, D_MODEL, D_IN_ODD), D_MODEL ** -0.5),
        'mla_q_norm_g': gain((N_ODD, MLA_Q_RANK)),
        'mla_w_uq': nrm((N_ODD, MLA_Q_RANK, MLA_HEADS * (MLA_NOPE + MLA_ROPE)), MLA_Q_RANK ** -0.5),
        'mla_kv_norm_g': gain((N_ODD, MLA_KV_RANK)),
        'mla_w_uk': nrm((N_ODD, MLA_KV_RANK, MLA_HEADS * MLA_NOPE), MLA_KV_RANK ** -0.5),
        'mla_w_uv': nrm((N_ODD, MLA_KV_RANK, MLA_HEADS * MLA_V), MLA_KV_RANK ** -0.5),
        'gmlp_norm_g': gain((N_ODD, GMLP_WIDTH)),
        'gmlp_norm_b': nrm((N_ODD, GMLP_WIDTH), 0.02),
        'gmlp_ws': nrm((N_ODD, GMLP_GROUPS, GMLP_CHUNK, GMLP_CHUNK), 0.05),
        'gmlp_bs': gain((N_ODD, GMLP_GROUPS, GMLP_CHUNK)),
        'w_out_odd': nrm((N_ODD, D_MIX, D_MODEL), BETA * D_MIX ** -0.5),
        'ln_mix_g': gain((DEPTH, D_MODEL)),
        'ln_mix_b': nrm((DEPTH, D_MODEL), 0.02),
        'router_group_w': nrm((DEPTH, D_MODEL, N_GROUPS), D_MODEL ** -0.5),
        'router_group_b': nrm((DEPTH, N_GROUPS), 0.01),
        'router_expert_w': nrm((DEPTH, D_MODEL, N_EXPERTS), D_MODEL ** -0.5),
        'router_expert_b': nrm((DEPTH, N_EXPERTS), 0.01),
        'expert_w1': nrm((DEPTH, N_EXPERTS, D_MODEL, D_EXPERT), D_MODEL ** -0.5),
        'expert_w3': nrm((DEPTH, N_EXPERTS, D_MODEL, D_EXPERT), D_MODEL ** -0.5),
        'expert_w2': nrm((DEPTH, N_EXPERTS, D_EXPERT, D_MODEL), BETA * D_EXPERT ** -0.5),
        'ln_ffn_g': gain((DEPTH, D_MODEL)),
        'ln_ffn_b': nrm((DEPTH, D_MODEL), 0.02),
    }


def reference(x_prompt, x_sample, state_gla, state_pool, cache_mla_ckv, cache_mla_kpe,
              w_in_even, w_gate_up, b_gate, gla_norm_g, pool_w, pool_scale, w_out_even,
              w_in_odd, mla_q_norm_g, mla_w_uq, mla_kv_norm_g, mla_w_uk, mla_w_uv,
              gmlp_norm_g, gmlp_norm_b, gmlp_ws, gmlp_bs, w_out_odd,
              ln_mix_g, ln_mix_b, router_group_w, router_group_b, router_expert_w, router_expert_b,
              expert_w1, expert_w3, expert_w2, ln_ffn_g, ln_ffn_b):
    hp, hs = x_prompt, x_sample
    bp = hp.shape[0]
    gla_p, gla_s, pool_p, pool_s = [], [], [], []
    ckv_p, ckv_s, kpe_p, kpe_s, gv_s = [], [], [], [], []
    for layer in range(DEPTH):
        i = layer // 2
        if layer % 2 == 0:
            ew = (w_in_even[i], w_gate_up[i], b_gate[i], gla_norm_g[i], pool_w[i], pool_scale[i], w_out_even[i])
            s0 = jnp.zeros((bp, GLA_HEADS, GLA_DK, GLA_DV), jnp.float32)
            hist0 = jnp.zeros((bp, POOL_HIST, POOL_WIDTH), hp.dtype)
            mp, sp, pp = even_mixer(hp, s0, hist0, 0, *ew)
            ms, ss, ps = even_mixer(hs, state_gla[i], state_pool[i], PAST_LEN, *ew)
            gla_p.append(sp); gla_s.append(ss); pool_p.append(pp); pool_s.append(ps)
        else:
            ow = (w_in_odd[i], mla_q_norm_g[i], mla_w_uq[i], mla_kv_norm_g[i], mla_w_uk[i], mla_w_uv[i],
                  gmlp_norm_g[i], gmlp_norm_b[i], gmlp_ws[i], gmlp_bs[i], w_out_odd[i])
            no_ckv = jnp.zeros((bp, 0, MLA_KV_RANK), hp.dtype)
            no_kpe = jnp.zeros((bp, 0, MLA_ROPE), hp.dtype)
            mp, cp, kp, _ = odd_mixer(hp, no_ckv, no_kpe, *ow)
            ms, cs, ksmp, vs = odd_mixer(hs, cache_mla_ckv[i], cache_mla_kpe[i], *ow)
            ckv_p.append(cp); ckv_s.append(cs); kpe_p.append(kp); kpe_s.append(ksmp); gv_s.append(vs)
        hp = layernorm(ALPHA * hp + mp, ln_mix_g[layer], ln_mix_b[layer])
        hs = layernorm(ALPHA * hs + ms, ln_mix_g[layer], ln_mix_b[layer])
        mw = (router_group_w[layer], router_group_b[layer], router_expert_w[layer], router_expert_b[layer],
              expert_w1[layer], expert_w3[layer], expert_w2[layer])
        hp = layernorm(ALPHA * hp + hier_moe(hp, *mw), ln_ffn_g[layer], ln_ffn_b[layer])
        hs = layernorm(ALPHA * hs + hier_moe(hs, *mw), ln_ffn_g[layer], ln_ffn_b[layer])
    return (hp, hs, jnp.stack(gla_p), jnp.stack(gla_s), jnp.stack(pool_p), jnp.stack(pool_s),
            jnp.stack(ckv_p), jnp.stack(ckv_s), jnp.stack(kpe_p), jnp.stack(kpe_s), jnp.stack(gv_s))
```

```python
import functools

import numpy as np
import jax
import jax.numpy as jnp
from jax import lax
from jax.experimental import pallas as pl
from jax.experimental.pallas import tpu as pltpu

F32 = jnp.float32
BF16 = jnp.bfloat16
I32 = jnp.int32

D_MODEL = 1024
DEPTH = 2
CHUNK = 64
ALPHA = (2 * DEPTH) ** 0.25
LN_EPS = 1e-5

GLA_HEADS = 4
GLA_DV = 128
GLA_DK = 64
GLA_QK = GLA_HEADS * GLA_DK
GLA_V = GLA_HEADS * GLA_DV
GLA_GATE_RANK = 16
GLA_GATE_TAU = 16.0
GLA_SUB = 16

POOL_WIDTH = 512
POOL_CH = 128
POOL_WINDOWS = (2, 4, 8, 16)
POOL_HIST = 15
POOL_HALO = 16

MLA_HEADS = 8
MLA_NOPE = 64
MLA_ROPE = 32
MLA_V = 64
MLA_Q_RANK = 256
MLA_KV_RANK = 128
ROPE_THETA = 10000.0
MLA_SCALE = (MLA_NOPE + MLA_ROPE) ** -0.5
MLA_QW = 256
GMLP_WIDTH = 512
GMLP_CH = 128
GMLP_GROUPS = 4
GMLP_CHUNK = 128

N_GROUPS = 4
EXPERTS_PER_GROUP = 8
N_EXPERTS = 32
TOP_K = 2
D_EXPERT = 256
MOE_ROWS = 512

LANES = 128
VMEM_LIMIT = 48 * 1024 * 1024


def _cp(*sem):
    return pltpu.CompilerParams(dimension_semantics=sem, vmem_limit_bytes=VMEM_LIMIT)


def _dot(a, b):
    return jnp.dot(a, b, preferred_element_type=F32)


def _dot_nt(a, b):
    return lax.dot_general(a, b, (((1,), (1,)), ((), ())), preferred_element_type=F32)


def _dot_tn(a, b):
    return lax.dot_general(a, b, (((0,), (0,)), ((), ())), preferred_element_type=F32)


def _split3(x):
    hi = x.astype(BF16)
    r1 = x - hi.astype(F32)
    mid = r1.astype(BF16)
    lo = (r1 - mid.astype(F32)).astype(BF16)
    return hi, mid, lo


def _layernorm(x, g, b):
    mu = jnp.mean(x, axis=-1, keepdims=True)
    xc = x - mu
    var = jnp.mean(xc * xc, axis=-1, keepdims=True)
    return xc * lax.rsqrt(var + LN_EPS) * g + b


def _gelu(x):
    return 0.5 * x * (1.0 + jnp.tanh(0.7978845608028654 * (x + 0.044715 * (x * x * x))))


def _sigmoid(x):
    return 1.0 / (1.0 + jnp.exp(-x))


def _full_spec(a, nargs):
    nd = a.ndim
    if nargs == 1:
        return pl.BlockSpec(a.shape, lambda i: (0,) * nd)
    return pl.BlockSpec(a.shape, lambda i, j: (0,) * nd)


def _even_in_kernel(x_ref, w_ref, wg_ref, wgu_ref, bg_ref,
                    q_ref, k_ref, la_ref, v_ref, r_ref, xp_ref):
    xb = x_ref[...].astype(BF16)
    z = _dot(xb, w_ref[...])
    q_ref[...] = z[:, 0:GLA_QK] * (GLA_DK ** -0.5)
    k_ref[...] = z[:, GLA_QK:2 * GLA_QK]
    v_ref[...] = z[:, 2 * GLA_QK:2 * GLA_QK + GLA_V]
    r_ref[...] = z[:, 2 * GLA_QK + GLA_V:2 * GLA_QK + 2 * GLA_V]
    xp_ref[...] = z[:, 2 * GLA_QK + 2 * GLA_V:]
    g = _dot(xb, wg_ref[...])
    pre = _dot(g.astype(BF16), wgu_ref[...]) + bg_ref[...]
    logsig = jnp.minimum(pre, 0.0) - jnp.log(1.0 + jnp.exp(-jnp.abs(pre)))
    la_ref[...] = logsig * (1.0 / GLA_GATE_TAU)


def _even_in(x2, w_main, w_g, w_gu, b_g):
    t = x2.shape[0]
    tm = min(512, t)
    row = lambda n: pl.BlockSpec((tm, n), lambda i: (i, 0))
    widths = (GLA_QK, GLA_QK, GLA_QK, GLA_V, GLA_V, POOL_WIDTH)
    return pl.pallas_call(
        _even_in_kernel,
        grid=(t // tm,),
        in_specs=[row(D_MODEL)] + [_full_spec(a, 1) for a in (w_main, w_g, w_gu, b_g)],
        out_specs=[row(n) for n in widths],
        out_shape=[jax.ShapeDtypeStruct((t, n), F32) for n in widths],
        compiler_params=_cp("parallel"),
        name="even_in",
    )(x2, w_main, w_g, w_gu, b_g)


def _gla_chunk(q, k, la, v, st, c):
    lane = lax.broadcasted_iota(I32, (1, GLA_QK), 1)
    head_of_lane = lane // GLA_DK
    row = lax.broadcasted_iota(I32, (c, 1), 0)
    ii = lax.broadcasted_iota(I32, (c, c), 0)
    jj = lax.broadcasted_iota(I32, (c, c), 1)

    tri = (jj <= ii).astype(BF16)
    hi, mid, lo = _split3(la)
    b = _dot(tri, hi) + _dot(tri, mid) + _dot(tri, lo)

    head_masks = [head_of_lane == h for h in range(GLA_HEADS)]
    vb = v.astype(BF16)
    kb16 = None

    a_off = [jnp.zeros((c, c), F32) for _ in range(GLA_HEADS)]
    s = c // 2
    while s >= GLA_SUB:
        nblk = c // (2 * s)
        blk = row // (2 * s)
        right = ((row // s) % 2) == 1
        bref = jnp.zeros((c, GLA_QK), F32)
        for m in range(nblk):
            r0 = m * 2 * s + s - 1
            bref = jnp.where(blk == m, b[r0:r0 + 1, :], bref)
        qe = jnp.where(right, q * jnp.exp(jnp.minimum(b - bref, 0.0)), 0.0)
        ke = jnp.where(right, 0.0, k * jnp.exp(jnp.minimum(bref - b, 0.0))).astype(BF16)
        same = (ii // (2 * s)) == (jj // (2 * s))
        for h in range(GLA_HEADS):
            a = _dot_nt(jnp.where(head_masks[h], qe, 0.0).astype(BF16), ke)
            a_off[h] = a_off[h] + (a if nblk == 1 else jnp.where(same, a, 0.0))
        s //= 2

    nsb = c // GLA_SUB
    parts = []
    for i in range(nsb):
        sl = slice(i * GLA_SUB, (i + 1) * GLA_SUB)
        bi, qi, ki = b[sl], q[sl], k[sl]
        diff = bi[:, None, :] - bi[None, :, :]
        p = qi[:, None, :] * ki[None, :, :] * jnp.exp(jnp.minimum(diff, 0.0))
        parts.append(p.reshape(GLA_SUB * GLA_SUB, GLA_QK))
    pcat = jnp.concatenate(parts, axis=0).astype(BF16)
    n = nsb * GLA_SUB * GLA_SUB
    sel_r = lax.broadcasted_iota(I32, (GLA_QK, GLA_V), 0) // GLA_DK
    sel_c = lax.broadcasted_iota(I32, (GLA_QK, GLA_V), 1) // GLA_DV
    headsum = (sel_r == sel_c).astype(BF16)
    rsum = _dot(pcat, headsum)
    idx = lax.broadcasted_iota(I32, (n, 1), 0)
    causal = (idx % GLA_SUB) <= ((idx // GLA_SUB) % GLA_SUB)
    msel = ((lax.broadcasted_iota(I32, (c, n), 1) // GLA_SUB)
            == lax.broadcasted_iota(I32, (c, n), 0)).astype(BF16)

    qb = q * jnp.exp(b)
    b_end = b[c - 1:c, :]
    kd = (k * jnp.exp(b_end - b)).astype(BF16)
    stb = st.astype(BF16)
    st_new = st * jnp.exp(b_end)

    outs = []
    for h in range(GLA_HEADS):
        vh = vb[:, h * GLA_DV:(h + 1) * GLA_DV]
        vt = jnp.concatenate(
            [jnp.broadcast_to(vh[i * GLA_SUB:(i + 1) * GLA_SUB][None], (GLA_SUB, GLA_SUB, GLA_DV))
             .reshape(GLA_SUB * GLA_SUB, GLA_DV) for i in range(nsb)], axis=0)
        xh = jnp.where(causal, rsum[:, h * GLA_DV:(h + 1) * GLA_DV], 0.0) * vt.astype(F32)
        o = _dot(msel, xh.astype(BF16))
        o = o + _dot(a_off[h].astype(BF16), vh)
        o = o + _dot_nt(jnp.where(head_masks[h], qb, 0.0).astype(BF16), stb)
        outs.append(o)
        st_new = st_new + jnp.where(head_masks[h], _dot_tn(vh, kd), 0.0)
    return outs, st_new


def _gla_kernel(q_ref, k_ref, la_ref, v_ref, r_ref, st0_ref, g_ref, o_ref, st_ref, st_scr,
                *, c, nchunks):
    @pl.when(pl.program_id(1) == 0)
    def _():
        st_scr[...] = st0_ref[0]

    def body(ci, carry):
        r0 = pl.multiple_of(ci * c, c)
        rows = pl.ds(r0, c)
        outs, st_new = _gla_chunk(q_ref[0, rows, :], k_ref[0, rows, :], la_ref[0, rows, :],
                                  v_ref[0, rows, :], st_scr[...], c)
        st_scr[...] = st_new
        r = r_ref[0, rows, :]
        g = g_ref[...]
        for h in range(GLA_HEADS):
            o = outs[h]
            sl = slice(h * GLA_DV, (h + 1) * GLA_DV)
            on = o * lax.rsqrt(jnp.mean(o * o, axis=-1, keepdims=True) + LN_EPS) * g
            rh = r[:, sl]
            o_ref[0, rows, sl] = on * (rh * _sigmoid(rh))
        return carry

    lax.fori_loop(0, nchunks, body, 0)
    st_ref[0] = st_scr[...]


def _gla(q3, k3, la3, v3, r3, st0, gnorm):
    bsz, l, _ = q3.shape
    c = min(l, CHUNK)
    tl = min(l, 512)
    blk = lambda n: pl.BlockSpec((1, tl, n), lambda b, i: (b, i, 0))
    st_spec = pl.BlockSpec((1, GLA_DV, GLA_QK), lambda b, i: (b, 0, 0))
    return pl.pallas_call(
        functools.partial(_gla_kernel, c=c, nchunks=tl // c),
        grid=(bsz, l // tl),
        in_specs=[blk(GLA_QK), blk(GLA_QK), blk(GLA_QK), blk(GLA_V), blk(GLA_V), st_spec,
                  _full_spec(gnorm, 2)],
        out_specs=[blk(GLA_V), st_spec],
        out_shape=[jax.ShapeDtypeStruct((bsz, l, GLA_V), F32),
                   jax.ShapeDtypeStruct((bsz, GLA_DV, GLA_QK), F32)],
        scratch_shapes=[pltpu.VMEM((GLA_DV, GLA_QK), F32)],
        compiler_params=_cp("parallel", "arbitrary"),
        name="gla",
    )(q3, k3, la3, v3, r3, st0, gnorm)


def _pool_kernel(x_ref, halo_ref, hist_ref, w_ref, scale_ref, o_ref, *, tl, pos0):
    i = pl.program_id(1)
    x = x_ref[0]
    prev = jnp.where(i == 0, hist_ref[0], halo_ref[0])
    e = jnp.concatenate([prev, x], axis=0)
    t = i * tl + lax.broadcasted_iota(I32, (tl, 1), 0)
    pos = pos0 + t
    sums = []
    shift = 1
    for g, w in enumerate(POOL_WINDOWS):
        e = e[:, POOL_CH:] if g > 0 else e
        while shift < w:
            e = e[shift:] + e[:-shift]
            shift *= 2
        off = POOL_HALO - (w - 1)
        sums.append(e[off:off + tl, :POOL_CH])
    outs = []
    for g, w in enumerate(POOL_WINDOWS):
        cnt = jnp.minimum(pos + 1, w).astype(F32)
        mix = sums[g] / cnt - x[:, g * POOL_CH:(g + 1) * POOL_CH]
        outs.append(_dot(mix.astype(BF16), w_ref[g]))
    o_ref[0] = jnp.concatenate(outs, axis=1) * scale_ref[...]


def _pool(xp3, hist16, pool_w, pool_scale, pos0):
    bsz, l, _ = xp3.shape
    tl = min(l, 512)
    per = tl // POOL_HALO
    return pl.pallas_call(
        functools.partial(_pool_kernel, tl=tl, pos0=pos0),
        grid=(bsz, l // tl),
        in_specs=[pl.BlockSpec((1, tl, POOL_WIDTH), lambda b, i: (b, i, 0)),
                  pl.BlockSpec((1, POOL_HALO, POOL_WIDTH), lambda b, i: (b, jnp.maximum(i * per - 1, 0), 0)),
                  pl.BlockSpec((1, POOL_HALO, POOL_WIDTH), lambda b, i: (b, 0, 0)),
                  _full_spec(pool_w, 2), _full_spec(pool_scale, 2)],
        out_specs=pl.BlockSpec((1, tl, POOL_WIDTH), lambda b, i: (b, i, 0)),
        out_shape=jax.ShapeDtypeStruct((bsz, l, POOL_WIDTH), F32),
        compiler_params=_cp("parallel", "parallel"),
        name="pool",
    )(xp3, xp3, hist16, pool_w, pool_scale)


def _out_route_kernel(a_ref, b_ref, h_ref, w_ref, g_ref, bt_ref, wr_ref, br_ref,
                      h1_ref, ri_ref, rg_ref, cnt_ref, *, tm):
    @pl.when(pl.program_id(0) == 0)
    def _():
        cnt_ref[...] = jnp.zeros_like(cnt_ref)

    half = w_ref.shape[0] // 2
    y = _dot(a_ref[...].astype(BF16), w_ref[0:half, :]) + _dot(b_ref[...].astype(BF16), w_ref[half:, :])
    x = _layernorm(ALPHA * h_ref[...] + y, g_ref[...], bt_ref[...])
    h1_ref[...] = x

    xh = x.astype(BF16)
    xl = (x - xh.astype(F32)).astype(BF16)
    wr = wr_ref[...]
    wh = wr.astype(BF16)
    wl = (wr - wh.astype(F32)).astype(BF16)
    logits = _dot(xh, wh) + _dot(xl, wh) + _dot(xh, wl) + br_ref[...]

    lane = lax.broadcasted_iota(I32, (tm, LANES), 1)
    lanef = lane.astype(F32)
    neg = -jnp.inf
    big = jnp.float32(1 << 20)

    def first_lane(hit):
        return jnp.min(jnp.where(hit, lanef, big), axis=-1, keepdims=True).astype(I32)

    gl = jnp.where(lane < N_GROUPS, logits, neg)
    gmax = jnp.max(gl, axis=-1, keepdims=True)
    g_sel = first_lane(gl == gmax)
    g_prob = 1.0 / jnp.sum(jnp.exp(gl - gmax), axis=-1, keepdims=True)
    eidx = lane - N_GROUPS
    in_grp = (eidx >= 0) & (eidx < N_EXPERTS) & ((eidx // EXPERTS_PER_GROUP) == g_sel)
    el = jnp.where(in_grp, logits, neg)
    v1 = jnp.max(el, axis=-1, keepdims=True)
    i1 = first_lane(el == v1)
    el2 = jnp.where(lane == i1, neg, el)
    v2 = jnp.max(el2, axis=-1, keepdims=True)
    i2 = first_lane(el2 == v2)
    e21 = jnp.exp(v2 - v1)
    gate1 = g_prob / (1.0 + e21)
    gate2 = g_prob * e21 / (1.0 + e21)
    e1 = i1 - N_GROUPS
    e2 = i2 - N_GROUPS

    oh1 = lane == e1
    oh2 = lane == e2
    oh = oh1.astype(F32) + oh2.astype(F32)
    ti = lax.broadcasted_iota(I32, (tm, tm), 0)
    tj = lax.broadcasted_iota(I32, (tm, tm), 1)
    before = _dot((tj < ti).astype(BF16), oh.astype(BF16)) + cnt_ref[...]
    rank1 = jnp.sum(jnp.where(oh1, before, 0.0), axis=-1, keepdims=True).astype(I32)
    rank2 = jnp.sum(jnp.where(oh2, before, 0.0), axis=-1, keepdims=True).astype(I32)
    cnt_ref[...] = cnt_ref[...] + jnp.sum(oh, axis=0, keepdims=True)

    ri_ref[...] = jnp.where(lane == 0, e1, jnp.where(lane == 1, e2,
                            jnp.where(lane == 2, rank1, jnp.where(lane == 3, rank2, 0))))
    rg_ref[...] = jnp.where(lane == 0, gate1, jnp.where(lane == 1, gate2, 0.0))


def _out_route(a2, b2, h2, w_out, ln_g, ln_b, w_r, b_r):
    t = h2.shape[0]
    tm = min(512, t)
    row = lambda n: pl.BlockSpec((tm, n), lambda i: (i, 0))
    return pl.pallas_call(
        functools.partial(_out_route_kernel, tm=tm),
        grid=(t // tm,),
        in_specs=[row(a2.shape[1]), row(b2.shape[1]), row(D_MODEL)]
                 + [_full_spec(a, 1) for a in (w_out, ln_g, ln_b, w_r, b_r)],
        out_specs=[row(D_MODEL), row(LANES), row(LANES), pl.BlockSpec((1, LANES), lambda i: (0, 0))],
        out_shape=[jax.ShapeDtypeStruct((t, D_MODEL), F32), jax.ShapeDtypeStruct((t, LANES), I32),
                   jax.ShapeDtypeStruct((t, LANES), F32), jax.ShapeDtypeStruct((1, LANES), F32)],
        compiler_params=_cp("arbitrary"),
        name="out_route",
    )(a2, b2, h2, w_out, ln_g, ln_b, w_r, b_r)


def _row_copy(src_ref, src_row, dst_ref, dst_row, sem):
    return pltpu.make_async_copy(src_ref.at[pl.ds(src_row, 1), :], dst_ref.at[pl.ds(dst_row, 1), :], sem)


def _dispatch_kernel(dest_ref, h_ref, zero_ref, xs_ref, sem, *, tm):
    del zero_ref

    def start(r, carry):
        _row_copy(h_ref, r, xs_ref, dest_ref[0, 0, r], sem).start()
        _row_copy(h_ref, r, xs_ref, dest_ref[0, 0, tm + r], sem).start()
        return carry

    def wait(r, carry):
        _row_copy(h_ref, r, xs_ref, dest_ref[0, 0, r], sem).wait()
        _row_copy(h_ref, r, xs_ref, dest_ref[0, 0, tm + r], sem).wait()
        return carry

    lax.fori_loop(0, tm, start, 0)
    lax.fori_loop(0, tm, wait, 0)


def _dispatch(h2, dest3, rows):
    t = h2.shape[0]
    tm = dest3.shape[2] // 2
    zeros = jnp.zeros((rows, D_MODEL), F32)
    return pl.pallas_call(
        functools.partial(_dispatch_kernel, tm=tm),
        grid=(t // tm,),
        in_specs=[pl.BlockSpec((1, 1, 2 * tm), lambda i: (i, 0, 0), memory_space=pltpu.SMEM),
                  pl.BlockSpec((tm, D_MODEL), lambda i: (i, 0)),
                  pl.BlockSpec(memory_space=pl.ANY)],
        out_specs=pl.BlockSpec(memory_space=pl.ANY),
        out_shape=jax.ShapeDtypeStruct((rows, D_MODEL), F32),
        scratch_shapes=[pltpu.SemaphoreType.DMA],
        input_output_aliases={2: 0},
        compiler_params=_cp("arbitrary"),
        name="moe_dispatch",
    )(dest3, h2, zeros)


def _expert_kernel(be_ref, nu_ref, x_ref, w1_ref, w3_ref, w2_ref, y_ref):
    del be_ref
    used = pl.program_id(0) < nu_ref[0]

    @pl.when(used)
    def _():
        xb = x_ref[...].astype(BF16)
        a = _dot(xb, w1_ref[0])
        hid = a * _sigmoid(a) * _dot(xb, w3_ref[0])
        y_ref[...] = _dot(hid.astype(BF16), w2_ref[0])

    @pl.when(jnp.logical_not(used))
    def _():
        y_ref[...] = jnp.zeros_like(y_ref)


def _experts(xs, block_e, n_used, w1, w3, w2):
    rows = xs.shape[0]
    nblk = rows // MOE_ROWS
    blk_idx = lambda i, be, nu: (jnp.minimum(i, nu[0] - 1), 0)
    return pl.pallas_call(
        _expert_kernel,
        grid_spec=pltpu.PrefetchScalarGridSpec(
            num_scalar_prefetch=2,
            grid=(nblk,),
            in_specs=[pl.BlockSpec((MOE_ROWS, D_MODEL), blk_idx),
                      pl.BlockSpec((1, D_MODEL, D_EXPERT), lambda i, be, nu: (be[i], 0, 0)),
                      pl.BlockSpec((1, D_MODEL, D_EXPERT), lambda i, be, nu: (be[i], 0, 0)),
                      pl.BlockSpec((1, D_EXPERT, D_MODEL), lambda i, be, nu: (be[i], 0, 0))],
            out_specs=pl.BlockSpec((MOE_ROWS, D_MODEL), lambda i, be, nu: (i, 0))),
        out_shape=jax.ShapeDtypeStruct((rows, D_MODEL), F32),
        compiler_params=_cp("arbitrary"),
        name="moe_experts",
    )(block_e, n_used, xs, w1, w3, w2)


def _combine_kernel(dest_ref, h_ref, rg_ref, g_ref, b_ref, ys_ref, o_ref, buf, sem, *, tm):
    def start(r, carry):
        _row_copy(ys_ref, dest_ref[0, 0, r], buf.at[0], r, sem).start()
        _row_copy(ys_ref, dest_ref[0, 0, tm + r], buf.at[1], r, sem).start()
        return carry

    def wait(r, carry):
        _row_copy(ys_ref, dest_ref[0, 0, r], buf.at[0], r, sem).wait()
        _row_copy(ys_ref, dest_ref[0, 0, tm + r], buf.at[1], r, sem).wait()
        return carry

    lax.fori_loop(0, tm, start, 0)
    lax.fori_loop(0, tm, wait, 0)
    rg = rg_ref[...]
    y = rg[:, 0:1] * buf[0] + rg[:, 1:2] * buf[1]
    o_ref[...] = _layernorm(ALPHA * h_ref[...] + y, g_ref[...], b_ref[...])


def _combine(h2, rg, dest3, ys, ln_g, ln_b):
    t = h2.shape[0]
    tm = dest3.shape[2] // 2
    return pl.pallas_call(
        functools.partial(_combine_kernel, tm=tm),
        grid=(t // tm,),
        in_specs=[pl.BlockSpec((1, 1, 2 * tm), lambda i: (i, 0, 0), memory_space=pltpu.SMEM),
                  pl.BlockSpec((tm, D_MODEL), lambda i: (i, 0)),
                  pl.BlockSpec((tm, LANES), lambda i: (i, 0)),
                  _full_spec(ln_g, 1), _full_spec(ln_b, 1),
                  pl.BlockSpec(memory_space=pl.ANY)],
        out_specs=pl.BlockSpec((tm, D_MODEL), lambda i: (i, 0)),
        out_shape=jax.ShapeDtypeStruct((t, D_MODEL), F32),
        scratch_shapes=[pltpu.VMEM((2, tm, D_MODEL), F32), pltpu.SemaphoreType.DMA],
        compiler_params=_cp("arbitrary"),
        name="moe_combine",
    )(dest3, h2, rg, ln_g, ln_b, ys)


def _moe(h1, ri, rg, cnt, w1, w3, w2, ln_g, ln_b):
    t = h1.shape[0]
    tm = min(512, t)
    counts = cnt[0, :N_EXPERTS].astype(I32)
    padded = (counts + MOE_ROWS - 1) // MOE_ROWS * MOE_ROWS
    pad_end = jnp.cumsum(padded)
    seg_start = pad_end - padded
    nblk = (t * TOP_K) // MOE_ROWS + N_EXPERTS
    rows = nblk * MOE_ROWS
    dest1 = jnp.take(seg_start, ri[:, 0]) + ri[:, 2]
    dest2 = jnp.take(seg_start, ri[:, 1]) + ri[:, 3]
    dest3 = jnp.concatenate([dest1.reshape(t // tm, 1, tm), dest2.reshape(t // tm, 1, tm)], axis=2)
    block_e = jnp.minimum(jnp.searchsorted(pad_end, jnp.arange(nblk, dtype=I32) * MOE_ROWS, side='right'),
                          N_EXPERTS - 1).astype(I32)
    n_used = (pad_end[-1:] // MOE_ROWS).astype(I32)
    xs = _dispatch(h1, dest3, rows)
    ys = _experts(xs, block_e, n_used, w1, w3, w2)
    return _combine(h1, rg, dest3, ys, ln_g, ln_b)


def _swap_halves(x):
    lane = lax.broadcasted_iota(I32, x.shape, 1)
    first = (lane % MLA_ROPE) < (MLA_ROPE // 2)
    return jnp.where(first, pltpu.roll(x, LANES - MLA_ROPE // 2, 1), pltpu.roll(x, MLA_ROPE // 2, 1))


def _rope(x, cos, sin):
    parts = []
    for t in range(x.shape[1] // LANES):
        sl = slice(t * LANES, (t + 1) * LANES)
        parts.append(x[:, sl] * cos[:, sl] + _swap_halves(x[:, sl]) * sin[:, sl])
    return parts[0] if len(parts) == 1 else jnp.concatenate(parts, axis=1)


def _odd_in_kernel(h_ref, w_ref, qg_ref, wqn_ref, wqp_ref, wuk_ref, perm_ref, kvg_ref,
                   cq_ref, sq_ref, ck_ref, sk_ref, gg_ref, gb_ref, ws_ref, bs_ref,
                   q_ref, kc_ref, ckv_ref, kpe_ref, gated_ref, vn_ref, *, tl, cl):
    hb = h_ref[0].astype(BF16)
    z = _dot(hb, w_ref[...])
    o_ckv = MLA_Q_RANK
    o_u = o_ckv + MLA_KV_RANK
    o_v = o_u + GMLP_WIDTH
    o_k = o_v + GMLP_WIDTH
    cq = z[:, :MLA_Q_RANK]
    cqn = cq * lax.rsqrt(jnp.mean(cq * cq, axis=-1, keepdims=True) + LN_EPS) * qg_ref[...]
    cqb = cqn.astype(BF16)
    qn = _dot(cqb, wqn_ref[...])
    qp = _dot(cqb, wqp_ref[...])
    qp = _rope(qp, cq_ref[...], sq_ref[...])
    qpe = _dot((qp * MLA_SCALE).astype(BF16), perm_ref[...])
    for h in range(MLA_HEADS):
        sl = slice(h * LANES, (h + 1) * LANES)
        qa = _dot((qn[:, sl] * MLA_SCALE).astype(BF16), wuk_ref[h])
        q_ref[0, h, :, 0:LANES] = qa.astype(BF16)
        q_ref[0, h, :, LANES:] = qpe[:, sl].astype(BF16)

    ckv = z[:, o_ckv:o_u]
    ckvn = ckv * lax.rsqrt(jnp.mean(ckv * ckv, axis=-1, keepdims=True) + LN_EPS) * kvg_ref[...]
    kp = z[:, o_k:]
    kp = _rope(kp, ck_ref[...], sk_ref[...])
    ckv_ref[0] = ckvn
    kpe_ref[0] = kp[:, :MLA_ROPE]
    kc_ref[0, :, 0:LANES] = ckvn.astype(BF16)
    kc_ref[0, :, LANES:] = kp.astype(BF16)

    gu = _gelu(z[:, o_u:o_v])
    vn = _layernorm(_gelu(z[:, o_v:o_k]), gg_ref[...], gb_ref[...])
    vn_ref[0] = vn
    vnb = vn.astype(BF16)
    for n in range(tl // cl):
        rs = slice(n * cl, (n + 1) * cl)
        for g in range(GMLP_GROUPS):
            ls = slice(g * GMLP_CH, (g + 1) * GMLP_CH)
            sg = _dot(ws_ref[g], vnb[rs, ls]) + bs_ref[:, ls]
            gated_ref[0, rs, ls] = gu[rs, ls] * sg


def _odd_in(h3, w_in, q_g, w_qn, w_qp, w_uk, perm, kv_g, cos_q, sin_q, cos_k, sin_k,
            gm_g, gm_b, ws, bs):
    bsz, l, _ = h3.shape
    tl = min(l, 512)
    cl = min(l, GMLP_CHUNK)
    rowb = lambda n: pl.BlockSpec((1, tl, n), lambda b, i: (b, i, 0))
    tab = lambda n: pl.BlockSpec((tl, n), lambda b, i: (i, 0))
    consts = (w_in, q_g, w_qn, w_qp, w_uk, perm, kv_g)
    consts2 = (gm_g, gm_b, ws, bs)
    return pl.pallas_call(
        functools.partial(_odd_in_kernel, tl=tl, cl=cl),
        grid=(bsz, l // tl),
        in_specs=[rowb(D_MODEL)] + [_full_spec(a, 2) for a in consts]
                 + [tab(MLA_HEADS * MLA_ROPE), tab(MLA_HEADS * MLA_ROPE), tab(LANES), tab(LANES)]
                 + [_full_spec(a, 2) for a in consts2],
        out_specs=[pl.BlockSpec((1, MLA_HEADS, tl, MLA_QW), lambda b, i: (b, 0, i, 0)),
                   rowb(MLA_QW), rowb(MLA_KV_RANK), rowb(MLA_ROPE), rowb(GMLP_WIDTH), rowb(GMLP_WIDTH)],
        out_shape=[jax.ShapeDtypeStruct((bsz, MLA_HEADS, l, MLA_QW), BF16),
                   jax.ShapeDtypeStruct((bsz, l, MLA_QW), BF16),
                   jax.ShapeDtypeStruct((bsz, l, MLA_KV_RANK), F32),
                   jax.ShapeDtypeStruct((bsz, l, MLA_ROPE), F32),
                   jax.ShapeDtypeStruct((bsz, l, GMLP_WIDTH), F32),
                   jax.ShapeDtypeStruct((bsz, l, GMLP_WIDTH), F32)],
        compiler_params=_cp("parallel", "parallel"),
        name="odd_in",
    )(h3, *consts, cos_q, sin_q, cos_k, sin_k, *consts2)


def _attn_kernel(qi_ref, kj_ref, first_ref, last_ref, q_ref, k_ref, wuv_ref, o_ref,
                 m_scr, l_scr, acc_scr, *, tq, tk, pos0, n_keys):
    p = pl.program_id(1)
    rows = MLA_HEADS * tq

    @pl.when(first_ref[p] == 1)
    def _():
        m_scr[...] = jnp.full_like(m_scr, -jnp.inf)
        l_scr[...] = jnp.zeros_like(l_scr)
        acc_scr[...] = jnp.zeros_like(acc_scr)

    q = q_ref[0].reshape(rows, MLA_QW)
    kk = k_ref[0]
    s = _dot_nt(q, kk)
    qpos = pos0 + qi_ref[p] * tq + lax.broadcasted_iota(I32, (rows, 1), 0) % tq
    kpos = kj_ref[p] * tk + lax.broadcasted_iota(I32, (1, tk), 1)
    visible = ((kpos // CHUNK) <= (qpos // CHUNK)) & (kpos < n_keys)
    s = jnp.where(visible, s, -jnp.inf)
    m_old = m_scr[...]
    m_new = jnp.maximum(m_old, jnp.max(s, axis=-1, keepdims=True))
    alpha = jnp.exp(m_old - m_new)
    pr = jnp.exp(s - m_new)
    l_scr[...] = alpha * l_scr[...] + jnp.sum(pr, axis=-1, keepdims=True)
    acc_scr[...] = alpha * acc_scr[...] + _dot(pr.astype(BF16), kk[:, :MLA_KV_RANK])
    m_scr[...] = m_new

    @pl.when(last_ref[p] == 1)
    def _():
        lat = (acc_scr[...] / l_scr[...]).astype(BF16)
        head_of_lane = lax.broadcasted_iota(I32, (1, MLA_HEADS * MLA_V), 1) // MLA_V
        out = jnp.zeros((tq, MLA_HEADS * MLA_V), F32)
        for h in range(MLA_HEADS):
            oh = _dot(lat[h * tq:(h + 1) * tq], wuv_ref[...])
            out = out + jnp.where(head_of_lane == h, oh, 0.0)
        o_ref[0] = out


def _attn_pairs(l, tq, tk, pos0, n_keys):
    qi, kj, first, last = [], [], [], []
    for i in range(l // tq):
        q_last = pos0 + (i + 1) * tq - 1
        vis = min(CHUNK * (q_last // CHUNK) + CHUNK - 1, n_keys - 1)
        nj = vis // tk + 1
        for j in range(nj):
            qi.append(i); kj.append(j); first.append(int(j == 0)); last.append(int(j == nj - 1))
    return [jnp.asarray(np.array(a, np.int32)) for a in (qi, kj, first, last)]


def _attention(q4, kc3, w_uv, pos0, n_keys, tq, tk):
    bsz, _, l, _ = q4.shape
    pairs = _attn_pairs(l, tq, tk, pos0, n_keys)
    npairs = int(pairs[0].shape[0])
    return pl.pallas_call(
        functools.partial(_attn_kernel, tq=tq, tk=tk, pos0=pos0, n_keys=n_keys),
        grid_spec=pltpu.PrefetchScalarGridSpec(
            num_scalar_prefetch=4,
            grid=(bsz, npairs),
            in_specs=[pl.BlockSpec((1, MLA_HEADS, tq, MLA_QW), lambda b, p, qi, kj, f, la: (b, 0, qi[p], 0)),
                      pl.BlockSpec((1, tk, MLA_QW), lambda b, p, qi, kj, f, la: (b, kj[p], 0)),
                      pl.BlockSpec(w_uv.shape, lambda b, p, qi, kj, f, la: (0, 0))],
            out_specs=pl.BlockSpec((1, tq, MLA_HEADS * MLA_V), lambda b, p, qi, kj, f, la: (b, qi[p], 0)),
            scratch_shapes=[pltpu.VMEM((MLA_HEADS * tq, 1), F32), pltpu.VMEM((MLA_HEADS * tq, 1), F32),
                            pltpu.VMEM((MLA_HEADS * tq, MLA_KV_RANK), F32)]),
        out_shape=jax.ShapeDtypeStruct((bsz, l, MLA_HEADS * MLA_V), F32),
        compiler_params=_cp("parallel", "arbitrary"),
        name="mla_attention",
    )(*pairs, q4, kc3, w_uv)


def _rope_tables(pos0, l, width):
    half = MLA_ROPE // 2
    inv = ROPE_THETA ** (-jnp.arange(half, dtype=F32) * 2.0 / MLA_ROPE)
    ang = (pos0 + jnp.arange(l)).astype(F32)[:, None] * inv[None, :]
    cos, sin = jnp.cos(ang), jnp.sin(ang)
    cos32 = jnp.concatenate([cos, cos], axis=1)
    sin32 = jnp.concatenate([-sin, sin], axis=1)
    reps = width // MLA_ROPE
    return jnp.tile(cos32, (1, reps)), jnp.tile(sin32, (1, reps))


def _prep_even(w_in, w_gate_up, b_gate, gla_norm_g, pool_w, pool_scale, w_out):
    o_r = 2 * GLA_QK + GLA_V
    o_g = o_r + GLA_V
    o_p = o_g + GLA_GATE_RANK
    w_main = jnp.concatenate([w_in[:, :o_g], w_in[:, o_p:]], axis=1).astype(BF16)
    w_g = jnp.pad(w_in[:, o_g:o_p], ((0, 0), (0, LANES - GLA_GATE_RANK))).astype(BF16)
    w_gu = jnp.pad(w_gate_up, ((0, LANES - GLA_GATE_RANK), (0, 0))).astype(BF16)
    return dict(w_main=w_main, w_g=w_g, w_gu=w_gu, b_g=b_gate.reshape(1, -1),
                gnorm=gla_norm_g.reshape(1, -1), pool_w=pool_w.astype(BF16),
                pool_scale=pool_scale.reshape(1, -1), w_out=w_out.astype(BF16))


def _prep_odd(w_in, q_norm_g, w_uq, kv_norm_g, w_uk, w_uv, gm_g, gm_b, gm_ws, gm_bs, w_out):
    o_ckv = MLA_Q_RANK
    o_kpe = o_ckv + MLA_KV_RANK
    o_u = o_kpe + MLA_ROPE
    w_in2 = jnp.concatenate([w_in[:, :o_kpe], w_in[:, o_u:], w_in[:, o_kpe:o_u],
                             jnp.zeros((D_MODEL, LANES - MLA_ROPE), F32)], axis=1).astype(BF16)
    uq = w_uq.reshape(MLA_Q_RANK, MLA_HEADS, MLA_NOPE + MLA_ROPE)
    w_qn = jnp.pad(uq[:, :, :MLA_NOPE], ((0, 0), (0, 0), (0, LANES - MLA_NOPE)))
    w_qn = w_qn.reshape(MLA_Q_RANK, MLA_HEADS * LANES).astype(BF16)
    w_qp = uq[:, :, MLA_NOPE:].reshape(MLA_Q_RANK, MLA_HEADS * MLA_ROPE).astype(BF16)
    uk = w_uk.reshape(MLA_KV_RANK, MLA_HEADS, MLA_NOPE).transpose(1, 2, 0)
    w_ukp = jnp.pad(uk, ((0, 0), (0, LANES - MLA_NOPE), (0, 0))).astype(BF16)
    src = np.arange(MLA_HEADS * MLA_ROPE)
    perm = np.zeros((MLA_HEADS * MLA_ROPE, MLA_HEADS * LANES), np.float32)
    perm[src, (src // MLA_ROPE) * LANES + src % MLA_ROPE] = 1.0
    return dict(w_in=w_in2, q_g=q_norm_g.reshape(1, -1), w_qn=w_qn, w_qp=w_qp, w_uk=w_ukp,
                perm=jnp.asarray(perm, BF16), kv_g=kv_norm_g.reshape(1, -1), w_uv=w_uv.astype(BF16),
                gm_g=gm_g.reshape(1, -1), gm_b=gm_b.reshape(1, -1), gm_ws=gm_ws, gm_bs=gm_bs,
                w_out=w_out.astype(BF16))


def _prep_route(wg, bg, we, be):
    w_r = jnp.pad(jnp.concatenate([wg, we], axis=1), ((0, 0), (0, LANES - N_GROUPS - N_EXPERTS)))
    b_r = jnp.pad(jnp.concatenate([bg, be]), (0, LANES - N_GROUPS - N_EXPERTS)).reshape(1, LANES)
    return w_r, b_r


def _even_mixer(h3, st0, hist, pos0, pw):
    bsz, l, _ = h3.shape
    q, k, la, v, r, xp = _even_in(h3.reshape(bsz * l, D_MODEL), pw['w_main'], pw['w_g'], pw['w_gu'], pw['b_g'])
    to3 = lambda a: a.reshape(bsz, l, a.shape[-1])
    o, st = _gla(to3(q), to3(k), to3(la), to3(v), to3(r), st0, pw['gnorm'])
    xp3 = to3(xp)
    hist16 = jnp.pad(hist, ((0, 0), (POOL_HALO - POOL_HIST, 0), (0, 0)))
    pooled = _pool(xp3, hist16, pw['pool_w'], pw['pool_scale'], pos0)
    hist_new = jnp.concatenate([hist, xp3], axis=1)[:, -POOL_HIST:]
    return o, pooled, st, hist_new


def _odd_mixer(h3, ckv_past, kpe_past, pw):
    bsz, l, _ = h3.shape
    n_past = ckv_past.shape[1]
    cos_q, sin_q = _rope_tables(n_past, l, MLA_HEADS * MLA_ROPE)
    cos_k, sin_k = _rope_tables(n_past, l, MLA_ROPE)
    padk = ((0, 0), (0, LANES - MLA_ROPE))
    cos_k, sin_k = jnp.pad(cos_k, padk), jnp.pad(sin_k, padk)
    cl = min(l, GMLP_CHUNK)
    ws = jnp.tril(pw['gm_ws'][:, :cl, :cl]).astype(BF16)
    bs = jnp.repeat(pw['gm_bs'][:, :cl].T, GMLP_CH, axis=1)
    q4, kc, ckv, kpe, gated, vn = _odd_in(h3, pw['w_in'], pw['q_g'], pw['w_qn'], pw['w_qp'], pw['w_uk'],
                                          pw['perm'], pw['kv_g'], cos_q, sin_q, cos_k, sin_k,
                                          pw['gm_g'], pw['gm_b'], ws, bs)
    n_keys = n_past + l
    if n_past:
        tq, tk = l, 256
        past = jnp.concatenate([ckv_past, kpe_past, jnp.zeros((bsz, n_past, MLA_QW - MLA_KV_RANK - MLA_ROPE), F32)],
                               axis=2).astype(BF16)
        kc_all = jnp.concatenate([past, kc], axis=1)
        kc_all = jnp.pad(kc_all, ((0, 0), (0, -n_keys % tk), (0, 0)))
    else:
        tq, tk = min(l, 256), min(l, 256)
        kc_all = kc
    attn = _attention(q4, kc_all, pw['w_uv'], n_past, n_keys, tq, tk)
    return attn, gated, ckv, kpe, vn


def _finish_layer(a3, b3, h3, w_out, lw):
    bsz, l, _ = h3.shape
    t = bsz * l
    h1, ri, rg, cnt = _out_route(a3.reshape(t, -1), b3.reshape(t, -1), h3.reshape(t, D_MODEL), w_out,
                                 lw['ln_mix_g'], lw['ln_mix_b'], lw['w_r'], lw['b_r'])
    h2 = _moe(h1, ri, rg, cnt, lw['w1'], lw['w3'], lw['w2'], lw['ln_ffn_g'], lw['ln_ffn_b'])
    return h2.reshape(bsz, l, D_MODEL)


def kernel(x_prompt, x_sample, state_gla, state_pool, cache_mla_ckv, cache_mla_kpe, w_in_even, w_gate_up, b_gate, gla_norm_g, pool_w, pool_scale, w_out_even, w_in_odd, mla_q_norm_g, mla_w_uq, mla_kv_norm_g, mla_w_uk, mla_w_uv, gmlp_norm_g, gmlp_norm_b, gmlp_ws, gmlp_bs, w_out_odd, ln_mix_g, ln_mix_b, router_group_w, router_group_b, router_expert_w, router_expert_b, expert_w1, expert_w3, expert_w2, ln_ffn_g, ln_ffn_b):
    hp, hs = x_prompt, x_sample
    bp = hp.shape[0]
    past_len = cache_mla_ckv.shape[2]
    gla_p, gla_s, pool_p, pool_s = [], [], [], []
    ckv_p, ckv_s, kpe_p, kpe_s, gv_s = [], [], [], [], []

    def state_to_t(s):
        return s.transpose(0, 3, 1, 2).reshape(s.shape[0], GLA_DV, GLA_QK)

    def state_from_t(st):
        return st.reshape(st.shape[0], GLA_DV, GLA_HEADS, GLA_DK).transpose(0, 2, 3, 1)

    for layer in range(DEPTH):
        i = layer // 2
        w_r, b_r = _prep_route(router_group_w[layer], router_group_b[layer],
                               router_expert_w[layer], router_expert_b[layer])
        lw = dict(ln_mix_g=ln_mix_g[layer].reshape(1, -1), ln_mix_b=ln_mix_b[layer].reshape(1, -1),
                  ln_ffn_g=ln_ffn_g[layer].reshape(1, -1), ln_ffn_b=ln_ffn_b[layer].reshape(1, -1),
                  w_r=w_r, b_r=b_r, w1=expert_w1[layer].astype(BF16), w3=expert_w3[layer].astype(BF16),
                  w2=expert_w2[layer].astype(BF16))
        if layer % 2 == 0:
            pw = _prep_even(w_in_even[i], w_gate_up[i], b_gate[i], gla_norm_g[i], pool_w[i], pool_scale[i],
                            w_out_even[i])
            st0 = jnp.zeros((bp, GLA_DV, GLA_QK), F32)
            hist0 = jnp.zeros((bp, POOL_HIST, POOL_WIDTH), F32)
            op, pp, stp, histp = _even_mixer(hp, st0, hist0, 0, pw)
            os_, ps, sts, hists = _even_mixer(hs, state_to_t(state_gla[i]), state_pool[i], past_len, pw)
            gla_p.append(state_from_t(stp)); gla_s.append(state_from_t(sts))
            pool_p.append(histp); pool_s.append(hists)
            ap, bpj, as_, bsj = op, pp, os_, ps
        else:
            pw = _prep_odd(w_in_odd[i], mla_q_norm_g[i], mla_w_uq[i], mla_kv_norm_g[i], mla_w_uk[i], mla_w_uv[i],
                           gmlp_norm_g[i], gmlp_norm_b[i], gmlp_ws[i], gmlp_bs[i], w_out_odd[i])
            no_ckv = jnp.zeros((bp, 0, MLA_KV_RANK), F32)
            no_kpe = jnp.zeros((bp, 0, MLA_ROPE), F32)
            ap, bpj, cp, kp, _ = _odd_mixer(hp, no_ckv, no_kpe, pw)
            as_, bsj, cs, ks, vs = _odd_mixer(hs, cache_mla_ckv[i], cache_mla_kpe[i], pw)
            ckv_p.append(cp); ckv_s.append(cs); kpe_p.append(kp); kpe_s.append(ks); gv_s.append(vs)
        hp = _finish_layer(ap, bpj, hp, pw['w_out'], lw)
        hs = _finish_layer(as_, bsj, hs, pw['w_out'], lw)
    return (hp, hs, jnp.stack(gla_p), jnp.stack(gla_s), jnp.stack(pool_p), jnp.stack(pool_s),
            jnp.stack(ckv_p), jnp.stack(ckv_s), jnp.stack(kpe_p), jnp.stack(kpe_s), jnp.stack(gv_s))
```

```python
import functools

import numpy as np
import jax
import jax.numpy as jnp
from jax import lax
from jax.experimental import pallas as pl
from jax.experimental.pallas import tpu as pltpu

F32 = jnp.float32
BF16 = jnp.bfloat16
I32 = jnp.int32

D_MODEL = 1024
DEPTH = 2
CHUNK = 64
ALPHA = (2 * DEPTH) ** 0.25
LN_EPS = 1e-5

GLA_HEADS = 4
GLA_DV = 128
GLA_DK = 64
GLA_QK = GLA_HEADS * GLA_DK
GLA_V = GLA_HEADS * GLA_DV
GLA_GATE_RANK = 16
GLA_GATE_TAU = 16.0
GLA_SUB = 16

POOL_WIDTH = 512
POOL_CH = 128
POOL_WINDOWS = (2, 4, 8, 16)
POOL_HIST = 15
POOL_HALO = 16

MLA_HEADS = 8
MLA_NOPE = 64
MLA_ROPE = 32
MLA_V = 64
MLA_Q_RANK = 256
MLA_KV_RANK = 128
ROPE_THETA = 10000.0
MLA_SCALE = (MLA_NOPE + MLA_ROPE) ** -0.5
MLA_QW = 256
MLA_ONE_LANE = MLA_KV_RANK + MLA_ROPE
GMLP_WIDTH = 512
GMLP_CH = 128
GMLP_GROUPS = 4
GMLP_CHUNK = 128

N_GROUPS = 4
EXPERTS_PER_GROUP = 8
N_EXPERTS = 32
TOP_K = 2
D_EXPERT = 256
MOE_ROWS = 512
MOE_CHUNK = 16
MOE_BLOCK_CHUNKS = MOE_ROWS // MOE_CHUNK

LANES = 128
VMEM_LIMIT = 48 * 1024 * 1024


def _cp(*sem):
    return pltpu.CompilerParams(dimension_semantics=sem, vmem_limit_bytes=VMEM_LIMIT)


def _dot(a, b):
    return jnp.dot(a, b, preferred_element_type=F32)


def _dot_nt(a, b):
    return lax.dot_general(a, b, (((1,), (1,)), ((), ())), preferred_element_type=F32)


def _dot_tn(a, b):
    return lax.dot_general(a, b, (((0,), (0,)), ((), ())), preferred_element_type=F32)


def _split3(x):
    hi = x.astype(BF16)
    r1 = x - hi.astype(F32)
    mid = r1.astype(BF16)
    lo = (r1 - mid.astype(F32)).astype(BF16)
    return hi, mid, lo


def _layernorm(x, g, b):
    mu = jnp.mean(x, axis=-1, keepdims=True)
    xc = x - mu
    var = jnp.mean(xc * xc, axis=-1, keepdims=True)
    return xc * lax.rsqrt(var + LN_EPS) * g + b


def _gelu(x):
    return 0.5 * x * (1.0 + jnp.tanh(0.7978845608028654 * (x + 0.044715 * (x * x * x))))


def _sigmoid(x):
    return 1.0 / (1.0 + jnp.exp(-x))


def _full_spec(a, nargs):
    nd = a.ndim
    if nargs == 1:
        return pl.BlockSpec(a.shape, lambda i: (0,) * nd)
    return pl.BlockSpec(a.shape, lambda i, j: (0,) * nd)


def _even_in_kernel(x_ref, w_ref, wg_ref, wgu_ref, bg_ref,
                    q_ref, k_ref, la_ref, v_ref, r_ref, xp_ref):
    xb = x_ref[...].astype(BF16)
    z = _dot(xb, w_ref[...])
    q_ref[...] = z[:, 0:GLA_QK] * (GLA_DK ** -0.5)
    k_ref[...] = z[:, GLA_QK:2 * GLA_QK]
    v_ref[...] = z[:, 2 * GLA_QK:2 * GLA_QK + GLA_V]
    r_ref[...] = z[:, 2 * GLA_QK + GLA_V:2 * GLA_QK + 2 * GLA_V]
    xp_ref[...] = z[:, 2 * GLA_QK + 2 * GLA_V:]
    g = _dot(xb, wg_ref[...])
    pre = _dot(g.astype(BF16), wgu_ref[...]) + bg_ref[...]
    logsig = jnp.minimum(pre, 0.0) - jnp.log(1.0 + jnp.exp(-jnp.abs(pre)))
    la_ref[...] = logsig * (1.0 / GLA_GATE_TAU)


def _even_in(x2, w_main, w_g, w_gu, b_g):
    t = x2.shape[0]
    tm = min(512, t)
    row = lambda n: pl.BlockSpec((tm, n), lambda i: (i, 0))
    widths = (GLA_QK, GLA_QK, GLA_QK, GLA_V, GLA_V, POOL_WIDTH)
    return pl.pallas_call(
        _even_in_kernel,
        grid=(t // tm,),
        in_specs=[row(D_MODEL)] + [_full_spec(a, 1) for a in (w_main, w_g, w_gu, b_g)],
        out_specs=[row(n) for n in widths],
        out_shape=[jax.ShapeDtypeStruct((t, n), F32) for n in widths],
        compiler_params=_cp("parallel"),
        name="even_in",
    )(x2, w_main, w_g, w_gu, b_g)


def _gla_chunk(q, k, la, v, st, c):
    lane = lax.broadcasted_iota(I32, (1, GLA_QK), 1)
    head_of_lane = lane // GLA_DK
    row = lax.broadcasted_iota(I32, (c, 1), 0)
    ii = lax.broadcasted_iota(I32, (c, c), 0)
    jj = lax.broadcasted_iota(I32, (c, c), 1)

    tri = (jj <= ii).astype(BF16)
    hi, mid, lo = _split3(la)
    b = _dot(tri, hi) + _dot(tri, mid) + _dot(tri, lo)

    head_masks = [head_of_lane == h for h in range(GLA_HEADS)]
    vb = v.astype(BF16)
    kb16 = None

    a_off = [jnp.zeros((c, c), F32) for _ in range(GLA_HEADS)]
    s = c // 2
    while s >= GLA_SUB:
        nblk = c // (2 * s)
        blk = row // (2 * s)
        right = ((row // s) % 2) == 1
        bref = jnp.zeros((c, GLA_QK), F32)
        for m in range(nblk):
            r0 = m * 2 * s + s - 1
            bref = jnp.where(blk == m, b[r0:r0 + 1, :], bref)
        qe = jnp.where(right, q * jnp.exp(jnp.minimum(b - bref, 0.0)), 0.0)
        ke = jnp.where(right, 0.0, k * jnp.exp(jnp.minimum(bref - b, 0.0))).astype(BF16)
        same = (ii // (2 * s)) == (jj // (2 * s))
        for h in range(GLA_HEADS):
            a = _dot_nt(jnp.where(head_masks[h], qe, 0.0).astype(BF16), ke)
            a_off[h] = a_off[h] + (a if nblk == 1 else jnp.where(same, a, 0.0))
        s //= 2

    nsb = c // GLA_SUB
    parts = []
    for i in range(nsb):
        sl = slice(i * GLA_SUB, (i + 1) * GLA_SUB)
        bi, qi, ki = b[sl], q[sl], k[sl]
        diff = bi[:, None, :] - bi[None, :, :]
        p = qi[:, None, :] * ki[None, :, :] * jnp.exp(jnp.minimum(diff, 0.0))
        parts.append(p.reshape(GLA_SUB * GLA_SUB, GLA_QK))
    pcat = jnp.concatenate(parts, axis=0).astype(BF16)
    n = nsb * GLA_SUB * GLA_SUB
    sel_r = lax.broadcasted_iota(I32, (GLA_QK, GLA_V), 0) // GLA_DK
    sel_c = lax.broadcasted_iota(I32, (GLA_QK, GLA_V), 1) // GLA_DV
    headsum = (sel_r == sel_c).astype(BF16)
    rsum = _dot(pcat, headsum)
    idx = lax.broadcasted_iota(I32, (n, 1), 0)
    causal = (idx % GLA_SUB) <= ((idx // GLA_SUB) % GLA_SUB)
    msel = ((lax.broadcasted_iota(I32, (c, n), 1) // GLA_SUB)
            == lax.broadcasted_iota(I32, (c, n), 0)).astype(BF16)

    qb = q * jnp.exp(b)
    b_end = b[c - 1:c, :]
    kd = (k * jnp.exp(b_end - b)).astype(BF16)
    stb = st.astype(BF16)
    st_new = st * jnp.exp(b_end)

    outs = []
    for h in range(GLA_HEADS):
        vh = vb[:, h * GLA_DV:(h + 1) * GLA_DV]
        vt = jnp.concatenate(
            [jnp.broadcast_to(vh[i * GLA_SUB:(i + 1) * GLA_SUB][None], (GLA_SUB, GLA_SUB, GLA_DV))
             .reshape(GLA_SUB * GLA_SUB, GLA_DV) for i in range(nsb)], axis=0)
        xh = jnp.where(causal, rsum[:, h * GLA_DV:(h + 1) * GLA_DV], 0.0) * vt.astype(F32)
        o = _dot(msel, xh.astype(BF16))
        o = o + _dot(a_off[h].astype(BF16), vh)
        o = o + _dot_nt(jnp.where(head_masks[h], qb, 0.0).astype(BF16), stb)
        outs.append(o)
        st_new = st_new + jnp.where(head_masks[h], _dot_tn(vh, kd), 0.0)
    return outs, st_new


def _gla_kernel(q_ref, k_ref, la_ref, v_ref, r_ref, st0_ref, g_ref, o_ref, st_ref, st_scr,
                *, c, nchunks):
    @pl.when(pl.program_id(1) == 0)
    def _():
        st_scr[...] = st0_ref[0]

    def body(ci, carry):
        r0 = pl.multiple_of(ci * c, c)
        rows = pl.ds(r0, c)
        outs, st_new = _gla_chunk(q_ref[0, rows, :], k_ref[0, rows, :], la_ref[0, rows, :],
                                  v_ref[0, rows, :], st_scr[...], c)
        st_scr[...] = st_new
        r = r_ref[0, rows, :]
        g = g_ref[...]
        for h in range(GLA_HEADS):
            o = outs[h]
            sl = slice(h * GLA_DV, (h + 1) * GLA_DV)
            on = o * lax.rsqrt(jnp.mean(o * o, axis=-1, keepdims=True) + LN_EPS) * g
            rh = r[:, sl]
            o_ref[0, rows, sl] = on * (rh * _sigmoid(rh))
        return carry

    lax.fori_loop(0, nchunks, body, 0)
    st_ref[0] = st_scr[...]


def _gla(q3, k3, la3, v3, r3, st0, gnorm):
    bsz, l, _ = q3.shape
    c = min(l, CHUNK)
    tl = min(l, 512)
    blk = lambda n: pl.BlockSpec((1, tl, n), lambda b, i: (b, i, 0))
    st_spec = pl.BlockSpec((1, GLA_DV, GLA_QK), lambda b, i: (b, 0, 0))
    return pl.pallas_call(
        functools.partial(_gla_kernel, c=c, nchunks=tl // c),
        grid=(bsz, l // tl),
        in_specs=[blk(GLA_QK), blk(GLA_QK), blk(GLA_QK), blk(GLA_V), blk(GLA_V), st_spec,
                  _full_spec(gnorm, 2)],
        out_specs=[blk(GLA_V), st_spec],
        out_shape=[jax.ShapeDtypeStruct((bsz, l, GLA_V), F32),
                   jax.ShapeDtypeStruct((bsz, GLA_DV, GLA_QK), F32)],
        scratch_shapes=[pltpu.VMEM((GLA_DV, GLA_QK), F32)],
        compiler_params=_cp("parallel", "arbitrary"),
        name="gla",
    )(q3, k3, la3, v3, r3, st0, gnorm)


def _pool_kernel(x_ref, halo_ref, hist_ref, w_ref, scale_ref, o_ref, *, tl, pos0):
    i = pl.program_id(1)
    x = x_ref[0]
    prev = jnp.where(i == 0, hist_ref[0], halo_ref[0])
    e = jnp.concatenate([prev, x], axis=0)
    t = i * tl + lax.broadcasted_iota(I32, (tl, 1), 0)
    pos = pos0 + t
    sums = []
    shift = 1
    for g, w in enumerate(POOL_WINDOWS):
        e = e[:, POOL_CH:] if g > 0 else e
        while shift < w:
            e = e[shift:] + e[:-shift]
            shift *= 2
        off = POOL_HALO - (w - 1)
        sums.append(e[off:off + tl, :POOL_CH])
    outs = []
    for g, w in enumerate(POOL_WINDOWS):
        cnt = jnp.minimum(pos + 1, w).astype(F32)
        mix = sums[g] / cnt - x[:, g * POOL_CH:(g + 1) * POOL_CH]
        outs.append(_dot(mix.astype(BF16), w_ref[g]))
    o_ref[0] = jnp.concatenate(outs, axis=1) * scale_ref[...]


def _pool(xp3, hist16, pool_w, pool_scale, pos0):
    bsz, l, _ = xp3.shape
    tl = min(l, 512)
    per = tl // POOL_HALO
    return pl.pallas_call(
        functools.partial(_pool_kernel, tl=tl, pos0=pos0),
        grid=(bsz, l // tl),
        in_specs=[pl.BlockSpec((1, tl, POOL_WIDTH), lambda b, i: (b, i, 0)),
                  pl.BlockSpec((1, POOL_HALO, POOL_WIDTH), lambda b, i: (b, jnp.maximum(i * per - 1, 0), 0)),
                  pl.BlockSpec((1, POOL_HALO, POOL_WIDTH), lambda b, i: (b, 0, 0)),
                  _full_spec(pool_w, 2), _full_spec(pool_scale, 2)],
        out_specs=pl.BlockSpec((1, tl, POOL_WIDTH), lambda b, i: (b, i, 0)),
        out_shape=jax.ShapeDtypeStruct((bsz, l, POOL_WIDTH), F32),
        compiler_params=_cp("parallel", "parallel"),
        name="pool",
    )(xp3, xp3, hist16, pool_w, pool_scale)


def _out_route_kernel(a_ref, b_ref, h_ref, w_ref, g_ref, bt_ref, wr_ref, br_ref,
                      h1_ref, xs_ref, rs_ref, rg_ref, n16_ref, *, tm, slots):
    half = w_ref.shape[0] // 2
    y = _dot(a_ref[...].astype(BF16), w_ref[0:half, :]) + _dot(b_ref[...].astype(BF16), w_ref[half:, :])
    x = _layernorm(ALPHA * h_ref[...] + y, g_ref[...], bt_ref[...])
    h1_ref[...] = x

    xh = x.astype(BF16)
    xl = (x - xh.astype(F32)).astype(BF16)
    wr = wr_ref[...]
    wh = wr.astype(BF16)
    wl = (wr - wh.astype(F32)).astype(BF16)
    logits = _dot(xh, wh) + _dot(xl, wh) + _dot(xh, wl) + br_ref[...]

    lane = lax.broadcasted_iota(I32, (tm, LANES), 1)
    lanef = lane.astype(F32)
    neg = -jnp.inf
    big = jnp.float32(1 << 20)

    def first_lane(hit):
        return jnp.min(jnp.where(hit, lanef, big), axis=-1, keepdims=True).astype(I32)

    gl = jnp.where(lane < N_GROUPS, logits, neg)
    gmax = jnp.max(gl, axis=-1, keepdims=True)
    g_sel = first_lane(gl == gmax)
    g_prob = 1.0 / jnp.sum(jnp.exp(gl - gmax), axis=-1, keepdims=True)
    eidx = lane - N_GROUPS
    in_grp = (eidx >= 0) & (eidx < N_EXPERTS) & ((eidx // EXPERTS_PER_GROUP) == g_sel)
    el = jnp.where(in_grp, logits, neg)
    v1 = jnp.max(el, axis=-1, keepdims=True)
    i1 = first_lane(el == v1)
    el2 = jnp.where(lane == i1, neg, el)
    v2 = jnp.max(el2, axis=-1, keepdims=True)
    i2 = first_lane(el2 == v2)
    e21 = jnp.exp(v2 - v1)
    gate1 = g_prob / (1.0 + e21)
    gate2 = g_prob * e21 / (1.0 + e21)
    e1 = i1 - N_GROUPS
    e2 = i2 - N_GROUPS

    oh1 = lane == e1
    oh2 = lane == e2
    oh = oh1.astype(F32) + oh2.astype(F32)
    ti = lax.broadcasted_iota(I32, (tm, tm), 0)
    tj = lax.broadcasted_iota(I32, (tm, tm), 1)
    before = _dot((tj < ti).astype(BF16), oh.astype(BF16))
    cnt = jnp.sum(oh, axis=0, keepdims=True)
    n16 = jnp.floor((cnt + (MOE_CHUNK - 1)) * (1.0 / MOE_CHUNK))
    n16_8 = jnp.broadcast_to(n16, (8, LANES))
    ui = lax.broadcasted_iota(I32, (LANES, LANES), 0)
    uj = lax.broadcasted_iota(I32, (LANES, LANES), 1)
    run_start = _dot(n16_8.astype(BF16), (ui < uj).astype(BF16))[0:1]
    slot_of = MOE_CHUNK * run_start + before
    slot1 = jnp.sum(jnp.where(oh1, slot_of, 0.0), axis=-1, keepdims=True).astype(I32)
    slot2 = jnp.sum(jnp.where(oh2, slot_of, 0.0), axis=-1, keepdims=True).astype(I32)
    sl = lax.broadcasted_iota(I32, (tm, slots), 1)
    place = ((sl == slot1) | (sl == slot2)).astype(BF16)
    xs_ref[...] = _dot_tn(place, xh).astype(BF16)

    rs_ref[...] = jnp.where(lane == 0, slot1, jnp.where(lane == 1, slot2, 0))
    rg_ref[...] = jnp.where(lane == 0, gate1, jnp.where(lane == 1, gate2, 0.0))
    n16_ref[...] = n16_8.astype(I32)


def _moe_slots(tm):
    worst = tm * TOP_K + N_EXPERTS * (MOE_CHUNK - 1)
    return -(-worst // MOE_ROWS) * MOE_ROWS


def _out_route(a2, b2, h2, w_out, ln_g, ln_b, w_r, b_r):
    t = h2.shape[0]
    tm = min(512, t)
    nt = t // tm
    slots = _moe_slots(tm)
    row = lambda n: pl.BlockSpec((tm, n), lambda i: (i, 0))
    return pl.pallas_call(
        functools.partial(_out_route_kernel, tm=tm, slots=slots),
        grid=(nt,),
        in_specs=[row(a2.shape[1]), row(b2.shape[1]), row(D_MODEL)]
                 + [_full_spec(a, 1) for a in (w_out, ln_g, ln_b, w_r, b_r)],
        out_specs=[row(D_MODEL), pl.BlockSpec((slots, D_MODEL), lambda i: (i, 0)), row(LANES), row(LANES),
                   pl.BlockSpec((8, LANES), lambda i: (i, 0))],
        out_shape=[jax.ShapeDtypeStruct((t, D_MODEL), F32), jax.ShapeDtypeStruct((nt * slots, D_MODEL), BF16),
                   jax.ShapeDtypeStruct((t, LANES), I32), jax.ShapeDtypeStruct((t, LANES), F32),
                   jax.ShapeDtypeStruct((nt * 8, LANES), I32)],
        compiler_params=_cp("parallel"),
        name="out_route",
    )(a2, b2, h2, w_out, ln_g, ln_b, w_r, b_r)


def _chunk_rows(chunk):
    return pl.ds(pl.multiple_of(chunk * MOE_CHUNK, MOE_CHUNK), MOE_CHUNK)


def _expert_kernel(src_ref, be_ref, nu_ref, xs_ref, w1_ref, w3_ref, w2_ref, ys_ref,
                   xbuf, ybuf, gsem, ssem):
    del be_ref
    b = pl.program_id(0)
    nu = nu_ref[0]

    def gather(blk, slot):
        def copy(j):
            return pltpu.make_async_copy(xs_ref.at[_chunk_rows(src_ref[blk * MOE_BLOCK_CHUNKS + j]), :],
                                         xbuf.at[slot, _chunk_rows(j), :], gsem.at[slot])
        return copy

    def scatter(blk, slot):
        def copy(j):
            return pltpu.make_async_copy(ybuf.at[slot, _chunk_rows(j), :],
                                         ys_ref.at[_chunk_rows(src_ref[blk * MOE_BLOCK_CHUNKS + j]), :],
                                         ssem.at[slot])
        return copy

    def for_chunks(blk, on_chunk, on_pad=None):
        def body(j, carry):
            real = src_ref[blk * MOE_BLOCK_CHUNKS + j] >= 0

            @pl.when(real)
            def _():
                on_chunk(j)

            if on_pad is not None:
                @pl.when(jnp.logical_not(real))
                def _():
                    on_pad(j)
            return carry
        lax.fori_loop(0, MOE_BLOCK_CHUNKS, body, 0)

    def start_gather(blk, slot):
        def zero(j):
            xbuf[slot, _chunk_rows(j), :] = jnp.zeros((MOE_CHUNK, D_MODEL), BF16)
        for_chunks(blk, lambda j: gather(blk, slot)(j).start(), zero)

    @pl.when(b < nu)
    def _():
        slot = b % 2

        @pl.when(b == 0)
        def _():
            start_gather(b, slot)

        @pl.when(b + 1 < nu)
        def _():
            start_gather(b + 1, 1 - slot)

        for_chunks(b, lambda j: gather(b, slot)(j).wait())

        @pl.when(b >= 2)
        def _():
            for_chunks(b - 2, lambda j: scatter(b - 2, slot)(j).wait())

        xb = xbuf[slot]
        a = _dot(xb, w1_ref[0])
        hid = a * _sigmoid(a) * _dot(xb, w3_ref[0])
        ybuf[slot] = _dot(hid.astype(BF16), w2_ref[0]).astype(BF16)
        for_chunks(b, lambda j: scatter(b, slot)(j).start())

        @pl.when(b == nu - 1)
        def _():
            for_chunks(b, lambda j: scatter(b, slot)(j).wait())

            @pl.when(b >= 1)
            def _():
                for_chunks(b - 1, lambda j: scatter(b - 1, 1 - slot)(j).wait())


def _experts(xs, src, block_e, n_used, w1, w3, w2):
    nblk = src.shape[0] // MOE_BLOCK_CHUNKS
    wspec = lambda shape: pl.BlockSpec(shape, lambda i, src, be, nu: (be[i], 0, 0))
    return pl.pallas_call(
        _expert_kernel,
        grid_spec=pltpu.PrefetchScalarGridSpec(
            num_scalar_prefetch=3,
            grid=(nblk,),
            in_specs=[pl.BlockSpec(memory_space=pl.ANY),
                      wspec((1, D_MODEL, D_EXPERT)), wspec((1, D_MODEL, D_EXPERT)), wspec((1, D_EXPERT, D_MODEL))],
            out_specs=pl.BlockSpec(memory_space=pl.ANY),
            scratch_shapes=[pltpu.VMEM((2, MOE_ROWS, D_MODEL), BF16), pltpu.VMEM((2, MOE_ROWS, D_MODEL), BF16),
                            pltpu.SemaphoreType.DMA((2,)), pltpu.SemaphoreType.DMA((2,))]),
        out_shape=jax.ShapeDtypeStruct(xs.shape, xs.dtype),
        input_output_aliases={3: 0},
        compiler_params=_cp("arbitrary"),
        name="moe_experts",
    )(src, block_e, n_used, xs, w1, w3, w2)


def _combine_kernel(h_ref, rs_ref, rg_ref, ys_ref, g_ref, b_ref, o_ref, *, tm, slots):
    rs = rs_ref[...]
    rg = rg_ref[...]
    sl = lax.broadcasted_iota(I32, (tm, slots), 1)
    weight = (jnp.where(sl == rs[:, 0:1], rg[:, 0:1], 0.0)
              + jnp.where(sl == rs[:, 1:2], rg[:, 1:2], 0.0)).astype(BF16)
    y = _dot(weight, ys_ref[...])
    o_ref[...] = _layernorm(ALPHA * h_ref[...] + y, g_ref[...], b_ref[...])


def _combine(h2, rs, rg, ys, ln_g, ln_b):
    t = h2.shape[0]
    tm = min(512, t)
    slots = ys.shape[0] // (t // tm)
    row = lambda n: pl.BlockSpec((tm, n), lambda i: (i, 0))
    return pl.pallas_call(
        functools.partial(_combine_kernel, tm=tm, slots=slots),
        grid=(t // tm,),
        in_specs=[row(D_MODEL), row(LANES), row(LANES), pl.BlockSpec((slots, D_MODEL), lambda i: (i, 0)),
                  _full_spec(ln_g, 1), _full_spec(ln_b, 1)],
        out_specs=row(D_MODEL),
        out_shape=jax.ShapeDtypeStruct((t, D_MODEL), F32),
        compiler_params=_cp("parallel"),
        name="moe_combine",
    )(h2, rs, rg, ys, ln_g, ln_b)


def _moe(h1, xs, rs, rg, n16_rows, w1, w3, w2, ln_g, ln_b):
    t = h1.shape[0]
    tm = min(512, t)
    nt = t // tm
    slots = xs.shape[0] // nt
    n16 = n16_rows[::8, :N_EXPERTS]
    per_e = jnp.sum(n16, axis=0)
    blocks_e = (per_e + MOE_BLOCK_CHUNKS - 1) // MOE_BLOCK_CHUNKS
    blk_end = jnp.cumsum(blocks_e)
    blk_start = blk_end - blocks_e
    n_used = blk_end[-1:].astype(I32)
    max_chunks = (t * TOP_K) // MOE_CHUNK + nt * N_EXPERTS
    nblk = max_chunks // MOE_BLOCK_CHUNKS + N_EXPERTS
    blk = jnp.arange(nblk, dtype=I32)
    block_e = jnp.minimum(jnp.sum(blk_end[None, :] <= blk[:, None], axis=1), N_EXPERTS - 1).astype(I32)
    run_end = jnp.cumsum(n16, axis=0)
    run_start = run_end - n16
    tile_start = jnp.cumsum(n16, axis=1) - n16
    sb = jnp.repeat(blk, MOE_BLOCK_CHUNKS)
    sj = jnp.tile(jnp.arange(MOE_BLOCK_CHUNKS, dtype=I32), nblk)
    se = block_e[sb]
    k = (sb - blk_start[se]) * MOE_BLOCK_CHUNKS + sj
    valid = (k < per_e[se]) & (sb < n_used[0])
    tile = jnp.minimum(jnp.sum(run_end.T[se] <= k[:, None], axis=1), nt - 1)
    src = tile * (slots // MOE_CHUNK) + tile_start[tile, se] + (k - run_start[tile, se])
    src = jnp.where(valid, src, -1).astype(I32)
    ys = _experts(xs, src, block_e, n_used, w1, w3, w2)
    return _combine(h1, rs, rg, ys, ln_g, ln_b)


def _swap_halves(x):
    lane = lax.broadcasted_iota(I32, x.shape, 1)
    first = (lane % MLA_ROPE) < (MLA_ROPE // 2)
    return jnp.where(first, pltpu.roll(x, LANES - MLA_ROPE // 2, 1), pltpu.roll(x, MLA_ROPE // 2, 1))


def _rope(x, cos, sin):
    parts = []
    for t in range(x.shape[1] // LANES):
        sl = slice(t * LANES, (t + 1) * LANES)
        parts.append(x[:, sl] * cos[:, sl] + _swap_halves(x[:, sl]) * sin[:, sl])
    return parts[0] if len(parts) == 1 else jnp.concatenate(parts, axis=1)


def _odd_in_kernel(h_ref, w_ref, qg_ref, wqn_ref, wqp_ref, wuk_ref, perm_ref, kvg_ref,
                   cq_ref, sq_ref, ck_ref, sk_ref, gg_ref, gb_ref, ws_ref, bs_ref,
                   q_ref, kc_ref, kt_ref, ckv_ref, kpe_ref, gated_ref, vn_ref, *, tl, cl):
    hb = h_ref[0].astype(BF16)
    z = _dot(hb, w_ref[...])
    o_ckv = MLA_Q_RANK
    o_u = o_ckv + MLA_KV_RANK
    o_v = o_u + GMLP_WIDTH
    o_k = o_v + GMLP_WIDTH
    cq = z[:, :MLA_Q_RANK]
    cqn = cq * lax.rsqrt(jnp.mean(cq * cq, axis=-1, keepdims=True) + LN_EPS) * qg_ref[...]
    cqb = cqn.astype(BF16)
    qn = _dot(cqb, wqn_ref[...])
    qp = _dot(cqb, wqp_ref[...])
    qp = _rope(qp, cq_ref[...], sq_ref[...])
    qpe = _dot((qp * MLA_SCALE).astype(BF16), perm_ref[...])
    for h in range(MLA_HEADS):
        sl = slice(h * LANES, (h + 1) * LANES)
        qa = _dot((qn[:, sl] * MLA_SCALE).astype(BF16), wuk_ref[h])
        q_ref[0, h, :, 0:LANES] = qa.astype(BF16)
        q_ref[0, h, :, LANES:] = qpe[:, sl].astype(BF16)

    ckv = z[:, o_ckv:o_u]
    ckvn = ckv * lax.rsqrt(jnp.mean(ckv * ckv, axis=-1, keepdims=True) + LN_EPS) * kvg_ref[...]
    kp = z[:, o_k:]
    kp = _rope(kp, ck_ref[...], sk_ref[...])
    ckv_ref[0] = ckvn
    kpe_ref[0] = kp[:, :MLA_ROPE]
    one = (lax.broadcasted_iota(I32, (1, LANES), 1) == MLA_ONE_LANE - LANES).astype(F32)
    kp1 = kp + one
    kc_ref[0, :, 0:LANES] = ckvn.astype(BF16)
    kc_ref[0, :, LANES:] = kp1.astype(BF16)
    kt_ref[0, 0:LANES, :] = ckvn.T.astype(BF16)
    kt_ref[0, LANES:, :] = kp1.T.astype(BF16)

    gu = _gelu(z[:, o_u:o_v])
    vn = _layernorm(_gelu(z[:, o_v:o_k]), gg_ref[...], gb_ref[...])
    vn_ref[0] = vn
    vnb = vn.astype(BF16)
    for n in range(tl // cl):
        rs = slice(n * cl, (n + 1) * cl)
        for g in range(GMLP_GROUPS):
            ls = slice(g * GMLP_CH, (g + 1) * GMLP_CH)
            sg = _dot(ws_ref[g], vnb[rs, ls]) + bs_ref[:, ls]
            gated_ref[0, rs, ls] = gu[rs, ls] * sg


def _odd_in(h3, w_in, q_g, w_qn, w_qp, w_uk, perm, kv_g, cos_q, sin_q, cos_k, sin_k,
            gm_g, gm_b, ws, bs):
    bsz, l, _ = h3.shape
    tl = min(l, 512)
    cl = min(l, GMLP_CHUNK)
    rowb = lambda n: pl.BlockSpec((1, tl, n), lambda b, i: (b, i, 0))
    tab = lambda n: pl.BlockSpec((tl, n), lambda b, i: (i, 0))
    consts = (w_in, q_g, w_qn, w_qp, w_uk, perm, kv_g)
    consts2 = (gm_g, gm_b, ws, bs)
    return pl.pallas_call(
        functools.partial(_odd_in_kernel, tl=tl, cl=cl),
        grid=(bsz, l // tl),
        in_specs=[rowb(D_MODEL)] + [_full_spec(a, 2) for a in consts]
                 + [tab(MLA_HEADS * MLA_ROPE), tab(MLA_HEADS * MLA_ROPE), tab(LANES), tab(LANES)]
                 + [_full_spec(a, 2) for a in consts2],
        out_specs=[pl.BlockSpec((1, MLA_HEADS, tl, MLA_QW), lambda b, i: (b, 0, i, 0)),
                   rowb(MLA_QW), pl.BlockSpec((1, MLA_QW, tl), lambda b, i: (b, 0, i)),
                   rowb(MLA_KV_RANK), rowb(MLA_ROPE), rowb(GMLP_WIDTH), rowb(GMLP_WIDTH)],
        out_shape=[jax.ShapeDtypeStruct((bsz, MLA_HEADS, l, MLA_QW), BF16),
                   jax.ShapeDtypeStruct((bsz, l, MLA_QW), BF16),
                   jax.ShapeDtypeStruct((bsz, MLA_QW, l), BF16),
                   jax.ShapeDtypeStruct((bsz, l, MLA_KV_RANK), F32),
                   jax.ShapeDtypeStruct((bsz, l, MLA_ROPE), F32),
                   jax.ShapeDtypeStruct((bsz, l, GMLP_WIDTH), F32),
                   jax.ShapeDtypeStruct((bsz, l, GMLP_WIDTH), F32)],
        compiler_params=_cp("parallel", "parallel"),
        name="odd_in",
    )(h3, *consts, cos_q, sin_q, cos_k, sin_k, *consts2)


def _attn_kernel(qi_ref, kj_ref, flag_ref, q_ref, k_ref, kt_ref, wuv_ref, o_ref,
                 m_scr, acc_scr, *, tq, tk, pos0, n_keys):
    p = pl.program_id(1)
    flag = flag_ref[p]

    @pl.when((flag & 1) != 0)
    def _():
        m_scr[...] = jnp.full_like(m_scr, -jnp.inf)
        acc_scr[...] = jnp.zeros_like(acc_scr)

    def step(hidden):
        kt = kt_ref[0]
        kk = k_ref[0]
        if hidden:
            qpos = pos0 + qi_ref[p] * tq + lax.broadcasted_iota(I32, (tq, 1), 0)
            kpos = kj_ref[p] * tk + lax.broadcasted_iota(I32, (1, tk), 1)
            visible = ((kpos // CHUNK) <= (qpos // CHUNK)) & (kpos < n_keys)
        for h in range(MLA_HEADS):
            s = _dot(q_ref[0, h], kt)
            if hidden:
                s = jnp.where(visible, s, -jnp.inf)
            m_old = m_scr[h]
            m_new = jnp.maximum(m_old, jnp.max(s, axis=-1, keepdims=True))
            alpha = jnp.exp(m_old - m_new)
            pr = jnp.exp(s - m_new).astype(BF16)
            acc_scr[h] = alpha * acc_scr[h] + _dot(pr, kk)
            m_scr[h] = m_new

    @pl.when((flag & 4) != 0)
    def _():
        step(True)

    @pl.when((flag & 4) == 0)
    def _():
        step(False)

    @pl.when((flag & 2) != 0)
    def _():
        head_of_lane = lax.broadcasted_iota(I32, (1, MLA_HEADS * MLA_V), 1) // MLA_V
        out = jnp.zeros((tq, MLA_HEADS * MLA_V), F32)
        for h in range(MLA_HEADS):
            acc = acc_scr[h]
            lat = (acc[:, :MLA_KV_RANK] / acc[:, MLA_ONE_LANE:MLA_ONE_LANE + 1]).astype(BF16)
            out = out + jnp.where(head_of_lane == h, _dot(lat, wuv_ref[...]), 0.0)
        o_ref[0] = out


def _attn_pairs(l, tq, tk, pos0, n_keys):
    qi, kj, flag = [], [], []
    for i in range(l // tq):
        q_first = pos0 + i * tq
        q_last = q_first + tq - 1
        vis = min(CHUNK * (q_last // CHUNK) + CHUNK - 1, n_keys - 1)
        nj = vis // tk + 1
        for j in range(nj):
            hidden = ((j + 1) * tk - 1) // CHUNK > q_first // CHUNK or (j + 1) * tk > n_keys
            qi.append(i); kj.append(j)
            flag.append(int(j == 0) + 2 * int(j == nj - 1) + 4 * int(hidden))
    return [jnp.asarray(np.array(a, np.int32)) for a in (qi, kj, flag)]


def _attention(q4, kc3, kt3, w_uv, pos0, n_keys, tq, tk):
    bsz, _, l, _ = q4.shape
    pairs = _attn_pairs(l, tq, tk, pos0, n_keys)
    npairs = int(pairs[0].shape[0])
    return pl.pallas_call(
        functools.partial(_attn_kernel, tq=tq, tk=tk, pos0=pos0, n_keys=n_keys),
        grid_spec=pltpu.PrefetchScalarGridSpec(
            num_scalar_prefetch=3,
            grid=(bsz, npairs),
            in_specs=[pl.BlockSpec((1, MLA_HEADS, tq, MLA_QW), lambda b, p, qi, kj, f: (b, 0, qi[p], 0)),
                      pl.BlockSpec((1, tk, MLA_QW), lambda b, p, qi, kj, f: (b, kj[p], 0)),
                      pl.BlockSpec((1, MLA_QW, tk), lambda b, p, qi, kj, f: (b, 0, kj[p])),
                      pl.BlockSpec(w_uv.shape, lambda b, p, qi, kj, f: (0, 0))],
            out_specs=pl.BlockSpec((1, tq, MLA_HEADS * MLA_V), lambda b, p, qi, kj, f: (b, qi[p], 0)),
            scratch_shapes=[pltpu.VMEM((MLA_HEADS, tq, 1), F32),
                            pltpu.VMEM((MLA_HEADS, tq, MLA_QW), F32)]),
        out_shape=jax.ShapeDtypeStruct((bsz, l, MLA_HEADS * MLA_V), F32),
        compiler_params=_cp("parallel", "arbitrary"),
        name="mla_attention",
    )(*pairs, q4, kc3, kt3, w_uv)


def _rope_tables(pos0, l, width):
    half = MLA_ROPE // 2
    inv = ROPE_THETA ** (-jnp.arange(half, dtype=F32) * 2.0 / MLA_ROPE)
    ang = (pos0 + jnp.arange(l)).astype(F32)[:, None] * inv[None, :]
    cos, sin = jnp.cos(ang), jnp.sin(ang)
    cos32 = jnp.concatenate([cos, cos], axis=1)
    sin32 = jnp.concatenate([-sin, sin], axis=1)
    reps = width // MLA_ROPE
    return jnp.tile(cos32, (1, reps)), jnp.tile(sin32, (1, reps))


def _prep_even(w_in, w_gate_up, b_gate, gla_norm_g, pool_w, pool_scale, w_out):
    o_r = 2 * GLA_QK + GLA_V
    o_g = o_r + GLA_V
    o_p = o_g + GLA_GATE_RANK
    w_main = jnp.concatenate([w_in[:, :o_g], w_in[:, o_p:]], axis=1).astype(BF16)
    w_g = jnp.pad(w_in[:, o_g:o_p], ((0, 0), (0, LANES - GLA_GATE_RANK))).astype(BF16)
    w_gu = jnp.pad(w_gate_up, ((0, LANES - GLA_GATE_RANK), (0, 0))).astype(BF16)
    return dict(w_main=w_main, w_g=w_g, w_gu=w_gu, b_g=b_gate.reshape(1, -1),
                gnorm=gla_norm_g.reshape(1, -1), pool_w=pool_w.astype(BF16),
                pool_scale=pool_scale.reshape(1, -1), w_out=w_out.astype(BF16))


def _prep_odd(w_in, q_norm_g, w_uq, kv_norm_g, w_uk, w_uv, gm_g, gm_b, gm_ws, gm_bs, w_out):
    o_ckv = MLA_Q_RANK
    o_kpe = o_ckv + MLA_KV_RANK
    o_u = o_kpe + MLA_ROPE
    w_in2 = jnp.concatenate([w_in[:, :o_kpe], w_in[:, o_u:], w_in[:, o_kpe:o_u],
                             jnp.zeros((D_MODEL, LANES - MLA_ROPE), F32)], axis=1).astype(BF16)
    uq = w_uq.reshape(MLA_Q_RANK, MLA_HEADS, MLA_NOPE + MLA_ROPE)
    w_qn = jnp.pad(uq[:, :, :MLA_NOPE], ((0, 0), (0, 0), (0, LANES - MLA_NOPE)))
    w_qn = w_qn.reshape(MLA_Q_RANK, MLA_HEADS * LANES).astype(BF16)
    w_qp = uq[:, :, MLA_NOPE:].reshape(MLA_Q_RANK, MLA_HEADS * MLA_ROPE).astype(BF16)
    uk = w_uk.reshape(MLA_KV_RANK, MLA_HEADS, MLA_NOPE).transpose(1, 2, 0)
    w_ukp = jnp.pad(uk, ((0, 0), (0, LANES - MLA_NOPE), (0, 0))).astype(BF16)
    src = np.arange(MLA_HEADS * MLA_ROPE)
    perm = np.zeros((MLA_HEADS * MLA_ROPE, MLA_HEADS * LANES), np.float32)
    perm[src, (src // MLA_ROPE) * LANES + src % MLA_ROPE] = 1.0
    return dict(w_in=w_in2, q_g=q_norm_g.reshape(1, -1), w_qn=w_qn, w_qp=w_qp, w_uk=w_ukp,
                perm=jnp.asarray(perm, BF16), kv_g=kv_norm_g.reshape(1, -1), w_uv=w_uv.astype(BF16),
                gm_g=gm_g.reshape(1, -1), gm_b=gm_b.reshape(1, -1), gm_ws=gm_ws, gm_bs=gm_bs,
                w_out=w_out.astype(BF16))


def _prep_route(wg, bg, we, be):
    w_r = jnp.pad(jnp.concatenate([wg, we], axis=1), ((0, 0), (0, LANES - N_GROUPS - N_EXPERTS)))
    b_r = jnp.pad(jnp.concatenate([bg, be]), (0, LANES - N_GROUPS - N_EXPERTS)).reshape(1, LANES)
    return w_r, b_r


def _even_mixer(h3, st0, hist, pos0, pw):
    bsz, l, _ = h3.shape
    q, k, la, v, r, xp = _even_in(h3.reshape(bsz * l, D_MODEL), pw['w_main'], pw['w_g'], pw['w_gu'], pw['b_g'])
    to3 = lambda a: a.reshape(bsz, l, a.shape[-1])
    o, st = _gla(to3(q), to3(k), to3(la), to3(v), to3(r), st0, pw['gnorm'])
    xp3 = to3(xp)
    hist16 = jnp.pad(hist, ((0, 0), (POOL_HALO - POOL_HIST, 0), (0, 0)))
    pooled = _pool(xp3, hist16, pw['pool_w'], pw['pool_scale'], pos0)
    hist_new = jnp.concatenate([hist, xp3], axis=1)[:, -POOL_HIST:]
    return o, pooled, st, hist_new


def _odd_mixer(h3, ckv_past, kpe_past, pw):
    bsz, l, _ = h3.shape
    n_past = ckv_past.shape[1]
    cos_q, sin_q = _rope_tables(n_past, l, MLA_HEADS * MLA_ROPE)
    cos_k, sin_k = _rope_tables(n_past, l, MLA_ROPE)
    padk = ((0, 0), (0, LANES - MLA_ROPE))
    cos_k, sin_k = jnp.pad(cos_k, padk), jnp.pad(sin_k, padk)
    cl = min(l, GMLP_CHUNK)
    ws = jnp.tril(pw['gm_ws'][:, :cl, :cl]).astype(BF16)
    bs = jnp.repeat(pw['gm_bs'][:, :cl].T, GMLP_CH, axis=1)
    q4, kc, kt, ckv, kpe, gated, vn = _odd_in(h3, pw['w_in'], pw['q_g'], pw['w_qn'], pw['w_qp'], pw['w_uk'],
                                              pw['perm'], pw['kv_g'], cos_q, sin_q, cos_k, sin_k,
                                              pw['gm_g'], pw['gm_b'], ws, bs)
    n_keys = n_past + l
    tq, tk = min(l, 256), 512
    if n_past:
        past = jnp.concatenate([ckv_past, kpe_past, jnp.ones((bsz, n_past, 1), F32),
                                jnp.zeros((bsz, n_past, MLA_QW - MLA_ONE_LANE - 1), F32)], axis=2).astype(BF16)
        kc = jnp.concatenate([past, kc], axis=1)
        kt = jnp.concatenate([past.transpose(0, 2, 1), kt], axis=2)
    kc = jnp.pad(kc, ((0, 0), (0, -n_keys % tk), (0, 0)))
    kt = jnp.pad(kt, ((0, 0), (0, 0), (0, -n_keys % tk)))
    attn = _attention(q4, kc, kt, pw['w_uv'], n_past, n_keys, tq, tk)
    return attn, gated, ckv, kpe, vn


def _finish_layer(a3, b3, h3, w_out, lw):
    bsz, l, _ = h3.shape
    t = bsz * l
    h1, xs, rs, rg, n16 = _out_route(a3.reshape(t, -1), b3.reshape(t, -1), h3.reshape(t, D_MODEL), w_out,
                                     lw['ln_mix_g'], lw['ln_mix_b'], lw['w_r'], lw['b_r'])
    h2 = _moe(h1, xs, rs, rg, n16, lw['w1'], lw['w3'], lw['w2'], lw['ln_ffn_g'], lw['ln_ffn_b'])
    return h2.reshape(bsz, l, D_MODEL)


def kernel(x_prompt, x_sample, state_gla, state_pool, cache_mla_ckv, cache_mla_kpe, w_in_even, w_gate_up, b_gate, gla_norm_g, pool_w, pool_scale, w_out_even, w_in_odd, mla_q_norm_g, mla_w_uq, mla_kv_norm_g, mla_w_uk, mla_w_uv, gmlp_norm_g, gmlp_norm_b, gmlp_ws, gmlp_bs, w_out_odd, ln_mix_g, ln_mix_b, router_group_w, router_group_b, router_expert_w, router_expert_b, expert_w1, expert_w3, expert_w2, ln_ffn_g, ln_ffn_b):
    hp, hs = x_prompt, x_sample
    bp = hp.shape[0]
    past_len = cache_mla_ckv.shape[2]
    gla_p, gla_s, pool_p, pool_s = [], [], [], []
    ckv_p, ckv_s, kpe_p, kpe_s, gv_s = [], [], [], [], []

    def state_to_t(s):
        return s.transpose(0, 3, 1, 2).reshape(s.shape[0], GLA_DV, GLA_QK)

    def state_from_t(st):
        return st.reshape(st.shape[0], GLA_DV, GLA_HEADS, GLA_DK).transpose(0, 2, 3, 1)

    for layer in range(DEPTH):
        i = layer // 2
        w_r, b_r = _prep_route(router_group_w[layer], router_group_b[layer],
                               router_expert_w[layer], router_expert_b[layer])
        lw = dict(ln_mix_g=ln_mix_g[layer].reshape(1, -1), ln_mix_b=ln_mix_b[layer].reshape(1, -1),
                  ln_ffn_g=ln_ffn_g[layer].reshape(1, -1), ln_ffn_b=ln_ffn_b[layer].reshape(1, -1),
                  w_r=w_r, b_r=b_r, w1=expert_w1[layer].astype(BF16), w3=expert_w3[layer].astype(BF16),
                  w2=expert_w2[layer].astype(BF16))
        if layer % 2 == 0:
            pw = _prep_even(w_in_even[i], w_gate_up[i], b_gate[i], gla_norm_g[i], pool_w[i], pool_scale[i],
                            w_out_even[i])
            st0 = jnp.zeros((bp, GLA_DV, GLA_QK), F32)
            hist0 = jnp.zeros((bp, POOL_HIST, POOL_WIDTH), F32)
            op, pp, stp, histp = _even_mixer(hp, st0, hist0, 0, pw)
            os_, ps, sts, hists = _even_mixer(hs, state_to_t(state_gla[i]), state_pool[i], past_len, pw)
            gla_p.append(state_from_t(stp)); gla_s.append(state_from_t(sts))
            pool_p.append(histp); pool_s.append(hists)
            ap, bpj, as_, bsj = op, pp, os_, ps
        else:
            pw = _prep_odd(w_in_odd[i], mla_q_norm_g[i], mla_w_uq[i], mla_kv_norm_g[i], mla_w_uk[i], mla_w_uv[i],
                           gmlp_norm_g[i], gmlp_norm_b[i], gmlp_ws[i], gmlp_bs[i], w_out_odd[i])
            no_ckv = jnp.zeros((bp, 0, MLA_KV_RANK), F32)
            no_kpe = jnp.zeros((bp, 0, MLA_ROPE), F32)
            ap, bpj, cp, kp, _ = _odd_mixer(hp, no_ckv, no_kpe, pw)
            as_, bsj, cs, ks, vs = _odd_mixer(hs, cache_mla_ckv[i], cache_mla_kpe[i], pw)
            ckv_p.append(cp); ckv_s.append(cs); kpe_p.append(kp); kpe_s.append(ks); gv_s.append(vs)
        hp = _finish_layer(ap, bpj, hp, pw['w_out'], lw)
        hs = _finish_layer(as_, bsj, hs, pw['w_out'], lw)
    return (hp, hs, jnp.stack(gla_p), jnp.stack(gla_s), jnp.stack(pool_p), jnp.stack(pool_s),
            jnp.stack(ckv_p), jnp.stack(ckv_s), jnp.stack(kpe_p), jnp.stack(kpe_s), jnp.stack(gv_s))
```

```python
import functools

import numpy as np
import jax
import jax.numpy as jnp
from jax import lax
from jax.experimental import pallas as pl
from jax.experimental.pallas import tpu as pltpu

F32 = jnp.float32
BF16 = jnp.bfloat16
I32 = jnp.int32

D_MODEL = 1024
DEPTH = 2
CHUNK = 64
ALPHA = (2 * DEPTH) ** 0.25
LN_EPS = 1e-5

GLA_HEADS = 4
GLA_DV = 128
GLA_DK = 64
GLA_QK = GLA_HEADS * GLA_DK
GLA_V = GLA_HEADS * GLA_DV
GLA_GATE_RANK = 16
GLA_GATE_TAU = 16.0
GLA_SUB = 8

POOL_WIDTH = 512
POOL_CH = 128
POOL_WINDOWS = (2, 4, 8, 16)
POOL_HIST = 15
POOL_HALO = 16

MLA_HEADS = 8
MLA_NOPE = 64
MLA_ROPE = 32
MLA_V = 64
MLA_Q_RANK = 256
MLA_KV_RANK = 128
ROPE_THETA = 10000.0
MLA_SCALE = (MLA_NOPE + MLA_ROPE) ** -0.5
MLA_QW = 256
MLA_ONE_LANE = MLA_KV_RANK + MLA_ROPE
GMLP_WIDTH = 512
GMLP_CH = 128
GMLP_GROUPS = 4
GMLP_CHUNK = 128

N_GROUPS = 4
EXPERTS_PER_GROUP = 8
N_EXPERTS = 32
TOP_K = 2
D_EXPERT = 256
MOE_ROWS = 512
MOE_CHUNK = 16
MOE_BLOCK_CHUNKS = MOE_ROWS // MOE_CHUNK

LANES = 128
VMEM_LIMIT = 48 * 1024 * 1024


def _cp(*sem):
    return pltpu.CompilerParams(dimension_semantics=sem, vmem_limit_bytes=VMEM_LIMIT)


def _dot(a, b):
    return jnp.dot(a, b, preferred_element_type=F32)


def _dot_nt(a, b):
    return lax.dot_general(a, b, (((1,), (1,)), ((), ())), preferred_element_type=F32)


def _dot_tn(a, b):
    return lax.dot_general(a, b, (((0,), (0,)), ((), ())), preferred_element_type=F32)


def _split3(x):
    hi = x.astype(BF16)
    r1 = x - hi.astype(F32)
    mid = r1.astype(BF16)
    lo = (r1 - mid.astype(F32)).astype(BF16)
    return hi, mid, lo


def _layernorm(x, g, b):
    mu = jnp.mean(x, axis=-1, keepdims=True)
    xc = x - mu
    var = jnp.mean(xc * xc, axis=-1, keepdims=True)
    return xc * lax.rsqrt(var + LN_EPS) * g + b


def _gelu(x):
    return 0.5 * x * (1.0 + jnp.tanh(0.7978845608028654 * (x + 0.044715 * (x * x * x))))


def _sigmoid(x):
    return 1.0 / (1.0 + jnp.exp(-x))


def _full_spec(a, nargs):
    nd = a.ndim
    if nargs == 1:
        return pl.BlockSpec(a.shape, lambda i: (0,) * nd)
    return pl.BlockSpec(a.shape, lambda i, j: (0,) * nd)


def _even_in_kernel(x_ref, w_ref, wg_ref, wgu_ref, bg_ref,
                    q_ref, k_ref, la_ref, v_ref, r_ref, xp_ref):
    xb = x_ref[...].astype(BF16)
    z = _dot(xb, w_ref[...])
    q_ref[...] = z[:, 0:GLA_QK] * (GLA_DK ** -0.5)
    k_ref[...] = z[:, GLA_QK:2 * GLA_QK]
    v_ref[...] = z[:, 2 * GLA_QK:2 * GLA_QK + GLA_V]
    r_ref[...] = z[:, 2 * GLA_QK + GLA_V:2 * GLA_QK + 2 * GLA_V]
    xp_ref[...] = z[:, 2 * GLA_QK + 2 * GLA_V:]
    g = _dot(xb, wg_ref[...])
    pre = _dot(g.astype(BF16), wgu_ref[...]) + bg_ref[...]
    logsig = jnp.minimum(pre, 0.0) - jnp.log(1.0 + jnp.exp(-jnp.abs(pre)))
    la_ref[...] = logsig * (1.0 / GLA_GATE_TAU)


def _even_in(x2, w_main, w_g, w_gu, b_g):
    t = x2.shape[0]
    tm = min(512, t)
    row = lambda n: pl.BlockSpec((tm, n), lambda i: (i, 0))
    widths = (GLA_QK, GLA_QK, GLA_QK, GLA_V, GLA_V, POOL_WIDTH)
    return pl.pallas_call(
        _even_in_kernel,
        grid=(t // tm,),
        in_specs=[row(D_MODEL)] + [_full_spec(a, 1) for a in (w_main, w_g, w_gu, b_g)],
        out_specs=[row(n) for n in widths],
        out_shape=[jax.ShapeDtypeStruct((t, n), F32) for n in widths],
        compiler_params=_cp("parallel"),
        name="even_in",
    )(x2, w_main, w_g, w_gu, b_g)


def _gla_consts(c):
    n = c * GLA_SUB
    tri = np.tril(np.ones((c, c), np.float32))
    headsum = (np.arange(GLA_QK)[:, None] // GLA_DK == np.arange(GLA_V)[None, :] // GLA_DV).astype(np.float32)
    msel = (np.arange(n)[None, :] // GLA_SUB == np.arange(c)[:, None]).astype(np.float32)
    return [jnp.asarray(a, BF16) for a in (tri, headsum, msel)]


def _gla_chunk(q, k, la, v, st, c, tri, headsum, msel):
    lane = lax.broadcasted_iota(I32, (1, GLA_QK), 1)
    head_of_lane = lane // GLA_DK
    row = lax.broadcasted_iota(I32, (c, 1), 0)
    ii = lax.broadcasted_iota(I32, (c, c), 0)
    jj = lax.broadcasted_iota(I32, (c, c), 1)

    hi, mid, lo = _split3(la)
    b = _dot(tri, hi) + _dot(tri, mid) + _dot(tri, lo)

    head_masks = [head_of_lane == h for h in range(GLA_HEADS)]
    vb = v.astype(BF16)

    a_off = [jnp.zeros((c, c), F32) for _ in range(GLA_HEADS)]
    s = c // 2
    while s >= GLA_SUB:
        nblk = c // (2 * s)
        blk = row // (2 * s)
        right = ((row // s) % 2) == 1
        bref = jnp.zeros((c, GLA_QK), F32)
        for m in range(nblk):
            r0 = m * 2 * s + s - 1
            bref = jnp.where(blk == m, b[r0:r0 + 1, :], bref)
        qe = jnp.where(right, q * jnp.exp(jnp.minimum(b - bref, 0.0)), 0.0)
        ke = jnp.where(right, 0.0, k * jnp.exp(jnp.minimum(bref - b, 0.0))).astype(BF16)
        same = (ii // (2 * s)) == (jj // (2 * s))
        for h in range(GLA_HEADS):
            a = _dot_nt(jnp.where(head_masks[h], qe, 0.0).astype(BF16), ke)
            a_off[h] = a_off[h] + (a if nblk == 1 else jnp.where(same, a, 0.0))
        s //= 2

    nsb = c // GLA_SUB
    parts = []
    for i in range(nsb):
        sl = slice(i * GLA_SUB, (i + 1) * GLA_SUB)
        bi, qi, ki = b[sl], q[sl], k[sl]
        diff = bi[:, None, :] - bi[None, :, :]
        p = qi[:, None, :] * ki[None, :, :] * jnp.exp(jnp.minimum(diff, 0.0))
        parts.append(p.reshape(GLA_SUB * GLA_SUB, GLA_QK))
    pcat = jnp.concatenate(parts, axis=0).astype(BF16)
    n = nsb * GLA_SUB * GLA_SUB
    rsum = _dot(pcat, headsum)
    idx = lax.broadcasted_iota(I32, (n, 1), 0)
    causal = (idx % GLA_SUB) <= ((idx // GLA_SUB) % GLA_SUB)

    qb = q * jnp.exp(b)
    b_end = b[c - 1:c, :]
    kd = (k * jnp.exp(b_end - b)).astype(BF16)
    stb = st.astype(BF16)
    st_new = st * jnp.exp(b_end)

    outs = []
    for h in range(GLA_HEADS):
        vh = vb[:, h * GLA_DV:(h + 1) * GLA_DV]
        vf = v[:, h * GLA_DV:(h + 1) * GLA_DV]
        vt = jnp.concatenate(
            [jnp.broadcast_to(vf[i * GLA_SUB:(i + 1) * GLA_SUB][None], (GLA_SUB, GLA_SUB, GLA_DV))
             .reshape(GLA_SUB * GLA_SUB, GLA_DV) for i in range(nsb)], axis=0)
        xh = jnp.where(causal, rsum[:, h * GLA_DV:(h + 1) * GLA_DV], 0.0) * vt
        o = _dot(msel, xh.astype(BF16))
        o = o + _dot(a_off[h].astype(BF16), vh)
        o = o + _dot_nt(jnp.where(head_masks[h], qb, 0.0).astype(BF16), stb)
        outs.append(o)
        st_new = st_new + jnp.where(head_masks[h], _dot_tn(vh, kd), 0.0)
    return outs, st_new


def _gla_kernel(q_ref, k_ref, la_ref, v_ref, r_ref, st0_ref, g_ref, tri_ref, hs_ref, ms_ref,
                o_ref, st_ref, st_scr, *, c, nchunks):
    @pl.when(pl.program_id(1) == 0)
    def _():
        st_scr[...] = st0_ref[0]

    def body(ci, carry):
        r0 = pl.multiple_of(ci * c, c)
        rows = pl.ds(r0, c)
        outs, st_new = _gla_chunk(q_ref[0, rows, :], k_ref[0, rows, :], la_ref[0, rows, :],
                                  v_ref[0, rows, :], st_scr[...], c, tri_ref[...], hs_ref[...], ms_ref[...])
        st_scr[...] = st_new
        r = r_ref[0, rows, :]
        g = g_ref[...]
        for h in range(GLA_HEADS):
            o = outs[h]
            sl = slice(h * GLA_DV, (h + 1) * GLA_DV)
            on = o * lax.rsqrt(jnp.mean(o * o, axis=-1, keepdims=True) + LN_EPS) * g
            rh = r[:, sl]
            o_ref[0, rows, sl] = on * (rh * _sigmoid(rh))
        return carry

    lax.fori_loop(0, nchunks, body, 0)
    st_ref[0] = st_scr[...]


def _gla(q3, k3, la3, v3, r3, st0, gnorm):
    bsz, l, _ = q3.shape
    c = min(l, CHUNK)
    tl = min(l, 512)
    blk = lambda n: pl.BlockSpec((1, tl, n), lambda b, i: (b, i, 0))
    st_spec = pl.BlockSpec((1, GLA_DV, GLA_QK), lambda b, i: (b, 0, 0))
    consts = _gla_consts(c)
    return pl.pallas_call(
        functools.partial(_gla_kernel, c=c, nchunks=tl // c),
        grid=(bsz, l // tl),
        in_specs=[blk(GLA_QK), blk(GLA_QK), blk(GLA_QK), blk(GLA_V), blk(GLA_V), st_spec,
                  _full_spec(gnorm, 2)] + [_full_spec(a, 2) for a in consts],
        out_specs=[blk(GLA_V), st_spec],
        out_shape=[jax.ShapeDtypeStruct((bsz, l, GLA_V), F32),
                   jax.ShapeDtypeStruct((bsz, GLA_DV, GLA_QK), F32)],
        scratch_shapes=[pltpu.VMEM((GLA_DV, GLA_QK), F32)],
        compiler_params=_cp("parallel", "arbitrary"),
        name="gla",
    )(q3, k3, la3, v3, r3, st0, gnorm, *consts)


def _pool_kernel(x_ref, halo_ref, hist_ref, w_ref, scale_ref, o_ref, *, tl, pos0):
    i = pl.program_id(1)
    x = x_ref[0]
    prev = jnp.where(i == 0, hist_ref[0], halo_ref[0])
    e = jnp.concatenate([prev, x], axis=0)
    t = i * tl + lax.broadcasted_iota(I32, (tl, 1), 0)
    pos = pos0 + t
    sums = []
    shift = 1
    for g, w in enumerate(POOL_WINDOWS):
        e = e[:, POOL_CH:] if g > 0 else e
        while shift < w:
            e = e[shift:] + e[:-shift]
            shift *= 2
        off = POOL_HALO - (w - 1)
        sums.append(e[off:off + tl, :POOL_CH])
    outs = []
    for g, w in enumerate(POOL_WINDOWS):
        cnt = jnp.minimum(pos + 1, w).astype(F32)
        mix = sums[g] / cnt - x[:, g * POOL_CH:(g + 1) * POOL_CH]
        outs.append(_dot(mix.astype(BF16), w_ref[g]))
    o_ref[0] = jnp.concatenate(outs, axis=1) * scale_ref[...]


def _pool(xp3, hist16, pool_w, pool_scale, pos0):
    bsz, l, _ = xp3.shape
    tl = min(l, 512)
    per = tl // POOL_HALO
    return pl.pallas_call(
        functools.partial(_pool_kernel, tl=tl, pos0=pos0),
        grid=(bsz, l // tl),
        in_specs=[pl.BlockSpec((1, tl, POOL_WIDTH), lambda b, i: (b, i, 0)),
                  pl.BlockSpec((1, POOL_HALO, POOL_WIDTH), lambda b, i: (b, jnp.maximum(i * per - 1, 0), 0)),
                  pl.BlockSpec((1, POOL_HALO, POOL_WIDTH), lambda b, i: (b, 0, 0)),
                  _full_spec(pool_w, 2), _full_spec(pool_scale, 2)],
        out_specs=pl.BlockSpec((1, tl, POOL_WIDTH), lambda b, i: (b, i, 0)),
        out_shape=jax.ShapeDtypeStruct((bsz, l, POOL_WIDTH), F32),
        compiler_params=_cp("parallel", "parallel"),
        name="pool",
    )(xp3, xp3, hist16, pool_w, pool_scale)


def _out_route_kernel(a_ref, b_ref, h_ref, w_ref, g_ref, bt_ref, wr_ref, br_ref,
                      h1_ref, xs_ref, rs_ref, rg_ref, n16_ref, *, tm, slots):
    half = w_ref.shape[0] // 2
    y = _dot(a_ref[...].astype(BF16), w_ref[0:half, :]) + _dot(b_ref[...].astype(BF16), w_ref[half:, :])
    x = _layernorm(ALPHA * h_ref[...] + y, g_ref[...], bt_ref[...])
    h1_ref[...] = x

    xh = x.astype(BF16)
    xl = (x - xh.astype(F32)).astype(BF16)
    wr = wr_ref[...]
    wh = wr.astype(BF16)
    wl = (wr - wh.astype(F32)).astype(BF16)
    logits = _dot(xh, wh) + _dot(xl, wh) + _dot(xh, wl) + br_ref[...]

    lane = lax.broadcasted_iota(I32, (tm, LANES), 1)
    lanef = lane.astype(F32)
    neg = -jnp.inf
    big = jnp.float32(1 << 20)

    def first_lane(hit):
        return jnp.min(jnp.where(hit, lanef, big), axis=-1, keepdims=True).astype(I32)

    gl = jnp.where(lane < N_GROUPS, logits, neg)
    gmax = jnp.max(gl, axis=-1, keepdims=True)
    g_sel = first_lane(gl == gmax)
    g_prob = 1.0 / jnp.sum(jnp.exp(gl - gmax), axis=-1, keepdims=True)
    eidx = lane - N_GROUPS
    in_grp = (eidx >= 0) & (eidx < N_EXPERTS) & ((eidx // EXPERTS_PER_GROUP) == g_sel)
    el = jnp.where(in_grp, logits, neg)
    v1 = jnp.max(el, axis=-1, keepdims=True)
    i1 = first_lane(el == v1)
    el2 = jnp.where(lane == i1, neg, el)
    v2 = jnp.max(el2, axis=-1, keepdims=True)
    i2 = first_lane(el2 == v2)
    e21 = jnp.exp(v2 - v1)
    gate1 = g_prob / (1.0 + e21)
    gate2 = g_prob * e21 / (1.0 + e21)
    e1 = i1 - N_GROUPS
    e2 = i2 - N_GROUPS

    oh1 = lane == e1
    oh2 = lane == e2
    oh = oh1.astype(F32) + oh2.astype(F32)
    ti = lax.broadcasted_iota(I32, (tm, tm), 0)
    tj = lax.broadcasted_iota(I32, (tm, tm), 1)
    before = _dot((tj < ti).astype(BF16), oh.astype(BF16))
    cnt = jnp.sum(oh, axis=0, keepdims=True)
    n16 = jnp.floor((cnt + (MOE_CHUNK - 1)) * (1.0 / MOE_CHUNK))
    n16_8 = jnp.broadcast_to(n16, (8, LANES))
    ui = lax.broadcasted_iota(I32, (LANES, LANES), 0)
    uj = lax.broadcasted_iota(I32, (LANES, LANES), 1)
    run_start = _dot(n16_8.astype(BF16), (ui < uj).astype(BF16))[0:1]
    slot_of = MOE_CHUNK * run_start + before
    slot1 = jnp.sum(jnp.where(oh1, slot_of, 0.0), axis=-1, keepdims=True).astype(I32)
    slot2 = jnp.sum(jnp.where(oh2, slot_of, 0.0), axis=-1, keepdims=True).astype(I32)
    sl = lax.broadcasted_iota(I32, (tm, slots), 1)
    place = ((sl == slot1) | (sl == slot2)).astype(BF16)
    xs_ref[...] = _dot_tn(place, xh).astype(BF16)

    rs_ref[...] = jnp.where(lane == 0, slot1, jnp.where(lane == 1, slot2, 0))
    rg_ref[...] = jnp.where(lane == 0, gate1, jnp.where(lane == 1, gate2, 0.0))
    n16_ref[...] = n16_8.astype(I32)


def _moe_slots(tm):
    worst = tm * TOP_K + N_EXPERTS * (MOE_CHUNK - 1)
    return -(-worst // MOE_ROWS) * MOE_ROWS


def _out_route(a2, b2, h2, w_out, ln_g, ln_b, w_r, b_r):
    t = h2.shape[0]
    tm = min(512, t)
    nt = t // tm
    slots = _moe_slots(tm)
    row = lambda n: pl.BlockSpec((tm, n), lambda i: (i, 0))
    return pl.pallas_call(
        functools.partial(_out_route_kernel, tm=tm, slots=slots),
        grid=(nt,),
        in_specs=[row(a2.shape[1]), row(b2.shape[1]), row(D_MODEL)]
                 + [_full_spec(a, 1) for a in (w_out, ln_g, ln_b, w_r, b_r)],
        out_specs=[row(D_MODEL), pl.BlockSpec((slots, D_MODEL), lambda i: (i, 0)), row(LANES), row(LANES),
                   pl.BlockSpec((8, LANES), lambda i: (i, 0))],
        out_shape=[jax.ShapeDtypeStruct((t, D_MODEL), F32), jax.ShapeDtypeStruct((nt * slots, D_MODEL), BF16),
                   jax.ShapeDtypeStruct((t, LANES), I32), jax.ShapeDtypeStruct((t, LANES), F32),
                   jax.ShapeDtypeStruct((nt * 8, LANES), I32)],
        compiler_params=_cp("parallel"),
        name="out_route",
    )(a2, b2, h2, w_out, ln_g, ln_b, w_r, b_r)


def _chunk_rows(chunk):
    return pl.ds(pl.multiple_of(chunk * MOE_CHUNK, MOE_CHUNK), MOE_CHUNK)


def _expert_kernel(src_ref, nreal_ref, be_ref, nu_ref, xs_ref, w1_ref, w3_ref, w2_ref, ys_ref,
                   xbuf, ybuf, gsem, ssem):
    del be_ref
    b = pl.program_id(0)
    nu = nu_ref[0]

    def gather(blk, slot, j):
        return pltpu.make_async_copy(xs_ref.at[_chunk_rows(src_ref[blk * MOE_BLOCK_CHUNKS + j]), :],
                                     xbuf.at[slot, pl.ds(j * MOE_CHUNK, MOE_CHUNK), :], gsem.at[slot])

    def scatter(blk, slot, j):
        return pltpu.make_async_copy(ybuf.at[slot, _chunk_rows(j), :],
                                     ys_ref.at[_chunk_rows(src_ref[blk * MOE_BLOCK_CHUNKS + j]), :],
                                     ssem.at[slot])

    def start_gather(blk, slot):
        for j in range(MOE_BLOCK_CHUNKS):
            gather(blk, slot, j).start()

    def wait_gather(slot):
        pltpu.make_async_copy(xs_ref.at[pl.ds(0, MOE_ROWS), :], xbuf.at[slot], gsem.at[slot]).wait()

    def for_real_chunks(blk, fn):
        def body(j, carry):
            fn(j)
            return carry
        lax.fori_loop(0, nreal_ref[blk], body, 0)

    def wait_scatter(blk, slot):
        full = nreal_ref[blk] == MOE_BLOCK_CHUNKS

        @pl.when(full)
        def _():
            pltpu.make_async_copy(ybuf.at[slot], ys_ref.at[pl.ds(0, MOE_ROWS), :], ssem.at[slot]).wait()

        @pl.when(jnp.logical_not(full))
        def _():
            for_real_chunks(blk, lambda j: scatter(blk, slot, j).wait())

    @pl.when(b < nu)
    def _():
        slot = b % 2

        @pl.when(b == 0)
        def _():
            start_gather(b, slot)

        @pl.when(b + 1 < nu)
        def _():
            start_gather(b + 1, 1 - slot)

        wait_gather(slot)

        @pl.when(b >= 2)
        def _():
            wait_scatter(b - 2, slot)

        xb = xbuf[slot]
        a = _dot(xb, w1_ref[0])
        hid = a * _sigmoid(a) * _dot(xb, w3_ref[0])
        ybuf[slot] = _dot(hid.astype(BF16), w2_ref[0]).astype(BF16)
        for_real_chunks(b, lambda j: scatter(b, slot, j).start())

        @pl.when(b == nu - 1)
        def _():
            wait_scatter(b, slot)

            @pl.when(b >= 1)
            def _():
                wait_scatter(b - 1, 1 - slot)


def _experts(xs, src, nreal, block_e, n_used, w1, w3, w2):
    nblk = src.shape[0] // MOE_BLOCK_CHUNKS
    wspec = lambda shape: pl.BlockSpec(shape, lambda i, src, nr, be, nu: (be[i], 0, 0))
    return pl.pallas_call(
        _expert_kernel,
        grid_spec=pltpu.PrefetchScalarGridSpec(
            num_scalar_prefetch=4,
            grid=(nblk,),
            in_specs=[pl.BlockSpec(memory_space=pl.ANY),
                      wspec((1, D_MODEL, D_EXPERT)), wspec((1, D_MODEL, D_EXPERT)), wspec((1, D_EXPERT, D_MODEL))],
            out_specs=pl.BlockSpec(memory_space=pl.ANY),
            scratch_shapes=[pltpu.VMEM((2, MOE_ROWS, D_MODEL), BF16), pltpu.VMEM((2, MOE_ROWS, D_MODEL), BF16),
                            pltpu.SemaphoreType.DMA((2,)), pltpu.SemaphoreType.DMA((2,))]),
        out_shape=jax.ShapeDtypeStruct(xs.shape, xs.dtype),
        input_output_aliases={4: 0},
        compiler_params=_cp("arbitrary"),
        name="moe_experts",
    )(src, nreal, block_e, n_used, xs, w1, w3, w2)


def _combine_kernel(h_ref, rs_ref, rg_ref, ys_ref, g_ref, b_ref, o_ref, *, tm, slots):
    rs = rs_ref[...]
    rg = rg_ref[...]
    sl = lax.broadcasted_iota(I32, (tm, slots), 1)
    weight = (jnp.where(sl == rs[:, 0:1], rg[:, 0:1], 0.0)
              + jnp.where(sl == rs[:, 1:2], rg[:, 1:2], 0.0)).astype(BF16)
    y = _dot(weight, ys_ref[...])
    o_ref[...] = _layernorm(ALPHA * h_ref[...] + y, g_ref[...], b_ref[...])


def _combine(h2, rs, rg, ys, ln_g, ln_b):
    t = h2.shape[0]
    tm = min(512, t)
    slots = ys.shape[0] // (t // tm)
    row = lambda n: pl.BlockSpec((tm, n), lambda i: (i, 0))
    return pl.pallas_call(
        functools.partial(_combine_kernel, tm=tm, slots=slots),
        grid=(t // tm,),
        in_specs=[row(D_MODEL), row(LANES), row(LANES), pl.BlockSpec((slots, D_MODEL), lambda i: (i, 0)),
                  _full_spec(ln_g, 1), _full_spec(ln_b, 1)],
        out_specs=row(D_MODEL),
        out_shape=jax.ShapeDtypeStruct((t, D_MODEL), F32),
        compiler_params=_cp("parallel"),
        name="moe_combine",
    )(h2, rs, rg, ys, ln_g, ln_b)


def _moe(h1, xs, rs, rg, n16_rows, w1, w3, w2, ln_g, ln_b):
    t = h1.shape[0]
    tm = min(512, t)
    nt = t // tm
    slots = xs.shape[0] // nt
    n16 = n16_rows[::8, :N_EXPERTS]
    per_e = jnp.sum(n16, axis=0)
    blocks_e = (per_e + MOE_BLOCK_CHUNKS - 1) // MOE_BLOCK_CHUNKS
    blk_end = jnp.cumsum(blocks_e)
    blk_start = blk_end - blocks_e
    n_used = blk_end[-1:].astype(I32)
    max_chunks = (t * TOP_K) // MOE_CHUNK + nt * N_EXPERTS
    nblk = max_chunks // MOE_BLOCK_CHUNKS + N_EXPERTS
    blk = jnp.arange(nblk, dtype=I32)
    block_e = jnp.minimum(jnp.sum(blk_end[None, :] <= blk[:, None], axis=1), N_EXPERTS - 1).astype(I32)
    of_e = block_e[:, None] == jnp.arange(N_EXPERTS, dtype=I32)[None, :]
    pick_e = lambda tab: jnp.sum(jnp.where(of_e, tab[None, :], 0), axis=1)
    pick_col = lambda tab: jnp.sum(jnp.where(of_e[:, None, :], tab[None, :, :], 0), axis=2)
    run_end = pick_col(jnp.cumsum(n16, axis=0))
    tile_off = pick_col(jnp.cumsum(n16, axis=1) - n16)
    k = ((blk - pick_e(blk_start)) * MOE_BLOCK_CHUNKS)[:, None] + jnp.arange(MOE_BLOCK_CHUNKS, dtype=I32)[None, :]
    real = (k < pick_e(per_e)[:, None]) & (blk < n_used[0])[:, None]
    done = run_end[:, None, :] <= k[:, :, None]
    tile = jnp.minimum(jnp.sum(done, axis=2), nt - 1)
    run_first = jnp.max(jnp.where(done, run_end[:, None, :], 0), axis=2)
    of_t = tile[:, :, None] == jnp.arange(nt, dtype=I32)[None, None, :]
    src = tile * (slots // MOE_CHUNK) + jnp.sum(jnp.where(of_t, tile_off[:, None, :], 0), axis=2) + (k - run_first)
    src = jnp.where(real, src, src[:, :1])
    src = jnp.where((blk < n_used[0])[:, None], src, 0).astype(I32).reshape(-1)
    nreal = jnp.sum(real, axis=1).astype(I32)
    ys = _experts(xs, src, nreal, block_e, n_used, w1, w3, w2)
    return _combine(h1, rs, rg, ys, ln_g, ln_b)


def _swap_halves(x):
    lane = lax.broadcasted_iota(I32, x.shape, 1)
    first = (lane % MLA_ROPE) < (MLA_ROPE // 2)
    return jnp.where(first, pltpu.roll(x, LANES - MLA_ROPE // 2, 1), pltpu.roll(x, MLA_ROPE // 2, 1))


def _rope(x, cos, sin):
    parts = []
    for t in range(x.shape[1] // LANES):
        sl = slice(t * LANES, (t + 1) * LANES)
        parts.append(x[:, sl] * cos[:, sl] + _swap_halves(x[:, sl]) * sin[:, sl])
    return parts[0] if len(parts) == 1 else jnp.concatenate(parts, axis=1)


def _odd_in_kernel(h_ref, w_ref, qg_ref, wqn_ref, wqp_ref, wuk_ref, perm_ref, kvg_ref,
                   cq_ref, sq_ref, ck_ref, sk_ref, gg_ref, gb_ref, ws_ref, bs_ref,
                   q_ref, kc_ref, kt_ref, ckv_ref, kpe_ref, gated_ref, vn_ref, *, tl, cl):
    hb = h_ref[0].astype(BF16)
    z = _dot(hb, w_ref[...])
    o_ckv = MLA_Q_RANK
    o_u = o_ckv + MLA_KV_RANK
    o_v = o_u + GMLP_WIDTH
    o_k = o_v + GMLP_WIDTH
    cq = z[:, :MLA_Q_RANK]
    cqn = cq * lax.rsqrt(jnp.mean(cq * cq, axis=-1, keepdims=True) + LN_EPS) * qg_ref[...]
    cqb = cqn.astype(BF16)
    qn = _dot(cqb, wqn_ref[...])
    qp = _dot(cqb, wqp_ref[...])
    qp = _rope(qp, cq_ref[...], sq_ref[...])
    qpe = _dot((qp * MLA_SCALE).astype(BF16), perm_ref[...])
    for h in range(MLA_HEADS):
        sl = slice(h * LANES, (h + 1) * LANES)
        qa = _dot((qn[:, sl] * MLA_SCALE).astype(BF16), wuk_ref[h])
        q_ref[0, h, :, 0:LANES] = qa.astype(BF16)
        q_ref[0, h, :, LANES:] = qpe[:, sl].astype(BF16)

    ckv = z[:, o_ckv:o_u]
    ckvn = ckv * lax.rsqrt(jnp.mean(ckv * ckv, axis=-1, keepdims=True) + LN_EPS) * kvg_ref[...]
    kp = z[:, o_k:]
    kp = _rope(kp, ck_ref[...], sk_ref[...])
    ckv_ref[0] = ckvn
    kpe_ref[0] = kp[:, :MLA_ROPE]
    one = (lax.broadcasted_iota(I32, (1, LANES), 1) == MLA_ONE_LANE - LANES).astype(F32)
    kp1 = kp + one
    kc_ref[0, :, 0:LANES] = ckvn.astype(BF16)
    kc_ref[0, :, LANES:] = kp1.astype(BF16)
    kt_ref[0, 0:LANES, :] = ckvn.T.astype(BF16)
    kt_ref[0, LANES:, :] = kp1.T.astype(BF16)

    gu = _gelu(z[:, o_u:o_v])
    vn = _layernorm(_gelu(z[:, o_v:o_k]), gg_ref[...], gb_ref[...])
    vn_ref[0] = vn
    vnb = vn.astype(BF16)
    for n in range(tl // cl):
        rs = slice(n * cl, (n + 1) * cl)
        for g in range(GMLP_GROUPS):
            ls = slice(g * GMLP_CH, (g + 1) * GMLP_CH)
            sg = _dot(ws_ref[g], vnb[rs, ls]) + bs_ref[:, ls]
            gated_ref[0, rs, ls] = gu[rs, ls] * sg


def _odd_in(h3, w_in, q_g, w_qn, w_qp, w_uk, perm, kv_g, cos_q, sin_q, cos_k, sin_k,
            gm_g, gm_b, ws, bs):
    bsz, l, _ = h3.shape
    tl = min(l, 512)
    cl = min(l, GMLP_CHUNK)
    rowb = lambda n: pl.BlockSpec((1, tl, n), lambda b, i: (b, i, 0))
    tab = lambda n: pl.BlockSpec((tl, n), lambda b, i: (i, 0))
    consts = (w_in, q_g, w_qn, w_qp, w_uk, perm, kv_g)
    consts2 = (gm_g, gm_b, ws, bs)
    return pl.pallas_call(
        functools.partial(_odd_in_kernel, tl=tl, cl=cl),
        grid=(bsz, l // tl),
        in_specs=[rowb(D_MODEL)] + [_full_spec(a, 2) for a in consts]
                 + [tab(MLA_HEADS * MLA_ROPE), tab(MLA_HEADS * MLA_ROPE), tab(LANES), tab(LANES)]
                 + [_full_spec(a, 2) for a in consts2],
        out_specs=[pl.BlockSpec((1, MLA_HEADS, tl, MLA_QW), lambda b, i: (b, 0, i, 0)),
                   rowb(MLA_QW), pl.BlockSpec((1, MLA_QW, tl), lambda b, i: (b, 0, i)),
                   rowb(MLA_KV_RANK), rowb(MLA_ROPE), rowb(GMLP_WIDTH), rowb(GMLP_WIDTH)],
        out_shape=[jax.ShapeDtypeStruct((bsz, MLA_HEADS, l, MLA_QW), BF16),
                   jax.ShapeDtypeStruct((bsz, l, MLA_QW), BF16),
                   jax.ShapeDtypeStruct((bsz, MLA_QW, l), BF16),
                   jax.ShapeDtypeStruct((bsz, l, MLA_KV_RANK), F32),
                   jax.ShapeDtypeStruct((bsz, l, MLA_ROPE), F32),
                   jax.ShapeDtypeStruct((bsz, l, GMLP_WIDTH), F32),
                   jax.ShapeDtypeStruct((bsz, l, GMLP_WIDTH), F32)],
        compiler_params=_cp("parallel", "parallel"),
        name="odd_in",
    )(h3, *consts, cos_q, sin_q, cos_k, sin_k, *consts2)


def _attn_kernel(qi_ref, kj_ref, flag_ref, q_ref, k_ref, kt_ref, wuv_ref, o_ref,
                 m_scr, acc_scr, *, tq, tk, pos0, n_keys):
    p = pl.program_id(1)
    flag = flag_ref[p]

    @pl.when((flag & 1) != 0)
    def _():
        m_scr[...] = jnp.full_like(m_scr, -jnp.inf)
        acc_scr[...] = jnp.zeros_like(acc_scr)

    def step(hidden):
        kt = kt_ref[0]
        kk = k_ref[0]
        if hidden:
            qpos = pos0 + qi_ref[p] * tq + lax.broadcasted_iota(I32, (tq, 1), 0)
            kpos = kj_ref[p] * tk + lax.broadcasted_iota(I32, (1, tk), 1)
            visible = ((kpos // CHUNK) <= (qpos // CHUNK)) & (kpos < n_keys)
        for h in range(MLA_HEADS):
            s = _dot(q_ref[0, h], kt)
            if hidden:
                s = jnp.where(visible, s, -jnp.inf)
            m_old = m_scr[h]
            m_new = jnp.maximum(m_old, jnp.max(s, axis=-1, keepdims=True))
            alpha = jnp.exp(m_old - m_new)
            pr = jnp.exp(s - m_new).astype(BF16)
            acc_scr[h] = alpha * acc_scr[h] + _dot(pr, kk)
            m_scr[h] = m_new

    @pl.when((flag & 4) != 0)
    def _():
        step(True)

    @pl.when((flag & 4) == 0)
    def _():
        step(False)

    @pl.when((flag & 2) != 0)
    def _():
        head_of_lane = lax.broadcasted_iota(I32, (1, MLA_HEADS * MLA_V), 1) // MLA_V
        out = jnp.zeros((tq, MLA_HEADS * MLA_V), F32)
        for h in range(MLA_HEADS):
            acc = acc_scr[h]
            lat = (acc[:, :MLA_KV_RANK] / acc[:, MLA_ONE_LANE:MLA_ONE_LANE + 1]).astype(BF16)
            out = out + jnp.where(head_of_lane == h, _dot(lat, wuv_ref[...]), 0.0)
        o_ref[0] = out


def _attn_pairs(l, tq, tk, pos0, n_keys):
    qi, kj, flag = [], [], []
    for i in range(l // tq):
        q_first = pos0 + i * tq
        q_last = q_first + tq - 1
        vis = min(CHUNK * (q_last // CHUNK) + CHUNK - 1, n_keys - 1)
        nj = vis // tk + 1
        for j in range(nj):
            hidden = ((j + 1) * tk - 1) // CHUNK > q_first // CHUNK or (j + 1) * tk > n_keys
            qi.append(i); kj.append(j)
            flag.append(int(j == 0) + 2 * int(j == nj - 1) + 4 * int(hidden))
    return [jnp.asarray(np.array(a, np.int32)) for a in (qi, kj, flag)]


def _attention(q4, kc3, kt3, w_uv, pos0, n_keys, tq, tk):
    bsz, _, l, _ = q4.shape
    pairs = _attn_pairs(l, tq, tk, pos0, n_keys)
    npairs = int(pairs[0].shape[0])
    return pl.pallas_call(
        functools.partial(_attn_kernel, tq=tq, tk=tk, pos0=pos0, n_keys=n_keys),
        grid_spec=pltpu.PrefetchScalarGridSpec(
            num_scalar_prefetch=3,
            grid=(bsz, npairs),
            in_specs=[pl.BlockSpec((1, MLA_HEADS, tq, MLA_QW), lambda b, p, qi, kj, f: (b, 0, qi[p], 0)),
                      pl.BlockSpec((1, tk, MLA_QW), lambda b, p, qi, kj, f: (b, kj[p], 0)),
                      pl.BlockSpec((1, MLA_QW, tk), lambda b, p, qi, kj, f: (b, 0, kj[p])),
                      pl.BlockSpec(w_uv.shape, lambda b, p, qi, kj, f: (0, 0))],
            out_specs=pl.BlockSpec((1, tq, MLA_HEADS * MLA_V), lambda b, p, qi, kj, f: (b, qi[p], 0)),
            scratch_shapes=[pltpu.VMEM((MLA_HEADS, tq, 1), F32),
                            pltpu.VMEM((MLA_HEADS, tq, MLA_QW), F32)]),
        out_shape=jax.ShapeDtypeStruct((bsz, l, MLA_HEADS * MLA_V), F32),
        compiler_params=_cp("parallel", "arbitrary"),
        name="mla_attention",
    )(*pairs, q4, kc3, kt3, w_uv)


def _rope_tables(pos0, l, width):
    half = MLA_ROPE // 2
    inv = ROPE_THETA ** (-jnp.arange(half, dtype=F32) * 2.0 / MLA_ROPE)
    ang = (pos0 + jnp.arange(l)).astype(F32)[:, None] * inv[None, :]
    cos, sin = jnp.cos(ang), jnp.sin(ang)
    cos32 = jnp.concatenate([cos, cos], axis=1)
    sin32 = jnp.concatenate([-sin, sin], axis=1)
    reps = width // MLA_ROPE
    return jnp.tile(cos32, (1, reps)), jnp.tile(sin32, (1, reps))


def _prep_even(w_in, w_gate_up, b_gate, gla_norm_g, pool_w, pool_scale, w_out):
    o_r = 2 * GLA_QK + GLA_V
    o_g = o_r + GLA_V
    o_p = o_g + GLA_GATE_RANK
    w_main = jnp.concatenate([w_in[:, :o_g], w_in[:, o_p:]], axis=1).astype(BF16)
    w_g = jnp.pad(w_in[:, o_g:o_p], ((0, 0), (0, LANES - GLA_GATE_RANK))).astype(BF16)
    w_gu = jnp.pad(w_gate_up, ((0, LANES - GLA_GATE_RANK), (0, 0))).astype(BF16)
    return dict(w_main=w_main, w_g=w_g, w_gu=w_gu, b_g=b_gate.reshape(1, -1),
                gnorm=gla_norm_g.reshape(1, -1), pool_w=pool_w.astype(BF16),
                pool_scale=pool_scale.reshape(1, -1), w_out=w_out.astype(BF16))


def _prep_odd(w_in, q_norm_g, w_uq, kv_norm_g, w_uk, w_uv, gm_g, gm_b, gm_ws, gm_bs, w_out):
    o_ckv = MLA_Q_RANK
    o_kpe = o_ckv + MLA_KV_RANK
    o_u = o_kpe + MLA_ROPE
    w_in2 = jnp.concatenate([w_in[:, :o_kpe], w_in[:, o_u:], w_in[:, o_kpe:o_u],
                             jnp.zeros((D_MODEL, LANES - MLA_ROPE), F32)], axis=1).astype(BF16)
    uq = w_uq.reshape(MLA_Q_RANK, MLA_HEADS, MLA_NOPE + MLA_ROPE)
    w_qn = jnp.pad(uq[:, :, :MLA_NOPE], ((0, 0), (0, 0), (0, LANES - MLA_NOPE)))
    w_qn = w_qn.reshape(MLA_Q_RANK, MLA_HEADS * LANES).astype(BF16)
    w_qp = uq[:, :, MLA_NOPE:].reshape(MLA_Q_RANK, MLA_HEADS * MLA_ROPE).astype(BF16)
    uk = w_uk.reshape(MLA_KV_RANK, MLA_HEADS, MLA_NOPE).transpose(1, 2, 0)
    w_ukp = jnp.pad(uk, ((0, 0), (0, LANES - MLA_NOPE), (0, 0))).astype(BF16)
    src = np.arange(MLA_HEADS * MLA_ROPE)
    perm = np.zeros((MLA_HEADS * MLA_ROPE, MLA_HEADS * LANES), np.float32)
    perm[src, (src // MLA_ROPE) * LANES + src % MLA_ROPE] = 1.0
    return dict(w_in=w_in2, q_g=q_norm_g.reshape(1, -1), w_qn=w_qn, w_qp=w_qp, w_uk=w_ukp,
                perm=jnp.asarray(perm, BF16), kv_g=kv_norm_g.reshape(1, -1), w_uv=w_uv.astype(BF16),
                gm_g=gm_g.reshape(1, -1), gm_b=gm_b.reshape(1, -1), gm_ws=gm_ws, gm_bs=gm_bs,
                w_out=w_out.astype(BF16))


def _prep_route(wg, bg, we, be):
    w_r = jnp.pad(jnp.concatenate([wg, we], axis=1), ((0, 0), (0, LANES - N_GROUPS - N_EXPERTS)))
    b_r = jnp.pad(jnp.concatenate([bg, be]), (0, LANES - N_GROUPS - N_EXPERTS)).reshape(1, LANES)
    return w_r, b_r


def _even_mixer(h3, st0, hist, pos0, pw):
    bsz, l, _ = h3.shape
    q, k, la, v, r, xp = _even_in(h3.reshape(bsz * l, D_MODEL), pw['w_main'], pw['w_g'], pw['w_gu'], pw['b_g'])
    to3 = lambda a: a.reshape(bsz, l, a.shape[-1])
    o, st = _gla(to3(q), to3(k), to3(la), to3(v), to3(r), st0, pw['gnorm'])
    xp3 = to3(xp)
    hist16 = jnp.pad(hist, ((0, 0), (POOL_HALO - POOL_HIST, 0), (0, 0)))
    pooled = _pool(xp3, hist16, pw['pool_w'], pw['pool_scale'], pos0)
    hist_new = jnp.concatenate([hist, xp3], axis=1)[:, -POOL_HIST:]
    return o, pooled, st, hist_new


def _odd_mixer(h3, ckv_past, kpe_past, pw):
    bsz, l, _ = h3.shape
    n_past = ckv_past.shape[1]
    cos_q, sin_q = _rope_tables(n_past, l, MLA_HEADS * MLA_ROPE)
    cos_k, sin_k = _rope_tables(n_past, l, MLA_ROPE)
    padk = ((0, 0), (0, LANES - MLA_ROPE))
    cos_k, sin_k = jnp.pad(cos_k, padk), jnp.pad(sin_k, padk)
    cl = min(l, GMLP_CHUNK)
    ws = jnp.tril(pw['gm_ws'][:, :cl, :cl]).astype(BF16)
    bs = jnp.repeat(pw['gm_bs'][:, :cl].T, GMLP_CH, axis=1)
    q4, kc, kt, ckv, kpe, gated, vn = _odd_in(h3, pw['w_in'], pw['q_g'], pw['w_qn'], pw['w_qp'], pw['w_uk'],
                                              pw['perm'], pw['kv_g'], cos_q, sin_q, cos_k, sin_k,
                                              pw['gm_g'], pw['gm_b'], ws, bs)
    n_keys = n_past + l
    tq, tk = min(l, 256), 512
    if n_past:
        past = jnp.concatenate([ckv_past, kpe_past, jnp.ones((bsz, n_past, 1), F32),
                                jnp.zeros((bsz, n_past, MLA_QW - MLA_ONE_LANE - 1), F32)], axis=2).astype(BF16)
        kc = jnp.concatenate([past, kc], axis=1)
        kt = jnp.concatenate([past.transpose(0, 2, 1), kt], axis=2)
    kc = jnp.pad(kc, ((0, 0), (0, -n_keys % tk), (0, 0)))
    kt = jnp.pad(kt, ((0, 0), (0, 0), (0, -n_keys % tk)))
    attn = _attention(q4, kc, kt, pw['w_uv'], n_past, n_keys, tq, tk)
    return attn, gated, ckv, kpe, vn


def _finish_layer(a3, b3, h3, w_out, lw):
    bsz, l, _ = h3.shape
    t = bsz * l
    h1, xs, rs, rg, n16 = _out_route(a3.reshape(t, -1), b3.reshape(t, -1), h3.reshape(t, D_MODEL), w_out,
                                     lw['ln_mix_g'], lw['ln_mix_b'], lw['w_r'], lw['b_r'])
    h2 = _moe(h1, xs, rs, rg, n16, lw['w1'], lw['w3'], lw['w2'], lw['ln_ffn_g'], lw['ln_ffn_b'])
    return h2.reshape(bsz, l, D_MODEL)


def kernel(x_prompt, x_sample, state_gla, state_pool, cache_mla_ckv, cache_mla_kpe, w_in_even, w_gate_up, b_gate, gla_norm_g, pool_w, pool_scale, w_out_even, w_in_odd, mla_q_norm_g, mla_w_uq, mla_kv_norm_g, mla_w_uk, mla_w_uv, gmlp_norm_g, gmlp_norm_b, gmlp_ws, gmlp_bs, w_out_odd, ln_mix_g, ln_mix_b, router_group_w, router_group_b, router_expert_w, router_expert_b, expert_w1, expert_w3, expert_w2, ln_ffn_g, ln_ffn_b):
    hp, hs = x_prompt, x_sample
    bp = hp.shape[0]
    past_len = cache_mla_ckv.shape[2]
    gla_p, gla_s, pool_p, pool_s = [], [], [], []
    ckv_p, ckv_s, kpe_p, kpe_s, gv_s = [], [], [], [], []

    def state_to_t(s):
        return s.transpose(0, 3, 1, 2).reshape(s.shape[0], GLA_DV, GLA_QK)

    def state_from_t(st):
        return st.reshape(st.shape[0], GLA_DV, GLA_HEADS, GLA_DK).transpose(0, 2, 3, 1)

    for layer in range(DEPTH):
        i = layer // 2
        w_r, b_r = _prep_route(router_group_w[layer], router_group_b[layer],
                               router_expert_w[layer], router_expert_b[layer])
        lw = dict(ln_mix_g=ln_mix_g[layer].reshape(1, -1), ln_mix_b=ln_mix_b[layer].reshape(1, -1),
                  ln_ffn_g=ln_ffn_g[layer].reshape(1, -1), ln_ffn_b=ln_ffn_b[layer].reshape(1, -1),
                  w_r=w_r, b_r=b_r, w1=expert_w1[layer].astype(BF16), w3=expert_w3[layer].astype(BF16),
                  w2=expert_w2[layer].astype(BF16))
        if layer % 2 == 0:
            pw = _prep_even(w_in_even[i], w_gate_up[i], b_gate[i], gla_norm_g[i], pool_w[i], pool_scale[i],
                            w_out_even[i])
            st0 = jnp.zeros((bp, GLA_DV, GLA_QK), F32)
            hist0 = jnp.zeros((bp, POOL_HIST, POOL_WIDTH), F32)
            op, pp, stp, histp = _even_mixer(hp, st0, hist0, 0, pw)
            os_, ps, sts, hists = _even_mixer(hs, state_to_t(state_gla[i]), state_pool[i], past_len, pw)
            gla_p.append(state_from_t(stp)); gla_s.append(state_from_t(sts))
            pool_p.append(histp); pool_s.append(hists)
            ap, bpj, as_, bsj = op, pp, os_, ps
        else:
            pw = _prep_odd(w_in_odd[i], mla_q_norm_g[i], mla_w_uq[i], mla_kv_norm_g[i], mla_w_uk[i], mla_w_uv[i],
                           gmlp_norm_g[i], gmlp_norm_b[i], gmlp_ws[i], gmlp_bs[i], w_out_odd[i])
            no_ckv = jnp.zeros((bp, 0, MLA_KV_RANK), F32)
            no_kpe = jnp.zeros((bp, 0, MLA_ROPE), F32)
            ap, bpj, cp, kp, _ = _odd_mixer(hp, no_ckv, no_kpe, pw)
            as_, bsj, cs, ks, vs = _odd_mixer(hs, cache_mla_ckv[i], cache_mla_kpe[i], pw)
            ckv_p.append(cp); ckv_s.append(cs); kpe_p.append(kp); kpe_s.append(ks); gv_s.append(vs)
        hp = _finish_layer(ap, bpj, hp, pw['w_out'], lw)
        hs = _finish_layer(as_, bsj, hs, pw['w_out'], lw)
    return (hp, hs, jnp.stack(gla_p), jnp.stack(gla_s), jnp.stack(pool_p), jnp.stack(pool_s),
            jnp.stack(ckv_p), jnp.stack(ckv_s), jnp.stack(kpe_p), jnp.stack(kpe_s), jnp.stack(gv_s))
```

```python
import functools

import numpy as np
import jax
import jax.numpy as jnp
from jax import lax
from jax.experimental import pallas as pl
from jax.experimental.pallas import tpu as pltpu

F32 = jnp.float32
BF16 = jnp.bfloat16
I32 = jnp.int32

D_MODEL = 1024
DEPTH = 2
CHUNK = 64
ALPHA = (2 * DEPTH) ** 0.25
LN_EPS = 1e-5

GLA_HEADS = 4
GLA_DV = 128
GLA_DK = 64
GLA_QK = GLA_HEADS * GLA_DK
GLA_V = GLA_HEADS * GLA_DV
GLA_GATE_RANK = 16
GLA_GATE_TAU = 16.0
GLA_SUB = 8

POOL_WIDTH = 512
POOL_CH = 128
POOL_WINDOWS = (2, 4, 8, 16)
POOL_HIST = 15
POOL_HALO = 16

MLA_HEADS = 8
MLA_NOPE = 64
MLA_ROPE = 32
MLA_V = 64
MLA_Q_RANK = 256
MLA_KV_RANK = 128
ROPE_THETA = 10000.0
MLA_SCALE = (MLA_NOPE + MLA_ROPE) ** -0.5
MLA_QW = 256
MLA_ONE_LANE = MLA_KV_RANK + MLA_ROPE
GMLP_WIDTH = 512
GMLP_CH = 128
GMLP_GROUPS = 4
GMLP_CHUNK = 128

N_GROUPS = 4
EXPERTS_PER_GROUP = 8
N_EXPERTS = 32
TOP_K = 2
D_EXPERT = 256
MOE_ROWS = 512
MOE_CHUNK = 16
MOE_BLOCK_CHUNKS = MOE_ROWS // MOE_CHUNK

LANES = 128
VMEM_LIMIT = 48 * 1024 * 1024


def _cp(*sem):
    return pltpu.CompilerParams(dimension_semantics=sem, vmem_limit_bytes=VMEM_LIMIT)


def _dot(a, b):
    return jnp.dot(a, b, preferred_element_type=F32)


def _dot_nt(a, b):
    return lax.dot_general(a, b, (((1,), (1,)), ((), ())), preferred_element_type=F32)


def _dot_tn(a, b):
    return lax.dot_general(a, b, (((0,), (0,)), ((), ())), preferred_element_type=F32)


def _split3(x):
    hi = x.astype(BF16)
    r1 = x - hi.astype(F32)
    mid = r1.astype(BF16)
    lo = (r1 - mid.astype(F32)).astype(BF16)
    return hi, mid, lo


def _layernorm(x, g, b):
    mu = jnp.mean(x, axis=-1, keepdims=True)
    xc = x - mu
    var = jnp.mean(xc * xc, axis=-1, keepdims=True)
    return xc * lax.rsqrt(var + LN_EPS) * g + b


def _gelu(x):
    return 0.5 * x * (1.0 + jnp.tanh(0.7978845608028654 * (x + 0.044715 * (x * x * x))))


def _sigmoid(x):
    return 1.0 / (1.0 + jnp.exp(-x))


def _full_spec(a, nargs):
    nd = a.ndim
    if nargs == 1:
        return pl.BlockSpec(a.shape, lambda i: (0,) * nd)
    return pl.BlockSpec(a.shape, lambda i, j: (0,) * nd)


def _even_in_kernel(x_ref, w_ref, wg_ref, wgu_ref, bg_ref,
                    q_ref, k_ref, la_ref, v_ref, r_ref, xp_ref):
    xb = x_ref[...].astype(BF16)
    z = _dot(xb, w_ref[...])
    q_ref[...] = z[:, 0:GLA_QK] * (GLA_DK ** -0.5)
    k_ref[...] = z[:, GLA_QK:2 * GLA_QK]
    v_ref[...] = z[:, 2 * GLA_QK:2 * GLA_QK + GLA_V]
    r_ref[...] = z[:, 2 * GLA_QK + GLA_V:2 * GLA_QK + 2 * GLA_V]
    xp_ref[...] = z[:, 2 * GLA_QK + 2 * GLA_V:]
    g = _dot(xb, wg_ref[...])
    pre = _dot(g.astype(BF16), wgu_ref[...]) + bg_ref[...]
    logsig = jnp.minimum(pre, 0.0) - jnp.log(1.0 + jnp.exp(-jnp.abs(pre)))
    la_ref[...] = logsig * (1.0 / GLA_GATE_TAU)


def _even_in(x2, w_main, w_g, w_gu, b_g):
    t = x2.shape[0]
    tm = min(512, t)
    row = lambda n: pl.BlockSpec((tm, n), lambda i: (i, 0))
    widths = (GLA_QK, GLA_QK, GLA_QK, GLA_V, GLA_V, POOL_WIDTH)
    return pl.pallas_call(
        _even_in_kernel,
        grid=(t // tm,),
        in_specs=[row(D_MODEL)] + [_full_spec(a, 1) for a in (w_main, w_g, w_gu, b_g)],
        out_specs=[row(n) for n in widths],
        out_shape=[jax.ShapeDtypeStruct((t, n), F32) for n in widths],
        compiler_params=_cp("parallel"),
        name="even_in",
    )(x2, w_main, w_g, w_gu, b_g)


def _gla_consts(c):
    n = c * GLA_SUB
    tri = np.tril(np.ones((c, c), np.float32))
    headsum = (np.arange(GLA_QK)[:, None] // GLA_DK == np.arange(GLA_V)[None, :] // GLA_DV).astype(np.float32)
    msel = (np.arange(n)[None, :] // GLA_SUB == np.arange(c)[:, None]).astype(np.float32)
    return [jnp.asarray(a, BF16) for a in (tri, headsum, msel)]


def _gla_chunk(q, k, la, v, st, c, tri, headsum, msel):
    lane = lax.broadcasted_iota(I32, (1, GLA_QK), 1)
    head_of_lane = lane // GLA_DK
    row = lax.broadcasted_iota(I32, (c, 1), 0)
    ii = lax.broadcasted_iota(I32, (c, c), 0)
    jj = lax.broadcasted_iota(I32, (c, c), 1)

    hi, mid, lo = _split3(la)
    b = _dot(tri, hi) + _dot(tri, mid) + _dot(tri, lo)

    head_masks = [head_of_lane == h for h in range(GLA_HEADS)]
    vb = v.astype(BF16)

    a_off = [jnp.zeros((c, c), F32) for _ in range(GLA_HEADS)]
    s = c // 2
    while s >= GLA_SUB:
        nblk = c // (2 * s)
        blk = row // (2 * s)
        right = ((row // s) % 2) == 1
        bref = jnp.zeros((c, GLA_QK), F32)
        for m in range(nblk):
            r0 = m * 2 * s + s - 1
            bref = jnp.where(blk == m, b[r0:r0 + 1, :], bref)
        qe = jnp.where(right, q * jnp.exp(jnp.minimum(b - bref, 0.0)), 0.0)
        ke = jnp.where(right, 0.0, k * jnp.exp(jnp.minimum(bref - b, 0.0))).astype(BF16)
        same = (ii // (2 * s)) == (jj // (2 * s))
        for h in range(GLA_HEADS):
            a = _dot_nt(jnp.where(head_masks[h], qe, 0.0).astype(BF16), ke)
            a_off[h] = a_off[h] + (a if nblk == 1 else jnp.where(same, a, 0.0))
        s //= 2

    nsb = c // GLA_SUB
    parts = []
    for i in range(nsb):
        sl = slice(i * GLA_SUB, (i + 1) * GLA_SUB)
        bi, qi, ki = b[sl], q[sl], k[sl]
        diff = bi[:, None, :] - bi[None, :, :]
        p = qi[:, None, :] * ki[None, :, :] * jnp.exp(jnp.minimum(diff, 0.0))
        parts.append(p.reshape(GLA_SUB * GLA_SUB, GLA_QK))
    pcat = jnp.concatenate(parts, axis=0).astype(BF16)
    n = nsb * GLA_SUB * GLA_SUB
    rsum = _dot(pcat, headsum)
    idx = lax.broadcasted_iota(I32, (n, 1), 0)
    causal = (idx % GLA_SUB) <= ((idx // GLA_SUB) % GLA_SUB)

    qb = q * jnp.exp(b)
    b_end = b[c - 1:c, :]
    kd = (k * jnp.exp(b_end - b)).astype(BF16)
    stb = st.astype(BF16)
    st_new = st * jnp.exp(b_end)

    outs = []
    for h in range(GLA_HEADS):
        vh = vb[:, h * GLA_DV:(h + 1) * GLA_DV]
        vf = v[:, h * GLA_DV:(h + 1) * GLA_DV]
        vt = jnp.concatenate(
            [jnp.broadcast_to(vf[i * GLA_SUB:(i + 1) * GLA_SUB][None], (GLA_SUB, GLA_SUB, GLA_DV))
             .reshape(GLA_SUB * GLA_SUB, GLA_DV) for i in range(nsb)], axis=0)
        xh = jnp.where(causal, rsum[:, h * GLA_DV:(h + 1) * GLA_DV], 0.0) * vt
        o = _dot(msel, xh.astype(BF16))
        o = o + _dot(a_off[h].astype(BF16), vh)
        o = o + _dot_nt(jnp.where(head_masks[h], qb, 0.0).astype(BF16), stb)
        outs.append(o)
        st_new = st_new + jnp.where(head_masks[h], _dot_tn(vh, kd), 0.0)
    return outs, st_new


def _gla_kernel(q_ref, k_ref, la_ref, v_ref, r_ref, st0_ref, g_ref, tri_ref, hs_ref, ms_ref,
                o_ref, st_ref, st_scr, *, c, nchunks):
    @pl.when(pl.program_id(1) == 0)
    def _():
        st_scr[...] = st0_ref[0]

    def body(ci, carry):
        r0 = pl.multiple_of(ci * c, c)
        rows = pl.ds(r0, c)
        outs, st_new = _gla_chunk(q_ref[0, rows, :], k_ref[0, rows, :], la_ref[0, rows, :],
                                  v_ref[0, rows, :], st_scr[...], c, tri_ref[...], hs_ref[...], ms_ref[...])
        st_scr[...] = st_new
        r = r_ref[0, rows, :]
        g = g_ref[...]
        for h in range(GLA_HEADS):
            o = outs[h]
            sl = slice(h * GLA_DV, (h + 1) * GLA_DV)
            on = o * lax.rsqrt(jnp.mean(o * o, axis=-1, keepdims=True) + LN_EPS) * g
            rh = r[:, sl]
            o_ref[0, rows, sl] = on * (rh * _sigmoid(rh))
        return carry

    lax.fori_loop(0, nchunks, body, 0, unroll=4 if nchunks % 4 == 0 else 1)
    st_ref[0] = st_scr[...]


def _gla(q3, k3, la3, v3, r3, st0, gnorm):
    bsz, l, _ = q3.shape
    c = min(l, CHUNK)
    tl = min(l, 512)
    blk = lambda n: pl.BlockSpec((1, tl, n), lambda b, i: (b, i, 0))
    st_spec = pl.BlockSpec((1, GLA_DV, GLA_QK), lambda b, i: (b, 0, 0))
    consts = _gla_consts(c)
    return pl.pallas_call(
        functools.partial(_gla_kernel, c=c, nchunks=tl // c),
        grid=(bsz, l // tl),
        in_specs=[blk(GLA_QK), blk(GLA_QK), blk(GLA_QK), blk(GLA_V), blk(GLA_V), st_spec,
                  _full_spec(gnorm, 2)] + [_full_spec(a, 2) for a in consts],
        out_specs=[blk(GLA_V), st_spec],
        out_shape=[jax.ShapeDtypeStruct((bsz, l, GLA_V), F32),
                   jax.ShapeDtypeStruct((bsz, GLA_DV, GLA_QK), F32)],
        scratch_shapes=[pltpu.VMEM((GLA_DV, GLA_QK), F32)],
        compiler_params=_cp("parallel", "arbitrary"),
        name="gla",
    )(q3, k3, la3, v3, r3, st0, gnorm, *consts)


def _pool_kernel(x_ref, halo_ref, hist_ref, w_ref, scale_ref, o_ref, *, tl, pos0):
    i = pl.program_id(1)
    x = x_ref[0]
    prev = jnp.where(i == 0, hist_ref[0], halo_ref[0])
    e = jnp.concatenate([prev, x], axis=0)
    t = i * tl + lax.broadcasted_iota(I32, (tl, 1), 0)
    pos = pos0 + t
    sums = []
    shift = 1
    for g, w in enumerate(POOL_WINDOWS):
        e = e[:, POOL_CH:] if g > 0 else e
        while shift < w:
            e = e[shift:] + e[:-shift]
            shift *= 2
        off = POOL_HALO - (w - 1)
        sums.append(e[off:off + tl, :POOL_CH])
    outs = []
    for g, w in enumerate(POOL_WINDOWS):
        cnt = jnp.minimum(pos + 1, w).astype(F32)
        mix = sums[g] / cnt - x[:, g * POOL_CH:(g + 1) * POOL_CH]
        outs.append(_dot(mix.astype(BF16), w_ref[g]))
    o_ref[0] = jnp.concatenate(outs, axis=1) * scale_ref[...]


def _pool(xp3, hist16, pool_w, pool_scale, pos0):
    bsz, l, _ = xp3.shape
    tl = min(l, 512)
    per = tl // POOL_HALO
    return pl.pallas_call(
        functools.partial(_pool_kernel, tl=tl, pos0=pos0),
        grid=(bsz, l // tl),
        in_specs=[pl.BlockSpec((1, tl, POOL_WIDTH), lambda b, i: (b, i, 0)),
                  pl.BlockSpec((1, POOL_HALO, POOL_WIDTH), lambda b, i: (b, jnp.maximum(i * per - 1, 0), 0)),
                  pl.BlockSpec((1, POOL_HALO, POOL_WIDTH), lambda b, i: (b, 0, 0)),
                  _full_spec(pool_w, 2), _full_spec(pool_scale, 2)],
        out_specs=pl.BlockSpec((1, tl, POOL_WIDTH), lambda b, i: (b, i, 0)),
        out_shape=jax.ShapeDtypeStruct((bsz, l, POOL_WIDTH), F32),
        compiler_params=_cp("parallel", "parallel"),
        name="pool",
    )(xp3, xp3, hist16, pool_w, pool_scale)


def _out_route_kernel(a_ref, b_ref, h_ref, w_ref, g_ref, bt_ref, wr_ref, br_ref,
                      h1_ref, xs_ref, rs_ref, rg_ref, n16_ref, *, tm, slots):
    half = w_ref.shape[0] // 2
    y = _dot(a_ref[...].astype(BF16), w_ref[0:half, :]) + _dot(b_ref[...].astype(BF16), w_ref[half:, :])
    x = _layernorm(ALPHA * h_ref[...] + y, g_ref[...], bt_ref[...])
    h1_ref[...] = x

    xh = x.astype(BF16)
    xl = (x - xh.astype(F32)).astype(BF16)
    hl = _dot(xh, wr_ref[...])
    logits = hl[:, :LANES] + hl[:, LANES:] + _dot(xl, wr_ref[:, 0:LANES]) + br_ref[...]

    lane = lax.broadcasted_iota(I32, (tm, LANES), 1)
    lanef = lane.astype(F32)
    neg = -jnp.inf
    big = jnp.float32(1 << 20)

    def first_lane(hit):
        return jnp.min(jnp.where(hit, lanef, big), axis=-1, keepdims=True).astype(I32)

    gl = jnp.where(lane < N_GROUPS, logits, neg)
    gmax = jnp.max(gl, axis=-1, keepdims=True)
    g_sel = first_lane(gl == gmax)
    g_prob = 1.0 / jnp.sum(jnp.exp(gl - gmax), axis=-1, keepdims=True)
    eidx = lane - N_GROUPS
    in_grp = (eidx >= 0) & (eidx < N_EXPERTS) & ((eidx // EXPERTS_PER_GROUP) == g_sel)
    el = jnp.where(in_grp, logits, neg)
    v1 = jnp.max(el, axis=-1, keepdims=True)
    i1 = first_lane(el == v1)
    el2 = jnp.where(lane == i1, neg, el)
    v2 = jnp.max(el2, axis=-1, keepdims=True)
    i2 = first_lane(el2 == v2)
    e21 = jnp.exp(v2 - v1)
    gate1 = g_prob / (1.0 + e21)
    gate2 = g_prob * e21 / (1.0 + e21)
    e1 = i1 - N_GROUPS
    e2 = i2 - N_GROUPS

    oh1 = lane == e1
    oh2 = lane == e2
    oh = oh1.astype(F32) + oh2.astype(F32)
    ti = lax.broadcasted_iota(I32, (tm, tm), 0)
    tj = lax.broadcasted_iota(I32, (tm, tm), 1)
    before = _dot((tj < ti).astype(BF16), oh.astype(BF16))
    cnt = jnp.sum(oh, axis=0, keepdims=True)
    n16 = jnp.floor((cnt + (MOE_CHUNK - 1)) * (1.0 / MOE_CHUNK))
    n16_8 = jnp.broadcast_to(n16, (8, LANES))
    ui = lax.broadcasted_iota(I32, (LANES, LANES), 0)
    uj = lax.broadcasted_iota(I32, (LANES, LANES), 1)
    run_start = _dot(n16_8.astype(BF16), (ui < uj).astype(BF16))[0:1]
    slot_of = MOE_CHUNK * run_start + before
    slot1 = jnp.sum(jnp.where(oh1, slot_of, 0.0), axis=-1, keepdims=True).astype(I32)
    slot2 = jnp.sum(jnp.where(oh2, slot_of, 0.0), axis=-1, keepdims=True).astype(I32)
    sl = lax.broadcasted_iota(I32, (tm, slots), 1)
    place = ((sl == slot1) | (sl == slot2)).astype(BF16)
    xs_ref[...] = _dot_tn(place, xh).astype(BF16)

    rs_ref[...] = jnp.where(lane == 0, slot1, jnp.where(lane == 1, slot2, 0))
    rg_ref[...] = jnp.where(lane == 0, gate1, jnp.where(lane == 1, gate2, 0.0))
    n16_ref[...] = n16_8.astype(I32)


def _moe_slots(tm):
    worst = tm * TOP_K + N_EXPERTS * (MOE_CHUNK - 1)
    return -(-worst // MOE_ROWS) * MOE_ROWS


def _out_route(a2, b2, h2, w_out, ln_g, ln_b, w_r, b_r):
    t = h2.shape[0]
    tm = min(512, t)
    nt = t // tm
    slots = _moe_slots(tm)
    row = lambda n: pl.BlockSpec((tm, n), lambda i: (i, 0))
    return pl.pallas_call(
        functools.partial(_out_route_kernel, tm=tm, slots=slots),
        grid=(nt,),
        in_specs=[row(a2.shape[1]), row(b2.shape[1]), row(D_MODEL)]
                 + [_full_spec(a, 1) for a in (w_out, ln_g, ln_b, w_r, b_r)],
        out_specs=[row(D_MODEL), pl.BlockSpec((slots, D_MODEL), lambda i: (i, 0)), row(LANES), row(LANES),
                   pl.BlockSpec((8, LANES), lambda i: (i, 0))],
        out_shape=[jax.ShapeDtypeStruct((t, D_MODEL), F32), jax.ShapeDtypeStruct((nt * slots, D_MODEL), BF16),
                   jax.ShapeDtypeStruct((t, LANES), I32), jax.ShapeDtypeStruct((t, LANES), F32),
                   jax.ShapeDtypeStruct((nt * 8, LANES), I32)],
        compiler_params=_cp("parallel"),
        name="out_route",
    )(a2, b2, h2, w_out, ln_g, ln_b, w_r, b_r)


def _chunk_rows(chunk):
    return pl.ds(pl.multiple_of(chunk * MOE_CHUNK, MOE_CHUNK), MOE_CHUNK)


def _expert_kernel(src_ref, nreal_ref, be_ref, nu_ref, xs_ref, w1_ref, w3_ref, w2_ref, ys_ref,
                   xbuf, ybuf, gsem, ssem):
    del be_ref
    b = pl.program_id(0)
    nu = nu_ref[0]

    def gather(blk, slot, j):
        return pltpu.make_async_copy(xs_ref.at[_chunk_rows(src_ref[blk * MOE_BLOCK_CHUNKS + j]), :],
                                     xbuf.at[slot, pl.ds(j * MOE_CHUNK, MOE_CHUNK), :], gsem.at[slot])

    def scatter(blk, slot, j):
        return pltpu.make_async_copy(ybuf.at[slot, _chunk_rows(j), :],
                                     ys_ref.at[_chunk_rows(src_ref[blk * MOE_BLOCK_CHUNKS + j]), :],
                                     ssem.at[slot])

    def start_gather(blk, slot):
        for j in range(MOE_BLOCK_CHUNKS):
            gather(blk, slot, j).start()

    def wait_gather(slot):
        pltpu.make_async_copy(xs_ref.at[pl.ds(0, MOE_ROWS), :], xbuf.at[slot], gsem.at[slot]).wait()

    def for_real_chunks(blk, fn):
        def body(j, carry):
            fn(j)
            return carry
        lax.fori_loop(0, nreal_ref[blk], body, 0)

    def wait_scatter(blk, slot):
        full = nreal_ref[blk] == MOE_BLOCK_CHUNKS

        @pl.when(full)
        def _():
            pltpu.make_async_copy(ybuf.at[slot], ys_ref.at[pl.ds(0, MOE_ROWS), :], ssem.at[slot]).wait()

        @pl.when(jnp.logical_not(full))
        def _():
            for_real_chunks(blk, lambda j: scatter(blk, slot, j).wait())

    @pl.when(b < nu)
    def _():
        slot = b % 2

        @pl.when(b == 0)
        def _():
            start_gather(b, slot)

        @pl.when(b + 1 < nu)
        def _():
            start_gather(b + 1, 1 - slot)

        wait_gather(slot)

        @pl.when(b >= 2)
        def _():
            wait_scatter(b - 2, slot)

        xb = xbuf[slot]
        a = _dot(xb, w1_ref[0])
        hid = a * _sigmoid(a) * _dot(xb, w3_ref[0])
        ybuf[slot] = _dot(hid.astype(BF16), w2_ref[0]).astype(BF16)
        for_real_chunks(b, lambda j: scatter(b, slot, j).start())

        @pl.when(b == nu - 1)
        def _():
            wait_scatter(b, slot)

            @pl.when(b >= 1)
            def _():
                wait_scatter(b - 1, 1 - slot)


def _experts(xs, src, nreal, block_e, n_used, w1, w3, w2):
    nblk = src.shape[0] // MOE_BLOCK_CHUNKS
    wspec = lambda shape: pl.BlockSpec(shape, lambda i, src, nr, be, nu: (be[i], 0, 0))
    return pl.pallas_call(
        _expert_kernel,
        grid_spec=pltpu.PrefetchScalarGridSpec(
            num_scalar_prefetch=4,
            grid=(nblk,),
            in_specs=[pl.BlockSpec(memory_space=pl.ANY),
                      wspec((1, D_MODEL, D_EXPERT)), wspec((1, D_MODEL, D_EXPERT)), wspec((1, D_EXPERT, D_MODEL))],
            out_specs=pl.BlockSpec(memory_space=pl.ANY),
            scratch_shapes=[pltpu.VMEM((2, MOE_ROWS, D_MODEL), BF16), pltpu.VMEM((2, MOE_ROWS, D_MODEL), BF16),
                            pltpu.SemaphoreType.DMA((2,)), pltpu.SemaphoreType.DMA((2,))]),
        out_shape=jax.ShapeDtypeStruct(xs.shape, xs.dtype),
        input_output_aliases={4: 0},
        compiler_params=_cp("arbitrary"),
        name="moe_experts",
    )(src, nreal, block_e, n_used, xs, w1, w3, w2)


def _combine_kernel(h_ref, rs_ref, rg_ref, ys_ref, g_ref, b_ref, o_ref, *, tm, slots):
    rs = rs_ref[...]
    rg = rg_ref[...]
    sl = lax.broadcasted_iota(I32, (tm, slots), 1)
    weight = (jnp.where(sl == rs[:, 0:1], rg[:, 0:1], 0.0)
              + jnp.where(sl == rs[:, 1:2], rg[:, 1:2], 0.0)).astype(BF16)
    y = _dot(weight, ys_ref[...])
    o_ref[...] = _layernorm(ALPHA * h_ref[...] + y, g_ref[...], b_ref[...])


def _combine(h2, rs, rg, ys, ln_g, ln_b):
    t = h2.shape[0]
    tm = min(512, t)
    slots = ys.shape[0] // (t // tm)
    row = lambda n: pl.BlockSpec((tm, n), lambda i: (i, 0))
    return pl.pallas_call(
        functools.partial(_combine_kernel, tm=tm, slots=slots),
        grid=(t // tm,),
        in_specs=[row(D_MODEL), row(LANES), row(LANES), pl.BlockSpec((slots, D_MODEL), lambda i: (i, 0)),
                  _full_spec(ln_g, 1), _full_spec(ln_b, 1)],
        out_specs=row(D_MODEL),
        out_shape=jax.ShapeDtypeStruct((t, D_MODEL), F32),
        compiler_params=_cp("parallel"),
        name="moe_combine",
    )(h2, rs, rg, ys, ln_g, ln_b)


def _moe(h1, xs, rs, rg, n16_rows, w1, w3, w2, ln_g, ln_b):
    t = h1.shape[0]
    tm = min(512, t)
    nt = t // tm
    slots = xs.shape[0] // nt
    n16 = n16_rows[::8, :N_EXPERTS]
    per_e = jnp.sum(n16, axis=0)
    blocks_e = (per_e + MOE_BLOCK_CHUNKS - 1) // MOE_BLOCK_CHUNKS
    blk_end = jnp.cumsum(blocks_e)
    blk_start = blk_end - blocks_e
    n_used = blk_end[-1:].astype(I32)
    max_chunks = (t * TOP_K) // MOE_CHUNK + nt * N_EXPERTS
    nblk = max_chunks // MOE_BLOCK_CHUNKS + N_EXPERTS
    blk = jnp.arange(nblk, dtype=I32)
    block_e = jnp.minimum(jnp.sum(blk_end[None, :] <= blk[:, None], axis=1), N_EXPERTS - 1).astype(I32)
    of_e = block_e[:, None] == jnp.arange(N_EXPERTS, dtype=I32)[None, :]
    pick_e = lambda tab: jnp.sum(jnp.where(of_e, tab[None, :], 0), axis=1)
    pick_col = lambda tab: jnp.sum(jnp.where(of_e[:, None, :], tab[None, :, :], 0), axis=2)
    run_end = pick_col(jnp.cumsum(n16, axis=0))
    tile_off = pick_col(jnp.cumsum(n16, axis=1) - n16)
    k = ((blk - pick_e(blk_start)) * MOE_BLOCK_CHUNKS)[:, None] + jnp.arange(MOE_BLOCK_CHUNKS, dtype=I32)[None, :]
    real = (k < pick_e(per_e)[:, None]) & (blk < n_used[0])[:, None]
    done = run_end[:, None, :] <= k[:, :, None]
    tile = jnp.minimum(jnp.sum(done, axis=2), nt - 1)
    run_first = jnp.max(jnp.where(done, run_end[:, None, :], 0), axis=2)
    of_t = tile[:, :, None] == jnp.arange(nt, dtype=I32)[None, None, :]
    src = tile * (slots // MOE_CHUNK) + jnp.sum(jnp.where(of_t, tile_off[:, None, :], 0), axis=2) + (k - run_first)
    src = jnp.where(real, src, src[:, :1])
    src = jnp.where((blk < n_used[0])[:, None], src, 0).astype(I32).reshape(-1)
    nreal = jnp.sum(real, axis=1).astype(I32)
    ys = _experts(xs, src, nreal, block_e, n_used, w1, w3, w2)
    return _combine(h1, rs, rg, ys, ln_g, ln_b)


def _swap_halves(x):
    lane = lax.broadcasted_iota(I32, x.shape, 1)
    first = (lane % MLA_ROPE) < (MLA_ROPE // 2)
    return jnp.where(first, pltpu.roll(x, LANES - MLA_ROPE // 2, 1), pltpu.roll(x, MLA_ROPE // 2, 1))


def _rope(x, cos, sin):
    parts = []
    for t in range(x.shape[1] // LANES):
        sl = slice(t * LANES, (t + 1) * LANES)
        parts.append(x[:, sl] * cos[:, sl] + _swap_halves(x[:, sl]) * sin[:, sl])
    return parts[0] if len(parts) == 1 else jnp.concatenate(parts, axis=1)


def _odd_in_kernel(h_ref, w_ref, qg_ref, wqn_ref, wqp_ref, wuk_ref, perm_ref, kvg_ref,
                   cq_ref, sq_ref, ck_ref, sk_ref, gg_ref, gb_ref, ws_ref, bs_ref,
                   q_ref, kc_ref, kt_ref, ckv_ref, kpe_ref, gated_ref, vn_ref, *, tl, cl):
    hb = h_ref[0].astype(BF16)
    z = _dot(hb, w_ref[...])
    o_ckv = MLA_Q_RANK
    o_u = o_ckv + MLA_KV_RANK
    o_v = o_u + GMLP_WIDTH
    o_k = o_v + GMLP_WIDTH
    cq = z[:, :MLA_Q_RANK]
    cqn = cq * lax.rsqrt(jnp.mean(cq * cq, axis=-1, keepdims=True) + LN_EPS) * qg_ref[...]
    cqb = cqn.astype(BF16)
    qn = _dot(cqb, wqn_ref[...])
    qp = _dot(cqb, wqp_ref[...])
    qp = _rope(qp, cq_ref[...], sq_ref[...])
    qpe = _dot((qp * MLA_SCALE).astype(BF16), perm_ref[...])
    for h in range(MLA_HEADS):
        sl = slice(h * LANES, (h + 1) * LANES)
        qa = _dot((qn[:, sl] * MLA_SCALE).astype(BF16), wuk_ref[h])
        q_ref[0, h, :, 0:LANES] = qa.astype(BF16)
        q_ref[0, h, :, LANES:] = qpe[:, sl].astype(BF16)

    ckv = z[:, o_ckv:o_u]
    ckvn = ckv * lax.rsqrt(jnp.mean(ckv * ckv, axis=-1, keepdims=True) + LN_EPS) * kvg_ref[...]
    kp = z[:, o_k:]
    kp = _rope(kp, ck_ref[...], sk_ref[...])
    ckv_ref[0] = ckvn
    kpe_ref[0] = kp[:, :MLA_ROPE]
    one = (lax.broadcasted_iota(I32, (1, LANES), 1) == MLA_ONE_LANE - LANES).astype(F32)
    kp1 = kp + one
    kc_ref[0, :, 0:LANES] = ckvn.astype(BF16)
    kc_ref[0, :, LANES:] = kp1.astype(BF16)
    kt_ref[0, 0:LANES, :] = ckvn.T.astype(BF16)
    kt_ref[0, LANES:, :] = kp1.T.astype(BF16)

    gu = _gelu(z[:, o_u:o_v])
    vn = _layernorm(_gelu(z[:, o_v:o_k]), gg_ref[...], gb_ref[...])
    vn_ref[0] = vn
    vnb = vn.astype(BF16)
    for n in range(tl // cl):
        rs = slice(n * cl, (n + 1) * cl)
        for g in range(GMLP_GROUPS):
            ls = slice(g * GMLP_CH, (g + 1) * GMLP_CH)
            sg = _dot(ws_ref[g], vnb[rs, ls]) + bs_ref[:, ls]
            gated_ref[0, rs, ls] = gu[rs, ls] * sg


def _odd_in(h3, w_in, q_g, w_qn, w_qp, w_uk, perm, kv_g, cos_q, sin_q, cos_k, sin_k,
            gm_g, gm_b, ws, bs):
    bsz, l, _ = h3.shape
    tl = min(l, 512)
    cl = min(l, GMLP_CHUNK)
    rowb = lambda n: pl.BlockSpec((1, tl, n), lambda b, i: (b, i, 0))
    tab = lambda n: pl.BlockSpec((tl, n), lambda b, i: (i, 0))
    consts = (w_in, q_g, w_qn, w_qp, w_uk, perm, kv_g)
    consts2 = (gm_g, gm_b, ws, bs)
    return pl.pallas_call(
        functools.partial(_odd_in_kernel, tl=tl, cl=cl),
        grid=(bsz, l // tl),
        in_specs=[rowb(D_MODEL)] + [_full_spec(a, 2) for a in consts]
                 + [tab(MLA_HEADS * MLA_ROPE), tab(MLA_HEADS * MLA_ROPE), tab(LANES), tab(LANES)]
                 + [_full_spec(a, 2) for a in consts2],
        out_specs=[pl.BlockSpec((1, MLA_HEADS, tl, MLA_QW), lambda b, i: (b, 0, i, 0)),
                   rowb(MLA_QW), pl.BlockSpec((1, MLA_QW, tl), lambda b, i: (b, 0, i)),
                   rowb(MLA_KV_RANK), rowb(MLA_ROPE), rowb(GMLP_WIDTH), rowb(GMLP_WIDTH)],
        out_shape=[jax.ShapeDtypeStruct((bsz, MLA_HEADS, l, MLA_QW), BF16),
                   jax.ShapeDtypeStruct((bsz, l, MLA_QW), BF16),
                   jax.ShapeDtypeStruct((bsz, MLA_QW, l), BF16),
                   jax.ShapeDtypeStruct((bsz, l, MLA_KV_RANK), F32),
                   jax.ShapeDtypeStruct((bsz, l, MLA_ROPE), F32),
                   jax.ShapeDtypeStruct((bsz, l, GMLP_WIDTH), F32),
                   jax.ShapeDtypeStruct((bsz, l, GMLP_WIDTH), F32)],
        compiler_params=_cp("parallel", "parallel"),
        name="odd_in",
    )(h3, *consts, cos_q, sin_q, cos_k, sin_k, *consts2)


def _attn_kernel(qi_ref, kj_ref, flag_ref, q_ref, k_ref, kt_ref, wuv_ref, o_ref,
                 m_scr, l_scr, acc_scr, *, tq, tk, pos0, n_keys):
    p = pl.program_id(1)
    flag = flag_ref[p]

    @pl.when((flag & 1) != 0)
    def _():
        m_scr[...] = jnp.full_like(m_scr, -jnp.inf)
        l_scr[...] = jnp.zeros_like(l_scr)
        acc_scr[...] = jnp.zeros_like(acc_scr)

    def step(hidden):
        kt = kt_ref[0]
        kk = k_ref[0]
        if hidden:
            qpos = pos0 + qi_ref[p] * tq + lax.broadcasted_iota(I32, (1, tq), 1)
            kpos = kj_ref[p] * tk + lax.broadcasted_iota(I32, (tk, 1), 0)
            visible = ((kpos // CHUNK) <= (qpos // CHUNK)) & (kpos < n_keys)
        for h in range(MLA_HEADS):
            s = _dot_nt(kk, q_ref[0, h])
            if hidden:
                s = jnp.where(visible, s, -jnp.inf)
            m_old = m_scr[h]
            m_new = jnp.maximum(m_old, jnp.max(s, axis=0, keepdims=True))
            alpha = jnp.exp(m_old - m_new)
            pr = jnp.exp(s - m_new).astype(BF16)
            acc_scr[h] = alpha * acc_scr[h] + _dot(kt[:MLA_KV_RANK], pr)
            l_scr[h] = alpha * l_scr[h] + _dot(kt[MLA_ONE_LANE:MLA_ONE_LANE + 16], pr)
            m_scr[h] = m_new

    @pl.when((flag & 4) != 0)
    def _():
        step(True)

    @pl.when((flag & 4) == 0)
    def _():
        step(False)

    @pl.when((flag & 2) != 0)
    def _():
        head_of_lane = lax.broadcasted_iota(I32, (1, MLA_HEADS * MLA_V), 1) // MLA_V
        out = jnp.zeros((tq, MLA_HEADS * MLA_V), F32)
        for h in range(MLA_HEADS):
            lat = (acc_scr[h] / l_scr[h, 0:1, :]).astype(BF16)
            out = out + jnp.where(head_of_lane == h, _dot_tn(lat, wuv_ref[...]), 0.0)
        o_ref[0] = out


def _attn_pairs(l, tq, tk, pos0, n_keys):
    qi, kj, flag = [], [], []
    for i in range(l // tq):
        q_first = pos0 + i * tq
        q_last = q_first + tq - 1
        vis = min(CHUNK * (q_last // CHUNK) + CHUNK - 1, n_keys - 1)
        nj = vis // tk + 1
        for j in range(nj):
            hidden = ((j + 1) * tk - 1) // CHUNK > q_first // CHUNK or (j + 1) * tk > n_keys
            qi.append(i); kj.append(j)
            flag.append(int(j == 0) + 2 * int(j == nj - 1) + 4 * int(hidden))
    return [jnp.asarray(np.array(a, np.int32)) for a in (qi, kj, flag)]


def _attention(q4, kc3, kt3, w_uv, pos0, n_keys, tq, tk):
    bsz, _, l, _ = q4.shape
    pairs = _attn_pairs(l, tq, tk, pos0, n_keys)
    npairs = int(pairs[0].shape[0])
    return pl.pallas_call(
        functools.partial(_attn_kernel, tq=tq, tk=tk, pos0=pos0, n_keys=n_keys),
        grid_spec=pltpu.PrefetchScalarGridSpec(
            num_scalar_prefetch=3,
            grid=(bsz, npairs),
            in_specs=[pl.BlockSpec((1, MLA_HEADS, tq, MLA_QW), lambda b, p, qi, kj, f: (b, 0, qi[p], 0)),
                      pl.BlockSpec((1, tk, MLA_QW), lambda b, p, qi, kj, f: (b, kj[p], 0)),
                      pl.BlockSpec((1, MLA_QW, tk), lambda b, p, qi, kj, f: (b, 0, kj[p])),
                      pl.BlockSpec(w_uv.shape, lambda b, p, qi, kj, f: (0, 0))],
            out_specs=pl.BlockSpec((1, tq, MLA_HEADS * MLA_V), lambda b, p, qi, kj, f: (b, qi[p], 0)),
            scratch_shapes=[pltpu.VMEM((MLA_HEADS, 1, tq), F32),
                            pltpu.VMEM((MLA_HEADS, 16, tq), F32),
                            pltpu.VMEM((MLA_HEADS, MLA_KV_RANK, tq), F32)]),
        out_shape=jax.ShapeDtypeStruct((bsz, l, MLA_HEADS * MLA_V), F32),
        compiler_params=_cp("parallel", "arbitrary"),
        name="mla_attention",
    )(*pairs, q4, kc3, kt3, w_uv)


def _rope_tables(pos0, l, width):
    half = MLA_ROPE // 2
    inv = ROPE_THETA ** (-jnp.arange(half, dtype=F32) * 2.0 / MLA_ROPE)
    ang = (pos0 + jnp.arange(l)).astype(F32)[:, None] * inv[None, :]
    cos, sin = jnp.cos(ang), jnp.sin(ang)
    cos32 = jnp.concatenate([cos, cos], axis=1)
    sin32 = jnp.concatenate([-sin, sin], axis=1)
    reps = width // MLA_ROPE
    return jnp.tile(cos32, (1, reps)), jnp.tile(sin32, (1, reps))


def _prep_even(w_in, w_gate_up, b_gate, gla_norm_g, pool_w, pool_scale, w_out):
    o_r = 2 * GLA_QK + GLA_V
    o_g = o_r + GLA_V
    o_p = o_g + GLA_GATE_RANK
    w_main = jnp.concatenate([w_in[:, :o_g], w_in[:, o_p:]], axis=1).astype(BF16)
    w_g = jnp.pad(w_in[:, o_g:o_p], ((0, 0), (0, LANES - GLA_GATE_RANK))).astype(BF16)
    w_gu = jnp.pad(w_gate_up, ((0, LANES - GLA_GATE_RANK), (0, 0))).astype(BF16)
    return dict(w_main=w_main, w_g=w_g, w_gu=w_gu, b_g=b_gate.reshape(1, -1),
                gnorm=gla_norm_g.reshape(1, -1), pool_w=pool_w.astype(BF16),
                pool_scale=pool_scale.reshape(1, -1), w_out=w_out.astype(BF16))


def _prep_odd(w_in, q_norm_g, w_uq, kv_norm_g, w_uk, w_uv, gm_g, gm_b, gm_ws, gm_bs, w_out):
    o_ckv = MLA_Q_RANK
    o_kpe = o_ckv + MLA_KV_RANK
    o_u = o_kpe + MLA_ROPE
    w_in2 = jnp.concatenate([w_in[:, :o_kpe], w_in[:, o_u:], w_in[:, o_kpe:o_u],
                             jnp.zeros((D_MODEL, LANES - MLA_ROPE), F32)], axis=1).astype(BF16)
    uq = w_uq.reshape(MLA_Q_RANK, MLA_HEADS, MLA_NOPE + MLA_ROPE)
    w_qn = jnp.pad(uq[:, :, :MLA_NOPE], ((0, 0), (0, 0), (0, LANES - MLA_NOPE)))
    w_qn = w_qn.reshape(MLA_Q_RANK, MLA_HEADS * LANES).astype(BF16)
    w_qp = uq[:, :, MLA_NOPE:].reshape(MLA_Q_RANK, MLA_HEADS * MLA_ROPE).astype(BF16)
    uk = w_uk.reshape(MLA_KV_RANK, MLA_HEADS, MLA_NOPE).transpose(1, 2, 0)
    w_ukp = jnp.pad(uk, ((0, 0), (0, LANES - MLA_NOPE), (0, 0))).astype(BF16)
    src = np.arange(MLA_HEADS * MLA_ROPE)
    perm = np.zeros((MLA_HEADS * MLA_ROPE, MLA_HEADS * LANES), np.float32)
    perm[src, (src // MLA_ROPE) * LANES + src % MLA_ROPE] = 1.0
    return dict(w_in=w_in2, q_g=q_norm_g.reshape(1, -1), w_qn=w_qn, w_qp=w_qp, w_uk=w_ukp,
                perm=jnp.asarray(perm, BF16), kv_g=kv_norm_g.reshape(1, -1), w_uv=w_uv.astype(BF16),
                gm_g=gm_g.reshape(1, -1), gm_b=gm_b.reshape(1, -1), gm_ws=gm_ws, gm_bs=gm_bs,
                w_out=w_out.astype(BF16))


def _prep_route(wg, bg, we, be):
    w_r = jnp.pad(jnp.concatenate([wg, we], axis=1), ((0, 0), (0, LANES - N_GROUPS - N_EXPERTS)))
    w_hi = w_r.astype(BF16)
    w_lo = (w_r - w_hi.astype(F32)).astype(BF16)
    b_r = jnp.pad(jnp.concatenate([bg, be]), (0, LANES - N_GROUPS - N_EXPERTS)).reshape(1, LANES)
    return jnp.concatenate([w_hi, w_lo], axis=1), b_r


def _even_mixer(h3, st0, hist, pos0, pw):
    bsz, l, _ = h3.shape
    q, k, la, v, r, xp = _even_in(h3.reshape(bsz * l, D_MODEL), pw['w_main'], pw['w_g'], pw['w_gu'], pw['b_g'])
    to3 = lambda a: a.reshape(bsz, l, a.shape[-1])
    o, st = _gla(to3(q), to3(k), to3(la), to3(v), to3(r), st0, pw['gnorm'])
    xp3 = to3(xp)
    hist16 = jnp.pad(hist, ((0, 0), (POOL_HALO - POOL_HIST, 0), (0, 0)))
    pooled = _pool(xp3, hist16, pw['pool_w'], pw['pool_scale'], pos0)
    hist_new = jnp.concatenate([hist, xp3], axis=1)[:, -POOL_HIST:]
    return o, pooled, st, hist_new


def _odd_mixer(h3, ckv_past, kpe_past, pw):
    bsz, l, _ = h3.shape
    n_past = ckv_past.shape[1]
    cos_q, sin_q = _rope_tables(n_past, l, MLA_HEADS * MLA_ROPE)
    cos_k, sin_k = _rope_tables(n_past, l, MLA_ROPE)
    padk = ((0, 0), (0, LANES - MLA_ROPE))
    cos_k, sin_k = jnp.pad(cos_k, padk), jnp.pad(sin_k, padk)
    cl = min(l, GMLP_CHUNK)
    ws = jnp.tril(pw['gm_ws'][:, :cl, :cl]).astype(BF16)
    bs = jnp.repeat(pw['gm_bs'][:, :cl].T, GMLP_CH, axis=1)
    q4, kc, kt, ckv, kpe, gated, vn = _odd_in(h3, pw['w_in'], pw['q_g'], pw['w_qn'], pw['w_qp'], pw['w_uk'],
                                              pw['perm'], pw['kv_g'], cos_q, sin_q, cos_k, sin_k,
                                              pw['gm_g'], pw['gm_b'], ws, bs)
    n_keys = n_past + l
    tq, tk = min(l, 256), 512
    if n_past:
        past = jnp.concatenate([ckv_past, kpe_past, jnp.ones((bsz, n_past, 1), F32),
                                jnp.zeros((bsz, n_past, MLA_QW - MLA_ONE_LANE - 1), F32)], axis=2).astype(BF16)
        kc = jnp.concatenate([past, kc], axis=1)
        kt = jnp.concatenate([past.transpose(0, 2, 1), kt], axis=2)
    kc = jnp.pad(kc, ((0, 0), (0, -n_keys % tk), (0, 0)))
    kt = jnp.pad(kt, ((0, 0), (0, 0), (0, -n_keys % tk)))
    attn = _attention(q4, kc, kt, pw['w_uv'], n_past, n_keys, tq, tk)
    return attn, gated, ckv, kpe, vn


def _finish_layer(a3, b3, h3, w_out, lw):
    bsz, l, _ = h3.shape
    t = bsz * l
    h1, xs, rs, rg, n16 = _out_route(a3.reshape(t, -1), b3.reshape(t, -1), h3.reshape(t, D_MODEL), w_out,
                                     lw['ln_mix_g'], lw['ln_mix_b'], lw['w_r'], lw['b_r'])
    h2 = _moe(h1, xs, rs, rg, n16, lw['w1'], lw['w3'], lw['w2'], lw['ln_ffn_g'], lw['ln_ffn_b'])
    return h2.reshape(bsz, l, D_MODEL)


def kernel(x_prompt, x_sample, state_gla, state_pool, cache_mla_ckv, cache_mla_kpe, w_in_even, w_gate_up, b_gate, gla_norm_g, pool_w, pool_scale, w_out_even, w_in_odd, mla_q_norm_g, mla_w_uq, mla_kv_norm_g, mla_w_uk, mla_w_uv, gmlp_norm_g, gmlp_norm_b, gmlp_ws, gmlp_bs, w_out_odd, ln_mix_g, ln_mix_b, router_group_w, router_group_b, router_expert_w, router_expert_b, expert_w1, expert_w3, expert_w2, ln_ffn_g, ln_ffn_b):
    hp, hs = x_prompt, x_sample
    bp = hp.shape[0]
    past_len = cache_mla_ckv.shape[2]
    gla_p, gla_s, pool_p, pool_s = [], [], [], []
    ckv_p, ckv_s, kpe_p, kpe_s, gv_s = [], [], [], [], []

    def state_to_t(s):
        return s.transpose(0, 3, 1, 2).reshape(s.shape[0], GLA_DV, GLA_QK)

    def state_from_t(st):
        return st.reshape(st.shape[0], GLA_DV, GLA_HEADS, GLA_DK).transpose(0, 2, 3, 1)

    for layer in range(DEPTH):
        i = layer // 2
        w_r, b_r = _prep_route(router_group_w[layer], router_group_b[layer],
                               router_expert_w[layer], router_expert_b[layer])
        lw = dict(ln_mix_g=ln_mix_g[layer].reshape(1, -1), ln_mix_b=ln_mix_b[layer].reshape(1, -1),
                  ln_ffn_g=ln_ffn_g[layer].reshape(1, -1), ln_ffn_b=ln_ffn_b[layer].reshape(1, -1),
                  w_r=w_r, b_r=b_r, w1=expert_w1[layer].astype(BF16), w3=expert_w3[layer].astype(BF16),
                  w2=expert_w2[layer].astype(BF16))
        if layer % 2 == 0:
            pw = _prep_even(w_in_even[i], w_gate_up[i], b_gate[i], gla_norm_g[i], pool_w[i], pool_scale[i],
                            w_out_even[i])
            st0 = jnp.zeros((bp, GLA_DV, GLA_QK), F32)
            hist0 = jnp.zeros((bp, POOL_HIST, POOL_WIDTH), F32)
            op, pp, stp, histp = _even_mixer(hp, st0, hist0, 0, pw)
            os_, ps, sts, hists = _even_mixer(hs, state_to_t(state_gla[i]), state_pool[i], past_len, pw)
            gla_p.append(state_from_t(stp)); gla_s.append(state_from_t(sts))
            pool_p.append(histp); pool_s.append(hists)
            ap, bpj, as_, bsj = op, pp, os_, ps
        else:
            pw = _prep_odd(w_in_odd[i], mla_q_norm_g[i], mla_w_uq[i], mla_kv_norm_g[i], mla_w_uk[i], mla_w_uv[i],
                           gmlp_norm_g[i], gmlp_norm_b[i], gmlp_ws[i], gmlp_bs[i], w_out_odd[i])
            no_ckv = jnp.zeros((bp, 0, MLA_KV_RANK), F32)
            no_kpe = jnp.zeros((bp, 0, MLA_ROPE), F32)
            ap, bpj, cp, kp, _ = _odd_mixer(hp, no_ckv, no_kpe, pw)
            as_, bsj, cs, ks, vs = _odd_mixer(hs, cache_mla_ckv[i], cache_mla_kpe[i], pw)
            ckv_p.append(cp); ckv_s.append(cs); kpe_p.append(kp); kpe_s.append(ks); gv_s.append(vs)
        hp = _finish_layer(ap, bpj, hp, pw['w_out'], lw)
        hs = _finish_layer(as_, bsj, hs, pw['w_out'], lw)
    return (hp, hs, jnp.stack(gla_p), jnp.stack(gla_s), jnp.stack(pool_p), jnp.stack(pool_s),
            jnp.stack(ckv_p), jnp.stack(ckv_s), jnp.stack(kpe_p), jnp.stack(kpe_s), jnp.stack(gv_s))
```

```python
import functools

import numpy as np
import jax
import jax.numpy as jnp
from jax import lax
from jax.experimental import pallas as pl
from jax.experimental.pallas import tpu as pltpu

F32 = jnp.float32
BF16 = jnp.bfloat16
I32 = jnp.int32

D_MODEL = 1024
DEPTH = 2
CHUNK = 64
ALPHA = (2 * DEPTH) ** 0.25
LN_EPS = 1e-5

GLA_HEADS = 4
GLA_DV = 128
GLA_DK = 64
GLA_QK = GLA_HEADS * GLA_DK
GLA_V = GLA_HEADS * GLA_DV
GLA_GATE_RANK = 16
GLA_GATE_TAU = 16.0
GLA_SUB = 8

POOL_WIDTH = 512
POOL_CH = 128
POOL_WINDOWS = (2, 4, 8, 16)
POOL_HIST = 15
POOL_HALO = 16

MLA_HEADS = 8
MLA_NOPE = 64
MLA_ROPE = 32
MLA_V = 64
MLA_Q_RANK = 256
MLA_KV_RANK = 128
ROPE_THETA = 10000.0
MLA_SCALE = (MLA_NOPE + MLA_ROPE) ** -0.5
MLA_QW = 256
MLA_ONE_LANE = MLA_KV_RANK + MLA_ROPE
GMLP_WIDTH = 512
GMLP_CH = 128
GMLP_GROUPS = 4
GMLP_CHUNK = 128

N_GROUPS = 4
EXPERTS_PER_GROUP = 8
N_EXPERTS = 32
TOP_K = 2
D_EXPERT = 256
MOE_ROWS = 512
MOE_CHUNK = 16
MOE_BLOCK_CHUNKS = MOE_ROWS // MOE_CHUNK

LANES = 128
VMEM_LIMIT = 48 * 1024 * 1024


def _cp(*sem):
    return pltpu.CompilerParams(dimension_semantics=sem, vmem_limit_bytes=VMEM_LIMIT)


def _dot(a, b):
    return jnp.dot(a, b, preferred_element_type=F32)


def _dot_nt(a, b):
    return lax.dot_general(a, b, (((1,), (1,)), ((), ())), preferred_element_type=F32)


def _dot_tn(a, b):
    return lax.dot_general(a, b, (((0,), (0,)), ((), ())), preferred_element_type=F32)


def _split3(x):
    hi = x.astype(BF16)
    r1 = x - hi.astype(F32)
    mid = r1.astype(BF16)
    lo = (r1 - mid.astype(F32)).astype(BF16)
    return hi, mid, lo


def _layernorm(x, g, b):
    mu = jnp.mean(x, axis=-1, keepdims=True)
    xc = x - mu
    var = jnp.mean(xc * xc, axis=-1, keepdims=True)
    return xc * lax.rsqrt(var + LN_EPS) * g + b


def _gelu(x):
    return 0.5 * x * (1.0 + jnp.tanh(0.7978845608028654 * (x + 0.044715 * (x * x * x))))


def _sigmoid(x):
    return 1.0 / (1.0 + jnp.exp(-x))


def _full_spec(a, nargs):
    nd = a.ndim
    if nargs == 1:
        return pl.BlockSpec(a.shape, lambda i: (0,) * nd)
    return pl.BlockSpec(a.shape, lambda i, j: (0,) * nd)


def _even_in_kernel(x_ref, w_ref, wg_ref, wgu_ref, bg_ref,
                    q_ref, k_ref, la_ref, v_ref, r_ref, xp_ref):
    xb = x_ref[...].astype(BF16)
    z = _dot(xb, w_ref[...])
    q_ref[...] = (z[:, 0:GLA_QK] * (GLA_DK ** -0.5)).astype(BF16)
    k_ref[...] = z[:, GLA_QK:2 * GLA_QK].astype(BF16)
    v_ref[...] = z[:, 2 * GLA_QK:2 * GLA_QK + GLA_V].astype(BF16)
    r_ref[...] = z[:, 2 * GLA_QK + GLA_V:2 * GLA_QK + 2 * GLA_V].astype(BF16)
    xp_ref[...] = z[:, 2 * GLA_QK + 2 * GLA_V:].astype(BF16)
    g = _dot(xb, wg_ref[...])
    pre = _dot(g.astype(BF16), wgu_ref[...]) + bg_ref[...]
    logsig = jnp.minimum(pre, 0.0) - jnp.log(1.0 + jnp.exp(-jnp.abs(pre)))
    la_ref[...] = logsig * (1.0 / GLA_GATE_TAU)


def _even_in(x2, w_main, w_g, w_gu, b_g):
    t = x2.shape[0]
    tm = min(512, t)
    row = lambda n: pl.BlockSpec((tm, n), lambda i: (i, 0))
    widths = (GLA_QK, GLA_QK, GLA_QK, GLA_V, GLA_V, POOL_WIDTH)
    return pl.pallas_call(
        _even_in_kernel,
        grid=(t // tm,),
        in_specs=[row(D_MODEL)] + [_full_spec(a, 1) for a in (w_main, w_g, w_gu, b_g)],
        out_specs=[row(n) for n in widths],
        out_shape=[jax.ShapeDtypeStruct((t, n), F32 if i == 2 else BF16) for i, n in enumerate(widths)],
        compiler_params=_cp("parallel"),
        name="even_in",
    )(x2, w_main, w_g, w_gu, b_g)


def _gla_consts(c):
    n = c * GLA_SUB
    tri = np.tril(np.ones((c, c), np.float32))
    headsum = (np.arange(GLA_QK)[:, None] // GLA_DK == np.arange(GLA_V)[None, :] // GLA_DV).astype(np.float32)
    msel = (np.arange(n)[None, :] // GLA_SUB == np.arange(c)[:, None]).astype(np.float32)
    return [jnp.asarray(a, BF16) for a in (tri, headsum, msel)]


def _gla_chunk(q, k, la, v, st, c, tri, headsum, msel):
    lane = lax.broadcasted_iota(I32, (1, GLA_QK), 1)
    head_of_lane = lane // GLA_DK
    row = lax.broadcasted_iota(I32, (c, 1), 0)
    ii = lax.broadcasted_iota(I32, (c, c), 0)
    jj = lax.broadcasted_iota(I32, (c, c), 1)

    hi, mid, lo = _split3(la)
    b = _dot(tri, hi) + _dot(tri, mid) + _dot(tri, lo)

    head_masks = [head_of_lane == h for h in range(GLA_HEADS)]
    vb = v.astype(BF16)

    a_off = [jnp.zeros((c, c), F32) for _ in range(GLA_HEADS)]
    s = c // 2
    while s >= GLA_SUB:
        nblk = c // (2 * s)
        blk = row // (2 * s)
        right = ((row // s) % 2) == 1
        bref = jnp.zeros((c, GLA_QK), F32)
        for m in range(nblk):
            r0 = m * 2 * s + s - 1
            bref = jnp.where(blk == m, b[r0:r0 + 1, :], bref)
        qe = jnp.where(right, q * jnp.exp(jnp.minimum(b - bref, 0.0)), 0.0)
        ke = jnp.where(right, 0.0, k * jnp.exp(jnp.minimum(bref - b, 0.0))).astype(BF16)
        same = (ii // (2 * s)) == (jj // (2 * s))
        for h in range(GLA_HEADS):
            a = _dot_nt(jnp.where(head_masks[h], qe, 0.0).astype(BF16), ke)
            a_off[h] = a_off[h] + (a if nblk == 1 else jnp.where(same, a, 0.0))
        s //= 2

    nsb = c // GLA_SUB
    parts = []
    for i in range(nsb):
        sl = slice(i * GLA_SUB, (i + 1) * GLA_SUB)
        bi, qi, ki = b[sl], q[sl], k[sl]
        diff = bi[:, None, :] - bi[None, :, :]
        p = qi[:, None, :] * ki[None, :, :] * jnp.exp(jnp.minimum(diff, 0.0))
        parts.append(p.reshape(GLA_SUB * GLA_SUB, GLA_QK))
    pcat = jnp.concatenate(parts, axis=0).astype(BF16)
    n = nsb * GLA_SUB * GLA_SUB
    rsum = _dot(pcat, headsum)
    idx = lax.broadcasted_iota(I32, (n, 1), 0)
    causal = (idx % GLA_SUB) <= ((idx // GLA_SUB) % GLA_SUB)

    qb = q * jnp.exp(b)
    b_end = b[c - 1:c, :]
    kd = (k * jnp.exp(b_end - b)).astype(BF16)
    stb = st.astype(BF16)
    st_new = st * jnp.exp(b_end)

    outs = []
    for h in range(GLA_HEADS):
        vh = vb[:, h * GLA_DV:(h + 1) * GLA_DV]
        vf = v[:, h * GLA_DV:(h + 1) * GLA_DV]
        vt = jnp.concatenate(
            [jnp.broadcast_to(vf[i * GLA_SUB:(i + 1) * GLA_SUB][None], (GLA_SUB, GLA_SUB, GLA_DV))
             .reshape(GLA_SUB * GLA_SUB, GLA_DV) for i in range(nsb)], axis=0)
        xh = jnp.where(causal, rsum[:, h * GLA_DV:(h + 1) * GLA_DV], 0.0) * vt
        o = _dot(msel, xh.astype(BF16))
        o = o + _dot(a_off[h].astype(BF16), vh)
        o = o + _dot_nt(jnp.where(head_masks[h], qb, 0.0).astype(BF16), stb)
        outs.append(o)
        st_new = st_new + jnp.where(head_masks[h], _dot_tn(vh, kd), 0.0)
    return outs, st_new


def _gla_kernel(q_ref, k_ref, la_ref, v_ref, r_ref, st0_ref, g_ref, tri_ref, hs_ref, ms_ref,
                o_ref, st_ref, st_scr, *, c, nchunks):
    @pl.when(pl.program_id(1) == 0)
    def _():
        st_scr[...] = st0_ref[0]

    def body(ci, carry):
        r0 = pl.multiple_of(ci * c, c)
        rows = pl.ds(r0, c)
        outs, st_new = _gla_chunk(q_ref[0, rows, :].astype(F32), k_ref[0, rows, :].astype(F32),
                                  la_ref[0, rows, :], v_ref[0, rows, :].astype(F32), st_scr[...], c,
                                  tri_ref[...], hs_ref[...], ms_ref[...])
        st_scr[...] = st_new
        r = r_ref[0, rows, :].astype(F32)
        g = g_ref[...]
        for h in range(GLA_HEADS):
            o = outs[h]
            sl = slice(h * GLA_DV, (h + 1) * GLA_DV)
            on = o * lax.rsqrt(jnp.mean(o * o, axis=-1, keepdims=True) + LN_EPS) * g
            rh = r[:, sl]
            o_ref[0, rows, sl] = (on * (rh * _sigmoid(rh))).astype(BF16)
        return carry

    lax.fori_loop(0, nchunks, body, 0, unroll=4 if nchunks % 4 == 0 else 1)
    st_ref[0] = st_scr[...]


def _gla(q3, k3, la3, v3, r3, st0, gnorm):
    bsz, l, _ = q3.shape
    c = min(l, CHUNK)
    tl = min(l, 512)
    blk = lambda n: pl.BlockSpec((1, tl, n), lambda b, i: (b, i, 0))
    st_spec = pl.BlockSpec((1, GLA_DV, GLA_QK), lambda b, i: (b, 0, 0))
    consts = _gla_consts(c)
    return pl.pallas_call(
        functools.partial(_gla_kernel, c=c, nchunks=tl // c),
        grid=(bsz, l // tl),
        in_specs=[blk(GLA_QK), blk(GLA_QK), blk(GLA_QK), blk(GLA_V), blk(GLA_V), st_spec,
                  _full_spec(gnorm, 2)] + [_full_spec(a, 2) for a in consts],
        out_specs=[blk(GLA_V), st_spec],
        out_shape=[jax.ShapeDtypeStruct((bsz, l, GLA_V), BF16),
                   jax.ShapeDtypeStruct((bsz, GLA_DV, GLA_QK), F32)],
        scratch_shapes=[pltpu.VMEM((GLA_DV, GLA_QK), F32)],
        compiler_params=_cp("parallel", "arbitrary"),
        name="gla",
    )(q3, k3, la3, v3, r3, st0, gnorm, *consts)


def _pool_kernel(x_ref, halo_ref, hist_ref, w_ref, scale_ref, o_ref, *, tl, pos0):
    i = pl.program_id(1)
    x = x_ref[0].astype(F32)
    prev = jnp.where(i == 0, hist_ref[0], halo_ref[0].astype(F32))
    e = jnp.concatenate([prev, x], axis=0)
    t = i * tl + lax.broadcasted_iota(I32, (tl, 1), 0)
    pos = pos0 + t
    sums = []
    shift = 1
    for g, w in enumerate(POOL_WINDOWS):
        e = e[:, POOL_CH:] if g > 0 else e
        while shift < w:
            e = e[shift:] + e[:-shift]
            shift *= 2
        off = POOL_HALO - (w - 1)
        sums.append(e[off:off + tl, :POOL_CH])
    outs = []
    for g, w in enumerate(POOL_WINDOWS):
        cnt = jnp.minimum(pos + 1, w).astype(F32)
        mix = sums[g] / cnt - x[:, g * POOL_CH:(g + 1) * POOL_CH]
        outs.append(_dot(mix.astype(BF16), w_ref[g]))
    o_ref[0] = (jnp.concatenate(outs, axis=1) * scale_ref[...]).astype(BF16)


def _pool(xp3, hist16, pool_w, pool_scale, pos0):
    bsz, l, _ = xp3.shape
    tl = min(l, 512)
    per = tl // POOL_HALO
    return pl.pallas_call(
        functools.partial(_pool_kernel, tl=tl, pos0=pos0),
        grid=(bsz, l // tl),
        in_specs=[pl.BlockSpec((1, tl, POOL_WIDTH), lambda b, i: (b, i, 0)),
                  pl.BlockSpec((1, POOL_HALO, POOL_WIDTH), lambda b, i: (b, jnp.maximum(i * per - 1, 0), 0)),
                  pl.BlockSpec((1, POOL_HALO, POOL_WIDTH), lambda b, i: (b, 0, 0)),
                  _full_spec(pool_w, 2), _full_spec(pool_scale, 2)],
        out_specs=pl.BlockSpec((1, tl, POOL_WIDTH), lambda b, i: (b, i, 0)),
        out_shape=jax.ShapeDtypeStruct((bsz, l, POOL_WIDTH), BF16),
        compiler_params=_cp("parallel", "parallel"),
        name="pool",
    )(xp3, xp3, hist16, pool_w, pool_scale)


def _out_route_kernel(a_ref, b_ref, h_ref, w_ref, g_ref, bt_ref, wr_ref, br_ref,
                      h1_ref, xs_ref, rs_ref, rg_ref, n16_ref, *, tm, slots):
    half = w_ref.shape[0] // 2
    y = _dot(a_ref[...].astype(BF16), w_ref[0:half, :]) + _dot(b_ref[...].astype(BF16), w_ref[half:, :])
    x = _layernorm(ALPHA * h_ref[...].astype(F32) + y, g_ref[...], bt_ref[...])
    h1_ref[...] = x.astype(BF16)

    xh = x.astype(BF16)
    xl = (x - xh.astype(F32)).astype(BF16)
    hl = _dot(xh, wr_ref[...])
    logits = hl[:, :LANES] + hl[:, LANES:] + _dot(xl, wr_ref[:, 0:LANES]) + br_ref[...]

    lane = lax.broadcasted_iota(I32, (tm, LANES), 1)
    lanef = lane.astype(F32)
    neg = -jnp.inf
    big = jnp.float32(1 << 20)

    def first_lane(hit):
        return jnp.min(jnp.where(hit, lanef, big), axis=-1, keepdims=True).astype(I32)

    gl = jnp.where(lane < N_GROUPS, logits, neg)
    gmax = jnp.max(gl, axis=-1, keepdims=True)
    g_sel = first_lane(gl == gmax)
    g_prob = 1.0 / jnp.sum(jnp.exp(gl - gmax), axis=-1, keepdims=True)
    eidx = lane - N_GROUPS
    in_grp = (eidx >= 0) & (eidx < N_EXPERTS) & ((eidx // EXPERTS_PER_GROUP) == g_sel)
    el = jnp.where(in_grp, logits, neg)
    v1 = jnp.max(el, axis=-1, keepdims=True)
    i1 = first_lane(el == v1)
    el2 = jnp.where(lane == i1, neg, el)
    v2 = jnp.max(el2, axis=-1, keepdims=True)
    i2 = first_lane(el2 == v2)
    e21 = jnp.exp(v2 - v1)
    gate1 = g_prob / (1.0 + e21)
    gate2 = g_prob * e21 / (1.0 + e21)
    e1 = i1 - N_GROUPS
    e2 = i2 - N_GROUPS

    oh1 = lane == e1
    oh2 = lane == e2
    oh = oh1.astype(F32) + oh2.astype(F32)
    ti = lax.broadcasted_iota(I32, (tm, tm), 0)
    tj = lax.broadcasted_iota(I32, (tm, tm), 1)
    before = _dot((tj < ti).astype(BF16), oh.astype(BF16))
    cnt = jnp.sum(oh, axis=0, keepdims=True)
    n16 = jnp.floor((cnt + (MOE_CHUNK - 1)) * (1.0 / MOE_CHUNK))
    n16_8 = jnp.broadcast_to(n16, (8, LANES))
    ui = lax.broadcasted_iota(I32, (LANES, LANES), 0)
    uj = lax.broadcasted_iota(I32, (LANES, LANES), 1)
    run_start = _dot(n16_8.astype(BF16), (ui < uj).astype(BF16))[0:1]
    slot_of = MOE_CHUNK * run_start + before
    slot1 = jnp.sum(jnp.where(oh1, slot_of, 0.0), axis=-1, keepdims=True).astype(I32)
    slot2 = jnp.sum(jnp.where(oh2, slot_of, 0.0), axis=-1, keepdims=True).astype(I32)
    sl = lax.broadcasted_iota(I32, (tm, slots), 1)
    place = ((sl == slot1) | (sl == slot2)).astype(BF16)
    xs_ref[...] = _dot_tn(place, xh).astype(BF16)

    rs_ref[...] = jnp.where(lane == 0, slot1, jnp.where(lane == 1, slot2, 0))
    rg_ref[...] = jnp.where(lane == 0, gate1, jnp.where(lane == 1, gate2, 0.0))
    n16_ref[...] = n16_8.astype(I32)


def _moe_slots(tm):
    worst = tm * TOP_K + N_EXPERTS * (MOE_CHUNK - 1)
    return -(-worst // MOE_ROWS) * MOE_ROWS


def _out_route(a2, b2, h2, w_out, ln_g, ln_b, w_r, b_r):
    t = h2.shape[0]
    tm = min(512, t)
    nt = t // tm
    slots = _moe_slots(tm)
    row = lambda n: pl.BlockSpec((tm, n), lambda i: (i, 0))
    return pl.pallas_call(
        functools.partial(_out_route_kernel, tm=tm, slots=slots),
        grid=(nt,),
        in_specs=[row(a2.shape[1]), row(b2.shape[1]), row(D_MODEL)]
                 + [_full_spec(a, 1) for a in (w_out, ln_g, ln_b, w_r, b_r)],
        out_specs=[row(D_MODEL), pl.BlockSpec((slots, D_MODEL), lambda i: (i, 0)), row(LANES), row(LANES),
                   pl.BlockSpec((8, LANES), lambda i: (i, 0))],
        out_shape=[jax.ShapeDtypeStruct((t, D_MODEL), BF16), jax.ShapeDtypeStruct((nt * slots, D_MODEL), BF16),
                   jax.ShapeDtypeStruct((t, LANES), I32), jax.ShapeDtypeStruct((t, LANES), F32),
                   jax.ShapeDtypeStruct((nt * 8, LANES), I32)],
        compiler_params=_cp("parallel"),
        name="out_route",
    )(a2, b2, h2, w_out, ln_g, ln_b, w_r, b_r)


def _chunk_rows(chunk):
    return pl.ds(pl.multiple_of(chunk * MOE_CHUNK, MOE_CHUNK), MOE_CHUNK)


def _expert_kernel(src_ref, nreal_ref, be_ref, nu_ref, xs_ref, w1_ref, w3_ref, w2_ref, ys_ref,
                   xbuf, ybuf, gsem, ssem):
    del be_ref
    b = pl.program_id(0)
    nu = nu_ref[0]

    def gather(blk, slot, j):
        return pltpu.make_async_copy(xs_ref.at[_chunk_rows(src_ref[blk * MOE_BLOCK_CHUNKS + j]), :],
                                     xbuf.at[slot, pl.ds(j * MOE_CHUNK, MOE_CHUNK), :], gsem.at[slot])

    def scatter(blk, slot, j):
        return pltpu.make_async_copy(ybuf.at[slot, _chunk_rows(j), :],
                                     ys_ref.at[_chunk_rows(src_ref[blk * MOE_BLOCK_CHUNKS + j]), :],
                                     ssem.at[slot])

    def start_gather(blk, slot):
        for j in range(MOE_BLOCK_CHUNKS):
            gather(blk, slot, j).start()

    def wait_gather(slot):
        pltpu.make_async_copy(xs_ref.at[pl.ds(0, MOE_ROWS), :], xbuf.at[slot], gsem.at[slot]).wait()

    def for_real_chunks(blk, fn):
        def body(j, carry):
            fn(j)
            return carry
        lax.fori_loop(0, nreal_ref[blk], body, 0)

    def wait_scatter(blk, slot):
        full = nreal_ref[blk] == MOE_BLOCK_CHUNKS

        @pl.when(full)
        def _():
            pltpu.make_async_copy(ybuf.at[slot], ys_ref.at[pl.ds(0, MOE_ROWS), :], ssem.at[slot]).wait()

        @pl.when(jnp.logical_not(full))
        def _():
            for_real_chunks(blk, lambda j: scatter(blk, slot, j).wait())

    @pl.when(b < nu)
    def _():
        slot = b % 2

        @pl.when(b == 0)
        def _():
            start_gather(b, slot)

        @pl.when(b + 1 < nu)
        def _():
            start_gather(b + 1, 1 - slot)

        wait_gather(slot)

        @pl.when(b >= 2)
        def _():
            wait_scatter(b - 2, slot)

        xb = xbuf[slot]
        a = _dot(xb, w1_ref[0].astype(BF16))
        hid = a * _sigmoid(a) * _dot(xb, w3_ref[0].astype(BF16))
        ybuf[slot] = _dot(hid.astype(BF16), w2_ref[0].astype(BF16)).astype(BF16)
        for_real_chunks(b, lambda j: scatter(b, slot, j).start())

        @pl.when(b == nu - 1)
        def _():
            wait_scatter(b, slot)

            @pl.when(b >= 1)
            def _():
                wait_scatter(b - 1, 1 - slot)


def _experts(xs, src, nreal, block_e, n_used, w1, w3, w2):
    nblk = src.shape[0] // MOE_BLOCK_CHUNKS
    wspec = lambda shape: pl.BlockSpec(shape, lambda i, src, nr, be, nu: (be[i], 0, 0))
    return pl.pallas_call(
        _expert_kernel,
        grid_spec=pltpu.PrefetchScalarGridSpec(
            num_scalar_prefetch=4,
            grid=(nblk,),
            in_specs=[pl.BlockSpec(memory_space=pl.ANY),
                      wspec((1, D_MODEL, D_EXPERT)), wspec((1, D_MODEL, D_EXPERT)), wspec((1, D_EXPERT, D_MODEL))],
            out_specs=pl.BlockSpec(memory_space=pl.ANY),
            scratch_shapes=[pltpu.VMEM((2, MOE_ROWS, D_MODEL), BF16), pltpu.VMEM((2, MOE_ROWS, D_MODEL), BF16),
                            pltpu.SemaphoreType.DMA((2,)), pltpu.SemaphoreType.DMA((2,))]),
        out_shape=jax.ShapeDtypeStruct(xs.shape, xs.dtype),
        input_output_aliases={4: 0},
        compiler_params=_cp("arbitrary"),
        name="moe_experts",
    )(src, nreal, block_e, n_used, xs, w1, w3, w2)


def _combine_kernel(h_ref, rs_ref, rg_ref, ys_ref, g_ref, b_ref, o_ref, *, tm, slots):
    rs = rs_ref[...]
    rg = rg_ref[...]
    sl = lax.broadcasted_iota(I32, (tm, slots), 1)
    weight = (jnp.where(sl == rs[:, 0:1], rg[:, 0:1], 0.0)
              + jnp.where(sl == rs[:, 1:2], rg[:, 1:2], 0.0)).astype(BF16)
    y = _dot(weight, ys_ref[...])
    o_ref[...] = _layernorm(ALPHA * h_ref[...].astype(F32) + y, g_ref[...], b_ref[...]).astype(o_ref.dtype)


def _combine(h2, rs, rg, ys, ln_g, ln_b, out_dtype):
    t = h2.shape[0]
    tm = min(512, t)
    slots = ys.shape[0] // (t // tm)
    row = lambda n: pl.BlockSpec((tm, n), lambda i: (i, 0))
    return pl.pallas_call(
        functools.partial(_combine_kernel, tm=tm, slots=slots),
        grid=(t // tm,),
        in_specs=[row(D_MODEL), row(LANES), row(LANES), pl.BlockSpec((slots, D_MODEL), lambda i: (i, 0)),
                  _full_spec(ln_g, 1), _full_spec(ln_b, 1)],
        out_specs=row(D_MODEL),
        out_shape=jax.ShapeDtypeStruct((t, D_MODEL), out_dtype),
        compiler_params=_cp("parallel"),
        name="moe_combine",
    )(h2, rs, rg, ys, ln_g, ln_b)


def _moe(h1, xs, rs, rg, n16_rows, w1, w3, w2, ln_g, ln_b, out_dtype):
    t = h1.shape[0]
    tm = min(512, t)
    nt = t // tm
    slots = xs.shape[0] // nt
    n16 = n16_rows[::8, :N_EXPERTS]
    per_e = jnp.sum(n16, axis=0)
    blocks_e = (per_e + MOE_BLOCK_CHUNKS - 1) // MOE_BLOCK_CHUNKS
    blk_end = jnp.cumsum(blocks_e)
    blk_start = blk_end - blocks_e
    n_used = blk_end[-1:].astype(I32)
    max_chunks = (t * TOP_K) // MOE_CHUNK + nt * N_EXPERTS
    nblk = max_chunks // MOE_BLOCK_CHUNKS + N_EXPERTS
    blk = jnp.arange(nblk, dtype=I32)
    block_e = jnp.minimum(jnp.sum(blk_end[None, :] <= blk[:, None], axis=1), N_EXPERTS - 1).astype(I32)
    of_e = block_e[:, None] == jnp.arange(N_EXPERTS, dtype=I32)[None, :]
    pick_e = lambda tab: jnp.sum(jnp.where(of_e, tab[None, :], 0), axis=1)
    pick_col = lambda tab: jnp.sum(jnp.where(of_e[:, None, :], tab[None, :, :], 0), axis=2)
    run_end = pick_col(jnp.cumsum(n16, axis=0))
    tile_off = pick_col(jnp.cumsum(n16, axis=1) - n16)
    k = ((blk - pick_e(blk_start)) * MOE_BLOCK_CHUNKS)[:, None] + jnp.arange(MOE_BLOCK_CHUNKS, dtype=I32)[None, :]
    real = (k < pick_e(per_e)[:, None]) & (blk < n_used[0])[:, None]
    done = run_end[:, None, :] <= k[:, :, None]
    tile = jnp.minimum(jnp.sum(done, axis=2), nt - 1)
    run_first = jnp.max(jnp.where(done, run_end[:, None, :], 0), axis=2)
    of_t = tile[:, :, None] == jnp.arange(nt, dtype=I32)[None, None, :]
    src = tile * (slots // MOE_CHUNK) + jnp.sum(jnp.where(of_t, tile_off[:, None, :], 0), axis=2) + (k - run_first)
    src = jnp.where(real, src, src[:, :1])
    src = jnp.where((blk < n_used[0])[:, None], src, 0).astype(I32).reshape(-1)
    nreal = jnp.sum(real, axis=1).astype(I32)
    ys = _experts(xs, src, nreal, block_e, n_used, w1, w3, w2)
    return _combine(h1, rs, rg, ys, ln_g, ln_b, out_dtype)


def _swap_halves(x):
    lane = lax.broadcasted_iota(I32, x.shape, 1)
    first = (lane % MLA_ROPE) < (MLA_ROPE // 2)
    return jnp.where(first, pltpu.roll(x, LANES - MLA_ROPE // 2, 1), pltpu.roll(x, MLA_ROPE // 2, 1))


def _rope(x, cos, sin):
    parts = []
    for t in range(x.shape[1] // LANES):
        sl = slice(t * LANES, (t + 1) * LANES)
        parts.append(x[:, sl] * cos[:, sl] + _swap_halves(x[:, sl]) * sin[:, sl])
    return parts[0] if len(parts) == 1 else jnp.concatenate(parts, axis=1)


def _odd_in_kernel(h_ref, w_ref, qg_ref, wqn_ref, wqp_ref, wuk_ref, perm_ref, kvg_ref,
                   cq_ref, sq_ref, ck_ref, sk_ref, gg_ref, gb_ref, ws_ref, bs_ref,
                   q_ref, kc_ref, kt_ref, ckv_ref, kpe_ref, gated_ref, vn_ref, *, tl, cl):
    hb = h_ref[0].astype(BF16)
    z = _dot(hb, w_ref[...])
    o_ckv = MLA_Q_RANK
    o_u = o_ckv + MLA_KV_RANK
    o_v = o_u + GMLP_WIDTH
    o_k = o_v + GMLP_WIDTH
    cq = z[:, :MLA_Q_RANK]
    cqn = cq * lax.rsqrt(jnp.mean(cq * cq, axis=-1, keepdims=True) + LN_EPS) * qg_ref[...]
    cqb = cqn.astype(BF16)
    qn = _dot(cqb, wqn_ref[...])
    qp = _dot(cqb, wqp_ref[...])
    qp = _rope(qp, cq_ref[...], sq_ref[...])
    qpe = _dot((qp * MLA_SCALE).astype(BF16), perm_ref[...])
    for h in range(MLA_HEADS):
        sl = slice(h * LANES, (h + 1) * LANES)
        qa = _dot((qn[:, sl] * MLA_SCALE).astype(BF16), wuk_ref[h])
        q_ref[0, h, :, 0:LANES] = qa.astype(BF16)
        q_ref[0, h, :, LANES:] = qpe[:, sl].astype(BF16)

    ckv = z[:, o_ckv:o_u]
    ckvn = ckv * lax.rsqrt(jnp.mean(ckv * ckv, axis=-1, keepdims=True) + LN_EPS) * kvg_ref[...]
    kp = z[:, o_k:]
    kp = _rope(kp, ck_ref[...], sk_ref[...])
    ckv_ref[0] = ckvn
    kpe_ref[0] = kp[:, :MLA_ROPE]
    one = (lax.broadcasted_iota(I32, (1, LANES), 1) == MLA_ONE_LANE - LANES).astype(F32)
    kp1 = kp + one
    kc_ref[0, :, 0:LANES] = ckvn.astype(BF16)
    kc_ref[0, :, LANES:] = kp1.astype(BF16)
    kt_ref[0, 0:LANES, :] = ckvn.T.astype(BF16)
    kt_ref[0, LANES:, :] = kp1.T.astype(BF16)

    gu = _gelu(z[:, o_u:o_v])
    vn = _layernorm(_gelu(z[:, o_v:o_k]), gg_ref[...], gb_ref[...])
    vn_ref[0] = vn
    vnb = vn.astype(BF16)
    for n in range(tl // cl):
        rs = slice(n * cl, (n + 1) * cl)
        for g in range(GMLP_GROUPS):
            ls = slice(g * GMLP_CH, (g + 1) * GMLP_CH)
            sg = _dot(ws_ref[g], vnb[rs, ls]) + bs_ref[:, ls]
            gated_ref[0, rs, ls] = (gu[rs, ls] * sg).astype(BF16)


def _odd_in(h3, w_in, q_g, w_qn, w_qp, w_uk, perm, kv_g, cos_q, sin_q, cos_k, sin_k,
            gm_g, gm_b, ws, bs):
    bsz, l, _ = h3.shape
    tl = min(l, 512)
    cl = min(l, GMLP_CHUNK)
    rowb = lambda n: pl.BlockSpec((1, tl, n), lambda b, i: (b, i, 0))
    tab = lambda n: pl.BlockSpec((tl, n), lambda b, i: (i, 0))
    consts = (w_in, q_g, w_qn, w_qp, w_uk, perm, kv_g)
    consts2 = (gm_g, gm_b, ws, bs)
    return pl.pallas_call(
        functools.partial(_odd_in_kernel, tl=tl, cl=cl),
        grid=(bsz, l // tl),
        in_specs=[rowb(D_MODEL)] + [_full_spec(a, 2) for a in consts]
                 + [tab(MLA_HEADS * MLA_ROPE), tab(MLA_HEADS * MLA_ROPE), tab(LANES), tab(LANES)]
                 + [_full_spec(a, 2) for a in consts2],
        out_specs=[pl.BlockSpec((1, MLA_HEADS, tl, MLA_QW), lambda b, i: (b, 0, i, 0)),
                   rowb(MLA_QW), pl.BlockSpec((1, MLA_QW, tl), lambda b, i: (b, 0, i)),
                   rowb(MLA_KV_RANK), rowb(MLA_ROPE), rowb(GMLP_WIDTH), rowb(GMLP_WIDTH)],
        out_shape=[jax.ShapeDtypeStruct((bsz, MLA_HEADS, l, MLA_QW), BF16),
                   jax.ShapeDtypeStruct((bsz, l, MLA_QW), BF16),
                   jax.ShapeDtypeStruct((bsz, MLA_QW, l), BF16),
                   jax.ShapeDtypeStruct((bsz, l, MLA_KV_RANK), F32),
                   jax.ShapeDtypeStruct((bsz, l, MLA_ROPE), F32),
                   jax.ShapeDtypeStruct((bsz, l, GMLP_WIDTH), BF16),
                   jax.ShapeDtypeStruct((bsz, l, GMLP_WIDTH), F32)],
        compiler_params=_cp("parallel", "parallel"),
        name="odd_in",
    )(h3, *consts, cos_q, sin_q, cos_k, sin_k, *consts2)


def _attn_kernel(qi_ref, kj_ref, flag_ref, q_ref, k_ref, kt_ref, wuv_ref, o_ref,
                 m_scr, l_scr, acc_scr, *, tq, hpb, tk, pos0, n_keys):
    p = pl.program_id(1)
    flag = flag_ref[p]
    nblk = MLA_HEADS // hpb
    width = hpb * tq

    @pl.when((flag & 1) != 0)
    def _():
        m_scr[...] = jnp.full_like(m_scr, -jnp.inf)
        l_scr[...] = jnp.zeros_like(l_scr)
        acc_scr[...] = jnp.zeros_like(acc_scr)

    def step(hidden):
        kt = kt_ref[0]
        kk = k_ref[0]
        if hidden:
            qpos = pos0 + qi_ref[p] * tq + lax.broadcasted_iota(I32, (1, width), 1) % tq
            kpos = kj_ref[p] * tk + lax.broadcasted_iota(I32, (tk, 1), 0)
            visible = ((kpos // CHUNK) <= (qpos // CHUNK)) & (kpos < n_keys)
        for h in range(nblk):
            s = _dot_nt(kk, q_ref[0, h])
            if hidden:
                s = jnp.where(visible, s, -jnp.inf)
            m_old = m_scr[h]
            m_new = jnp.maximum(m_old, jnp.max(s, axis=0, keepdims=True))
            alpha = jnp.exp(m_old - m_new)
            pr = jnp.exp(s - m_new).astype(BF16)
            acc_scr[h] = alpha * acc_scr[h] + _dot(kt[:MLA_KV_RANK], pr)
            l_scr[h] = alpha * l_scr[h] + _dot(kt[MLA_ONE_LANE:MLA_ONE_LANE + 16], pr)
            m_scr[h] = m_new

    @pl.when((flag & 4) != 0)
    def _():
        step(True)

    @pl.when((flag & 4) == 0)
    def _():
        step(False)

    @pl.when((flag & 2) != 0)
    def _():
        head_of_lane = lax.broadcasted_iota(I32, (1, MLA_HEADS * MLA_V), 1) // MLA_V
        out = jnp.zeros((tq, MLA_HEADS * MLA_V), F32)
        for h in range(nblk):
            lat = (acc_scr[h] / l_scr[h, 0:1, :]).astype(BF16)
            full = _dot_tn(lat, wuv_ref[...])
            for hh in range(hpb):
                out = out + jnp.where(head_of_lane == h * hpb + hh, full[hh * tq:(hh + 1) * tq], 0.0)
        o_ref[0] = out.astype(BF16)


def _attn_pairs(l, tq, tk, pos0, n_keys):
    qi, kj, flag = [], [], []
    for i in range(l // tq):
        q_first = pos0 + i * tq
        q_last = q_first + tq - 1
        vis = min(CHUNK * (q_last // CHUNK) + CHUNK - 1, n_keys - 1)
        nj = vis // tk + 1
        for j in range(nj):
            hidden = ((j + 1) * tk - 1) // CHUNK > q_first // CHUNK or (j + 1) * tk > n_keys
            qi.append(i); kj.append(j)
            flag.append(int(j == 0) + 2 * int(j == nj - 1) + 4 * int(hidden))
    return [jnp.asarray(np.array(a, np.int32)) for a in (qi, kj, flag)]


def _attention(q4, kc3, kt3, w_uv, pos0, n_keys, tq, tk):
    bsz, _, l, _ = q4.shape
    pairs = _attn_pairs(l, tq, tk, pos0, n_keys)
    npairs = int(pairs[0].shape[0])
    hpb = max(1, min(MLA_HEADS, 256 // tq)) if l == tq else 1
    nblk = MLA_HEADS // hpb
    width = hpb * tq
    q4 = q4.reshape(bsz, nblk, hpb * l, MLA_QW)
    return pl.pallas_call(
        functools.partial(_attn_kernel, tq=tq, hpb=hpb, tk=tk, pos0=pos0, n_keys=n_keys),
        grid_spec=pltpu.PrefetchScalarGridSpec(
            num_scalar_prefetch=3,
            grid=(bsz, npairs),
            in_specs=[pl.BlockSpec((1, nblk, width, MLA_QW), lambda b, p, qi, kj, f: (b, 0, qi[p], 0)),
                      pl.BlockSpec((1, tk, MLA_QW), lambda b, p, qi, kj, f: (b, kj[p], 0)),
                      pl.BlockSpec((1, MLA_QW, tk), lambda b, p, qi, kj, f: (b, 0, kj[p])),
                      pl.BlockSpec(w_uv.shape, lambda b, p, qi, kj, f: (0, 0))],
            out_specs=pl.BlockSpec((1, tq, MLA_HEADS * MLA_V), lambda b, p, qi, kj, f: (b, qi[p], 0)),
            scratch_shapes=[pltpu.VMEM((nblk, 1, width), F32),
                            pltpu.VMEM((nblk, 16, width), F32),
                            pltpu.VMEM((nblk, MLA_KV_RANK, width), F32)]),
        out_shape=jax.ShapeDtypeStruct((bsz, l, MLA_HEADS * MLA_V), BF16),
        compiler_params=_cp("parallel", "arbitrary"),
        name="mla_attention",
    )(*pairs, q4, kc3, kt3, w_uv)


def _rope_tables(pos0, l, width):
    half = MLA_ROPE // 2
    inv = ROPE_THETA ** (-jnp.arange(half, dtype=F32) * 2.0 / MLA_ROPE)
    ang = (pos0 + jnp.arange(l)).astype(F32)[:, None] * inv[None, :]
    cos, sin = jnp.cos(ang), jnp.sin(ang)
    cos32 = jnp.concatenate([cos, cos], axis=1)
    sin32 = jnp.concatenate([-sin, sin], axis=1)
    reps = width // MLA_ROPE
    return jnp.tile(cos32, (1, reps)), jnp.tile(sin32, (1, reps))


def _prep_even(w_in, w_gate_up, b_gate, gla_norm_g, pool_w, pool_scale, w_out):
    o_r = 2 * GLA_QK + GLA_V
    o_g = o_r + GLA_V
    o_p = o_g + GLA_GATE_RANK
    w_main = jnp.concatenate([w_in[:, :o_g], w_in[:, o_p:]], axis=1).astype(BF16)
    w_g = jnp.pad(w_in[:, o_g:o_p], ((0, 0), (0, LANES - GLA_GATE_RANK))).astype(BF16)
    w_gu = jnp.pad(w_gate_up, ((0, LANES - GLA_GATE_RANK), (0, 0))).astype(BF16)
    return dict(w_main=w_main, w_g=w_g, w_gu=w_gu, b_g=b_gate.reshape(1, -1),
                gnorm=gla_norm_g.reshape(1, -1), pool_w=pool_w.astype(BF16),
                pool_scale=pool_scale.reshape(1, -1), w_out=w_out.astype(BF16))


def _prep_odd(w_in, q_norm_g, w_uq, kv_norm_g, w_uk, w_uv, gm_g, gm_b, gm_ws, gm_bs, w_out):
    o_ckv = MLA_Q_RANK
    o_kpe = o_ckv + MLA_KV_RANK
    o_u = o_kpe + MLA_ROPE
    w_in2 = jnp.concatenate([w_in[:, :o_kpe], w_in[:, o_u:], w_in[:, o_kpe:o_u],
                             jnp.zeros((D_MODEL, LANES - MLA_ROPE), F32)], axis=1).astype(BF16)
    uq = w_uq.reshape(MLA_Q_RANK, MLA_HEADS, MLA_NOPE + MLA_ROPE)
    w_qn = jnp.pad(uq[:, :, :MLA_NOPE], ((0, 0), (0, 0), (0, LANES - MLA_NOPE)))
    w_qn = w_qn.reshape(MLA_Q_RANK, MLA_HEADS * LANES).astype(BF16)
    w_qp = uq[:, :, MLA_NOPE:].reshape(MLA_Q_RANK, MLA_HEADS * MLA_ROPE).astype(BF16)
    uk = w_uk.reshape(MLA_KV_RANK, MLA_HEADS, MLA_NOPE).transpose(1, 2, 0)
    w_ukp = jnp.pad(uk, ((0, 0), (0, LANES - MLA_NOPE), (0, 0))).astype(BF16)
    src = np.arange(MLA_HEADS * MLA_ROPE)
    perm = np.zeros((MLA_HEADS * MLA_ROPE, MLA_HEADS * LANES), np.float32)
    perm[src, (src // MLA_ROPE) * LANES + src % MLA_ROPE] = 1.0
    return dict(w_in=w_in2, q_g=q_norm_g.reshape(1, -1), w_qn=w_qn, w_qp=w_qp, w_uk=w_ukp,
                perm=jnp.asarray(perm, BF16), kv_g=kv_norm_g.reshape(1, -1), w_uv=w_uv.astype(BF16),
                gm_g=gm_g.reshape(1, -1), gm_b=gm_b.reshape(1, -1), gm_ws=gm_ws, gm_bs=gm_bs,
                w_out=w_out.astype(BF16))


def _prep_route(wg, bg, we, be):
    w_r = jnp.pad(jnp.concatenate([wg, we], axis=1), ((0, 0), (0, LANES - N_GROUPS - N_EXPERTS)))
    w_hi = w_r.astype(BF16)
    w_lo = (w_r - w_hi.astype(F32)).astype(BF16)
    b_r = jnp.pad(jnp.concatenate([bg, be]), (0, LANES - N_GROUPS - N_EXPERTS)).reshape(1, LANES)
    return jnp.concatenate([w_hi, w_lo], axis=1), b_r


def _even_mixer(h3, st0, hist, pos0, pw):
    bsz, l, _ = h3.shape
    q, k, la, v, r, xp = _even_in(h3.reshape(bsz * l, D_MODEL), pw['w_main'], pw['w_g'], pw['w_gu'], pw['b_g'])
    to3 = lambda a: a.reshape(bsz, l, a.shape[-1])
    o, st = _gla(to3(q), to3(k), to3(la), to3(v), to3(r), st0, pw['gnorm'])
    xp3 = to3(xp)
    hist16 = jnp.pad(hist, ((0, 0), (POOL_HALO - POOL_HIST, 0), (0, 0)))
    pooled = _pool(xp3, hist16, pw['pool_w'], pw['pool_scale'], pos0)
    hist_new = jnp.concatenate([hist, xp3[:, -POOL_HIST:].astype(F32)], axis=1)[:, -POOL_HIST:]
    return o, pooled, st, hist_new


def _odd_mixer(h3, ckv_past, kpe_past, pw):
    bsz, l, _ = h3.shape
    n_past = ckv_past.shape[1]
    cos_q, sin_q = _rope_tables(n_past, l, MLA_HEADS * MLA_ROPE)
    cos_k, sin_k = _rope_tables(n_past, l, MLA_ROPE)
    padk = ((0, 0), (0, LANES - MLA_ROPE))
    cos_k, sin_k = jnp.pad(cos_k, padk), jnp.pad(sin_k, padk)
    cl = min(l, GMLP_CHUNK)
    ws = jnp.tril(pw['gm_ws'][:, :cl, :cl]).astype(BF16)
    bs = jnp.repeat(pw['gm_bs'][:, :cl].T, GMLP_CH, axis=1)
    q4, kc, kt, ckv, kpe, gated, vn = _odd_in(h3, pw['w_in'], pw['q_g'], pw['w_qn'], pw['w_qp'], pw['w_uk'],
                                              pw['perm'], pw['kv_g'], cos_q, sin_q, cos_k, sin_k,
                                              pw['gm_g'], pw['gm_b'], ws, bs)
    n_keys = n_past + l
    tq, tk = min(l, 256), 512
    if n_past:
        past = jnp.concatenate([ckv_past, kpe_past, jnp.ones((bsz, n_past, 1), F32),
                                jnp.zeros((bsz, n_past, MLA_QW - MLA_ONE_LANE - 1), F32)], axis=2).astype(BF16)
        kc = jnp.concatenate([past, kc], axis=1)
        kt = jnp.concatenate([past.transpose(0, 2, 1), kt], axis=2)
    kc = jnp.pad(kc, ((0, 0), (0, -n_keys % tk), (0, 0)))
    kt = jnp.pad(kt, ((0, 0), (0, 0), (0, -n_keys % tk)))
    attn = _attention(q4, kc, kt, pw['w_uv'], n_past, n_keys, tq, tk)
    return attn, gated, ckv, kpe, vn


def _finish_layer(a3, b3, h3, w_out, lw):
    bsz, l, _ = h3.shape
    t = bsz * l
    h1, xs, rs, rg, n16 = _out_route(a3.reshape(t, -1), b3.reshape(t, -1), h3.reshape(t, D_MODEL), w_out,
                                     lw['ln_mix_g'], lw['ln_mix_b'], lw['w_r'], lw['b_r'])
    h2 = _moe(h1, xs, rs, rg, n16, lw['w1'], lw['w3'], lw['w2'], lw['ln_ffn_g'], lw['ln_ffn_b'], lw['out_dtype'])
    return h2.reshape(bsz, l, D_MODEL)


def kernel(x_prompt, x_sample, state_gla, state_pool, cache_mla_ckv, cache_mla_kpe, w_in_even, w_gate_up, b_gate, gla_norm_g, pool_w, pool_scale, w_out_even, w_in_odd, mla_q_norm_g, mla_w_uq, mla_kv_norm_g, mla_w_uk, mla_w_uv, gmlp_norm_g, gmlp_norm_b, gmlp_ws, gmlp_bs, w_out_odd, ln_mix_g, ln_mix_b, router_group_w, router_group_b, router_expert_w, router_expert_b, expert_w1, expert_w3, expert_w2, ln_ffn_g, ln_ffn_b):
    hp, hs = x_prompt, x_sample
    bp = hp.shape[0]
    past_len = cache_mla_ckv.shape[2]
    gla_p, gla_s, pool_p, pool_s = [], [], [], []
    ckv_p, ckv_s, kpe_p, kpe_s, gv_s = [], [], [], [], []

    def state_to_t(s):
        return s.transpose(0, 3, 1, 2).reshape(s.shape[0], GLA_DV, GLA_QK)

    def state_from_t(st):
        return st.reshape(st.shape[0], GLA_DV, GLA_HEADS, GLA_DK).transpose(0, 2, 3, 1)

    for layer in range(DEPTH):
        i = layer // 2
        w_r, b_r = _prep_route(router_group_w[layer], router_group_b[layer],
                               router_expert_w[layer], router_expert_b[layer])
        lw = dict(ln_mix_g=ln_mix_g[layer].reshape(1, -1), ln_mix_b=ln_mix_b[layer].reshape(1, -1),
                  ln_ffn_g=ln_ffn_g[layer].reshape(1, -1), ln_ffn_b=ln_ffn_b[layer].reshape(1, -1),
                  w_r=w_r, b_r=b_r, w1=expert_w1[layer], w3=expert_w3[layer], w2=expert_w2[layer],
                  out_dtype=F32 if layer == DEPTH - 1 else BF16)
        if layer % 2 == 0:
            pw = _prep_even(w_in_even[i], w_gate_up[i], b_gate[i], gla_norm_g[i], pool_w[i], pool_scale[i],
                            w_out_even[i])
            st0 = jnp.zeros((bp, GLA_DV, GLA_QK), F32)
            hist0 = jnp.zeros((bp, POOL_HIST, POOL_WIDTH), F32)
            op, pp, stp, histp = _even_mixer(hp, st0, hist0, 0, pw)
            os_, ps, sts, hists = _even_mixer(hs, state_to_t(state_gla[i]), state_pool[i], past_len, pw)
            gla_p.append(state_from_t(stp)); gla_s.append(state_from_t(sts))
            pool_p.append(histp); pool_s.append(hists)
            ap, bpj, as_, bsj = op, pp, os_, ps
        else:
            pw = _prep_odd(w_in_odd[i], mla_q_norm_g[i], mla_w_uq[i], mla_kv_norm_g[i], mla_w_uk[i], mla_w_uv[i],
                           gmlp_norm_g[i], gmlp_norm_b[i], gmlp_ws[i], gmlp_bs[i], w_out_odd[i])
            no_ckv = jnp.zeros((bp, 0, MLA_KV_RANK), F32)
            no_kpe = jnp.zeros((bp, 0, MLA_ROPE), F32)
            ap, bpj, cp, kp, _ = _odd_mixer(hp, no_ckv, no_kpe, pw)
            as_, bsj, cs, ks, vs = _odd_mixer(hs, cache_mla_ckv[i], cache_mla_kpe[i], pw)
            ckv_p.append(cp); ckv_s.append(cs); kpe_p.append(kp); kpe_s.append(ks); gv_s.append(vs)
        hp = _finish_layer(ap, bpj, hp, pw['w_out'], lw)
        hs = _finish_layer(as_, bsj, hs, pw['w_out'], lw)
    return (hp, hs, jnp.stack(gla_p), jnp.stack(gla_s), jnp.stack(pool_p), jnp.stack(pool_s),
            jnp.stack(ckv_p), jnp.stack(ckv_s), jnp.stack(kpe_p), jnp.stack(kpe_s), jnp.stack(gv_s))
```

```python
import functools

import numpy as np
import jax
import jax.numpy as jnp
from jax import lax
from jax.experimental import pallas as pl
from jax.experimental.pallas import tpu as pltpu

F32 = jnp.float32
BF16 = jnp.bfloat16
I32 = jnp.int32

D_MODEL = 1024
DEPTH = 2
CHUNK = 64
ALPHA = (2 * DEPTH) ** 0.25
LN_EPS = 1e-5

GLA_HEADS = 4
GLA_DV = 128
GLA_DK = 64
GLA_QK = GLA_HEADS * GLA_DK
GLA_V = GLA_HEADS * GLA_DV
GLA_GATE_RANK = 16
GLA_GATE_TAU = 16.0
GLA_SUB = 8

POOL_WIDTH = 512
POOL_CH = 128
POOL_WINDOWS = (2, 4, 8, 16)
POOL_HIST = 15
POOL_HALO = 16

MLA_HEADS = 8
MLA_NOPE = 64
MLA_ROPE = 32
MLA_V = 64
MLA_Q_RANK = 256
MLA_KV_RANK = 128
ROPE_THETA = 10000.0
MLA_SCALE = (MLA_NOPE + MLA_ROPE) ** -0.5
MLA_QW = 256
MLA_ONE_LANE = MLA_KV_RANK + MLA_ROPE
MLA_KEY_PARTS = 2
GMLP_WIDTH = 512
GMLP_CH = 128
GMLP_GROUPS = 4
GMLP_CHUNK = 128

N_GROUPS = 4
EXPERTS_PER_GROUP = 8
N_EXPERTS = 32
TOP_K = 2
D_EXPERT = 256
MOE_ROWS = 512
MOE_CHUNK = 16
MOE_BLOCK_CHUNKS = MOE_ROWS // MOE_CHUNK

LANES = 128
VMEM_LIMIT = 48 * 1024 * 1024


def _cp(*sem):
    return pltpu.CompilerParams(dimension_semantics=sem, vmem_limit_bytes=VMEM_LIMIT)


def _dot(a, b):
    return jnp.dot(a, b, preferred_element_type=F32)


def _dot_nt(a, b):
    return lax.dot_general(a, b, (((1,), (1,)), ((), ())), preferred_element_type=F32)


def _dot_tn(a, b):
    return lax.dot_general(a, b, (((0,), (0,)), ((), ())), preferred_element_type=F32)


def _split3(x):
    hi = x.astype(BF16)
    r1 = x - hi.astype(F32)
    mid = r1.astype(BF16)
    lo = (r1 - mid.astype(F32)).astype(BF16)
    return hi, mid, lo


def _layernorm(x, g, b):
    mu = jnp.mean(x, axis=-1, keepdims=True)
    xc = x - mu
    var = jnp.mean(xc * xc, axis=-1, keepdims=True)
    return xc * lax.rsqrt(var + LN_EPS) * g + b


def _gelu(x):
    return 0.5 * x * (1.0 + jnp.tanh(0.7978845608028654 * (x + 0.044715 * (x * x * x))))


def _sigmoid(x):
    return 1.0 / (1.0 + jnp.exp(-x))


def _full_spec(a, nargs):
    nd = a.ndim
    if nargs == 1:
        return pl.BlockSpec(a.shape, lambda i: (0,) * nd)
    return pl.BlockSpec(a.shape, lambda i, j: (0,) * nd)


def _even_in_kernel(x_ref, w_ref, wg_ref, wgu_ref, bg_ref,
                    q_ref, k_ref, la_ref, v_ref, r_ref, xp_ref):
    xb = x_ref[...].astype(BF16)
    z = _dot(xb, w_ref[...])
    q_ref[...] = (z[:, 0:GLA_QK] * (GLA_DK ** -0.5)).astype(BF16)
    k_ref[...] = z[:, GLA_QK:2 * GLA_QK].astype(BF16)
    v_ref[...] = z[:, 2 * GLA_QK:2 * GLA_QK + GLA_V].astype(BF16)
    r_ref[...] = z[:, 2 * GLA_QK + GLA_V:2 * GLA_QK + 2 * GLA_V].astype(BF16)
    xp_ref[...] = z[:, 2 * GLA_QK + 2 * GLA_V:].astype(BF16)
    g = _dot(xb, wg_ref[...])
    pre = _dot(g.astype(BF16), wgu_ref[...]) + bg_ref[...]
    logsig = jnp.minimum(pre, 0.0) - jnp.log(1.0 + jnp.exp(-jnp.abs(pre)))
    la_ref[...] = logsig * (1.0 / GLA_GATE_TAU)


def _even_in(x2, w_main, w_g, w_gu, b_g):
    t = x2.shape[0]
    tm = min(512, t)
    row = lambda n: pl.BlockSpec((tm, n), lambda i: (i, 0))
    widths = (GLA_QK, GLA_QK, GLA_QK, GLA_V, GLA_V, POOL_WIDTH)
    return pl.pallas_call(
        _even_in_kernel,
        grid=(t // tm,),
        in_specs=[row(D_MODEL)] + [_full_spec(a, 1) for a in (w_main, w_g, w_gu, b_g)],
        out_specs=[row(n) for n in widths],
        out_shape=[jax.ShapeDtypeStruct((t, n), F32 if i == 2 else BF16) for i, n in enumerate(widths)],
        compiler_params=_cp("parallel"),
        name="even_in",
    )(x2, w_main, w_g, w_gu, b_g)


def _gla_consts(c):
    n = c * GLA_SUB
    tri = np.tril(np.ones((c, c), np.float32))
    headsum = (np.arange(GLA_QK)[:, None] // GLA_DK == np.arange(GLA_V)[None, :] // GLA_DV).astype(np.float32)
    msel = (np.arange(n)[None, :] // GLA_SUB == np.arange(c)[:, None]).astype(np.float32)
    return [jnp.asarray(a, BF16) for a in (tri, headsum, msel)]


def _gla_chunk(q, k, la, v, st, c, tri, headsum, msel):
    lane = lax.broadcasted_iota(I32, (1, GLA_QK), 1)
    head_of_lane = lane // GLA_DK
    row = lax.broadcasted_iota(I32, (c, 1), 0)
    ii = lax.broadcasted_iota(I32, (c, c), 0)
    jj = lax.broadcasted_iota(I32, (c, c), 1)

    hi, mid, lo = _split3(la)
    b = _dot(tri, hi) + _dot(tri, mid) + _dot(tri, lo)

    head_masks = [head_of_lane == h for h in range(GLA_HEADS)]
    vb = v.astype(BF16)

    a_off = [jnp.zeros((c, c), F32) for _ in range(GLA_HEADS)]
    s = c // 2
    while s >= GLA_SUB:
        nblk = c // (2 * s)
        blk = row // (2 * s)
        right = ((row // s) % 2) == 1
        bref = jnp.zeros((c, GLA_QK), F32)
        for m in range(nblk):
            r0 = m * 2 * s + s - 1
            bref = jnp.where(blk == m, b[r0:r0 + 1, :], bref)
        qe = jnp.where(right, q * jnp.exp(jnp.minimum(b - bref, 0.0)), 0.0)
        ke = jnp.where(right, 0.0, k * jnp.exp(jnp.minimum(bref - b, 0.0))).astype(BF16)
        same = (ii // (2 * s)) == (jj // (2 * s))
        for h in range(GLA_HEADS):
            a = _dot_nt(jnp.where(head_masks[h], qe, 0.0).astype(BF16), ke)
            a_off[h] = a_off[h] + (a if nblk == 1 else jnp.where(same, a, 0.0))
        s //= 2

    nsb = c // GLA_SUB
    parts = []
    for i in range(nsb):
        sl = slice(i * GLA_SUB, (i + 1) * GLA_SUB)
        bi, qi, ki = b[sl], q[sl], k[sl]
        diff = bi[:, None, :] - bi[None, :, :]
        p = qi[:, None, :] * ki[None, :, :] * jnp.exp(jnp.minimum(diff, 0.0))
        parts.append(p.reshape(GLA_SUB * GLA_SUB, GLA_QK))
    n = nsb * GLA_SUB * GLA_SUB
    idx = lax.broadcasted_iota(I32, (n, 1), 0)
    causal = (idx % GLA_SUB) <= ((idx // GLA_SUB) % GLA_SUB)
    pcat = jnp.where(causal, jnp.concatenate(parts, axis=0), 0.0).astype(BF16)
    rsum = _dot(pcat, headsum)

    qb = q * jnp.exp(b)
    b_end = b[c - 1:c, :]
    kd = (k * jnp.exp(b_end - b)).astype(BF16)
    stb = st.astype(BF16)
    st_new = st * jnp.exp(b_end)

    outs = []
    for h in range(GLA_HEADS):
        vh = vb[:, h * GLA_DV:(h + 1) * GLA_DV]
        vf = v[:, h * GLA_DV:(h + 1) * GLA_DV]
        vt = jnp.concatenate(
            [jnp.broadcast_to(vf[i * GLA_SUB:(i + 1) * GLA_SUB][None], (GLA_SUB, GLA_SUB, GLA_DV))
             .reshape(GLA_SUB * GLA_SUB, GLA_DV) for i in range(nsb)], axis=0)
        xh = rsum[:, h * GLA_DV:(h + 1) * GLA_DV] * vt
        o = _dot(msel, xh.astype(BF16))
        o = o + _dot(a_off[h].astype(BF16), vh)
        o = o + _dot_nt(jnp.where(head_masks[h], qb, 0.0).astype(BF16), stb)
        outs.append(o)
        st_new = st_new + jnp.where(head_masks[h], _dot_tn(vh, kd), 0.0)
    return outs, st_new


def _gla_kernel(q_ref, k_ref, la_ref, v_ref, r_ref, st0_ref, g_ref, tri_ref, hs_ref, ms_ref,
                o_ref, st_ref, st_scr, *, c, nchunks):
    @pl.when(pl.program_id(1) == 0)
    def _():
        st_scr[...] = st0_ref[0]

    def body(ci, carry):
        r0 = pl.multiple_of(ci * c, c)
        rows = pl.ds(r0, c)
        outs, st_new = _gla_chunk(q_ref[0, rows, :].astype(F32), k_ref[0, rows, :].astype(F32),
                                  la_ref[0, rows, :], v_ref[0, rows, :].astype(F32), st_scr[...], c,
                                  tri_ref[...], hs_ref[...], ms_ref[...])
        st_scr[...] = st_new
        r = r_ref[0, rows, :].astype(F32)
        g = g_ref[...]
        for h in range(GLA_HEADS):
            o = outs[h]
            sl = slice(h * GLA_DV, (h + 1) * GLA_DV)
            on = o * lax.rsqrt(jnp.mean(o * o, axis=-1, keepdims=True) + LN_EPS) * g
            rh = r[:, sl]
            o_ref[0, rows, sl] = (on * (rh * _sigmoid(rh))).astype(BF16)
        return carry

    lax.fori_loop(0, nchunks, body, 0, unroll=4 if nchunks % 4 == 0 else 1)
    st_ref[0] = st_scr[...]


def _gla(q3, k3, la3, v3, r3, st0, gnorm):
    bsz, l, _ = q3.shape
    c = min(l, CHUNK)
    tl = min(l, 512)
    blk = lambda n: pl.BlockSpec((1, tl, n), lambda b, i: (b, i, 0))
    st_spec = pl.BlockSpec((1, GLA_DV, GLA_QK), lambda b, i: (b, 0, 0))
    consts = _gla_consts(c)
    return pl.pallas_call(
        functools.partial(_gla_kernel, c=c, nchunks=tl // c),
        grid=(bsz, l // tl),
        in_specs=[blk(GLA_QK), blk(GLA_QK), blk(GLA_QK), blk(GLA_V), blk(GLA_V), st_spec,
                  _full_spec(gnorm, 2)] + [_full_spec(a, 2) for a in consts],
        out_specs=[blk(GLA_V), st_spec],
        out_shape=[jax.ShapeDtypeStruct((bsz, l, GLA_V), BF16),
                   jax.ShapeDtypeStruct((bsz, GLA_DV, GLA_QK), F32)],
        scratch_shapes=[pltpu.VMEM((GLA_DV, GLA_QK), F32)],
        compiler_params=_cp("parallel", "arbitrary"),
        name="gla",
    )(q3, k3, la3, v3, r3, st0, gnorm, *consts)


def _pool_kernel(x_ref, halo_ref, hist_ref, w_ref, scale_ref, o_ref, *, tl, pos0):
    i = pl.program_id(1)
    x = x_ref[0].astype(F32)
    prev = jnp.where(i == 0, hist_ref[0], halo_ref[0].astype(F32))
    e = jnp.concatenate([prev, x], axis=0)
    t = i * tl + lax.broadcasted_iota(I32, (tl, 1), 0)
    pos = pos0 + t
    sums = []
    shift = 1
    for g, w in enumerate(POOL_WINDOWS):
        e = e[:, POOL_CH:] if g > 0 else e
        while shift < w:
            e = e[shift:] + e[:-shift]
            shift *= 2
        off = POOL_HALO - (w - 1)
        sums.append(e[off:off + tl, :POOL_CH])
    outs = []
    for g, w in enumerate(POOL_WINDOWS):
        cnt = jnp.minimum(pos + 1, w).astype(F32)
        mix = sums[g] / cnt - x[:, g * POOL_CH:(g + 1) * POOL_CH]
        outs.append(_dot(mix.astype(BF16), w_ref[g]))
    o_ref[0] = (jnp.concatenate(outs, axis=1) * scale_ref[...]).astype(BF16)


def _pool(xp3, hist16, pool_w, pool_scale, pos0):
    bsz, l, _ = xp3.shape
    tl = min(l, 512)
    per = tl // POOL_HALO
    return pl.pallas_call(
        functools.partial(_pool_kernel, tl=tl, pos0=pos0),
        grid=(bsz, l // tl),
        in_specs=[pl.BlockSpec((1, tl, POOL_WIDTH), lambda b, i: (b, i, 0)),
                  pl.BlockSpec((1, POOL_HALO, POOL_WIDTH), lambda b, i: (b, jnp.maximum(i * per - 1, 0), 0)),
                  pl.BlockSpec((1, POOL_HALO, POOL_WIDTH), lambda b, i: (b, 0, 0)),
                  _full_spec(pool_w, 2), _full_spec(pool_scale, 2)],
        out_specs=pl.BlockSpec((1, tl, POOL_WIDTH), lambda b, i: (b, i, 0)),
        out_shape=jax.ShapeDtypeStruct((bsz, l, POOL_WIDTH), BF16),
        compiler_params=_cp("parallel", "parallel"),
        name="pool",
    )(xp3, xp3, hist16, pool_w, pool_scale)


def _out_route_kernel(a_ref, b_ref, h_ref, w_ref, g_ref, bt_ref, wr_ref, br_ref,
                      h1_ref, xs_ref, rs_ref, rg_ref, n16_ref, *, tm, slots):
    half = w_ref.shape[0] // 2
    y = _dot(a_ref[...].astype(BF16), w_ref[0:half, :]) + _dot(b_ref[...].astype(BF16), w_ref[half:, :])
    x = _layernorm(ALPHA * h_ref[...].astype(F32) + y, g_ref[...], bt_ref[...])
    h1_ref[...] = x.astype(BF16)

    xh = x.astype(BF16)
    xl = (x - xh.astype(F32)).astype(BF16)
    hl = _dot(xh, wr_ref[...])
    logits = hl[:, :LANES] + hl[:, LANES:] + _dot(xl, wr_ref[:, 0:LANES]) + br_ref[...]

    lane = lax.broadcasted_iota(I32, (tm, LANES), 1)
    lanef = lane.astype(F32)
    neg = -jnp.inf
    big = jnp.float32(1 << 20)

    def first_lane(hit):
        return jnp.min(jnp.where(hit, lanef, big), axis=-1, keepdims=True).astype(I32)

    gl = jnp.where(lane < N_GROUPS, logits, neg)
    gmax = jnp.max(gl, axis=-1, keepdims=True)
    g_sel = first_lane(gl == gmax)
    g_prob = 1.0 / jnp.sum(jnp.exp(gl - gmax), axis=-1, keepdims=True)
    eidx = lane - N_GROUPS
    in_grp = (eidx >= 0) & (eidx < N_EXPERTS) & ((eidx // EXPERTS_PER_GROUP) == g_sel)
    el = jnp.where(in_grp, logits, neg)
    v1 = jnp.max(el, axis=-1, keepdims=True)
    i1 = first_lane(el == v1)
    el2 = jnp.where(lane == i1, neg, el)
    v2 = jnp.max(el2, axis=-1, keepdims=True)
    i2 = first_lane(el2 == v2)
    e21 = jnp.exp(v2 - v1)
    gate1 = g_prob / (1.0 + e21)
    gate2 = g_prob * e21 / (1.0 + e21)
    e1 = i1 - N_GROUPS
    e2 = i2 - N_GROUPS

    oh1 = lane == e1
    oh2 = lane == e2
    oh = oh1.astype(F32) + oh2.astype(F32)
    ti = lax.broadcasted_iota(I32, (tm, tm), 0)
    tj = lax.broadcasted_iota(I32, (tm, tm), 1)
    before = _dot((tj < ti).astype(BF16), oh.astype(BF16))
    cnt = jnp.sum(oh, axis=0, keepdims=True)
    n16 = jnp.floor((cnt + (MOE_CHUNK - 1)) * (1.0 / MOE_CHUNK))
    n16_8 = jnp.broadcast_to(n16, (8, LANES))
    ui = lax.broadcasted_iota(I32, (LANES, LANES), 0)
    uj = lax.broadcasted_iota(I32, (LANES, LANES), 1)
    run_start = _dot(n16_8.astype(BF16), (ui < uj).astype(BF16))[0:1]
    slot_of = MOE_CHUNK * run_start + before
    slot1 = jnp.sum(jnp.where(oh1, slot_of, 0.0), axis=-1, keepdims=True).astype(I32)
    slot2 = jnp.sum(jnp.where(oh2, slot_of, 0.0), axis=-1, keepdims=True).astype(I32)
    sl = lax.broadcasted_iota(I32, (tm, slots), 1)
    place = ((sl == slot1) | (sl == slot2)).astype(BF16)
    xs_ref[...] = _dot_tn(place, xh).astype(BF16)

    rs_ref[...] = jnp.where(lane == 0, slot1, jnp.where(lane == 1, slot2, 0))
    rg_ref[...] = jnp.where(lane == 0, gate1, jnp.where(lane == 1, gate2, 0.0))
    n16_ref[...] = n16_8.astype(I32)


def _moe_slots(tm):
    worst = tm * TOP_K + N_EXPERTS * (MOE_CHUNK - 1)
    return -(-worst // MOE_ROWS) * MOE_ROWS


def _out_route(a2, b2, h2, w_out, ln_g, ln_b, w_r, b_r):
    t = h2.shape[0]
    tm = min(512, t)
    nt = t // tm
    slots = _moe_slots(tm)
    row = lambda n: pl.BlockSpec((tm, n), lambda i: (i, 0))
    return pl.pallas_call(
        functools.partial(_out_route_kernel, tm=tm, slots=slots),
        grid=(nt,),
        in_specs=[row(a2.shape[1]), row(b2.shape[1]), row(D_MODEL)]
                 + [_full_spec(a, 1) for a in (w_out, ln_g, ln_b, w_r, b_r)],
        out_specs=[row(D_MODEL), pl.BlockSpec((slots, D_MODEL), lambda i: (i, 0)), row(LANES), row(LANES),
                   pl.BlockSpec((8, LANES), lambda i: (i, 0))],
        out_shape=[jax.ShapeDtypeStruct((t, D_MODEL), BF16), jax.ShapeDtypeStruct((nt * slots, D_MODEL), BF16),
                   jax.ShapeDtypeStruct((t, LANES), I32), jax.ShapeDtypeStruct((t, LANES), F32),
                   jax.ShapeDtypeStruct((nt * 8, LANES), I32)],
        compiler_params=_cp("parallel"),
        name="out_route",
    )(a2, b2, h2, w_out, ln_g, ln_b, w_r, b_r)


def _chunk_rows(chunk):
    return pl.ds(pl.multiple_of(chunk * MOE_CHUNK, MOE_CHUNK), MOE_CHUNK)


def _expert_kernel(src_ref, nreal_ref, be_ref, nu_ref, xs_ref, w1_ref, w3_ref, w2_ref, ys_ref,
                   xbuf, ybuf, gsem, ssem):
    del be_ref
    b = pl.program_id(0)
    nu = nu_ref[0]

    def gather(blk, slot, j):
        return pltpu.make_async_copy(xs_ref.at[_chunk_rows(src_ref[blk * MOE_BLOCK_CHUNKS + j]), :],
                                     xbuf.at[slot, pl.ds(j * MOE_CHUNK, MOE_CHUNK), :], gsem.at[slot])

    def scatter(blk, slot, j):
        return pltpu.make_async_copy(ybuf.at[slot, _chunk_rows(j), :],
                                     ys_ref.at[_chunk_rows(src_ref[blk * MOE_BLOCK_CHUNKS + j]), :],
                                     ssem.at[slot])

    def start_gather(blk, slot):
        for j in range(MOE_BLOCK_CHUNKS):
            gather(blk, slot, j).start()

    def wait_gather(slot):
        pltpu.make_async_copy(xs_ref.at[pl.ds(0, MOE_ROWS), :], xbuf.at[slot], gsem.at[slot]).wait()

    def for_real_chunks(blk, fn):
        def body(j, carry):
            fn(j)
            return carry
        lax.fori_loop(0, nreal_ref[blk], body, 0)

    def wait_scatter(blk, slot):
        full = nreal_ref[blk] == MOE_BLOCK_CHUNKS

        @pl.when(full)
        def _():
            pltpu.make_async_copy(ybuf.at[slot], ys_ref.at[pl.ds(0, MOE_ROWS), :], ssem.at[slot]).wait()

        @pl.when(jnp.logical_not(full))
        def _():
            for_real_chunks(blk, lambda j: scatter(blk, slot, j).wait())

    @pl.when(b < nu)
    def _():
        slot = b % 2

        @pl.when(b == 0)
        def _():
            start_gather(b, slot)

        @pl.when(b + 1 < nu)
        def _():
            start_gather(b + 1, 1 - slot)

        wait_gather(slot)

        @pl.when(b >= 2)
        def _():
            wait_scatter(b - 2, slot)

        xb = xbuf[slot]
        a = _dot(xb, w1_ref[0, 0].astype(BF16))
        hid = a * _sigmoid(a) * _dot(xb, w3_ref[0, 0].astype(BF16))
        ybuf[slot] = _dot(hid.astype(BF16), w2_ref[0, 0].astype(BF16)).astype(BF16)
        for_real_chunks(b, lambda j: scatter(b, slot, j).start())

        @pl.when(b == nu - 1)
        def _():
            wait_scatter(b, slot)

            @pl.when(b >= 1)
            def _():
                wait_scatter(b - 1, 1 - slot)


def _experts(xs, src, nreal, block_e, n_used, w1, w3, w2, layer):
    nblk = src.shape[0] // MOE_BLOCK_CHUNKS
    wspec = lambda shape: pl.BlockSpec((1,) + shape, lambda i, src, nr, be, nu: (layer, be[i], 0, 0))
    return pl.pallas_call(
        _expert_kernel,
        grid_spec=pltpu.PrefetchScalarGridSpec(
            num_scalar_prefetch=4,
            grid=(nblk,),
            in_specs=[pl.BlockSpec(memory_space=pl.ANY),
                      wspec((1, D_MODEL, D_EXPERT)), wspec((1, D_MODEL, D_EXPERT)), wspec((1, D_EXPERT, D_MODEL))],
            out_specs=pl.BlockSpec(memory_space=pl.ANY),
            scratch_shapes=[pltpu.VMEM((2, MOE_ROWS, D_MODEL), BF16), pltpu.VMEM((2, MOE_ROWS, D_MODEL), BF16),
                            pltpu.SemaphoreType.DMA((2,)), pltpu.SemaphoreType.DMA((2,))]),
        out_shape=jax.ShapeDtypeStruct(xs.shape, xs.dtype),
        input_output_aliases={4: 0},
        compiler_params=_cp("arbitrary"),
        name="moe_experts",
    )(src, nreal, block_e, n_used, xs, w1, w3, w2)


def _combine_kernel(h_ref, rs_ref, rg_ref, ys_ref, g_ref, b_ref, o_ref, *, tm, slots):
    rs = rs_ref[...]
    rg = rg_ref[...]
    sl = lax.broadcasted_iota(I32, (tm, slots), 1)
    weight = (jnp.where(sl == rs[:, 0:1], rg[:, 0:1], 0.0)
              + jnp.where(sl == rs[:, 1:2], rg[:, 1:2], 0.0)).astype(BF16)
    y = _dot(weight, ys_ref[...])
    o_ref[...] = _layernorm(ALPHA * h_ref[...].astype(F32) + y, g_ref[...], b_ref[...]).astype(o_ref.dtype)


def _combine(h2, rs, rg, ys, ln_g, ln_b, out_dtype):
    t = h2.shape[0]
    tm = min(512, t)
    slots = ys.shape[0] // (t // tm)
    row = lambda n: pl.BlockSpec((tm, n), lambda i: (i, 0))
    return pl.pallas_call(
        functools.partial(_combine_kernel, tm=tm, slots=slots),
        grid=(t // tm,),
        in_specs=[row(D_MODEL), row(LANES), row(LANES), pl.BlockSpec((slots, D_MODEL), lambda i: (i, 0)),
                  _full_spec(ln_g, 1), _full_spec(ln_b, 1)],
        out_specs=row(D_MODEL),
        out_shape=jax.ShapeDtypeStruct((t, D_MODEL), out_dtype),
        compiler_params=_cp("parallel"),
        name="moe_combine",
    )(h2, rs, rg, ys, ln_g, ln_b)


def _moe(h1, xs, rs, rg, n16_rows, w1, w3, w2, layer, ln_g, ln_b, out_dtype):
    t = h1.shape[0]
    tm = min(512, t)
    nt = t // tm
    slots = xs.shape[0] // nt
    n16 = n16_rows[::8, :N_EXPERTS]
    per_e = jnp.sum(n16, axis=0)
    blocks_e = (per_e + MOE_BLOCK_CHUNKS - 1) // MOE_BLOCK_CHUNKS
    blk_end = jnp.cumsum(blocks_e)
    blk_start = blk_end - blocks_e
    n_used = blk_end[-1:].astype(I32)
    max_chunks = (t * TOP_K) // MOE_CHUNK + nt * N_EXPERTS
    nblk = max_chunks // MOE_BLOCK_CHUNKS + N_EXPERTS
    blk = jnp.arange(nblk, dtype=I32)
    block_e = jnp.minimum(jnp.sum(blk_end[None, :] <= blk[:, None], axis=1), N_EXPERTS - 1).astype(I32)
    of_e = block_e[:, None] == jnp.arange(N_EXPERTS, dtype=I32)[None, :]
    pick_e = lambda tab: jnp.sum(jnp.where(of_e, tab[None, :], 0), axis=1)
    pick_col = lambda tab: jnp.sum(jnp.where(of_e[:, None, :], tab[None, :, :], 0), axis=2)
    run_end = pick_col(jnp.cumsum(n16, axis=0))
    tile_off = pick_col(jnp.cumsum(n16, axis=1) - n16)
    k = ((blk - pick_e(blk_start)) * MOE_BLOCK_CHUNKS)[:, None] + jnp.arange(MOE_BLOCK_CHUNKS, dtype=I32)[None, :]
    real = (k < pick_e(per_e)[:, None]) & (blk < n_used[0])[:, None]
    done = run_end[:, None, :] <= k[:, :, None]
    tile = jnp.minimum(jnp.sum(done, axis=2), nt - 1)
    run_first = jnp.max(jnp.where(done, run_end[:, None, :], 0), axis=2)
    of_t = tile[:, :, None] == jnp.arange(nt, dtype=I32)[None, None, :]
    src = tile * (slots // MOE_CHUNK) + jnp.sum(jnp.where(of_t, tile_off[:, None, :], 0), axis=2) + (k - run_first)
    src = jnp.where(real, src, src[:, :1])
    src = jnp.where((blk < n_used[0])[:, None], src, 0).astype(I32).reshape(-1)
    nreal = jnp.sum(real, axis=1).astype(I32)
    ys = _experts(xs, src, nreal, block_e, n_used, w1, w3, w2, layer)
    return _combine(h1, rs, rg, ys, ln_g, ln_b, out_dtype)


def _swap_halves(x):
    lane = lax.broadcasted_iota(I32, x.shape, 1)
    first = (lane % MLA_ROPE) < (MLA_ROPE // 2)
    return jnp.where(first, pltpu.roll(x, LANES - MLA_ROPE // 2, 1), pltpu.roll(x, MLA_ROPE // 2, 1))


def _rope(x, cos, sin):
    parts = []
    for t in range(x.shape[1] // LANES):
        sl = slice(t * LANES, (t + 1) * LANES)
        parts.append(x[:, sl] * cos[:, sl] + _swap_halves(x[:, sl]) * sin[:, sl])
    return parts[0] if len(parts) == 1 else jnp.concatenate(parts, axis=1)


def _odd_in_kernel(h_ref, w_ref, qg_ref, wqn_ref, wqp_ref, wuk_ref, perm_ref, kvg_ref,
                   cq_ref, sq_ref, ck_ref, sk_ref, gg_ref, gb_ref, ws_ref, bs_ref,
                   q_ref, kc_ref, kt_ref, ckv_ref, kpe_ref, gated_ref, vn_ref, *, tl, cl):
    hb = h_ref[0].astype(BF16)
    z = _dot(hb, w_ref[...])
    o_ckv = MLA_Q_RANK
    o_u = o_ckv + MLA_KV_RANK
    o_v = o_u + GMLP_WIDTH
    o_k = o_v + GMLP_WIDTH
    cq = z[:, :MLA_Q_RANK]
    cqn = cq * lax.rsqrt(jnp.mean(cq * cq, axis=-1, keepdims=True) + LN_EPS) * qg_ref[...]
    cqb = cqn.astype(BF16)
    qn = _dot(cqb, wqn_ref[...])
    qp = _dot(cqb, wqp_ref[...])
    qp = _rope(qp, cq_ref[...], sq_ref[...])
    qpe = _dot((qp * MLA_SCALE).astype(BF16), perm_ref[...])
    for h in range(MLA_HEADS):
        sl = slice(h * LANES, (h + 1) * LANES)
        qa = _dot((qn[:, sl] * MLA_SCALE).astype(BF16), wuk_ref[h])
        q_ref[0, h, :, 0:LANES] = qa.astype(BF16)
        q_ref[0, h, :, LANES:] = qpe[:, sl].astype(BF16)

    ckv = z[:, o_ckv:o_u]
    ckvn = ckv * lax.rsqrt(jnp.mean(ckv * ckv, axis=-1, keepdims=True) + LN_EPS) * kvg_ref[...]
    kp = z[:, o_k:]
    kp = _rope(kp, ck_ref[...], sk_ref[...])
    ckv_ref[0] = ckvn
    kpe_ref[0] = kp[:, :MLA_ROPE]
    one = (lax.broadcasted_iota(I32, (1, LANES), 1) == MLA_ONE_LANE - LANES).astype(F32)
    kp1 = kp + one
    kc_ref[0, :, 0:LANES] = ckvn.astype(BF16)
    kc_ref[0, :, LANES:] = kp1.astype(BF16)
    kt_ref[0, 0:LANES, :] = ckvn.T.astype(BF16)
    kt_ref[0, LANES:, :] = kp1.T.astype(BF16)

    gu = _gelu(z[:, o_u:o_v])
    vn = _layernorm(_gelu(z[:, o_v:o_k]), gg_ref[...], gb_ref[...])
    vn_ref[0] = vn
    vnb = vn.astype(BF16)
    for n in range(tl // cl):
        rs = slice(n * cl, (n + 1) * cl)
        for g in range(GMLP_GROUPS):
            ls = slice(g * GMLP_CH, (g + 1) * GMLP_CH)
            sg = _dot(ws_ref[g], vnb[rs, ls]) + bs_ref[:, ls]
            gated_ref[0, rs, ls] = (gu[rs, ls] * sg).astype(BF16)


def _odd_in(h3, w_in, q_g, w_qn, w_qp, w_uk, perm, kv_g, cos_q, sin_q, cos_k, sin_k,
            gm_g, gm_b, ws, bs):
    bsz, l, _ = h3.shape
    tl = min(l, 512)
    cl = min(l, GMLP_CHUNK)
    rowb = lambda n: pl.BlockSpec((1, tl, n), lambda b, i: (b, i, 0))
    tab = lambda n: pl.BlockSpec((tl, n), lambda b, i: (i, 0))
    consts = (w_in, q_g, w_qn, w_qp, w_uk, perm, kv_g)
    consts2 = (gm_g, gm_b, ws, bs)
    return pl.pallas_call(
        functools.partial(_odd_in_kernel, tl=tl, cl=cl),
        grid=(bsz, l // tl),
        in_specs=[rowb(D_MODEL)] + [_full_spec(a, 2) for a in consts]
                 + [tab(MLA_HEADS * MLA_ROPE), tab(MLA_HEADS * MLA_ROPE), tab(LANES), tab(LANES)]
                 + [_full_spec(a, 2) for a in consts2],
        out_specs=[pl.BlockSpec((1, MLA_HEADS, tl, MLA_QW), lambda b, i: (b, 0, i, 0)),
                   rowb(MLA_QW), pl.BlockSpec((1, MLA_QW, tl), lambda b, i: (b, 0, i)),
                   rowb(MLA_KV_RANK), rowb(MLA_ROPE), rowb(GMLP_WIDTH), rowb(GMLP_WIDTH)],
        out_shape=[jax.ShapeDtypeStruct((bsz, MLA_HEADS, l, MLA_QW), BF16),
                   jax.ShapeDtypeStruct((bsz, l, MLA_QW), BF16),
                   jax.ShapeDtypeStruct((bsz, MLA_QW, l), BF16),
                   jax.ShapeDtypeStruct((bsz, l, MLA_KV_RANK), F32),
                   jax.ShapeDtypeStruct((bsz, l, MLA_ROPE), F32),
                   jax.ShapeDtypeStruct((bsz, l, GMLP_WIDTH), BF16),
                   jax.ShapeDtypeStruct((bsz, l, GMLP_WIDTH), F32)],
        compiler_params=_cp("parallel", "parallel"),
        name="odd_in",
    )(h3, *consts, cos_q, sin_q, cos_k, sin_k, *consts2)


def _attn_kernel(qi_ref, kj_ref, flag_ref, q_ref, k_ref, kt_ref, wuv_ref, o_ref,
                 m_scr, l_scr, acc_scr, *, tq, hpb, tk, pos0, n_keys):
    p = pl.program_id(1)
    flag = flag_ref[p]
    nblk = MLA_HEADS // hpb
    width = hpb * tq

    @pl.when((flag & 1) != 0)
    def _():
        m_scr[...] = jnp.full_like(m_scr, -jnp.inf)
        l_scr[...] = jnp.zeros_like(l_scr)
        acc_scr[...] = jnp.zeros_like(acc_scr)

    def step(hidden):
        kt = kt_ref[0]
        kk = k_ref[0]
        if hidden:
            qpos = pos0 + qi_ref[p] * tq + lax.broadcasted_iota(I32, (1, width), 1) % tq
            kpos = kj_ref[p] * tk + lax.broadcasted_iota(I32, (tk, 1), 0)
            visible = ((kpos // CHUNK) <= (qpos // CHUNK)) & (kpos < n_keys)
        half = tk // MLA_KEY_PARTS
        for h in range(nblk):
            s = _dot_nt(kk, q_ref[0, h])
            if hidden:
                s = jnp.where(visible, s, -jnp.inf)
            m = m_scr[h]
            acc = acc_scr[h]
            l = l_scr[h]
            for part in range(MLA_KEY_PARTS):
                ks = slice(part * half, (part + 1) * half)
                sp = s[ks]
                m_new = jnp.maximum(m, jnp.max(sp, axis=0, keepdims=True))
                alpha = jnp.exp(m - m_new)
                pr = jnp.exp(sp - m_new).astype(BF16)
                acc = alpha * acc + _dot(kt[:MLA_KV_RANK, ks], pr)
                l = alpha * l + _dot(kt[MLA_ONE_LANE:MLA_ONE_LANE + 16, ks], pr)
                m = m_new
            acc_scr[h] = acc
            l_scr[h] = l
            m_scr[h] = m

    @pl.when((flag & 4) != 0)
    def _():
        step(True)

    @pl.when((flag & 4) == 0)
    def _():
        step(False)

    @pl.when((flag & 2) != 0)
    def _():
        head_of_lane = lax.broadcasted_iota(I32, (1, MLA_HEADS * MLA_V), 1) // MLA_V
        out = jnp.zeros((tq, MLA_HEADS * MLA_V), F32)
        for h in range(nblk):
            lat = (acc_scr[h] / l_scr[h, 0:1, :]).astype(BF16)
            full = _dot_tn(lat, wuv_ref[...])
            for hh in range(hpb):
                out = out + jnp.where(head_of_lane == h * hpb + hh, full[hh * tq:(hh + 1) * tq], 0.0)
        o_ref[0] = out.astype(BF16)


def _attn_pairs(l, tq, tk, pos0, n_keys):
    qi, kj, flag = [], [], []
    for i in range(l // tq):
        q_first = pos0 + i * tq
        q_last = q_first + tq - 1
        vis = min(CHUNK * (q_last // CHUNK) + CHUNK - 1, n_keys - 1)
        nj = vis // tk + 1
        for j in range(nj):
            hidden = ((j + 1) * tk - 1) // CHUNK > q_first // CHUNK or (j + 1) * tk > n_keys
            qi.append(i); kj.append(j)
            flag.append(int(j == 0) + 2 * int(j == nj - 1) + 4 * int(hidden))
    return [jnp.asarray(np.array(a, np.int32)) for a in (qi, kj, flag)]


def _attention(q4, kc3, kt3, w_uv, pos0, n_keys, tq, tk):
    bsz, _, l, _ = q4.shape
    pairs = _attn_pairs(l, tq, tk, pos0, n_keys)
    npairs = int(pairs[0].shape[0])
    hpb = max(1, min(MLA_HEADS, 256 // tq)) if l == tq else 1
    nblk = MLA_HEADS // hpb
    width = hpb * tq
    q4 = q4.reshape(bsz, nblk, hpb * l, MLA_QW)
    return pl.pallas_call(
        functools.partial(_attn_kernel, tq=tq, hpb=hpb, tk=tk, pos0=pos0, n_keys=n_keys),
        grid_spec=pltpu.PrefetchScalarGridSpec(
            num_scalar_prefetch=3,
            grid=(bsz, npairs),
            in_specs=[pl.BlockSpec((1, nblk, width, MLA_QW), lambda b, p, qi, kj, f: (b, 0, qi[p], 0)),
                      pl.BlockSpec((1, tk, MLA_QW), lambda b, p, qi, kj, f: (b, kj[p], 0)),
                      pl.BlockSpec((1, MLA_QW, tk), lambda b, p, qi, kj, f: (b, 0, kj[p])),
                      pl.BlockSpec(w_uv.shape, lambda b, p, qi, kj, f: (0, 0))],
            out_specs=pl.BlockSpec((1, tq, MLA_HEADS * MLA_V), lambda b, p, qi, kj, f: (b, qi[p], 0)),
            scratch_shapes=[pltpu.VMEM((nblk, 1, width), F32),
                            pltpu.VMEM((nblk, 16, width), F32),
                            pltpu.VMEM((nblk, MLA_KV_RANK, width), F32)]),
        out_shape=jax.ShapeDtypeStruct((bsz, l, MLA_HEADS * MLA_V), BF16),
        compiler_params=_cp("parallel", "arbitrary"),
        name="mla_attention",
    )(*pairs, q4, kc3, kt3, w_uv)


def _rope_tables(pos0, l, width):
    half = MLA_ROPE // 2
    inv = ROPE_THETA ** (-jnp.arange(half, dtype=F32) * 2.0 / MLA_ROPE)
    ang = (pos0 + jnp.arange(l)).astype(F32)[:, None] * inv[None, :]
    cos, sin = jnp.cos(ang), jnp.sin(ang)
    cos32 = jnp.concatenate([cos, cos], axis=1)
    sin32 = jnp.concatenate([-sin, sin], axis=1)
    reps = width // MLA_ROPE
    return jnp.tile(cos32, (1, reps)), jnp.tile(sin32, (1, reps))


def _prep_even(w_in, w_gate_up, b_gate, gla_norm_g, pool_w, pool_scale, w_out):
    o_r = 2 * GLA_QK + GLA_V
    o_g = o_r + GLA_V
    o_p = o_g + GLA_GATE_RANK
    w_main = jnp.concatenate([w_in[:, :o_g], w_in[:, o_p:]], axis=1).astype(BF16)
    w_g = jnp.pad(w_in[:, o_g:o_p], ((0, 0), (0, LANES - GLA_GATE_RANK))).astype(BF16)
    w_gu = jnp.pad(w_gate_up, ((0, LANES - GLA_GATE_RANK), (0, 0))).astype(BF16)
    return dict(w_main=w_main, w_g=w_g, w_gu=w_gu, b_g=b_gate.reshape(1, -1),
                gnorm=gla_norm_g.reshape(1, -1), pool_w=pool_w.astype(BF16),
                pool_scale=pool_scale.reshape(1, -1), w_out=w_out.astype(BF16))


def _prep_odd(w_in, q_norm_g, w_uq, kv_norm_g, w_uk, w_uv, gm_g, gm_b, gm_ws, gm_bs, w_out):
    o_ckv = MLA_Q_RANK
    o_kpe = o_ckv + MLA_KV_RANK
    o_u = o_kpe + MLA_ROPE
    w_in2 = jnp.concatenate([w_in[:, :o_kpe], w_in[:, o_u:], w_in[:, o_kpe:o_u],
                             jnp.zeros((D_MODEL, LANES - MLA_ROPE), F32)], axis=1).astype(BF16)
    uq = w_uq.reshape(MLA_Q_RANK, MLA_HEADS, MLA_NOPE + MLA_ROPE)
    w_qn = jnp.pad(uq[:, :, :MLA_NOPE], ((0, 0), (0, 0), (0, LANES - MLA_NOPE)))
    w_qn = w_qn.reshape(MLA_Q_RANK, MLA_HEADS * LANES).astype(BF16)
    w_qp = uq[:, :, MLA_NOPE:].reshape(MLA_Q_RANK, MLA_HEADS * MLA_ROPE).astype(BF16)
    uk = w_uk.reshape(MLA_KV_RANK, MLA_HEADS, MLA_NOPE).transpose(1, 2, 0)
    w_ukp = jnp.pad(uk, ((0, 0), (0, LANES - MLA_NOPE), (0, 0))).astype(BF16)
    src = np.arange(MLA_HEADS * MLA_ROPE)
    perm = np.zeros((MLA_HEADS * MLA_ROPE, MLA_HEADS * LANES), np.float32)
    perm[src, (src // MLA_ROPE) * LANES + src % MLA_ROPE] = 1.0
    return dict(w_in=w_in2, q_g=q_norm_g.reshape(1, -1), w_qn=w_qn, w_qp=w_qp, w_uk=w_ukp,
                perm=jnp.asarray(perm, BF16), kv_g=kv_norm_g.reshape(1, -1), w_uv=w_uv.astype(BF16),
                gm_g=gm_g.reshape(1, -1), gm_b=gm_b.reshape(1, -1), gm_ws=gm_ws, gm_bs=gm_bs,
                w_out=w_out.astype(BF16))


def _prep_route(wg, bg, we, be):
    w_r = jnp.pad(jnp.concatenate([wg, we], axis=1), ((0, 0), (0, LANES - N_GROUPS - N_EXPERTS)))
    w_hi = w_r.astype(BF16)
    w_lo = (w_r - w_hi.astype(F32)).astype(BF16)
    b_r = jnp.pad(jnp.concatenate([bg, be]), (0, LANES - N_GROUPS - N_EXPERTS)).reshape(1, LANES)
    return jnp.concatenate([w_hi, w_lo], axis=1), b_r


def _even_mixer(h3, st0, hist, pos0, pw):
    bsz, l, _ = h3.shape
    q, k, la, v, r, xp = _even_in(h3.reshape(bsz * l, D_MODEL), pw['w_main'], pw['w_g'], pw['w_gu'], pw['b_g'])
    to3 = lambda a: a.reshape(bsz, l, a.shape[-1])
    o, st = _gla(to3(q), to3(k), to3(la), to3(v), to3(r), st0, pw['gnorm'])
    xp3 = to3(xp)
    hist16 = jnp.pad(hist, ((0, 0), (POOL_HALO - POOL_HIST, 0), (0, 0)))
    pooled = _pool(xp3, hist16, pw['pool_w'], pw['pool_scale'], pos0)
    hist_new = jnp.concatenate([hist, xp3[:, -POOL_HIST:].astype(F32)], axis=1)[:, -POOL_HIST:]
    return o, pooled, st, hist_new


def _odd_mixer(h3, ckv_past, kpe_past, pw):
    bsz, l, _ = h3.shape
    n_past = ckv_past.shape[1]
    cos_q, sin_q = _rope_tables(n_past, l, MLA_HEADS * MLA_ROPE)
    cos_k, sin_k = _rope_tables(n_past, l, MLA_ROPE)
    padk = ((0, 0), (0, LANES - MLA_ROPE))
    cos_k, sin_k = jnp.pad(cos_k, padk), jnp.pad(sin_k, padk)
    cl = min(l, GMLP_CHUNK)
    ws = jnp.tril(pw['gm_ws'][:, :cl, :cl]).astype(BF16)
    bs = jnp.repeat(pw['gm_bs'][:, :cl].T, GMLP_CH, axis=1)
    q4, kc, kt, ckv, kpe, gated, vn = _odd_in(h3, pw['w_in'], pw['q_g'], pw['w_qn'], pw['w_qp'], pw['w_uk'],
                                              pw['perm'], pw['kv_g'], cos_q, sin_q, cos_k, sin_k,
                                              pw['gm_g'], pw['gm_b'], ws, bs)
    n_keys = n_past + l
    tq, tk = min(l, 256), 512
    if n_past:
        past = jnp.concatenate([ckv_past, kpe_past, jnp.ones((bsz, n_past, 1), F32),
                                jnp.zeros((bsz, n_past, MLA_QW - MLA_ONE_LANE - 1), F32)], axis=2).astype(BF16)
        kc = jnp.concatenate([past, kc], axis=1)
        kt = jnp.concatenate([past.transpose(0, 2, 1), kt], axis=2)
    kc = jnp.pad(kc, ((0, 0), (0, -n_keys % tk), (0, 0)))
    kt = jnp.pad(kt, ((0, 0), (0, 0), (0, -n_keys % tk)))
    attn = _attention(q4, kc, kt, pw['w_uv'], n_past, n_keys, tq, tk)
    return attn, gated, ckv, kpe, vn


def _finish_layer(a3, b3, h3, w_out, lw):
    bsz, l, _ = h3.shape
    t = bsz * l
    h1, xs, rs, rg, n16 = _out_route(a3.reshape(t, -1), b3.reshape(t, -1), h3.reshape(t, D_MODEL), w_out,
                                     lw['ln_mix_g'], lw['ln_mix_b'], lw['w_r'], lw['b_r'])
    h2 = _moe(h1, xs, rs, rg, n16, lw['w1'], lw['w3'], lw['w2'], lw['layer'], lw['ln_ffn_g'], lw['ln_ffn_b'],
              lw['out_dtype'])
    return h2.reshape(bsz, l, D_MODEL)


def kernel(x_prompt, x_sample, state_gla, state_pool, cache_mla_ckv, cache_mla_kpe, w_in_even, w_gate_up, b_gate, gla_norm_g, pool_w, pool_scale, w_out_even, w_in_odd, mla_q_norm_g, mla_w_uq, mla_kv_norm_g, mla_w_uk, mla_w_uv, gmlp_norm_g, gmlp_norm_b, gmlp_ws, gmlp_bs, w_out_odd, ln_mix_g, ln_mix_b, router_group_w, router_group_b, router_expert_w, router_expert_b, expert_w1, expert_w3, expert_w2, ln_ffn_g, ln_ffn_b):
    hp, hs = x_prompt, x_sample
    bp = hp.shape[0]
    past_len = cache_mla_ckv.shape[2]
    gla_p, gla_s, pool_p, pool_s = [], [], [], []
    ckv_p, ckv_s, kpe_p, kpe_s, gv_s = [], [], [], [], []

    def state_to_t(s):
        return s.transpose(0, 3, 1, 2).reshape(s.shape[0], GLA_DV, GLA_QK)

    def state_from_t(st):
        return st.reshape(st.shape[0], GLA_DV, GLA_HEADS, GLA_DK).transpose(0, 2, 3, 1)

    for layer in range(DEPTH):
        i = layer // 2
        w_r, b_r = _prep_route(router_group_w[layer], router_group_b[layer],
                               router_expert_w[layer], router_expert_b[layer])
        lw = dict(ln_mix_g=ln_mix_g[layer].reshape(1, -1), ln_mix_b=ln_mix_b[layer].reshape(1, -1),
                  ln_ffn_g=ln_ffn_g[layer].reshape(1, -1), ln_ffn_b=ln_ffn_b[layer].reshape(1, -1),
                  w_r=w_r, b_r=b_r, w1=expert_w1, w3=expert_w3, w2=expert_w2, layer=layer,
                  out_dtype=F32 if layer == DEPTH - 1 else BF16)
        if layer % 2 == 0:
            pw = _prep_even(w_in_even[i], w_gate_up[i], b_gate[i], gla_norm_g[i], pool_w[i], pool_scale[i],
                            w_out_even[i])
            st0 = jnp.zeros((bp, GLA_DV, GLA_QK), F32)
            hist0 = jnp.zeros((bp, POOL_HIST, POOL_WIDTH), F32)
            op, pp, stp, histp = _even_mixer(hp, st0, hist0, 0, pw)
            os_, ps, sts, hists = _even_mixer(hs, state_to_t(state_gla[i]), state_pool[i], past_len, pw)
            gla_p.append(state_from_t(stp)); gla_s.append(state_from_t(sts))
            pool_p.append(histp); pool_s.append(hists)
            ap, bpj, as_, bsj = op, pp, os_, ps
        else:
            pw = _prep_odd(w_in_odd[i], mla_q_norm_g[i], mla_w_uq[i], mla_kv_norm_g[i], mla_w_uk[i], mla_w_uv[i],
                           gmlp_norm_g[i], gmlp_norm_b[i], gmlp_ws[i], gmlp_bs[i], w_out_odd[i])
            no_ckv = jnp.zeros((bp, 0, MLA_KV_RANK), F32)
            no_kpe = jnp.zeros((bp, 0, MLA_ROPE), F32)
            ap, bpj, cp, kp, _ = _odd_mixer(hp, no_ckv, no_kpe, pw)
            as_, bsj, cs, ks, vs = _odd_mixer(hs, cache_mla_ckv[i], cache_mla_kpe[i], pw)
            ckv_p.append(cp); ckv_s.append(cs); kpe_p.append(kp); kpe_s.append(ks); gv_s.append(vs)
        hp = _finish_layer(ap, bpj, hp, pw['w_out'], lw)
        hs = _finish_layer(as_, bsj, hs, pw['w_out'], lw)
    return (hp, hs, jnp.stack(gla_p), jnp.stack(gla_s), jnp.stack(pool_p), jnp.stack(pool_s),
            jnp.stack(ckv_p), jnp.stack(ckv_s), jnp.stack(kpe_p), jnp.stack(kpe_s), jnp.stack(gv_s))
```

```python
import functools

import numpy as np
import jax
import jax.numpy as jnp
from jax import lax
from jax.experimental import pallas as pl
from jax.experimental.pallas import tpu as pltpu

F32 = jnp.float32
BF16 = jnp.bfloat16
I32 = jnp.int32

D_MODEL = 1024
DEPTH = 2
CHUNK = 64
ALPHA = (2 * DEPTH) ** 0.25
LN_EPS = 1e-5

GLA_HEADS = 4
GLA_DV = 128
GLA_DK = 64
GLA_QK = GLA_HEADS * GLA_DK
GLA_V = GLA_HEADS * GLA_DV
GLA_GATE_RANK = 16
GLA_GATE_TAU = 16.0
GLA_SUB = 8

POOL_WIDTH = 512
POOL_CH = 128
POOL_WINDOWS = (2, 4, 8, 16)
POOL_HIST = 15
POOL_HALO = 16

MLA_HEADS = 8
MLA_NOPE = 64
MLA_ROPE = 32
MLA_V = 64
MLA_Q_RANK = 256
MLA_KV_RANK = 128
ROPE_THETA = 10000.0
MLA_SCALE = (MLA_NOPE + MLA_ROPE) ** -0.5
MLA_QW = 256
MLA_ONE_LANE = MLA_KV_RANK + MLA_ROPE
MLA_KEY_PARTS = 2
GMLP_WIDTH = 512
GMLP_CH = 128
GMLP_GROUPS = 4
GMLP_CHUNK = 128

N_GROUPS = 4
EXPERTS_PER_GROUP = 8
N_EXPERTS = 32
TOP_K = 2
D_EXPERT = 256
MOE_ROWS = 512
MOE_CHUNK = 16
MOE_BLOCK_CHUNKS = MOE_ROWS // MOE_CHUNK

LANES = 128
VMEM_LIMIT = 48 * 1024 * 1024


def _cp(*sem):
    return pltpu.CompilerParams(dimension_semantics=sem, vmem_limit_bytes=VMEM_LIMIT)


def _dot(a, b):
    return jnp.dot(a, b, preferred_element_type=F32)


def _dot_nt(a, b):
    return lax.dot_general(a, b, (((1,), (1,)), ((), ())), preferred_element_type=F32)


def _dot_tn(a, b):
    return lax.dot_general(a, b, (((0,), (0,)), ((), ())), preferred_element_type=F32)


def _split3(x):
    hi = x.astype(BF16)
    r1 = x - hi.astype(F32)
    mid = r1.astype(BF16)
    lo = (r1 - mid.astype(F32)).astype(BF16)
    return hi, mid, lo


def _layernorm(x, g, b):
    mu = jnp.mean(x, axis=-1, keepdims=True)
    xc = x - mu
    var = jnp.mean(xc * xc, axis=-1, keepdims=True)
    return xc * lax.rsqrt(var + LN_EPS) * g + b


def _gelu(x):
    return 0.5 * x * (1.0 + jnp.tanh(0.7978845608028654 * (x + 0.044715 * (x * x * x))))


def _sigmoid(x):
    return 1.0 / (1.0 + jnp.exp(-x))


def _full_spec(a, nargs):
    nd = a.ndim
    if nargs == 1:
        return pl.BlockSpec(a.shape, lambda i: (0,) * nd)
    return pl.BlockSpec(a.shape, lambda i, j: (0,) * nd)


def _even_in_kernel(x_ref, w_ref, wg_ref, wgu_ref, bg_ref,
                    q_ref, k_ref, la_ref, v_ref, r_ref, xp_ref):
    xb = x_ref[...].astype(BF16)
    z = _dot(xb, w_ref[...])
    q_ref[...] = (z[:, 0:GLA_QK] * (GLA_DK ** -0.5)).astype(BF16)
    k_ref[...] = z[:, GLA_QK:2 * GLA_QK].astype(BF16)
    v_ref[...] = z[:, 2 * GLA_QK:2 * GLA_QK + GLA_V].astype(BF16)
    r_ref[...] = z[:, 2 * GLA_QK + GLA_V:2 * GLA_QK + 2 * GLA_V].astype(BF16)
    xp_ref[...] = z[:, 2 * GLA_QK + 2 * GLA_V:].astype(BF16)
    g = _dot(xb, wg_ref[...])
    pre = _dot(g.astype(BF16), wgu_ref[...]) + bg_ref[...]
    logsig = jnp.minimum(pre, 0.0) - jnp.log(1.0 + jnp.exp(-jnp.abs(pre)))
    la_ref[...] = logsig * (1.0 / GLA_GATE_TAU)


def _even_in(x2, w_main, w_g, w_gu, b_g):
    t = x2.shape[0]
    tm = min(512, t)
    row = lambda n: pl.BlockSpec((tm, n), lambda i: (i, 0))
    widths = (GLA_QK, GLA_QK, GLA_QK, GLA_V, GLA_V, POOL_WIDTH)
    return pl.pallas_call(
        _even_in_kernel,
        grid=(t // tm,),
        in_specs=[row(D_MODEL)] + [_full_spec(a, 1) for a in (w_main, w_g, w_gu, b_g)],
        out_specs=[row(n) for n in widths],
        out_shape=[jax.ShapeDtypeStruct((t, n), F32 if i == 2 else BF16) for i, n in enumerate(widths)],
        compiler_params=_cp("parallel"),
        name="even_in",
    )(x2, w_main, w_g, w_gu, b_g)


def _gla_consts(c):
    n = c * GLA_SUB
    tri = np.tril(np.ones((c, c), np.float32))
    headsum = (np.arange(GLA_QK)[:, None] // GLA_DK == np.arange(GLA_V)[None, :] // GLA_DV).astype(np.float32)
    msel = (np.arange(n)[None, :] // GLA_SUB == np.arange(c)[:, None]).astype(np.float32)
    return [jnp.asarray(a, BF16) for a in (tri, headsum, msel)]


def _gla_chunk(q, k, la, v, st, c, tri, headsum, msel):
    lane = lax.broadcasted_iota(I32, (1, GLA_QK), 1)
    head_of_lane = lane // GLA_DK
    row = lax.broadcasted_iota(I32, (c, 1), 0)
    ii = lax.broadcasted_iota(I32, (c, c), 0)
    jj = lax.broadcasted_iota(I32, (c, c), 1)

    hi, mid, lo = _split3(la)
    b = _dot(tri, hi) + _dot(tri, mid) + _dot(tri, lo)

    head_masks = [head_of_lane == h for h in range(GLA_HEADS)]
    vb = v.astype(BF16)

    a_off = [jnp.zeros((c, c), F32) for _ in range(GLA_HEADS)]
    s = c // 2
    while s >= GLA_SUB:
        nblk = c // (2 * s)
        blk = row // (2 * s)
        right = ((row // s) % 2) == 1
        bref = jnp.zeros((c, GLA_QK), F32)
        for m in range(nblk):
            r0 = m * 2 * s + s - 1
            bref = jnp.where(blk == m, b[r0:r0 + 1, :], bref)
        qe = jnp.where(right, q * jnp.exp(jnp.minimum(b - bref, 0.0)), 0.0)
        ke = jnp.where(right, 0.0, k * jnp.exp(jnp.minimum(bref - b, 0.0))).astype(BF16)
        same = (ii // (2 * s)) == (jj // (2 * s))
        for h in range(GLA_HEADS):
            a = _dot_nt(jnp.where(head_masks[h], qe, 0.0).astype(BF16), ke)
            a_off[h] = a_off[h] + (a if nblk == 1 else jnp.where(same, a, 0.0))
        s //= 2

    nsb = c // GLA_SUB
    parts = []
    for i in range(nsb):
        sl = slice(i * GLA_SUB, (i + 1) * GLA_SUB)
        bi, qi, ki = b[sl], q[sl], k[sl]
        diff = bi[:, None, :] - bi[None, :, :]
        p = qi[:, None, :] * ki[None, :, :] * jnp.exp(jnp.minimum(diff, 0.0))
        parts.append(p.reshape(GLA_SUB * GLA_SUB, GLA_QK))
    n = nsb * GLA_SUB * GLA_SUB
    idx = lax.broadcasted_iota(I32, (n, 1), 0)
    causal = (idx % GLA_SUB) <= ((idx // GLA_SUB) % GLA_SUB)
    pcat = jnp.where(causal, jnp.concatenate(parts, axis=0), 0.0).astype(BF16)
    rsum = _dot(pcat, headsum)

    qb = q * jnp.exp(b)
    b_end = b[c - 1:c, :]
    kd = (k * jnp.exp(b_end - b)).astype(BF16)
    stb = st.astype(BF16)
    st_new = st * jnp.exp(b_end)

    outs = []
    for h in range(GLA_HEADS):
        vh = vb[:, h * GLA_DV:(h + 1) * GLA_DV]
        vf = v[:, h * GLA_DV:(h + 1) * GLA_DV]
        vt = jnp.concatenate(
            [jnp.broadcast_to(vf[i * GLA_SUB:(i + 1) * GLA_SUB][None], (GLA_SUB, GLA_SUB, GLA_DV))
             .reshape(GLA_SUB * GLA_SUB, GLA_DV) for i in range(nsb)], axis=0)
        xh = rsum[:, h * GLA_DV:(h + 1) * GLA_DV] * vt
        o = _dot(msel, xh.astype(BF16))
        o = o + _dot(a_off[h].astype(BF16), vh)
        o = o + _dot_nt(jnp.where(head_masks[h], qb, 0.0).astype(BF16), stb)
        outs.append(o)
        st_new = st_new + jnp.where(head_masks[h], _dot_tn(vh, kd), 0.0)
    return outs, st_new


def _gla_kernel(q_ref, k_ref, la_ref, v_ref, r_ref, st0_ref, g_ref, tri_ref, hs_ref, ms_ref,
                o_ref, st_ref, st_scr, *, c, nchunks):
    @pl.when(pl.program_id(1) == 0)
    def _():
        st_scr[...] = st0_ref[0]

    def body(ci, carry):
        r0 = pl.multiple_of(ci * c, c)
        rows = pl.ds(r0, c)
        outs, st_new = _gla_chunk(q_ref[0, rows, :].astype(F32), k_ref[0, rows, :].astype(F32),
                                  la_ref[0, rows, :], v_ref[0, rows, :].astype(F32), st_scr[...], c,
                                  tri_ref[...], hs_ref[...], ms_ref[...])
        st_scr[...] = st_new
        r = r_ref[0, rows, :].astype(F32)
        g = g_ref[...]
        for h in range(GLA_HEADS):
            o = outs[h]
            sl = slice(h * GLA_DV, (h + 1) * GLA_DV)
            on = o * lax.rsqrt(jnp.mean(o * o, axis=-1, keepdims=True) + LN_EPS) * g
            rh = r[:, sl]
            o_ref[0, rows, sl] = (on * (rh * _sigmoid(rh))).astype(BF16)
        return carry

    lax.fori_loop(0, nchunks, body, 0, unroll=4 if nchunks % 4 == 0 else 1)
    st_ref[0] = st_scr[...]


def _gla(q3, k3, la3, v3, r3, st0, gnorm):
    bsz, l, _ = q3.shape
    c = min(l, CHUNK)
    tl = min(l, 512)
    blk = lambda n: pl.BlockSpec((1, tl, n), lambda b, i: (b, i, 0))
    st_spec = pl.BlockSpec((1, GLA_DV, GLA_QK), lambda b, i: (b, 0, 0))
    consts = _gla_consts(c)
    return pl.pallas_call(
        functools.partial(_gla_kernel, c=c, nchunks=tl // c),
        grid=(bsz, l // tl),
        in_specs=[blk(GLA_QK), blk(GLA_QK), blk(GLA_QK), blk(GLA_V), blk(GLA_V), st_spec,
                  _full_spec(gnorm, 2)] + [_full_spec(a, 2) for a in consts],
        out_specs=[blk(GLA_V), st_spec],
        out_shape=[jax.ShapeDtypeStruct((bsz, l, GLA_V), BF16),
                   jax.ShapeDtypeStruct((bsz, GLA_DV, GLA_QK), F32)],
        scratch_shapes=[pltpu.VMEM((GLA_DV, GLA_QK), F32)],
        compiler_params=_cp("parallel", "arbitrary"),
        name="gla",
    )(q3, k3, la3, v3, r3, st0, gnorm, *consts)


def _pool_kernel(x_ref, halo_ref, hist_ref, w_ref, scale_ref, o_ref, *, tl, pos0):
    i = pl.program_id(1)
    x = x_ref[0].astype(F32)
    prev = jnp.where(i == 0, hist_ref[0], halo_ref[0].astype(F32))
    e = jnp.concatenate([prev, x], axis=0)
    t = i * tl + lax.broadcasted_iota(I32, (tl, 1), 0)
    pos = pos0 + t
    sums = []
    shift = 1
    for g, w in enumerate(POOL_WINDOWS):
        e = e[:, POOL_CH:] if g > 0 else e
        while shift < w:
            e = e[shift:] + e[:-shift]
            shift *= 2
        off = POOL_HALO - (w - 1)
        sums.append(e[off:off + tl, :POOL_CH])
    outs = []
    for g, w in enumerate(POOL_WINDOWS):
        cnt = jnp.minimum(pos + 1, w).astype(F32)
        mix = sums[g] / cnt - x[:, g * POOL_CH:(g + 1) * POOL_CH]
        outs.append(_dot(mix.astype(BF16), w_ref[g]))
    o_ref[0] = (jnp.concatenate(outs, axis=1) * scale_ref[...]).astype(BF16)


def _pool(xp3, hist16, pool_w, pool_scale, pos0):
    bsz, l, _ = xp3.shape
    tl = min(l, 512)
    per = tl // POOL_HALO
    return pl.pallas_call(
        functools.partial(_pool_kernel, tl=tl, pos0=pos0),
        grid=(bsz, l // tl),
        in_specs=[pl.BlockSpec((1, tl, POOL_WIDTH), lambda b, i: (b, i, 0)),
                  pl.BlockSpec((1, POOL_HALO, POOL_WIDTH), lambda b, i: (b, jnp.maximum(i * per - 1, 0), 0)),
                  pl.BlockSpec((1, POOL_HALO, POOL_WIDTH), lambda b, i: (b, 0, 0)),
                  _full_spec(pool_w, 2), _full_spec(pool_scale, 2)],
        out_specs=pl.BlockSpec((1, tl, POOL_WIDTH), lambda b, i: (b, i, 0)),
        out_shape=jax.ShapeDtypeStruct((bsz, l, POOL_WIDTH), BF16),
        compiler_params=_cp("parallel", "parallel"),
        name="pool",
    )(xp3, xp3, hist16, pool_w, pool_scale)


def _out_route_kernel(a_ref, b_ref, h_ref, w_ref, g_ref, bt_ref, wr_ref, br_ref,
                      h1_ref, xs_ref, rs_ref, rg_ref, n16_ref, *, tm, slots):
    half = w_ref.shape[0] // 2
    y = _dot(a_ref[...].astype(BF16), w_ref[0:half, :]) + _dot(b_ref[...].astype(BF16), w_ref[half:, :])
    x = _layernorm(ALPHA * h_ref[...].astype(F32) + y, g_ref[...], bt_ref[...])
    h1_ref[...] = x.astype(BF16)

    xh = x.astype(BF16)
    xl = (x - xh.astype(F32)).astype(BF16)
    hl = _dot(xh, wr_ref[...])
    logits = hl[:, :LANES] + hl[:, LANES:] + _dot(xl, wr_ref[:, 0:LANES]) + br_ref[...]

    lane = lax.broadcasted_iota(I32, (tm, LANES), 1)
    lanef = lane.astype(F32)
    neg = -jnp.inf
    big = jnp.float32(1 << 20)

    def first_lane(hit):
        return jnp.min(jnp.where(hit, lanef, big), axis=-1, keepdims=True).astype(I32)

    gl = jnp.where(lane < N_GROUPS, logits, neg)
    gmax = jnp.max(gl, axis=-1, keepdims=True)
    g_sel = first_lane(gl == gmax)
    g_prob = 1.0 / jnp.sum(jnp.exp(gl - gmax), axis=-1, keepdims=True)
    eidx = lane - N_GROUPS
    in_grp = (eidx >= 0) & (eidx < N_EXPERTS) & ((eidx // EXPERTS_PER_GROUP) == g_sel)
    el = jnp.where(in_grp, logits, neg)
    v1 = jnp.max(el, axis=-1, keepdims=True)
    i1 = first_lane(el == v1)
    el2 = jnp.where(lane == i1, neg, el)
    v2 = jnp.max(el2, axis=-1, keepdims=True)
    i2 = first_lane(el2 == v2)
    e21 = jnp.exp(v2 - v1)
    gate1 = g_prob / (1.0 + e21)
    gate2 = g_prob * e21 / (1.0 + e21)
    e1 = i1 - N_GROUPS
    e2 = i2 - N_GROUPS

    oh1 = lane == e1
    oh2 = lane == e2
    oh = oh1.astype(F32) + oh2.astype(F32)
    ti = lax.broadcasted_iota(I32, (tm, tm), 0)
    tj = lax.broadcasted_iota(I32, (tm, tm), 1)
    before = _dot((tj < ti).astype(BF16), oh.astype(BF16))
    cnt = jnp.sum(oh, axis=0, keepdims=True)
    n16 = jnp.floor((cnt + (MOE_CHUNK - 1)) * (1.0 / MOE_CHUNK))
    n16_8 = jnp.broadcast_to(n16, (8, LANES))
    ui = lax.broadcasted_iota(I32, (LANES, LANES), 0)
    uj = lax.broadcasted_iota(I32, (LANES, LANES), 1)
    run_start = _dot(n16_8.astype(BF16), (ui < uj).astype(BF16))[0:1]
    slot_of = MOE_CHUNK * run_start + before
    slot1 = jnp.sum(jnp.where(oh1, slot_of, 0.0), axis=-1, keepdims=True).astype(I32)
    slot2 = jnp.sum(jnp.where(oh2, slot_of, 0.0), axis=-1, keepdims=True).astype(I32)
    sl = lax.broadcasted_iota(I32, (tm, slots), 1)
    place = ((sl == slot1) | (sl == slot2)).astype(BF16)
    xs_ref[...] = _dot_tn(place, xh).astype(BF16)

    rs_ref[...] = jnp.where(lane == 0, slot1, jnp.where(lane == 1, slot2, 0))
    rg_ref[...] = jnp.where(lane == 0, gate1, jnp.where(lane == 1, gate2, 0.0))
    n16_ref[...] = n16_8.astype(I32)


def _moe_slots(tm):
    worst = tm * TOP_K + N_EXPERTS * (MOE_CHUNK - 1)
    return -(-worst // MOE_ROWS) * MOE_ROWS


def _out_route(a2, b2, h2, w_out, ln_g, ln_b, w_r, b_r):
    t = h2.shape[0]
    tm = min(512, t)
    nt = t // tm
    slots = _moe_slots(tm)
    row = lambda n: pl.BlockSpec((tm, n), lambda i: (i, 0))
    return pl.pallas_call(
        functools.partial(_out_route_kernel, tm=tm, slots=slots),
        grid=(nt,),
        in_specs=[row(a2.shape[1]), row(b2.shape[1]), row(D_MODEL)]
                 + [_full_spec(a, 1) for a in (w_out, ln_g, ln_b, w_r, b_r)],
        out_specs=[row(D_MODEL), pl.BlockSpec((slots, D_MODEL), lambda i: (i, 0)), row(LANES), row(LANES),
                   pl.BlockSpec((8, LANES), lambda i: (i, 0))],
        out_shape=[jax.ShapeDtypeStruct((t, D_MODEL), BF16), jax.ShapeDtypeStruct((nt * slots, D_MODEL), BF16),
                   jax.ShapeDtypeStruct((t, LANES), I32), jax.ShapeDtypeStruct((t, LANES), F32),
                   jax.ShapeDtypeStruct((nt * 8, LANES), I32)],
        compiler_params=_cp("parallel"),
        name="out_route",
    )(a2, b2, h2, w_out, ln_g, ln_b, w_r, b_r)


def _chunk_rows(chunk):
    return pl.ds(pl.multiple_of(chunk * MOE_CHUNK, MOE_CHUNK), MOE_CHUNK)


def _expert_kernel(src_ref, nreal_ref, be_ref, nu_ref, xs_ref, w1_ref, w3_ref, w2_ref, ys_ref,
                   xbuf, ybuf, gsem, ssem, *, cpb):
    del be_ref
    b = pl.program_id(0)
    nu = nu_ref[0]

    rows = cpb * MOE_CHUNK

    def gather(blk, slot, j):
        return pltpu.make_async_copy(xs_ref.at[_chunk_rows(src_ref[blk * cpb + j]), :],
                                     xbuf.at[slot, pl.ds(j * MOE_CHUNK, MOE_CHUNK), :], gsem.at[slot])

    def scatter(blk, slot, j):
        return pltpu.make_async_copy(ybuf.at[slot, _chunk_rows(j), :],
                                     ys_ref.at[_chunk_rows(src_ref[blk * cpb + j]), :],
                                     ssem.at[slot])

    def start_gather(blk, slot):
        for j in range(cpb):
            gather(blk, slot, j).start()

    def wait_gather(slot):
        pltpu.make_async_copy(xs_ref.at[pl.ds(0, rows), :], xbuf.at[slot], gsem.at[slot]).wait()

    def for_real_chunks(blk, fn):
        def body(j, carry):
            fn(j)
            return carry
        lax.fori_loop(0, nreal_ref[blk], body, 0)

    def wait_scatter(blk, slot):
        full = nreal_ref[blk] == cpb

        @pl.when(full)
        def _():
            pltpu.make_async_copy(ybuf.at[slot], ys_ref.at[pl.ds(0, rows), :], ssem.at[slot]).wait()

        @pl.when(jnp.logical_not(full))
        def _():
            for_real_chunks(blk, lambda j: scatter(blk, slot, j).wait())

    @pl.when(b < nu)
    def _():
        slot = b % 2

        @pl.when(b == 0)
        def _():
            start_gather(b, slot)

        @pl.when(b + 1 < nu)
        def _():
            start_gather(b + 1, 1 - slot)

        wait_gather(slot)

        @pl.when(b >= 2)
        def _():
            wait_scatter(b - 2, slot)

        xb = xbuf[slot]
        a = _dot(xb, w1_ref[0, 0].astype(BF16))
        hid = a * _sigmoid(a) * _dot(xb, w3_ref[0, 0].astype(BF16))
        ybuf[slot] = _dot(hid.astype(BF16), w2_ref[0, 0].astype(BF16)).astype(BF16)
        for_real_chunks(b, lambda j: scatter(b, slot, j).start())

        @pl.when(b == nu - 1)
        def _():
            wait_scatter(b, slot)

            @pl.when(b >= 1)
            def _():
                wait_scatter(b - 1, 1 - slot)


def _experts(xs, src, nreal, block_e, n_used, w1, w3, w2, layer):
    nblk = nreal.shape[0]
    cpb = src.shape[0] // nblk
    rows = cpb * MOE_CHUNK
    wspec = lambda shape: pl.BlockSpec((1,) + shape, lambda i, src, nr, be, nu: (layer, be[i], 0, 0))
    return pl.pallas_call(
        functools.partial(_expert_kernel, cpb=cpb),
        grid_spec=pltpu.PrefetchScalarGridSpec(
            num_scalar_prefetch=4,
            grid=(nblk,),
            in_specs=[pl.BlockSpec(memory_space=pl.ANY),
                      wspec((1, D_MODEL, D_EXPERT)), wspec((1, D_MODEL, D_EXPERT)), wspec((1, D_EXPERT, D_MODEL))],
            out_specs=pl.BlockSpec(memory_space=pl.ANY),
            scratch_shapes=[pltpu.VMEM((2, rows, D_MODEL), BF16), pltpu.VMEM((2, rows, D_MODEL), BF16),
                            pltpu.SemaphoreType.DMA((2,)), pltpu.SemaphoreType.DMA((2,))]),
        out_shape=jax.ShapeDtypeStruct(xs.shape, xs.dtype),
        input_output_aliases={4: 0},
        compiler_params=_cp("arbitrary"),
        name="moe_experts",
    )(src, nreal, block_e, n_used, xs, w1, w3, w2)


def _combine_kernel(h_ref, rs_ref, rg_ref, ys_ref, g_ref, b_ref, o_ref, *, tm, slots):
    rs = rs_ref[...]
    rg = rg_ref[...]
    sl = lax.broadcasted_iota(I32, (tm, slots), 1)
    weight = (jnp.where(sl == rs[:, 0:1], rg[:, 0:1], 0.0)
              + jnp.where(sl == rs[:, 1:2], rg[:, 1:2], 0.0)).astype(BF16)
    y = _dot(weight, ys_ref[...])
    o_ref[...] = _layernorm(ALPHA * h_ref[...].astype(F32) + y, g_ref[...], b_ref[...]).astype(o_ref.dtype)


def _combine(h2, rs, rg, ys, ln_g, ln_b, out_dtype):
    t = h2.shape[0]
    tm = min(512, t)
    slots = ys.shape[0] // (t // tm)
    row = lambda n: pl.BlockSpec((tm, n), lambda i: (i, 0))
    return pl.pallas_call(
        functools.partial(_combine_kernel, tm=tm, slots=slots),
        grid=(t // tm,),
        in_specs=[row(D_MODEL), row(LANES), row(LANES), pl.BlockSpec((slots, D_MODEL), lambda i: (i, 0)),
                  _full_spec(ln_g, 1), _full_spec(ln_b, 1)],
        out_specs=row(D_MODEL),
        out_shape=jax.ShapeDtypeStruct((t, D_MODEL), out_dtype),
        compiler_params=_cp("parallel"),
        name="moe_combine",
    )(h2, rs, rg, ys, ln_g, ln_b)


def _moe(h1, xs, rs, rg, n16_rows, w1, w3, w2, layer, ln_g, ln_b, out_dtype):
    t = h1.shape[0]
    tm = min(512, t)
    nt = t // tm
    slots = xs.shape[0] // nt
    n16 = n16_rows[::8, :N_EXPERTS]
    cpb = max(4, min(MOE_BLOCK_CHUNKS, (t * TOP_K) // (N_EXPERTS * MOE_CHUNK)))
    per_e = jnp.sum(n16, axis=0)
    blocks_e = (per_e + cpb - 1) // cpb
    blk_end = jnp.cumsum(blocks_e)
    blk_start = blk_end - blocks_e
    n_used = blk_end[-1:].astype(I32)
    max_chunks = (t * TOP_K) // MOE_CHUNK + nt * N_EXPERTS
    nblk = max_chunks // cpb + N_EXPERTS
    blk = jnp.arange(nblk, dtype=I32)
    block_e = jnp.minimum(jnp.sum(blk_end[None, :] <= blk[:, None], axis=1), N_EXPERTS - 1).astype(I32)
    of_e = block_e[:, None] == jnp.arange(N_EXPERTS, dtype=I32)[None, :]
    pick_e = lambda tab: jnp.sum(jnp.where(of_e, tab[None, :], 0), axis=1)
    pick_col = lambda tab: jnp.sum(jnp.where(of_e[:, None, :], tab[None, :, :], 0), axis=2)
    run_end = pick_col(jnp.cumsum(n16, axis=0))
    tile_off = pick_col(jnp.cumsum(n16, axis=1) - n16)
    k = ((blk - pick_e(blk_start)) * cpb)[:, None] + jnp.arange(cpb, dtype=I32)[None, :]
    real = (k < pick_e(per_e)[:, None]) & (blk < n_used[0])[:, None]
    done = run_end[:, None, :] <= k[:, :, None]
    tile = jnp.minimum(jnp.sum(done, axis=2), nt - 1)
    run_first = jnp.max(jnp.where(done, run_end[:, None, :], 0), axis=2)
    of_t = tile[:, :, None] == jnp.arange(nt, dtype=I32)[None, None, :]
    src = tile * (slots // MOE_CHUNK) + jnp.sum(jnp.where(of_t, tile_off[:, None, :], 0), axis=2) + (k - run_first)
    src = jnp.where(real, src, src[:, :1])
    src = jnp.where((blk < n_used[0])[:, None], src, 0).astype(I32).reshape(-1)
    nreal = jnp.sum(real, axis=1).astype(I32)
    ys = _experts(xs, src, nreal, block_e, n_used, w1, w3, w2, layer)
    return _combine(h1, rs, rg, ys, ln_g, ln_b, out_dtype)


def _swap_halves(x):
    lane = lax.broadcasted_iota(I32, x.shape, 1)
    first = (lane % MLA_ROPE) < (MLA_ROPE // 2)
    return jnp.where(first, pltpu.roll(x, LANES - MLA_ROPE // 2, 1), pltpu.roll(x, MLA_ROPE // 2, 1))


def _rope(x, cos, sin):
    parts = []
    for t in range(x.shape[1] // LANES):
        sl = slice(t * LANES, (t + 1) * LANES)
        parts.append(x[:, sl] * cos[:, sl] + _swap_halves(x[:, sl]) * sin[:, sl])
    return parts[0] if len(parts) == 1 else jnp.concatenate(parts, axis=1)


def _odd_in_kernel(h_ref, w_ref, qg_ref, wqn_ref, wqp_ref, wuk_ref, perm_ref, kvg_ref,
                   cq_ref, sq_ref, ck_ref, sk_ref, gg_ref, gb_ref, ws_ref, bs_ref,
                   q_ref, kc_ref, kt_ref, ckv_ref, kpe_ref, gated_ref, vn_ref, *, tl, cl):
    hb = h_ref[0].astype(BF16)
    z = _dot(hb, w_ref[...])
    o_ckv = MLA_Q_RANK
    o_u = o_ckv + MLA_KV_RANK
    o_v = o_u + GMLP_WIDTH
    o_k = o_v + GMLP_WIDTH
    cq = z[:, :MLA_Q_RANK]
    cqn = cq * lax.rsqrt(jnp.mean(cq * cq, axis=-1, keepdims=True) + LN_EPS) * qg_ref[...]
    cqb = cqn.astype(BF16)
    qn = _dot(cqb, wqn_ref[...])
    qp = _dot(cqb, wqp_ref[...])
    qp = _rope(qp, cq_ref[...], sq_ref[...])
    qpe = _dot((qp * MLA_SCALE).astype(BF16), perm_ref[...])
    for h in range(MLA_HEADS):
        sl = slice(h * LANES, (h + 1) * LANES)
        qa = _dot((qn[:, sl] * MLA_SCALE).astype(BF16), wuk_ref[h])
        q_ref[0, h, :, 0:LANES] = qa.astype(BF16)
        q_ref[0, h, :, LANES:] = qpe[:, sl].astype(BF16)

    ckv = z[:, o_ckv:o_u]
    ckvn = ckv * lax.rsqrt(jnp.mean(ckv * ckv, axis=-1, keepdims=True) + LN_EPS) * kvg_ref[...]
    kp = z[:, o_k:]
    kp = _rope(kp, ck_ref[...], sk_ref[...])
    ckv_ref[0] = ckvn
    kpe_ref[0] = kp[:, :MLA_ROPE]
    one = (lax.broadcasted_iota(I32, (1, LANES), 1) == MLA_ONE_LANE - LANES).astype(F32)
    kp1 = kp + one
    kc_ref[0, :, 0:LANES] = ckvn.astype(BF16)
    kc_ref[0, :, LANES:] = kp1.astype(BF16)
    kt_ref[0, 0:LANES, :] = ckvn.T.astype(BF16)
    kt_ref[0, LANES:, :] = kp1.T.astype(BF16)

    gu = _gelu(z[:, o_u:o_v])
    vn = _layernorm(_gelu(z[:, o_v:o_k]), gg_ref[...], gb_ref[...])
    vn_ref[0] = vn
    vnb = vn.astype(BF16)
    for n in range(tl // cl):
        rs = slice(n * cl, (n + 1) * cl)
        for g in range(GMLP_GROUPS):
            ls = slice(g * GMLP_CH, (g + 1) * GMLP_CH)
            sg = _dot(ws_ref[g], vnb[rs, ls]) + bs_ref[:, ls]
            gated_ref[0, rs, ls] = (gu[rs, ls] * sg).astype(BF16)


def _odd_in(h3, w_in, q_g, w_qn, w_qp, w_uk, perm, kv_g, cos_q, sin_q, cos_k, sin_k,
            gm_g, gm_b, ws, bs):
    bsz, l, _ = h3.shape
    tl = min(l, 512)
    cl = min(l, GMLP_CHUNK)
    rowb = lambda n: pl.BlockSpec((1, tl, n), lambda b, i: (b, i, 0))
    tab = lambda n: pl.BlockSpec((tl, n), lambda b, i: (i, 0))
    consts = (w_in, q_g, w_qn, w_qp, w_uk, perm, kv_g)
    consts2 = (gm_g, gm_b, ws, bs)
    return pl.pallas_call(
        functools.partial(_odd_in_kernel, tl=tl, cl=cl),
        grid=(bsz, l // tl),
        in_specs=[rowb(D_MODEL)] + [_full_spec(a, 2) for a in consts]
                 + [tab(MLA_HEADS * MLA_ROPE), tab(MLA_HEADS * MLA_ROPE), tab(LANES), tab(LANES)]
                 + [_full_spec(a, 2) for a in consts2],
        out_specs=[pl.BlockSpec((1, MLA_HEADS, tl, MLA_QW), lambda b, i: (b, 0, i, 0)),
                   rowb(MLA_QW), pl.BlockSpec((1, MLA_QW, tl), lambda b, i: (b, 0, i)),
                   rowb(MLA_KV_RANK), rowb(MLA_ROPE), rowb(GMLP_WIDTH), rowb(GMLP_WIDTH)],
        out_shape=[jax.ShapeDtypeStruct((bsz, MLA_HEADS, l, MLA_QW), BF16),
                   jax.ShapeDtypeStruct((bsz, l, MLA_QW), BF16),
                   jax.ShapeDtypeStruct((bsz, MLA_QW, l), BF16),
                   jax.ShapeDtypeStruct((bsz, l, MLA_KV_RANK), F32),
                   jax.ShapeDtypeStruct((bsz, l, MLA_ROPE), F32),
                   jax.ShapeDtypeStruct((bsz, l, GMLP_WIDTH), BF16),
                   jax.ShapeDtypeStruct((bsz, l, GMLP_WIDTH), F32)],
        compiler_params=_cp("parallel", "parallel"),
        name="odd_in",
    )(h3, *consts, cos_q, sin_q, cos_k, sin_k, *consts2)


def _attn_kernel(qi_ref, kj_ref, flag_ref, q_ref, k_ref, kt_ref, wuv_ref, o_ref,
                 m_scr, l_scr, acc_scr, *, tq, hpb, tk, nsub, pos0, n_keys):
    p = pl.program_id(1)
    flag = flag_ref[p]
    nblk = MLA_HEADS // hpb
    width = hpb * tq

    @pl.when((flag & 1) != 0)
    def _():
        m_scr[...] = jnp.full_like(m_scr, -jnp.inf)
        l_scr[...] = jnp.zeros_like(l_scr)
        acc_scr[...] = jnp.zeros_like(acc_scr)

    def step(sub, hidden):
        kt = kt_ref[0, :, sub * tk:(sub + 1) * tk]
        kk = k_ref[0, sub * tk:(sub + 1) * tk, :]
        if hidden:
            qpos = pos0 + qi_ref[p] * tq + lax.broadcasted_iota(I32, (1, width), 1) % tq
            kpos = (kj_ref[p] * nsub + sub) * tk + lax.broadcasted_iota(I32, (tk, 1), 0)
            visible = ((kpos // CHUNK) <= (qpos // CHUNK)) & (kpos < n_keys)
        half = tk // MLA_KEY_PARTS
        for h in range(nblk):
            s = _dot_nt(kk, q_ref[0, h])
            if hidden:
                s = jnp.where(visible, s, -jnp.inf)
            m = m_scr[h]
            acc = acc_scr[h]
            l = l_scr[h]
            for part in range(MLA_KEY_PARTS):
                ks = slice(part * half, (part + 1) * half)
                sp = s[ks]
                m_new = jnp.maximum(m, jnp.max(sp, axis=0, keepdims=True))
                alpha = jnp.exp(m - m_new)
                pr = jnp.exp(sp - m_new).astype(BF16)
                acc = alpha * acc + _dot(kt[:MLA_KV_RANK, ks], pr)
                l = alpha * l + _dot(kt[MLA_ONE_LANE:MLA_ONE_LANE + 16, ks], pr)
                m = m_new
            acc_scr[h] = acc
            l_scr[h] = l
            m_scr[h] = m

    for sub in range(nsub):
        seen = (flag & (4 << (2 * sub))) != 0
        some_hidden = (flag & (8 << (2 * sub))) != 0

        @pl.when(seen & some_hidden)
        def _():
            step(sub, True)

        @pl.when(seen & jnp.logical_not(some_hidden))
        def _():
            step(sub, False)

    @pl.when((flag & 2) != 0)
    def _():
        if hpb == 1:
            for pair in range(MLA_HEADS // 2):
                tile = jnp.zeros((tq, LANES), F32)
                for h in (2 * pair, 2 * pair + 1):
                    lat = (acc_scr[h] / l_scr[h, 0:1, :]).astype(BF16)
                    tile = tile + _dot_tn(lat, wuv_ref[h])
                o_ref[0, :, pair * LANES:(pair + 1) * LANES] = tile.astype(BF16)
        else:
            head_of_lane = lax.broadcasted_iota(I32, (1, MLA_HEADS * MLA_V), 1) // MLA_V
            out = jnp.zeros((tq, MLA_HEADS * MLA_V), F32)
            for h in range(nblk):
                lat = (acc_scr[h] / l_scr[h, 0:1, :]).astype(BF16)
                full = _dot_tn(lat, wuv_ref[...])
                for hh in range(hpb):
                    out = out + jnp.where(head_of_lane == h * hpb + hh, full[hh * tq:(hh + 1) * tq], 0.0)
            o_ref[0] = out.astype(BF16)


def _attn_pairs(l, tq, tk, nsub, pos0, n_keys):
    qi, kj, flag = [], [], []
    for i in range(l // tq):
        q_first = pos0 + i * tq
        q_last = q_first + tq - 1
        vis = min(CHUNK * (q_last // CHUNK) + CHUNK - 1, n_keys - 1)
        ntiles = vis // tk + 1
        nsteps = -(-ntiles // nsub)
        for j in range(nsteps):
            f = int(j == 0) + 2 * int(j == nsteps - 1)
            for s in range(nsub):
                t = j * nsub + s
                if t < ntiles:
                    hidden = ((t + 1) * tk - 1) // CHUNK > q_first // CHUNK or (t + 1) * tk > n_keys
                    f += (4 + 8 * int(hidden)) << (2 * s)
            qi.append(i); kj.append(j); flag.append(f)
    return [jnp.asarray(np.array(a, np.int32)) for a in (qi, kj, flag)]


def _attention(q4, kc3, kt3, w_uv, pos0, n_keys, tq, tk, nsub):
    bsz, _, l, _ = q4.shape
    pairs = _attn_pairs(l, tq, tk, nsub, pos0, n_keys)
    npairs = int(pairs[0].shape[0])
    hpb = max(1, min(MLA_HEADS, 256 // tq)) if l == tq else 1
    nblk = MLA_HEADS // hpb
    width = hpb * tq
    q4 = q4.reshape(bsz, nblk, hpb * l, MLA_QW)
    if hpb == 1:
        heads = w_uv.reshape(MLA_KV_RANK, MLA_HEADS, MLA_V).transpose(1, 0, 2)
        w_uv = jnp.stack([jnp.pad(heads[h], ((0, 0), ((h % 2) * MLA_V, (1 - h % 2) * MLA_V)))
                          for h in range(MLA_HEADS)])
    return pl.pallas_call(
        functools.partial(_attn_kernel, tq=tq, hpb=hpb, tk=tk, nsub=nsub, pos0=pos0, n_keys=n_keys),
        grid_spec=pltpu.PrefetchScalarGridSpec(
            num_scalar_prefetch=3,
            grid=(bsz, npairs),
            in_specs=[pl.BlockSpec((1, nblk, width, MLA_QW), lambda b, p, qi, kj, f: (b, 0, qi[p], 0)),
                      pl.BlockSpec((1, nsub * tk, MLA_QW), lambda b, p, qi, kj, f: (b, kj[p], 0)),
                      pl.BlockSpec((1, MLA_QW, nsub * tk), lambda b, p, qi, kj, f: (b, 0, kj[p])),
                      pl.BlockSpec(w_uv.shape, lambda b, p, qi, kj, f: (0,) * w_uv.ndim)],
            out_specs=pl.BlockSpec((1, tq, MLA_HEADS * MLA_V), lambda b, p, qi, kj, f: (b, qi[p], 0)),
            scratch_shapes=[pltpu.VMEM((nblk, 1, width), F32),
                            pltpu.VMEM((nblk, 16, width), F32),
                            pltpu.VMEM((nblk, MLA_KV_RANK, width), F32)]),
        out_shape=jax.ShapeDtypeStruct((bsz, l, MLA_HEADS * MLA_V), BF16),
        compiler_params=_cp("parallel", "arbitrary"),
        name="mla_attention",
    )(*pairs, q4, kc3, kt3, w_uv)


def _rope_tables(pos0, l, width):
    half = MLA_ROPE // 2
    inv = ROPE_THETA ** (-jnp.arange(half, dtype=F32) * 2.0 / MLA_ROPE)
    ang = (pos0 + jnp.arange(l)).astype(F32)[:, None] * inv[None, :]
    cos, sin = jnp.cos(ang), jnp.sin(ang)
    cos32 = jnp.concatenate([cos, cos], axis=1)
    sin32 = jnp.concatenate([-sin, sin], axis=1)
    reps = width // MLA_ROPE
    return jnp.tile(cos32, (1, reps)), jnp.tile(sin32, (1, reps))


def _prep_even(w_in, w_gate_up, b_gate, gla_norm_g, pool_w, pool_scale, w_out):
    o_r = 2 * GLA_QK + GLA_V
    o_g = o_r + GLA_V
    o_p = o_g + GLA_GATE_RANK
    w_main = jnp.concatenate([w_in[:, :o_g], w_in[:, o_p:]], axis=1).astype(BF16)
    w_g = jnp.pad(w_in[:, o_g:o_p], ((0, 0), (0, LANES - GLA_GATE_RANK))).astype(BF16)
    w_gu = jnp.pad(w_gate_up, ((0, LANES - GLA_GATE_RANK), (0, 0))).astype(BF16)
    return dict(w_main=w_main, w_g=w_g, w_gu=w_gu, b_g=b_gate.reshape(1, -1),
                gnorm=gla_norm_g.reshape(1, -1), pool_w=pool_w.astype(BF16),
                pool_scale=pool_scale.reshape(1, -1), w_out=w_out.astype(BF16))


def _prep_odd(w_in, q_norm_g, w_uq, kv_norm_g, w_uk, w_uv, gm_g, gm_b, gm_ws, gm_bs, w_out):
    o_ckv = MLA_Q_RANK
    o_kpe = o_ckv + MLA_KV_RANK
    o_u = o_kpe + MLA_ROPE
    w_in2 = jnp.concatenate([w_in[:, :o_kpe], w_in[:, o_u:], w_in[:, o_kpe:o_u],
                             jnp.zeros((D_MODEL, LANES - MLA_ROPE), F32)], axis=1).astype(BF16)
    uq = w_uq.reshape(MLA_Q_RANK, MLA_HEADS, MLA_NOPE + MLA_ROPE)
    w_qn = jnp.pad(uq[:, :, :MLA_NOPE], ((0, 0), (0, 0), (0, LANES - MLA_NOPE)))
    w_qn = w_qn.reshape(MLA_Q_RANK, MLA_HEADS * LANES).astype(BF16)
    w_qp = uq[:, :, MLA_NOPE:].reshape(MLA_Q_RANK, MLA_HEADS * MLA_ROPE).astype(BF16)
    uk = w_uk.reshape(MLA_KV_RANK, MLA_HEADS, MLA_NOPE).transpose(1, 2, 0)
    w_ukp = jnp.pad(uk, ((0, 0), (0, LANES - MLA_NOPE), (0, 0))).astype(BF16)
    src = np.arange(MLA_HEADS * MLA_ROPE)
    perm = np.zeros((MLA_HEADS * MLA_ROPE, MLA_HEADS * LANES), np.float32)
    perm[src, (src // MLA_ROPE) * LANES + src % MLA_ROPE] = 1.0
    return dict(w_in=w_in2, q_g=q_norm_g.reshape(1, -1), w_qn=w_qn, w_qp=w_qp, w_uk=w_ukp,
                perm=jnp.asarray(perm, BF16), kv_g=kv_norm_g.reshape(1, -1), w_uv=w_uv.astype(BF16),
                gm_g=gm_g.reshape(1, -1), gm_b=gm_b.reshape(1, -1), gm_ws=gm_ws, gm_bs=gm_bs,
                w_out=w_out.astype(BF16))


def _prep_route(wg, bg, we, be):
    w_r = jnp.pad(jnp.concatenate([wg, we], axis=1), ((0, 0), (0, LANES - N_GROUPS - N_EXPERTS)))
    w_hi = w_r.astype(BF16)
    w_lo = (w_r - w_hi.astype(F32)).astype(BF16)
    b_r = jnp.pad(jnp.concatenate([bg, be]), (0, LANES - N_GROUPS - N_EXPERTS)).reshape(1, LANES)
    return jnp.concatenate([w_hi, w_lo], axis=1), b_r


def _even_mixer(h3, st0, hist, pos0, pw):
    bsz, l, _ = h3.shape
    q, k, la, v, r, xp = _even_in(h3.reshape(bsz * l, D_MODEL), pw['w_main'], pw['w_g'], pw['w_gu'], pw['b_g'])
    to3 = lambda a: a.reshape(bsz, l, a.shape[-1])
    o, st = _gla(to3(q), to3(k), to3(la), to3(v), to3(r), st0, pw['gnorm'])
    xp3 = to3(xp)
    hist16 = jnp.pad(hist, ((0, 0), (POOL_HALO - POOL_HIST, 0), (0, 0)))
    pooled = _pool(xp3, hist16, pw['pool_w'], pw['pool_scale'], pos0)
    hist_new = jnp.concatenate([hist, xp3[:, -POOL_HIST:].astype(F32)], axis=1)[:, -POOL_HIST:]
    return o, pooled, st, hist_new


def _odd_mixer(h3, ckv_past, kpe_past, pw):
    bsz, l, _ = h3.shape
    n_past = ckv_past.shape[1]
    cos_q, sin_q = _rope_tables(n_past, l, MLA_HEADS * MLA_ROPE)
    cos_k, sin_k = _rope_tables(n_past, l, MLA_ROPE)
    padk = ((0, 0), (0, LANES - MLA_ROPE))
    cos_k, sin_k = jnp.pad(cos_k, padk), jnp.pad(sin_k, padk)
    cl = min(l, GMLP_CHUNK)
    ws = jnp.tril(pw['gm_ws'][:, :cl, :cl]).astype(BF16)
    bs = jnp.repeat(pw['gm_bs'][:, :cl].T, GMLP_CH, axis=1)
    q4, kc, kt, ckv, kpe, gated, vn = _odd_in(h3, pw['w_in'], pw['q_g'], pw['w_qn'], pw['w_qp'], pw['w_uk'],
                                              pw['perm'], pw['kv_g'], cos_q, sin_q, cos_k, sin_k,
                                              pw['gm_g'], pw['gm_b'], ws, bs)
    n_keys = n_past + l
    tq, tk, nsub = min(l, 256), 512, 2
    span = tk * nsub
    if n_past:
        past = jnp.concatenate([ckv_past, kpe_past, jnp.ones((bsz, n_past, 1), F32),
                                jnp.zeros((bsz, n_past, MLA_QW - MLA_ONE_LANE - 1), F32)], axis=2).astype(BF16)
        kc = jnp.concatenate([past, kc], axis=1)
        kt = jnp.concatenate([past.transpose(0, 2, 1), kt], axis=2)
    kc = jnp.pad(kc, ((0, 0), (0, -n_keys % span), (0, 0)))
    kt = jnp.pad(kt, ((0, 0), (0, 0), (0, -n_keys % span)))
    attn = _attention(q4, kc, kt, pw['w_uv'], n_past, n_keys, tq, tk, nsub)
    return attn, gated, ckv, kpe, vn


def _finish_layer(a3, b3, h3, w_out, lw):
    bsz, l, _ = h3.shape
    t = bsz * l
    h1, xs, rs, rg, n16 = _out_route(a3.reshape(t, -1), b3.reshape(t, -1), h3.reshape(t, D_MODEL), w_out,
                                     lw['ln_mix_g'], lw['ln_mix_b'], lw['w_r'], lw['b_r'])
    h2 = _moe(h1, xs, rs, rg, n16, lw['w1'], lw['w3'], lw['w2'], lw['layer'], lw['ln_ffn_g'], lw['ln_ffn_b'],
              lw['out_dtype'])
    return h2.reshape(bsz, l, D_MODEL)


def kernel(x_prompt, x_sample, state_gla, state_pool, cache_mla_ckv, cache_mla_kpe, w_in_even, w_gate_up, b_gate, gla_norm_g, pool_w, pool_scale, w_out_even, w_in_odd, mla_q_norm_g, mla_w_uq, mla_kv_norm_g, mla_w_uk, mla_w_uv, gmlp_norm_g, gmlp_norm_b, gmlp_ws, gmlp_bs, w_out_odd, ln_mix_g, ln_mix_b, router_group_w, router_group_b, router_expert_w, router_expert_b, expert_w1, expert_w3, expert_w2, ln_ffn_g, ln_ffn_b):
    hp, hs = x_prompt, x_sample
    bp = hp.shape[0]
    past_len = cache_mla_ckv.shape[2]
    gla_p, gla_s, pool_p, pool_s = [], [], [], []
    ckv_p, ckv_s, kpe_p, kpe_s, gv_s = [], [], [], [], []

    def state_to_t(s):
        return s.transpose(0, 3, 1, 2).reshape(s.shape[0], GLA_DV, GLA_QK)

    def state_from_t(st):
        return st.reshape(st.shape[0], GLA_DV, GLA_HEADS, GLA_DK).transpose(0, 2, 3, 1)

    for layer in range(DEPTH):
        i = layer // 2
        w_r, b_r = _prep_route(router_group_w[layer], router_group_b[layer],
                               router_expert_w[layer], router_expert_b[layer])
        lw = dict(ln_mix_g=ln_mix_g[layer].reshape(1, -1), ln_mix_b=ln_mix_b[layer].reshape(1, -1),
                  ln_ffn_g=ln_ffn_g[layer].reshape(1, -1), ln_ffn_b=ln_ffn_b[layer].reshape(1, -1),
                  w_r=w_r, b_r=b_r, w1=expert_w1, w3=expert_w3, w2=expert_w2, layer=layer,
                  out_dtype=F32 if layer == DEPTH - 1 else BF16)
        if layer % 2 == 0:
            pw = _prep_even(w_in_even[i], w_gate_up[i], b_gate[i], gla_norm_g[i], pool_w[i], pool_scale[i],
                            w_out_even[i])
            st0 = jnp.zeros((bp, GLA_DV, GLA_QK), F32)
            hist0 = jnp.zeros((bp, POOL_HIST, POOL_WIDTH), F32)
            op, pp, stp, histp = _even_mixer(hp, st0, hist0, 0, pw)
            os_, ps, sts, hists = _even_mixer(hs, state_to_t(state_gla[i]), state_pool[i], past_len, pw)
            gla_p.append(state_from_t(stp)); gla_s.append(state_from_t(sts))
            pool_p.append(histp); pool_s.append(hists)
            ap, bpj, as_, bsj = op, pp, os_, ps
        else:
            pw = _prep_odd(w_in_odd[i], mla_q_norm_g[i], mla_w_uq[i], mla_kv_norm_g[i], mla_w_uk[i], mla_w_uv[i],
                           gmlp_norm_g[i], gmlp_norm_b[i], gmlp_ws[i], gmlp_bs[i], w_out_odd[i])
            no_ckv = jnp.zeros((bp, 0, MLA_KV_RANK), F32)
            no_kpe = jnp.zeros((bp, 0, MLA_ROPE), F32)
            ap, bpj, cp, kp, _ = _odd_mixer(hp, no_ckv, no_kpe, pw)
            as_, bsj, cs, ks, vs = _odd_mixer(hs, cache_mla_ckv[i], cache_mla_kpe[i], pw)
            ckv_p.append(cp); ckv_s.append(cs); kpe_p.append(kp); kpe_s.append(ks); gv_s.append(vs)
        hp = _finish_layer(ap, bpj, hp, pw['w_out'], lw)
        hs = _finish_layer(as_, bsj, hs, pw['w_out'], lw)
    return (hp, hs, jnp.stack(gla_p), jnp.stack(gla_s), jnp.stack(pool_p), jnp.stack(pool_s),
            jnp.stack(ckv_p), jnp.stack(ckv_s), jnp.stack(kpe_p), jnp.stack(kpe_s), jnp.stack(gv_s))
```

```python
import functools

import numpy as np
import jax
import jax.numpy as jnp
from jax import lax
from jax.experimental import pallas as pl
from jax.experimental.pallas import tpu as pltpu

F32 = jnp.float32
BF16 = jnp.bfloat16
I32 = jnp.int32

D_MODEL = 1024
DEPTH = 2
CHUNK = 64
ALPHA = (2 * DEPTH) ** 0.25
LN_EPS = 1e-5

GLA_HEADS = 4
GLA_DV = 128
GLA_DK = 64
GLA_QK = GLA_HEADS * GLA_DK
GLA_V = GLA_HEADS * GLA_DV
GLA_GATE_RANK = 16
GLA_GATE_TAU = 16.0
GLA_SUB = 8

POOL_WIDTH = 512
POOL_CH = 128
POOL_WINDOWS = (2, 4, 8, 16)
POOL_HIST = 15
POOL_HALO = 16

MLA_HEADS = 8
MLA_NOPE = 64
MLA_ROPE = 32
MLA_V = 64
MLA_Q_RANK = 256
MLA_KV_RANK = 128
ROPE_THETA = 10000.0
MLA_SCALE = (MLA_NOPE + MLA_ROPE) ** -0.5
MLA_QW = 256
MLA_ONE_LANE = MLA_KV_RANK + MLA_ROPE
MLA_KEY_PARTS = 2
GMLP_WIDTH = 512
GMLP_CH = 128
GMLP_GROUPS = 4
GMLP_CHUNK = 128

N_GROUPS = 4
EXPERTS_PER_GROUP = 8
N_EXPERTS = 32
TOP_K = 2
D_EXPERT = 256
MOE_ROWS = 512
MOE_CHUNK = 16
MOE_BLOCK_CHUNKS = MOE_ROWS // MOE_CHUNK

LANES = 128
VMEM_LIMIT = 48 * 1024 * 1024


def _cp(*sem):
    return pltpu.CompilerParams(dimension_semantics=sem, vmem_limit_bytes=VMEM_LIMIT)


def _dot(a, b):
    return jnp.dot(a, b, preferred_element_type=F32)


def _dot_nt(a, b):
    return lax.dot_general(a, b, (((1,), (1,)), ((), ())), preferred_element_type=F32)


def _dot_tn(a, b):
    return lax.dot_general(a, b, (((0,), (0,)), ((), ())), preferred_element_type=F32)


def _split3(x):
    hi = x.astype(BF16)
    r1 = x - hi.astype(F32)
    mid = r1.astype(BF16)
    lo = (r1 - mid.astype(F32)).astype(BF16)
    return hi, mid, lo


def _layernorm(x, g, b):
    mu = jnp.mean(x, axis=-1, keepdims=True)
    xc = x - mu
    var = jnp.mean(xc * xc, axis=-1, keepdims=True)
    return xc * lax.rsqrt(var + LN_EPS) * g + b


def _gelu(x):
    return 0.5 * x * (1.0 + jnp.tanh(0.7978845608028654 * (x + 0.044715 * (x * x * x))))


def _sigmoid(x):
    return 1.0 / (1.0 + jnp.exp(-x))


def _full_spec(a, nargs):
    nd = a.ndim
    if nargs == 1:
        return pl.BlockSpec(a.shape, lambda i: (0,) * nd)
    return pl.BlockSpec(a.shape, lambda i, j: (0,) * nd)


def _even_in_kernel(x_ref, w_ref, wg_ref, wgu_ref, bg_ref,
                    q_ref, k_ref, la_ref, v_ref, r_ref, xp_ref):
    xb = x_ref[...].astype(BF16)
    z = _dot(xb, w_ref[...])
    q_ref[...] = z[:, 0:GLA_QK] * (GLA_DK ** -0.5)
    k_ref[...] = z[:, GLA_QK:2 * GLA_QK]
    v_ref[...] = z[:, 2 * GLA_QK:2 * GLA_QK + GLA_V]
    r_ref[...] = z[:, 2 * GLA_QK + GLA_V:2 * GLA_QK + 2 * GLA_V]
    xp_ref[...] = z[:, 2 * GLA_QK + 2 * GLA_V:].astype(BF16)
    g = _dot(xb, wg_ref[...])
    pre = _dot(g.astype(BF16), wgu_ref[...]) + bg_ref[...]
    logsig = jnp.minimum(pre, 0.0) - jnp.log(1.0 + jnp.exp(-jnp.abs(pre)))
    la_ref[...] = logsig * (1.0 / GLA_GATE_TAU)


def _even_in(x2, w_main, w_g, w_gu, b_g):
    t = x2.shape[0]
    tm = min(512, t)
    row = lambda n: pl.BlockSpec((tm, n), lambda i: (i, 0))
    widths = (GLA_QK, GLA_QK, GLA_QK, GLA_V, GLA_V, POOL_WIDTH)
    return pl.pallas_call(
        _even_in_kernel,
        grid=(t // tm,),
        in_specs=[row(D_MODEL)] + [_full_spec(a, 1) for a in (w_main, w_g, w_gu, b_g)],
        out_specs=[row(n) for n in widths],
        out_shape=[jax.ShapeDtypeStruct((t, n), BF16 if i == 5 else F32) for i, n in enumerate(widths)],
        compiler_params=_cp("parallel"),
        name="even_in",
    )(x2, w_main, w_g, w_gu, b_g)


def _gla_consts(c):
    n = c * GLA_SUB
    tri = np.tril(np.ones((c, c), np.float32))
    headsum = (np.arange(GLA_QK)[:, None] // GLA_DK == np.arange(GLA_V)[None, :] // GLA_DV).astype(np.float32)
    msel = (np.arange(n)[None, :] // GLA_SUB == np.arange(c)[:, None]).astype(np.float32)
    return [jnp.asarray(a, BF16) for a in (tri, headsum, msel)]


def _gla_chunk(q, k, la, v, st, c, tri, headsum, msel):
    lane = lax.broadcasted_iota(I32, (1, GLA_QK), 1)
    head_of_lane = lane // GLA_DK
    row = lax.broadcasted_iota(I32, (c, 1), 0)
    ii = lax.broadcasted_iota(I32, (c, c), 0)
    jj = lax.broadcasted_iota(I32, (c, c), 1)

    hi, mid, lo = _split3(la)
    b = _dot(tri, hi) + _dot(tri, mid) + _dot(tri, lo)

    head_masks = [head_of_lane == h for h in range(GLA_HEADS)]
    vb = v.astype(BF16)

    a_off = [jnp.zeros((c, c), F32) for _ in range(GLA_HEADS)]
    s = c // 2
    while s >= GLA_SUB:
        nblk = c // (2 * s)
        blk = row // (2 * s)
        right = ((row // s) % 2) == 1
        bref = jnp.zeros((c, GLA_QK), F32)
        for m in range(nblk):
            r0 = m * 2 * s + s - 1
            bref = jnp.where(blk == m, b[r0:r0 + 1, :], bref)
        qe = jnp.where(right, q * jnp.exp(jnp.minimum(b - bref, 0.0)), 0.0)
        ke = jnp.where(right, 0.0, k * jnp.exp(jnp.minimum(bref - b, 0.0))).astype(BF16)
        same = (ii // (2 * s)) == (jj // (2 * s))
        for h in range(GLA_HEADS):
            a = _dot_nt(jnp.where(head_masks[h], qe, 0.0).astype(BF16), ke)
            a_off[h] = a_off[h] + (a if nblk == 1 else jnp.where(same, a, 0.0))
        s //= 2

    nsb = c // GLA_SUB
    parts = []
    for i in range(nsb):
        sl = slice(i * GLA_SUB, (i + 1) * GLA_SUB)
        bi, qi, ki = b[sl], q[sl], k[sl]
        diff = bi[:, None, :] - bi[None, :, :]
        p = qi[:, None, :] * ki[None, :, :] * jnp.exp(jnp.minimum(diff, 0.0))
        parts.append(p.reshape(GLA_SUB * GLA_SUB, GLA_QK))
    n = nsb * GLA_SUB * GLA_SUB
    idx = lax.broadcasted_iota(I32, (n, 1), 0)
    causal = (idx % GLA_SUB) <= ((idx // GLA_SUB) % GLA_SUB)
    pcat = jnp.where(causal, jnp.concatenate(parts, axis=0), 0.0).astype(BF16)
    rsum = _dot(pcat, headsum)

    qb = q * jnp.exp(b)
    b_end = b[c - 1:c, :]
    kd = (k * jnp.exp(b_end - b)).astype(BF16)
    stb = st.astype(BF16)
    st_new = st * jnp.exp(b_end)

    outs = []
    for h in range(GLA_HEADS):
        vh = vb[:, h * GLA_DV:(h + 1) * GLA_DV]
        vf = v[:, h * GLA_DV:(h + 1) * GLA_DV]
        vt = jnp.concatenate(
            [jnp.broadcast_to(vf[i * GLA_SUB:(i + 1) * GLA_SUB][None], (GLA_SUB, GLA_SUB, GLA_DV))
             .reshape(GLA_SUB * GLA_SUB, GLA_DV) for i in range(nsb)], axis=0)
        xh = rsum[:, h * GLA_DV:(h + 1) * GLA_DV] * vt
        o = _dot(msel, xh.astype(BF16))
        o = o + _dot(a_off[h].astype(BF16), vh)
        o = o + _dot_nt(jnp.where(head_masks[h], qb, 0.0).astype(BF16), stb)
        outs.append(o)
        st_new = st_new + jnp.where(head_masks[h], _dot_tn(vh, kd), 0.0)
    return outs, st_new


def _gla_kernel(q_ref, k_ref, la_ref, v_ref, r_ref, st0_ref, g_ref, tri_ref, hs_ref, ms_ref,
                o_ref, st_ref, st_scr, *, c, nchunks):
    @pl.when(pl.program_id(1) == 0)
    def _():
        st_scr[...] = st0_ref[0]

    def body(ci, carry):
        r0 = pl.multiple_of(ci * c, c)
        rows = pl.ds(r0, c)
        outs, st_new = _gla_chunk(q_ref[0, rows, :], k_ref[0, rows, :], la_ref[0, rows, :], v_ref[0, rows, :],
                                  st_scr[...], c, tri_ref[...], hs_ref[...], ms_ref[...])
        st_scr[...] = st_new
        r = r_ref[0, rows, :]
        g = g_ref[...]
        for h in range(GLA_HEADS):
            o = outs[h]
            sl = slice(h * GLA_DV, (h + 1) * GLA_DV)
            on = o * lax.rsqrt(jnp.mean(o * o, axis=-1, keepdims=True) + LN_EPS) * g
            rh = r[:, sl]
            o_ref[0, rows, sl] = (on * (rh * _sigmoid(rh))).astype(BF16)
        return carry

    lax.fori_loop(0, nchunks, body, 0, unroll=8 if nchunks % 8 == 0 else 1)
    st_ref[0] = st_scr[...]


def _gla(q3, k3, la3, v3, r3, st0, gnorm):
    bsz, l, _ = q3.shape
    c = min(l, CHUNK)
    tl = min(l, 512)
    blk = lambda n: pl.BlockSpec((1, tl, n), lambda b, i: (b, i, 0))
    st_spec = pl.BlockSpec((1, GLA_DV, GLA_QK), lambda b, i: (b, 0, 0))
    consts = _gla_consts(c)
    return pl.pallas_call(
        functools.partial(_gla_kernel, c=c, nchunks=tl // c),
        grid=(bsz, l // tl),
        in_specs=[blk(GLA_QK), blk(GLA_QK), blk(GLA_QK), blk(GLA_V), blk(GLA_V), st_spec,
                  _full_spec(gnorm, 2)] + [_full_spec(a, 2) for a in consts],
        out_specs=[blk(GLA_V), st_spec],
        out_shape=[jax.ShapeDtypeStruct((bsz, l, GLA_V), BF16),
                   jax.ShapeDtypeStruct((bsz, GLA_DV, GLA_QK), F32)],
        scratch_shapes=[pltpu.VMEM((GLA_DV, GLA_QK), F32)],
        compiler_params=_cp("parallel", "arbitrary"),
        name="gla",
    )(q3, k3, la3, v3, r3, st0, gnorm, *consts)


def _pool_kernel(x_ref, halo_ref, hist_ref, w_ref, scale_ref, o_ref, *, tl, pos0):
    i = pl.program_id(1)
    x = x_ref[0].astype(F32)
    prev = jnp.where(i == 0, hist_ref[0], halo_ref[0].astype(F32))
    e = jnp.concatenate([prev, x], axis=0)
    t = i * tl + lax.broadcasted_iota(I32, (tl, 1), 0)
    pos = pos0 + t
    sums = []
    shift = 1
    for g, w in enumerate(POOL_WINDOWS):
        e = e[:, POOL_CH:] if g > 0 else e
        while shift < w:
            e = e[shift:] + e[:-shift]
            shift *= 2
        off = POOL_HALO - (w - 1)
        sums.append(e[off:off + tl, :POOL_CH])
    outs = []
    for g, w in enumerate(POOL_WINDOWS):
        cnt = jnp.minimum(pos + 1, w).astype(F32)
        mix = sums[g] / cnt - x[:, g * POOL_CH:(g + 1) * POOL_CH]
        outs.append(_dot(mix.astype(BF16), w_ref[g]))
    o_ref[0] = (jnp.concatenate(outs, axis=1) * scale_ref[...]).astype(BF16)


def _pool(xp3, hist16, pool_w, pool_scale, pos0):
    bsz, l, _ = xp3.shape
    tl = min(l, 512)
    per = tl // POOL_HALO
    return pl.pallas_call(
        functools.partial(_pool_kernel, tl=tl, pos0=pos0),
        grid=(bsz, l // tl),
        in_specs=[pl.BlockSpec((1, tl, POOL_WIDTH), lambda b, i: (b, i, 0)),
                  pl.BlockSpec((1, POOL_HALO, POOL_WIDTH), lambda b, i: (b, jnp.maximum(i * per - 1, 0), 0)),
                  pl.BlockSpec((1, POOL_HALO, POOL_WIDTH), lambda b, i: (b, 0, 0)),
                  _full_spec(pool_w, 2), _full_spec(pool_scale, 2)],
        out_specs=pl.BlockSpec((1, tl, POOL_WIDTH), lambda b, i: (b, i, 0)),
        out_shape=jax.ShapeDtypeStruct((bsz, l, POOL_WIDTH), BF16),
        compiler_params=_cp("parallel", "parallel"),
        name="pool",
    )(xp3, xp3, hist16, pool_w, pool_scale)


def _out_route_kernel(a_ref, b_ref, h_ref, w_ref, g_ref, bt_ref, wr_ref, br_ref,
                      h1_ref, xs_ref, rs_ref, rg_ref, n16_ref, *, tm, slots):
    half = w_ref.shape[0] // 2
    y = _dot(a_ref[...].astype(BF16), w_ref[0:half, :]) + _dot(b_ref[...].astype(BF16), w_ref[half:, :])
    x = _layernorm(ALPHA * h_ref[...].astype(F32) + y, g_ref[...], bt_ref[...])
    h1_ref[...] = x.astype(BF16)

    xh = x.astype(BF16)
    xl = (x - xh.astype(F32)).astype(BF16)
    hl = _dot(xh, wr_ref[...])
    logits = hl[:, :LANES] + hl[:, LANES:] + _dot(xl, wr_ref[:, 0:LANES]) + br_ref[...]

    lane = lax.broadcasted_iota(I32, (tm, LANES), 1)
    lanef = lane.astype(F32)
    neg = -jnp.inf
    big = jnp.float32(1 << 20)

    def first_lane(hit):
        return jnp.min(jnp.where(hit, lanef, big), axis=-1, keepdims=True).astype(I32)

    gl = jnp.where(lane < N_GROUPS, logits, neg)
    gmax = jnp.max(gl, axis=-1, keepdims=True)
    g_sel = first_lane(gl == gmax)
    g_prob = 1.0 / jnp.sum(jnp.exp(gl - gmax), axis=-1, keepdims=True)
    eidx = lane - N_GROUPS
    in_grp = (eidx >= 0) & (eidx < N_EXPERTS) & ((eidx // EXPERTS_PER_GROUP) == g_sel)
    el = jnp.where(in_grp, logits, neg)
    v1 = jnp.max(el, axis=-1, keepdims=True)
    i1 = first_lane(el == v1)
    el2 = jnp.where(lane == i1, neg, el)
    v2 = jnp.max(el2, axis=-1, keepdims=True)
    i2 = first_lane(el2 == v2)
    e21 = jnp.exp(v2 - v1)
    gate1 = g_prob / (1.0 + e21)
    gate2 = g_prob * e21 / (1.0 + e21)
    e1 = i1 - N_GROUPS
    e2 = i2 - N_GROUPS

    oh1 = lane == e1
    oh2 = lane == e2
    oh = oh1.astype(F32) + oh2.astype(F32)
    ti = lax.broadcasted_iota(I32, (tm, tm), 0)
    tj = lax.broadcasted_iota(I32, (tm, tm), 1)
    before = _dot((tj < ti).astype(BF16), oh.astype(BF16))
    cnt = jnp.sum(oh, axis=0, keepdims=True)
    n16 = jnp.floor((cnt + (MOE_CHUNK - 1)) * (1.0 / MOE_CHUNK))
    n16_8 = jnp.broadcast_to(n16, (8, LANES))
    ui = lax.broadcasted_iota(I32, (LANES, LANES), 0)
    uj = lax.broadcasted_iota(I32, (LANES, LANES), 1)
    run_start = _dot(n16_8.astype(BF16), (ui < uj).astype(BF16))[0:1]
    slot_of = MOE_CHUNK * run_start + before
    slot1 = jnp.sum(jnp.where(oh1, slot_of, 0.0), axis=-1, keepdims=True).astype(I32)
    slot2 = jnp.sum(jnp.where(oh2, slot_of, 0.0), axis=-1, keepdims=True).astype(I32)
    sl = lax.broadcasted_iota(I32, (tm, slots), 1)
    place = ((sl == slot1) | (sl == slot2)).astype(BF16)
    xs_ref[...] = _dot_tn(place, xh).astype(BF16)

    rs_ref[...] = jnp.where(lane == 0, slot1, jnp.where(lane == 1, slot2, 0))
    rg_ref[...] = jnp.where(lane == 0, gate1, jnp.where(lane == 1, gate2, 0.0))
    n16_ref[...] = n16_8.astype(I32)


def _moe_slots(tm):
    worst = tm * TOP_K + N_EXPERTS * (MOE_CHUNK - 1)
    return -(-worst // MOE_ROWS) * MOE_ROWS


def _out_route(a2, b2, h2, w_out, ln_g, ln_b, w_r, b_r):
    t = h2.shape[0]
    tm = min(512, t)
    nt = t // tm
    slots = _moe_slots(tm)
    row = lambda n: pl.BlockSpec((tm, n), lambda i: (i, 0))
    return pl.pallas_call(
        functools.partial(_out_route_kernel, tm=tm, slots=slots),
        grid=(nt,),
        in_specs=[row(a2.shape[1]), row(b2.shape[1]), row(D_MODEL)]
                 + [_full_spec(a, 1) for a in (w_out, ln_g, ln_b, w_r, b_r)],
        out_specs=[row(D_MODEL), pl.BlockSpec((slots, D_MODEL), lambda i: (i, 0)), row(LANES), row(LANES),
                   pl.BlockSpec((8, LANES), lambda i: (i, 0))],
        out_shape=[jax.ShapeDtypeStruct((t, D_MODEL), BF16), jax.ShapeDtypeStruct((nt * slots, D_MODEL), BF16),
                   jax.ShapeDtypeStruct((t, LANES), I32), jax.ShapeDtypeStruct((t, LANES), F32),
                   jax.ShapeDtypeStruct((nt * 8, LANES), I32)],
        compiler_params=_cp("parallel"),
        name="out_route",
    )(a2, b2, h2, w_out, ln_g, ln_b, w_r, b_r)


def _chunk_rows(chunk):
    return pl.ds(pl.multiple_of(chunk * MOE_CHUNK, MOE_CHUNK), MOE_CHUNK)


def _expert_kernel(src_ref, nreal_ref, be_ref, nu_ref, xs_ref, w1_ref, w3_ref, w2_ref, ys_ref,
                   xbuf, ybuf, gsem, ssem, *, cpb):
    del be_ref
    b = pl.program_id(0)
    nu = nu_ref[0]

    rows = cpb * MOE_CHUNK

    def gather(blk, slot, j):
        return pltpu.make_async_copy(xs_ref.at[_chunk_rows(src_ref[blk * cpb + j]), :],
                                     xbuf.at[slot, pl.ds(j * MOE_CHUNK, MOE_CHUNK), :], gsem.at[slot])

    def scatter(blk, slot, j):
        return pltpu.make_async_copy(ybuf.at[slot, _chunk_rows(j), :],
                                     ys_ref.at[_chunk_rows(src_ref[blk * cpb + j]), :],
                                     ssem.at[slot])

    def start_gather(blk, slot):
        for j in range(cpb):
            gather(blk, slot, j).start()

    def wait_gather(slot):
        pltpu.make_async_copy(xs_ref.at[pl.ds(0, rows), :], xbuf.at[slot], gsem.at[slot]).wait()

    def for_real_chunks(blk, fn):
        def body(j, carry):
            fn(j)
            return carry
        lax.fori_loop(0, nreal_ref[blk], body, 0)

    def wait_scatter(blk, slot):
        full = nreal_ref[blk] == cpb

        @pl.when(full)
        def _():
            pltpu.make_async_copy(ybuf.at[slot], ys_ref.at[pl.ds(0, rows), :], ssem.at[slot]).wait()

        @pl.when(jnp.logical_not(full))
        def _():
            for_real_chunks(blk, lambda j: scatter(blk, slot, j).wait())

    @pl.when(b < nu)
    def _():
        slot = b % 2

        @pl.when(b == 0)
        def _():
            start_gather(b, slot)

        @pl.when(b + 1 < nu)
        def _():
            start_gather(b + 1, 1 - slot)

        wait_gather(slot)

        @pl.when(b >= 2)
        def _():
            wait_scatter(b - 2, slot)

        xb = xbuf[slot]
        a = _dot(xb, w1_ref[0, 0].astype(BF16))
        hid = a * _sigmoid(a) * _dot(xb, w3_ref[0, 0].astype(BF16))
        ybuf[slot] = _dot(hid.astype(BF16), w2_ref[0, 0].astype(BF16)).astype(BF16)
        for_real_chunks(b, lambda j: scatter(b, slot, j).start())

        @pl.when(b == nu - 1)
        def _():
            wait_scatter(b, slot)

            @pl.when(b >= 1)
            def _():
                wait_scatter(b - 1, 1 - slot)


def _experts(xs, src, nreal, block_e, n_used, w1, w3, w2, layer):
    nblk = nreal.shape[0]
    cpb = src.shape[0] // nblk
    rows = cpb * MOE_CHUNK
    wspec = lambda shape: pl.BlockSpec((1,) + shape, lambda i, src, nr, be, nu: (layer, be[i], 0, 0))
    return pl.pallas_call(
        functools.partial(_expert_kernel, cpb=cpb),
        grid_spec=pltpu.PrefetchScalarGridSpec(
            num_scalar_prefetch=4,
            grid=(nblk,),
            in_specs=[pl.BlockSpec(memory_space=pl.ANY),
                      wspec((1, D_MODEL, D_EXPERT)), wspec((1, D_MODEL, D_EXPERT)), wspec((1, D_EXPERT, D_MODEL))],
            out_specs=pl.BlockSpec(memory_space=pl.ANY),
            scratch_shapes=[pltpu.VMEM((2, rows, D_MODEL), BF16), pltpu.VMEM((2, rows, D_MODEL), BF16),
                            pltpu.SemaphoreType.DMA((2,)), pltpu.SemaphoreType.DMA((2,))]),
        out_shape=jax.ShapeDtypeStruct(xs.shape, xs.dtype),
        input_output_aliases={4: 0},
        compiler_params=_cp("arbitrary"),
        name="moe_experts",
    )(src, nreal, block_e, n_used, xs, w1, w3, w2)


def _combine_kernel(h_ref, rs_ref, rg_ref, ys_ref, g_ref, b_ref, o_ref, *, tm, slots):
    rs = rs_ref[...]
    rg = rg_ref[...]
    sl = lax.broadcasted_iota(I32, (tm, slots), 1).astype(jnp.int16)
    s1, s2 = rs[:, 0:1].astype(jnp.int16), rs[:, 1:2].astype(jnp.int16)
    g1, g2 = rg[:, 0:1].astype(BF16), rg[:, 1:2].astype(BF16)
    weight = jnp.where(sl == s1, g1, jnp.where(sl == s2, g2, jnp.zeros((), BF16)))
    y = _dot(weight, ys_ref[...])
    o_ref[...] = _layernorm(ALPHA * h_ref[...].astype(F32) + y, g_ref[...], b_ref[...]).astype(o_ref.dtype)


def _combine(h2, rs, rg, ys, ln_g, ln_b, out_dtype):
    t = h2.shape[0]
    tm = min(512, t)
    slots = ys.shape[0] // (t // tm)
    row = lambda n: pl.BlockSpec((tm, n), lambda i: (i, 0))
    return pl.pallas_call(
        functools.partial(_combine_kernel, tm=tm, slots=slots),
        grid=(t // tm,),
        in_specs=[row(D_MODEL), row(LANES), row(LANES), pl.BlockSpec((slots, D_MODEL), lambda i: (i, 0)),
                  _full_spec(ln_g, 1), _full_spec(ln_b, 1)],
        out_specs=row(D_MODEL),
        out_shape=jax.ShapeDtypeStruct((t, D_MODEL), out_dtype),
        compiler_params=_cp("parallel"),
        name="moe_combine",
    )(h2, rs, rg, ys, ln_g, ln_b)


def _moe(h1, xs, rs, rg, n16_rows, w1, w3, w2, layer, ln_g, ln_b, out_dtype):
    t = h1.shape[0]
    tm = min(512, t)
    nt = t // tm
    slots = xs.shape[0] // nt
    n16 = n16_rows[::8, :N_EXPERTS]
    cpb = max(4, min(MOE_BLOCK_CHUNKS, (t * TOP_K) // (N_EXPERTS * MOE_CHUNK)))
    per_e = jnp.sum(n16, axis=0)
    blocks_e = (per_e + cpb - 1) // cpb
    blk_end = jnp.cumsum(blocks_e)
    blk_start = blk_end - blocks_e
    n_used = blk_end[-1:].astype(I32)
    max_chunks = (t * TOP_K) // MOE_CHUNK + nt * N_EXPERTS
    nblk = max_chunks // cpb + N_EXPERTS
    blk = jnp.arange(nblk, dtype=I32)
    block_e = jnp.minimum(jnp.sum(blk_end[None, :] <= blk[:, None], axis=1), N_EXPERTS - 1).astype(I32)
    of_e = block_e[:, None] == jnp.arange(N_EXPERTS, dtype=I32)[None, :]
    pick_e = lambda tab: jnp.sum(jnp.where(of_e, tab[None, :], 0), axis=1)
    pick_col = lambda tab: jnp.sum(jnp.where(of_e[:, None, :], tab[None, :, :], 0), axis=2)
    run_end = pick_col(jnp.cumsum(n16, axis=0))
    tile_off = pick_col(jnp.cumsum(n16, axis=1) - n16)
    k = ((blk - pick_e(blk_start)) * cpb)[:, None] + jnp.arange(cpb, dtype=I32)[None, :]
    real = (k < pick_e(per_e)[:, None]) & (blk < n_used[0])[:, None]
    done = run_end[:, None, :] <= k[:, :, None]
    tile = jnp.minimum(jnp.sum(done, axis=2), nt - 1)
    run_first = jnp.max(jnp.where(done, run_end[:, None, :], 0), axis=2)
    of_t = tile[:, :, None] == jnp.arange(nt, dtype=I32)[None, None, :]
    src = tile * (slots // MOE_CHUNK) + jnp.sum(jnp.where(of_t, tile_off[:, None, :], 0), axis=2) + (k - run_first)
    src = jnp.where(real, src, src[:, :1])
    src = jnp.where((blk < n_used[0])[:, None], src, 0).astype(I32).reshape(-1)
    nreal = jnp.sum(real, axis=1).astype(I32)
    ys = _experts(xs, src, nreal, block_e, n_used, w1, w3, w2, layer)
    return _combine(h1, rs, rg, ys, ln_g, ln_b, out_dtype)


def _swap_halves(x):
    lane = lax.broadcasted_iota(I32, x.shape, 1)
    first = (lane % MLA_ROPE) < (MLA_ROPE // 2)
    return jnp.where(first, pltpu.roll(x, LANES - MLA_ROPE // 2, 1), pltpu.roll(x, MLA_ROPE // 2, 1))


def _rope(x, cos, sin):
    parts = []
    for t in range(x.shape[1] // LANES):
        sl = slice(t * LANES, (t + 1) * LANES)
        parts.append(x[:, sl] * cos[:, sl] + _swap_halves(x[:, sl]) * sin[:, sl])
    return parts[0] if len(parts) == 1 else jnp.concatenate(parts, axis=1)


def _odd_in_kernel(h_ref, w_ref, qg_ref, wqn_ref, wqp_ref, wuk_ref, perm_ref, kvg_ref,
                   cq_ref, sq_ref, ck_ref, sk_ref, gg_ref, gb_ref, ws_ref, bs_ref,
                   q_ref, kc_ref, kt_ref, ckv_ref, kpe_ref, gated_ref, vn_ref, *, tl, cl):
    hb = h_ref[0].astype(BF16)
    z = _dot(hb, w_ref[...])
    o_ckv = MLA_Q_RANK
    o_u = o_ckv + MLA_KV_RANK
    o_v = o_u + GMLP_WIDTH
    o_k = o_v + GMLP_WIDTH
    cq = z[:, :MLA_Q_RANK]
    cqn = cq * lax.rsqrt(jnp.mean(cq * cq, axis=-1, keepdims=True) + LN_EPS) * qg_ref[...]
    cqb = cqn.astype(BF16)
    qn = _dot(cqb, wqn_ref[...])
    qp = _dot(cqb, wqp_ref[...])
    qp = _rope(qp, cq_ref[...], sq_ref[...])
    qpe = _dot((qp * MLA_SCALE).astype(BF16), perm_ref[...])
    for h in range(MLA_HEADS):
        sl = slice(h * LANES, (h + 1) * LANES)
        qa = _dot((qn[:, sl] * MLA_SCALE).astype(BF16), wuk_ref[h])
        q_ref[0, h, :, 0:LANES] = qa.astype(BF16)
        q_ref[0, h, :, LANES:] = qpe[:, sl].astype(BF16)

    ckv = z[:, o_ckv:o_u]
    ckvn = ckv * lax.rsqrt(jnp.mean(ckv * ckv, axis=-1, keepdims=True) + LN_EPS) * kvg_ref[...]
    kp = z[:, o_k:]
    kp = _rope(kp, ck_ref[...], sk_ref[...])
    ckv_ref[0] = ckvn
    kpe_ref[0] = kp[:, :MLA_ROPE]
    one = (lax.broadcasted_iota(I32, (1, LANES), 1) == MLA_ONE_LANE - LANES).astype(F32)
    kp1 = kp + one
    kc_ref[0, :, 0:LANES] = ckvn.astype(BF16)
    kc_ref[0, :, LANES:] = kp1.astype(BF16)
    kt_ref[0, 0:LANES, :] = ckvn.T.astype(BF16)
    kt_ref[0, LANES:, :] = kp1.T.astype(BF16)

    gu = _gelu(z[:, o_u:o_v])
    vn = _layernorm(_gelu(z[:, o_v:o_k]), gg_ref[...], gb_ref[...])
    vn_ref[0] = vn
    vnb = vn.astype(BF16)
    for n in range(tl // cl):
        rs = slice(n * cl, (n + 1) * cl)
        for g in range(GMLP_GROUPS):
            ls = slice(g * GMLP_CH, (g + 1) * GMLP_CH)
            sg = _dot(ws_ref[g], vnb[rs, ls]) + bs_ref[:, ls]
            gated_ref[0, rs, ls] = (gu[rs, ls] * sg).astype(BF16)


def _odd_in(h3, w_in, q_g, w_qn, w_qp, w_uk, perm, kv_g, cos_q, sin_q, cos_k, sin_k,
            gm_g, gm_b, ws, bs):
    bsz, l, _ = h3.shape
    tl = min(l, 512)
    cl = min(l, GMLP_CHUNK)
    rowb = lambda n: pl.BlockSpec((1, tl, n), lambda b, i: (b, i, 0))
    tab = lambda n: pl.BlockSpec((tl, n), lambda b, i: (i, 0))
    consts = (w_in, q_g, w_qn, w_qp, w_uk, perm, kv_g)
    consts2 = (gm_g, gm_b, ws, bs)
    return pl.pallas_call(
        functools.partial(_odd_in_kernel, tl=tl, cl=cl),
        grid=(bsz, l // tl),
        in_specs=[rowb(D_MODEL)] + [_full_spec(a, 2) for a in consts]
                 + [tab(MLA_HEADS * MLA_ROPE), tab(MLA_HEADS * MLA_ROPE), tab(LANES), tab(LANES)]
                 + [_full_spec(a, 2) for a in consts2],
        out_specs=[pl.BlockSpec((1, MLA_HEADS, tl, MLA_QW), lambda b, i: (b, 0, i, 0)),
                   rowb(MLA_QW), pl.BlockSpec((1, MLA_QW, tl), lambda b, i: (b, 0, i)),
                   rowb(MLA_KV_RANK), rowb(MLA_ROPE), rowb(GMLP_WIDTH), rowb(GMLP_WIDTH)],
        out_shape=[jax.ShapeDtypeStruct((bsz, MLA_HEADS, l, MLA_QW), BF16),
                   jax.ShapeDtypeStruct((bsz, l, MLA_QW), BF16),
                   jax.ShapeDtypeStruct((bsz, MLA_QW, l), BF16),
                   jax.ShapeDtypeStruct((bsz, l, MLA_KV_RANK), F32),
                   jax.ShapeDtypeStruct((bsz, l, MLA_ROPE), F32),
                   jax.ShapeDtypeStruct((bsz, l, GMLP_WIDTH), BF16),
                   jax.ShapeDtypeStruct((bsz, l, GMLP_WIDTH), F32)],
        compiler_params=_cp("parallel", "parallel"),
        name="odd_in",
    )(h3, *consts, cos_q, sin_q, cos_k, sin_k, *consts2)


def _attn_kernel(qi_ref, kj_ref, flag_ref, q_ref, k_ref, kt_ref, wuv_ref, o_ref,
                 m_scr, l_scr, acc_scr, *, tq, hpb, tk, nsub, pos0, n_keys):
    p = pl.program_id(1)
    flag = flag_ref[p]
    nblk = MLA_HEADS // hpb
    width = hpb * tq

    @pl.when((flag & 1) != 0)
    def _():
        m_scr[...] = jnp.full_like(m_scr, -jnp.inf)
        l_scr[...] = jnp.zeros_like(l_scr)
        acc_scr[...] = jnp.zeros_like(acc_scr)

    def step(sub, hidden):
        kt = kt_ref[0, :, sub * tk:(sub + 1) * tk]
        kk = k_ref[0, sub * tk:(sub + 1) * tk, :]
        if hidden:
            qpos = pos0 + qi_ref[p] * tq + lax.broadcasted_iota(I32, (1, width), 1) % tq
            kpos = (kj_ref[p] * nsub + sub) * tk + lax.broadcasted_iota(I32, (tk, 1), 0)
            visible = ((kpos // CHUNK) <= (qpos // CHUNK)) & (kpos < n_keys)
        half = tk // MLA_KEY_PARTS
        for h in range(nblk):
            s = _dot_nt(kk, q_ref[0, h])
            if hidden:
                s = jnp.where(visible, s, -jnp.inf)
            m = m_scr[h]
            acc = acc_scr[h]
            l = l_scr[h]
            for part in range(MLA_KEY_PARTS):
                ks = slice(part * half, (part + 1) * half)
                sp = s[ks]
                m_new = jnp.maximum(m, jnp.max(sp, axis=0, keepdims=True))
                alpha = jnp.exp(m - m_new)
                pr = jnp.exp(sp - m_new).astype(BF16)
                acc = alpha * acc + _dot(kt[:MLA_KV_RANK, ks], pr)
                l = alpha * l + _dot(kt[MLA_ONE_LANE:MLA_ONE_LANE + 16, ks], pr)
                m = m_new
            acc_scr[h] = acc
            l_scr[h] = l
            m_scr[h] = m

    for sub in range(nsub):
        seen = (flag & (4 << (2 * sub))) != 0
        some_hidden = (flag & (8 << (2 * sub))) != 0

        @pl.when(seen & some_hidden)
        def _():
            step(sub, True)

        @pl.when(seen & jnp.logical_not(some_hidden))
        def _():
            step(sub, False)

    @pl.when((flag & 2) != 0)
    def _():
        if hpb == 1:
            for pair in range(MLA_HEADS // 2):
                tile = jnp.zeros((tq, LANES), F32)
                for h in (2 * pair, 2 * pair + 1):
                    lat = (acc_scr[h] / l_scr[h, 0:1, :]).astype(BF16)
                    tile = tile + _dot_tn(lat, wuv_ref[h])
                o_ref[0, :, pair * LANES:(pair + 1) * LANES] = tile.astype(BF16)
        else:
            head_of_lane = lax.broadcasted_iota(I32, (1, MLA_HEADS * MLA_V), 1) // MLA_V
            out = jnp.zeros((tq, MLA_HEADS * MLA_V), F32)
            for h in range(nblk):
                lat = (acc_scr[h] / l_scr[h, 0:1, :]).astype(BF16)
                full = _dot_tn(lat, wuv_ref[...])
                for hh in range(hpb):
                    out = out + jnp.where(head_of_lane == h * hpb + hh, full[hh * tq:(hh + 1) * tq], 0.0)
            o_ref[0] = out.astype(BF16)


def _attn_pairs(l, tq, tk, nsub, pos0, n_keys):
    qi, kj, flag = [], [], []
    for i in range(l // tq):
        q_first = pos0 + i * tq
        q_last = q_first + tq - 1
        vis = min(CHUNK * (q_last // CHUNK) + CHUNK - 1, n_keys - 1)
        ntiles = vis // tk + 1
        nsteps = -(-ntiles // nsub)
        for j in range(nsteps):
            f = int(j == 0) + 2 * int(j == nsteps - 1)
            for s in range(nsub):
                t = j * nsub + s
                if t < ntiles:
                    hidden = ((t + 1) * tk - 1) // CHUNK > q_first // CHUNK or (t + 1) * tk > n_keys
                    f += (4 + 8 * int(hidden)) << (2 * s)
            qi.append(i); kj.append(j); flag.append(f)
    return [jnp.asarray(np.array(a, np.int32)) for a in (qi, kj, flag)]


def _attention(q4, kc3, kt3, w_uv, pos0, n_keys, tq, tk, nsub):
    bsz, _, l, _ = q4.shape
    pairs = _attn_pairs(l, tq, tk, nsub, pos0, n_keys)
    npairs = int(pairs[0].shape[0])
    hpb = max(1, min(MLA_HEADS, 256 // tq)) if l == tq else 1
    nblk = MLA_HEADS // hpb
    width = hpb * tq
    q4 = q4.reshape(bsz, nblk, hpb * l, MLA_QW)
    if hpb == 1:
        heads = w_uv.reshape(MLA_KV_RANK, MLA_HEADS, MLA_V).transpose(1, 0, 2)
        w_uv = jnp.stack([jnp.pad(heads[h], ((0, 0), ((h % 2) * MLA_V, (1 - h % 2) * MLA_V)))
                          for h in range(MLA_HEADS)])
    return pl.pallas_call(
        functools.partial(_attn_kernel, tq=tq, hpb=hpb, tk=tk, nsub=nsub, pos0=pos0, n_keys=n_keys),
        grid_spec=pltpu.PrefetchScalarGridSpec(
            num_scalar_prefetch=3,
            grid=(bsz, npairs),
            in_specs=[pl.BlockSpec((1, nblk, width, MLA_QW), lambda b, p, qi, kj, f: (b, 0, qi[p], 0)),
                      pl.BlockSpec((1, nsub * tk, MLA_QW), lambda b, p, qi, kj, f: (b, kj[p], 0)),
                      pl.BlockSpec((1, MLA_QW, nsub * tk), lambda b, p, qi, kj, f: (b, 0, kj[p])),
                      pl.BlockSpec(w_uv.shape, lambda b, p, qi, kj, f: (0,) * w_uv.ndim)],
            out_specs=pl.BlockSpec((1, tq, MLA_HEADS * MLA_V), lambda b, p, qi, kj, f: (b, qi[p], 0)),
            scratch_shapes=[pltpu.VMEM((nblk, 1, width), F32),
                            pltpu.VMEM((nblk, 16, width), F32),
                            pltpu.VMEM((nblk, MLA_KV_RANK, width), F32)]),
        out_shape=jax.ShapeDtypeStruct((bsz, l, MLA_HEADS * MLA_V), BF16),
        compiler_params=_cp("parallel", "arbitrary"),
        name="mla_attention",
    )(*pairs, q4, kc3, kt3, w_uv)


def _rope_tables(pos0, l, width):
    half = MLA_ROPE // 2
    inv = ROPE_THETA ** (-jnp.arange(half, dtype=F32) * 2.0 / MLA_ROPE)
    ang = (pos0 + jnp.arange(l)).astype(F32)[:, None] * inv[None, :]
    cos, sin = jnp.cos(ang), jnp.sin(ang)
    cos32 = jnp.concatenate([cos, cos], axis=1)
    sin32 = jnp.concatenate([-sin, sin], axis=1)
    reps = width // MLA_ROPE
    return jnp.tile(cos32, (1, reps)), jnp.tile(sin32, (1, reps))


def _prep_even(w_in, w_gate_up, b_gate, gla_norm_g, pool_w, pool_scale, w_out):
    o_r = 2 * GLA_QK + GLA_V
    o_g = o_r + GLA_V
    o_p = o_g + GLA_GATE_RANK
    w_main = jnp.concatenate([w_in[:, :o_g], w_in[:, o_p:]], axis=1).astype(BF16)
    w_g = jnp.pad(w_in[:, o_g:o_p], ((0, 0), (0, LANES - GLA_GATE_RANK))).astype(BF16)
    w_gu = jnp.pad(w_gate_up, ((0, LANES - GLA_GATE_RANK), (0, 0))).astype(BF16)
    return dict(w_main=w_main, w_g=w_g, w_gu=w_gu, b_g=b_gate.reshape(1, -1),
                gnorm=gla_norm_g.reshape(1, -1), pool_w=pool_w.astype(BF16),
                pool_scale=pool_scale.reshape(1, -1), w_out=w_out.astype(BF16))


def _prep_odd(w_in, q_norm_g, w_uq, kv_norm_g, w_uk, w_uv, gm_g, gm_b, gm_ws, gm_bs, w_out):
    o_ckv = MLA_Q_RANK
    o_kpe = o_ckv + MLA_KV_RANK
    o_u = o_kpe + MLA_ROPE
    w_in2 = jnp.concatenate([w_in[:, :o_kpe], w_in[:, o_u:], w_in[:, o_kpe:o_u],
                             jnp.zeros((D_MODEL, LANES - MLA_ROPE), F32)], axis=1).astype(BF16)
    uq = w_uq.reshape(MLA_Q_RANK, MLA_HEADS, MLA_NOPE + MLA_ROPE)
    w_qn = jnp.pad(uq[:, :, :MLA_NOPE], ((0, 0), (0, 0), (0, LANES - MLA_NOPE)))
    w_qn = w_qn.reshape(MLA_Q_RANK, MLA_HEADS * LANES).astype(BF16)
    w_qp = uq[:, :, MLA_NOPE:].reshape(MLA_Q_RANK, MLA_HEADS * MLA_ROPE).astype(BF16)
    uk = w_uk.reshape(MLA_KV_RANK, MLA_HEADS, MLA_NOPE).transpose(1, 2, 0)
    w_ukp = jnp.pad(uk, ((0, 0), (0, LANES - MLA_NOPE), (0, 0))).astype(BF16)
    src = np.arange(MLA_HEADS * MLA_ROPE)
    perm = np.zeros((MLA_HEADS * MLA_ROPE, MLA_HEADS * LANES), np.float32)
    perm[src, (src // MLA_ROPE) * LANES + src % MLA_ROPE] = 1.0
    return dict(w_in=w_in2, q_g=q_norm_g.reshape(1, -1), w_qn=w_qn, w_qp=w_qp, w_uk=w_ukp,
                perm=jnp.asarray(perm, BF16), kv_g=kv_norm_g.reshape(1, -1), w_uv=w_uv.astype(BF16),
                gm_g=gm_g.reshape(1, -1), gm_b=gm_b.reshape(1, -1), gm_ws=gm_ws, gm_bs=gm_bs,
                w_out=w_out.astype(BF16))


def _prep_route(wg, bg, we, be):
    w_r = jnp.pad(jnp.concatenate([wg, we], axis=1), ((0, 0), (0, LANES - N_GROUPS - N_EXPERTS)))
    w_hi = w_r.astype(BF16)
    w_lo = (w_r - w_hi.astype(F32)).astype(BF16)
    b_r = jnp.pad(jnp.concatenate([bg, be]), (0, LANES - N_GROUPS - N_EXPERTS)).reshape(1, LANES)
    return jnp.concatenate([w_hi, w_lo], axis=1), b_r


def _even_mixer(h3, st0, hist, pos0, pw):
    bsz, l, _ = h3.shape
    q, k, la, v, r, xp = _even_in(h3.reshape(bsz * l, D_MODEL), pw['w_main'], pw['w_g'], pw['w_gu'], pw['b_g'])
    to3 = lambda a: a.reshape(bsz, l, a.shape[-1])
    o, st = _gla(to3(q), to3(k), to3(la), to3(v), to3(r), st0, pw['gnorm'])
    xp3 = to3(xp)
    hist16 = jnp.pad(hist, ((0, 0), (POOL_HALO - POOL_HIST, 0), (0, 0)))
    pooled = _pool(xp3, hist16, pw['pool_w'], pw['pool_scale'], pos0)
    hist_new = jnp.concatenate([hist, xp3[:, -POOL_HIST:].astype(F32)], axis=1)[:, -POOL_HIST:]
    return o, pooled, st, hist_new


def _odd_mixer(h3, ckv_past, kpe_past, pw):
    bsz, l, _ = h3.shape
    n_past = ckv_past.shape[1]
    cos_q, sin_q = _rope_tables(n_past, l, MLA_HEADS * MLA_ROPE)
    cos_k, sin_k = _rope_tables(n_past, l, MLA_ROPE)
    padk = ((0, 0), (0, LANES - MLA_ROPE))
    cos_k, sin_k = jnp.pad(cos_k, padk), jnp.pad(sin_k, padk)
    cl = min(l, GMLP_CHUNK)
    ws = jnp.tril(pw['gm_ws'][:, :cl, :cl]).astype(BF16)
    bs = jnp.repeat(pw['gm_bs'][:, :cl].T, GMLP_CH, axis=1)
    q4, kc, kt, ckv, kpe, gated, vn = _odd_in(h3, pw['w_in'], pw['q_g'], pw['w_qn'], pw['w_qp'], pw['w_uk'],
                                              pw['perm'], pw['kv_g'], cos_q, sin_q, cos_k, sin_k,
                                              pw['gm_g'], pw['gm_b'], ws, bs)
    n_keys = n_past + l
    tq, tk, nsub = min(l, 256), 512, 2
    span = tk * nsub
    if n_past:
        past = jnp.concatenate([ckv_past, kpe_past, jnp.ones((bsz, n_past, 1), F32),
                                jnp.zeros((bsz, n_past, MLA_QW - MLA_ONE_LANE - 1), F32)], axis=2).astype(BF16)
        kc = jnp.concatenate([past, kc], axis=1)
        kt = jnp.concatenate([past.transpose(0, 2, 1), kt], axis=2)
    kc = jnp.pad(kc, ((0, 0), (0, -n_keys % span), (0, 0)))
    kt = jnp.pad(kt, ((0, 0), (0, 0), (0, -n_keys % span)))
    attn = _attention(q4, kc, kt, pw['w_uv'], n_past, n_keys, tq, tk, nsub)
    return attn, gated, ckv, kpe, vn


def _finish_layer(a3, b3, h3, w_out, lw):
    bsz, l, _ = h3.shape
    t = bsz * l
    h1, xs, rs, rg, n16 = _out_route(a3.reshape(t, -1), b3.reshape(t, -1), h3.reshape(t, D_MODEL), w_out,
                                     lw['ln_mix_g'], lw['ln_mix_b'], lw['w_r'], lw['b_r'])
    h2 = _moe(h1, xs, rs, rg, n16, lw['w1'], lw['w3'], lw['w2'], lw['layer'], lw['ln_ffn_g'], lw['ln_ffn_b'],
              lw['out_dtype'])
    return h2.reshape(bsz, l, D_MODEL)


def kernel(x_prompt, x_sample, state_gla, state_pool, cache_mla_ckv, cache_mla_kpe, w_in_even, w_gate_up, b_gate, gla_norm_g, pool_w, pool_scale, w_out_even, w_in_odd, mla_q_norm_g, mla_w_uq, mla_kv_norm_g, mla_w_uk, mla_w_uv, gmlp_norm_g, gmlp_norm_b, gmlp_ws, gmlp_bs, w_out_odd, ln_mix_g, ln_mix_b, router_group_w, router_group_b, router_expert_w, router_expert_b, expert_w1, expert_w3, expert_w2, ln_ffn_g, ln_ffn_b):
    hp, hs = x_prompt, x_sample
    bp = hp.shape[0]
    past_len = cache_mla_ckv.shape[2]
    gla_p, gla_s, pool_p, pool_s = [], [], [], []
    ckv_p, ckv_s, kpe_p, kpe_s, gv_s = [], [], [], [], []

    def state_to_t(s):
        return s.transpose(0, 3, 1, 2).reshape(s.shape[0], GLA_DV, GLA_QK)

    def state_from_t(st):
        return st.reshape(st.shape[0], GLA_DV, GLA_HEADS, GLA_DK).transpose(0, 2, 3, 1)

    for layer in range(DEPTH):
        i = layer // 2
        w_r, b_r = _prep_route(router_group_w[layer], router_group_b[layer],
                               router_expert_w[layer], router_expert_b[layer])
        lw = dict(ln_mix_g=ln_mix_g[layer].reshape(1, -1), ln_mix_b=ln_mix_b[layer].reshape(1, -1),
                  ln_ffn_g=ln_ffn_g[layer].reshape(1, -1), ln_ffn_b=ln_ffn_b[layer].reshape(1, -1),
                  w_r=w_r, b_r=b_r, w1=expert_w1, w3=expert_w3, w2=expert_w2, layer=layer,
                  out_dtype=F32 if layer == DEPTH - 1 else BF16)
        if layer % 2 == 0:
            pw = _prep_even(w_in_even[i], w_gate_up[i], b_gate[i], gla_norm_g[i], pool_w[i], pool_scale[i],
                            w_out_even[i])
            st0 = jnp.zeros((bp, GLA_DV, GLA_QK), F32)
            hist0 = jnp.zeros((bp, POOL_HIST, POOL_WIDTH), F32)
            op, pp, stp, histp = _even_mixer(hp, st0, hist0, 0, pw)
            os_, ps, sts, hists = _even_mixer(hs, state_to_t(state_gla[i]), state_pool[i], past_len, pw)
            gla_p.append(state_from_t(stp)); gla_s.append(state_from_t(sts))
            pool_p.append(histp); pool_s.append(hists)
            ap, bpj, as_, bsj = op, pp, os_, ps
        else:
            pw = _prep_odd(w_in_odd[i], mla_q_norm_g[i], mla_w_uq[i], mla_kv_norm_g[i], mla_w_uk[i], mla_w_uv[i],
                           gmlp_norm_g[i], gmlp_norm_b[i], gmlp_ws[i], gmlp_bs[i], w_out_odd[i])
            no_ckv = jnp.zeros((bp, 0, MLA_KV_RANK), F32)
            no_kpe = jnp.zeros((bp, 0, MLA_ROPE), F32)
            ap, bpj, cp, kp, _ = _odd_mixer(hp, no_ckv, no_kpe, pw)
            as_, bsj, cs, ks, vs = _odd_mixer(hs, cache_mla_ckv[i], cache_mla_kpe[i], pw)
            ckv_p.append(cp); ckv_s.append(cs); kpe_p.append(kp); kpe_s.append(ks); gv_s.append(vs)
        hp = _finish_layer(ap, bpj, hp, pw['w_out'], lw)
        hs = _finish_layer(as_, bsj, hs, pw['w_out'], lw)
    return (hp, hs, jnp.stack(gla_p), jnp.stack(gla_s), jnp.stack(pool_p), jnp.stack(pool_s),
            jnp.stack(ckv_p), jnp.stack(ckv_s), jnp.stack(kpe_p), jnp.stack(kpe_s), jnp.stack(gv_s))
```

```python
import functools

import numpy as np
import jax
import jax.numpy as jnp
from jax import lax
from jax.experimental import pallas as pl
from jax.experimental.pallas import tpu as pltpu

F32 = jnp.float32
BF16 = jnp.bfloat16
I32 = jnp.int32

D_MODEL = 1024
DEPTH = 2
CHUNK = 64
ALPHA = (2 * DEPTH) ** 0.25
LN_EPS = 1e-5

GLA_HEADS = 4
GLA_DV = 128
GLA_DK = 64
GLA_QK = GLA_HEADS * GLA_DK
GLA_V = GLA_HEADS * GLA_DV
GLA_GATE_RANK = 16
GLA_GATE_TAU = 16.0
GLA_SUB = 8

POOL_WIDTH = 512
POOL_CH = 128
POOL_WINDOWS = (2, 4, 8, 16)
POOL_HIST = 15
POOL_HALO = 16

MLA_HEADS = 8
MLA_NOPE = 64
MLA_ROPE = 32
MLA_V = 64
MLA_Q_RANK = 256
MLA_KV_RANK = 128
ROPE_THETA = 10000.0
MLA_SCALE = (MLA_NOPE + MLA_ROPE) ** -0.5
MLA_QW = 256
MLA_ONE_LANE = MLA_KV_RANK + MLA_ROPE
MLA_KEY_PARTS = 2
GMLP_WIDTH = 512
GMLP_CH = 128
GMLP_GROUPS = 4
GMLP_CHUNK = 128

N_GROUPS = 4
EXPERTS_PER_GROUP = 8
N_EXPERTS = 32
TOP_K = 2
D_EXPERT = 256
MOE_ROWS = 512
MOE_CHUNK = 16
MOE_BLOCK_CHUNKS = MOE_ROWS // MOE_CHUNK

LANES = 128
VMEM_LIMIT = 48 * 1024 * 1024


def _cp(*sem):
    return pltpu.CompilerParams(dimension_semantics=sem, vmem_limit_bytes=VMEM_LIMIT)


def _dot(a, b):
    return jnp.dot(a, b, preferred_element_type=F32)


def _dot_nt(a, b):
    return lax.dot_general(a, b, (((1,), (1,)), ((), ())), preferred_element_type=F32)


def _dot_tn(a, b):
    return lax.dot_general(a, b, (((0,), (0,)), ((), ())), preferred_element_type=F32)


def _split3(x):
    hi = x.astype(BF16)
    r1 = x - hi.astype(F32)
    mid = r1.astype(BF16)
    lo = (r1 - mid.astype(F32)).astype(BF16)
    return hi, mid, lo


def _layernorm(x, g, b):
    mu = jnp.mean(x, axis=-1, keepdims=True)
    xc = x - mu
    var = jnp.mean(xc * xc, axis=-1, keepdims=True)
    return xc * lax.rsqrt(var + LN_EPS) * g + b


def _gelu(x):
    return 0.5 * x * (1.0 + jnp.tanh(0.7978845608028654 * (x + 0.044715 * (x * x * x))))


def _sigmoid(x):
    return 1.0 / (1.0 + jnp.exp(-x))


def _full_spec(a, nargs):
    nd = a.ndim
    if nargs == 1:
        return pl.BlockSpec(a.shape, lambda i: (0,) * nd)
    return pl.BlockSpec(a.shape, lambda i, j: (0,) * nd)


def _even_in_kernel(x_ref, w_ref, wg_ref, wgu_ref, bg_ref,
                    q_ref, k_ref, la_ref, v_ref, r_ref, xp_ref):
    xb = x_ref[...].astype(BF16)
    z = _dot(xb, w_ref[...])
    q_ref[...] = z[:, 0:GLA_QK] * (GLA_DK ** -0.5)
    k_ref[...] = z[:, GLA_QK:2 * GLA_QK]
    v_ref[...] = z[:, 2 * GLA_QK:2 * GLA_QK + GLA_V]
    r_ref[...] = z[:, 2 * GLA_QK + GLA_V:2 * GLA_QK + 2 * GLA_V]
    xp_ref[...] = z[:, 2 * GLA_QK + 2 * GLA_V:].astype(BF16)
    g = _dot(xb, wg_ref[...])
    pre = _dot(g.astype(BF16), wgu_ref[...]) + bg_ref[...]
    logsig = jnp.minimum(pre, 0.0) - jnp.log(1.0 + jnp.exp(-jnp.abs(pre)))
    la_ref[...] = logsig * (1.0 / GLA_GATE_TAU)


def _even_in(x2, w_main, w_g, w_gu, b_g):
    t = x2.shape[0]
    tm = min(512, t)
    row = lambda n: pl.BlockSpec((tm, n), lambda i: (i, 0))
    widths = (GLA_QK, GLA_QK, GLA_QK, GLA_V, GLA_V, POOL_WIDTH)
    return pl.pallas_call(
        _even_in_kernel,
        grid=(t // tm,),
        in_specs=[row(D_MODEL)] + [_full_spec(a, 1) for a in (w_main, w_g, w_gu, b_g)],
        out_specs=[row(n) for n in widths],
        out_shape=[jax.ShapeDtypeStruct((t, n), BF16 if i == 5 else F32) for i, n in enumerate(widths)],
        compiler_params=_cp("parallel"),
        name="even_in",
    )(x2, w_main, w_g, w_gu, b_g)


def _gla_consts(c):
    n = c * GLA_SUB
    tri = np.tril(np.ones((c, c), np.float32))
    headsum = (np.arange(GLA_QK)[:, None] // GLA_DK == np.arange(GLA_V)[None, :] // GLA_DV).astype(np.float32)
    msel = (np.arange(n)[None, :] // GLA_SUB == np.arange(c)[:, None]).astype(np.float32)
    return [jnp.asarray(a, BF16) for a in (tri, headsum, msel)]


def _gla_chunk(q, k, la, v, st, c, tri, headsum, msel):
    lane = lax.broadcasted_iota(I32, (1, GLA_QK), 1)
    head_of_lane = lane // GLA_DK
    row = lax.broadcasted_iota(I32, (c, 1), 0)
    ii = lax.broadcasted_iota(I32, (c, c), 0)
    jj = lax.broadcasted_iota(I32, (c, c), 1)

    hi, mid, lo = _split3(la)
    b = _dot(tri, hi) + _dot(tri, mid) + _dot(tri, lo)

    head_masks = [head_of_lane == h for h in range(GLA_HEADS)]
    vb = v.astype(BF16)

    a_off = [jnp.zeros((c, c), F32) for _ in range(GLA_HEADS)]
    s = c // 2
    while s >= GLA_SUB:
        nblk = c // (2 * s)
        blk = row // (2 * s)
        right = ((row // s) % 2) == 1
        bref = jnp.zeros((c, GLA_QK), F32)
        for m in range(nblk):
            r0 = m * 2 * s + s - 1
            bref = jnp.where(blk == m, b[r0:r0 + 1, :], bref)
        qe = jnp.where(right, q * jnp.exp(jnp.minimum(b - bref, 0.0)), 0.0)
        ke = jnp.where(right, 0.0, k * jnp.exp(jnp.minimum(bref - b, 0.0))).astype(BF16)
        same = (ii // (2 * s)) == (jj // (2 * s))
        for h in range(GLA_HEADS):
            a = _dot_nt(jnp.where(head_masks[h], qe, 0.0).astype(BF16), ke)
            a_off[h] = a_off[h] + (a if nblk == 1 else jnp.where(same, a, 0.0))
        s //= 2

    nsb = c // GLA_SUB
    parts = []
    for i in range(nsb):
        sl = slice(i * GLA_SUB, (i + 1) * GLA_SUB)
        bi, qi, ki = b[sl], q[sl], k[sl]
        diff = bi[:, None, :] - bi[None, :, :]
        p = qi[:, None, :] * ki[None, :, :] * jnp.exp(jnp.minimum(diff, 0.0))
        parts.append(p.reshape(GLA_SUB * GLA_SUB, GLA_QK))
    n = nsb * GLA_SUB * GLA_SUB
    idx = lax.broadcasted_iota(I32, (n, 1), 0)
    causal = (idx % GLA_SUB) <= ((idx // GLA_SUB) % GLA_SUB)
    pcat = jnp.where(causal, jnp.concatenate(parts, axis=0), 0.0).astype(BF16)
    rsum = _dot(pcat, headsum)

    qb = q * jnp.exp(b)
    b_end = b[c - 1:c, :]
    kd = (k * jnp.exp(b_end - b)).astype(BF16)
    stb = st.astype(BF16)
    st_new = st * jnp.exp(b_end)

    outs = []
    for h in range(GLA_HEADS):
        vh = vb[:, h * GLA_DV:(h + 1) * GLA_DV]
        vf = v[:, h * GLA_DV:(h + 1) * GLA_DV]
        vt = jnp.concatenate(
            [jnp.broadcast_to(vf[i * GLA_SUB:(i + 1) * GLA_SUB][None], (GLA_SUB, GLA_SUB, GLA_DV))
             .reshape(GLA_SUB * GLA_SUB, GLA_DV) for i in range(nsb)], axis=0)
        xh = rsum[:, h * GLA_DV:(h + 1) * GLA_DV] * vt
        o = _dot(msel, xh.astype(BF16))
        o = o + _dot(a_off[h].astype(BF16), vh)
        o = o + _dot_nt(jnp.where(head_masks[h], qb, 0.0).astype(BF16), stb)
        outs.append(o)
        st_new = st_new + jnp.where(head_masks[h], _dot_tn(vh, kd), 0.0)
    return outs, st_new


def _gla_kernel(q_ref, k_ref, la_ref, v_ref, r_ref, st0_ref, g_ref, tri_ref, hs_ref, ms_ref,
                o_ref, st_ref, st_scr, *, c, nchunks):
    @pl.when(pl.program_id(1) == 0)
    def _():
        st_scr[...] = st0_ref[0]

    def body(ci, carry):
        r0 = pl.multiple_of(ci * c, c)
        rows = pl.ds(r0, c)
        outs, st_new = _gla_chunk(q_ref[0, rows, :], k_ref[0, rows, :], la_ref[0, rows, :], v_ref[0, rows, :],
                                  st_scr[...], c, tri_ref[...], hs_ref[...], ms_ref[...])
        st_scr[...] = st_new
        r = r_ref[0, rows, :]
        g = g_ref[...]
        for h in range(GLA_HEADS):
            o = outs[h]
            sl = slice(h * GLA_DV, (h + 1) * GLA_DV)
            on = o * lax.rsqrt(jnp.mean(o * o, axis=-1, keepdims=True) + LN_EPS) * g
            rh = r[:, sl]
            o_ref[0, rows, sl] = (on * (rh * _sigmoid(rh))).astype(BF16)
        return carry

    lax.fori_loop(0, nchunks, body, 0, unroll=8 if nchunks % 8 == 0 else 1)
    st_ref[0] = st_scr[...]


def _gla(q3, k3, la3, v3, r3, st0, gnorm):
    bsz, l, _ = q3.shape
    c = min(l, CHUNK)
    tl = min(l, 512)
    blk = lambda n: pl.BlockSpec((1, tl, n), lambda b, i: (b, i, 0))
    st_spec = pl.BlockSpec((1, GLA_DV, GLA_QK), lambda b, i: (b, 0, 0))
    consts = _gla_consts(c)
    return pl.pallas_call(
        functools.partial(_gla_kernel, c=c, nchunks=tl // c),
        grid=(bsz, l // tl),
        in_specs=[blk(GLA_QK), blk(GLA_QK), blk(GLA_QK), blk(GLA_V), blk(GLA_V), st_spec,
                  _full_spec(gnorm, 2)] + [_full_spec(a, 2) for a in consts],
        out_specs=[blk(GLA_V), st_spec],
        out_shape=[jax.ShapeDtypeStruct((bsz, l, GLA_V), BF16),
                   jax.ShapeDtypeStruct((bsz, GLA_DV, GLA_QK), F32)],
        scratch_shapes=[pltpu.VMEM((GLA_DV, GLA_QK), F32)],
        compiler_params=_cp("parallel", "arbitrary"),
        name="gla",
    )(q3, k3, la3, v3, r3, st0, gnorm, *consts)


def _pool_kernel(x_ref, halo_ref, hist_ref, w_ref, scale_ref, o_ref, *, tl, pos0):
    i = pl.program_id(1)
    x = x_ref[0].astype(F32)
    prev = jnp.where(i == 0, hist_ref[0], halo_ref[0].astype(F32))
    e = jnp.concatenate([prev, x], axis=0)
    t = i * tl + lax.broadcasted_iota(I32, (tl, 1), 0)
    pos = pos0 + t
    sums = []
    shift = 1
    for g, w in enumerate(POOL_WINDOWS):
        e = e[:, POOL_CH:] if g > 0 else e
        while shift < w:
            e = e[shift:] + e[:-shift]
            shift *= 2
        off = POOL_HALO - (w - 1)
        sums.append(e[off:off + tl, :POOL_CH])
    outs = []
    for g, w in enumerate(POOL_WINDOWS):
        cnt = jnp.minimum(pos + 1, w).astype(F32)
        mix = sums[g] / cnt - x[:, g * POOL_CH:(g + 1) * POOL_CH]
        outs.append(_dot(mix.astype(BF16), w_ref[g]))
    o_ref[0] = (jnp.concatenate(outs, axis=1) * scale_ref[...]).astype(BF16)


def _pool(xp3, hist16, pool_w, pool_scale, pos0):
    bsz, l, _ = xp3.shape
    tl = min(l, 512)
    per = tl // POOL_HALO
    return pl.pallas_call(
        functools.partial(_pool_kernel, tl=tl, pos0=pos0),
        grid=(bsz, l // tl),
        in_specs=[pl.BlockSpec((1, tl, POOL_WIDTH), lambda b, i: (b, i, 0)),
                  pl.BlockSpec((1, POOL_HALO, POOL_WIDTH), lambda b, i: (b, jnp.maximum(i * per - 1, 0), 0)),
                  pl.BlockSpec((1, POOL_HALO, POOL_WIDTH), lambda b, i: (b, 0, 0)),
                  _full_spec(pool_w, 2), _full_spec(pool_scale, 2)],
        out_specs=pl.BlockSpec((1, tl, POOL_WIDTH), lambda b, i: (b, i, 0)),
        out_shape=jax.ShapeDtypeStruct((bsz, l, POOL_WIDTH), BF16),
        compiler_params=_cp("parallel", "parallel"),
        name="pool",
    )(xp3, xp3, hist16, pool_w, pool_scale)


def _out_route_kernel(a_ref, b_ref, h_ref, w_ref, g_ref, bt_ref, wr_ref, br_ref,
                      h1_ref, xs_ref, rs_ref, rg_ref, n16_ref, *, tm, slots):
    half = w_ref.shape[0] // 2
    y = _dot(a_ref[...].astype(BF16), w_ref[0:half, :]) + _dot(b_ref[...].astype(BF16), w_ref[half:, :])
    x = _layernorm(ALPHA * h_ref[...].astype(F32) + y, g_ref[...], bt_ref[...])
    h1_ref[...] = x.astype(BF16)

    xh = x.astype(BF16)
    xl = (x - xh.astype(F32)).astype(BF16)
    hl = _dot(xh, wr_ref[...])
    logits = hl[:, :LANES] + hl[:, LANES:] + _dot(xl, wr_ref[:, 0:LANES]) + br_ref[...]

    lane = lax.broadcasted_iota(I32, (tm, LANES), 1)
    lanef = lane.astype(F32)
    neg = -jnp.inf
    big = jnp.float32(1 << 20)

    def first_lane(hit):
        return jnp.min(jnp.where(hit, lanef, big), axis=-1, keepdims=True).astype(I32)

    gl = jnp.where(lane < N_GROUPS, logits, neg)
    gmax = jnp.max(gl, axis=-1, keepdims=True)
    g_sel = first_lane(gl == gmax)
    g_prob = 1.0 / jnp.sum(jnp.exp(gl - gmax), axis=-1, keepdims=True)
    eidx = lane - N_GROUPS
    in_grp = (eidx >= 0) & (eidx < N_EXPERTS) & ((eidx // EXPERTS_PER_GROUP) == g_sel)
    el = jnp.where(in_grp, logits, neg)
    v1 = jnp.max(el, axis=-1, keepdims=True)
    i1 = first_lane(el == v1)
    el2 = jnp.where(lane == i1, neg, el)
    v2 = jnp.max(el2, axis=-1, keepdims=True)
    i2 = first_lane(el2 == v2)
    e21 = jnp.exp(v2 - v1)
    gate1 = g_prob / (1.0 + e21)
    gate2 = g_prob * e21 / (1.0 + e21)
    e1 = i1 - N_GROUPS
    e2 = i2 - N_GROUPS

    oh1 = lane == e1
    oh2 = lane == e2
    oh = oh1.astype(F32) + oh2.astype(F32)
    ti = lax.broadcasted_iota(I32, (tm, tm), 0)
    tj = lax.broadcasted_iota(I32, (tm, tm), 1)
    before = _dot((tj < ti).astype(BF16), oh.astype(BF16))
    cnt = jnp.sum(oh, axis=0, keepdims=True)
    n16 = jnp.floor((cnt + (MOE_CHUNK - 1)) * (1.0 / MOE_CHUNK))
    n16_8 = jnp.broadcast_to(n16, (8, LANES))
    ui = lax.broadcasted_iota(I32, (LANES, LANES), 0)
    uj = lax.broadcasted_iota(I32, (LANES, LANES), 1)
    run_start = _dot(n16_8.astype(BF16), (ui < uj).astype(BF16))[0:1]
    slot_of = MOE_CHUNK * run_start + before
    slot1 = jnp.sum(jnp.where(oh1, slot_of, 0.0), axis=-1, keepdims=True).astype(I32)
    slot2 = jnp.sum(jnp.where(oh2, slot_of, 0.0), axis=-1, keepdims=True).astype(I32)
    sl = lax.broadcasted_iota(I32, (tm, slots), 1)
    place = ((sl == slot1) | (sl == slot2)).astype(BF16)
    xs_ref[...] = _dot_tn(place, xh).astype(BF16)

    rs_ref[...] = jnp.where(lane == 0, slot1, jnp.where(lane == 1, slot2, 0))
    rg_ref[...] = jnp.where(lane == 0, gate1, jnp.where(lane == 1, gate2, 0.0))
    n16_ref[...] = n16_8.astype(I32)


def _moe_slots(tm):
    worst = tm * TOP_K + N_EXPERTS * (MOE_CHUNK - 1)
    return -(-worst // MOE_ROWS) * MOE_ROWS


def _out_route(a2, b2, h2, w_out, ln_g, ln_b, w_r, b_r):
    t = h2.shape[0]
    tm = min(512, t)
    nt = t // tm
    slots = _moe_slots(tm)
    row = lambda n: pl.BlockSpec((tm, n), lambda i: (i, 0))
    return pl.pallas_call(
        functools.partial(_out_route_kernel, tm=tm, slots=slots),
        grid=(nt,),
        in_specs=[row(a2.shape[1]), row(b2.shape[1]), row(D_MODEL)]
                 + [_full_spec(a, 1) for a in (w_out, ln_g, ln_b, w_r, b_r)],
        out_specs=[row(D_MODEL), pl.BlockSpec((slots, D_MODEL), lambda i: (i, 0)), row(LANES), row(LANES),
                   pl.BlockSpec((8, LANES), lambda i: (i, 0))],
        out_shape=[jax.ShapeDtypeStruct((t, D_MODEL), BF16), jax.ShapeDtypeStruct((nt * slots, D_MODEL), BF16),
                   jax.ShapeDtypeStruct((t, LANES), I32), jax.ShapeDtypeStruct((t, LANES), F32),
                   jax.ShapeDtypeStruct((nt * 8, LANES), I32)],
        compiler_params=_cp("parallel"),
        name="out_route",
    )(a2, b2, h2, w_out, ln_g, ln_b, w_r, b_r)


def _chunk_rows(chunk):
    return pl.ds(pl.multiple_of(chunk * MOE_CHUNK, MOE_CHUNK), MOE_CHUNK)


def _expert_kernel(src_ref, nreal_ref, be_ref, nu_ref, xs_ref, w1_ref, w3_ref, w2_ref, ys_ref,
                   xbuf, ybuf, gsem, ssem, *, cpb):
    del be_ref
    b = pl.program_id(0)
    nu = nu_ref[0]

    rows = cpb * MOE_CHUNK

    def gather(blk, slot, j):
        return pltpu.make_async_copy(xs_ref.at[_chunk_rows(src_ref[blk * cpb + j]), :],
                                     xbuf.at[slot, pl.ds(j * MOE_CHUNK, MOE_CHUNK), :], gsem.at[slot])

    def scatter(blk, slot, j):
        rows_j = pl.ds(j * MOE_CHUNK, MOE_CHUNK) if isinstance(j, int) else _chunk_rows(j)
        return pltpu.make_async_copy(ybuf.at[slot, rows_j, :],
                                     ys_ref.at[_chunk_rows(src_ref[blk * cpb + j]), :],
                                     ssem.at[slot])

    def start_gather(blk, slot):
        for j in range(cpb):
            gather(blk, slot, j).start()

    def wait_gather(slot):
        pltpu.make_async_copy(xs_ref.at[pl.ds(0, rows), :], xbuf.at[slot], gsem.at[slot]).wait()

    def for_real_chunks(blk, fn):
        def body(j, carry):
            fn(j)
            return carry
        lax.fori_loop(0, nreal_ref[blk], body, 0)

    def wait_scatter(blk, slot):
        full = nreal_ref[blk] == cpb

        @pl.when(full)
        def _():
            pltpu.make_async_copy(ybuf.at[slot], ys_ref.at[pl.ds(0, rows), :], ssem.at[slot]).wait()

        @pl.when(jnp.logical_not(full))
        def _():
            for_real_chunks(blk, lambda j: scatter(blk, slot, j).wait())

    @pl.when(b < nu)
    def _():
        slot = b % 2

        @pl.when(b == 0)
        def _():
            start_gather(b, slot)

        @pl.when(b + 1 < nu)
        def _():
            start_gather(b + 1, 1 - slot)

        wait_gather(slot)

        @pl.when(b >= 2)
        def _():
            wait_scatter(b - 2, slot)

        xb = xbuf[slot]
        a = _dot(xb, w1_ref[0, 0].astype(BF16))
        hid = a * _sigmoid(a) * _dot(xb, w3_ref[0, 0].astype(BF16))
        ybuf[slot] = _dot(hid.astype(BF16), w2_ref[0, 0].astype(BF16)).astype(BF16)
        full = nreal_ref[b] == cpb

        @pl.when(full)
        def _():
            for j in range(cpb):
                scatter(b, slot, j).start()

        @pl.when(jnp.logical_not(full))
        def _():
            for_real_chunks(b, lambda j: scatter(b, slot, j).start())

        @pl.when(b == nu - 1)
        def _():
            wait_scatter(b, slot)

            @pl.when(b >= 1)
            def _():
                wait_scatter(b - 1, 1 - slot)


def _experts(xs, src, nreal, block_e, n_used, w1, w3, w2, layer):
    nblk = nreal.shape[0]
    cpb = src.shape[0] // nblk
    rows = cpb * MOE_CHUNK
    wspec = lambda shape: pl.BlockSpec((1,) + shape, lambda i, src, nr, be, nu: (layer, be[i], 0, 0))
    return pl.pallas_call(
        functools.partial(_expert_kernel, cpb=cpb),
        grid_spec=pltpu.PrefetchScalarGridSpec(
            num_scalar_prefetch=4,
            grid=(nblk,),
            in_specs=[pl.BlockSpec(memory_space=pl.ANY),
                      wspec((1, D_MODEL, D_EXPERT)), wspec((1, D_MODEL, D_EXPERT)), wspec((1, D_EXPERT, D_MODEL))],
            out_specs=pl.BlockSpec(memory_space=pl.ANY),
            scratch_shapes=[pltpu.VMEM((2, rows, D_MODEL), BF16), pltpu.VMEM((2, rows, D_MODEL), BF16),
                            pltpu.SemaphoreType.DMA((2,)), pltpu.SemaphoreType.DMA((2,))]),
        out_shape=jax.ShapeDtypeStruct(xs.shape, xs.dtype),
        input_output_aliases={4: 0},
        compiler_params=_cp("arbitrary"),
        name="moe_experts",
    )(src, nreal, block_e, n_used, xs, w1, w3, w2)


def _combine_kernel(h_ref, rs_ref, rg_ref, ys_ref, g_ref, b_ref, o_ref, *, tm, slots):
    rs = rs_ref[...]
    rg = rg_ref[...]
    sl = lax.broadcasted_iota(I32, (tm, slots), 1).astype(jnp.int16)
    s1, s2 = rs[:, 0:1].astype(jnp.int16), rs[:, 1:2].astype(jnp.int16)
    g1, g2 = rg[:, 0:1].astype(BF16), rg[:, 1:2].astype(BF16)
    weight = jnp.where(sl == s1, g1, jnp.where(sl == s2, g2, jnp.zeros((), BF16)))
    y = _dot(weight, ys_ref[...])
    o_ref[...] = _layernorm(ALPHA * h_ref[...].astype(F32) + y, g_ref[...], b_ref[...]).astype(o_ref.dtype)


def _combine(h2, rs, rg, ys, ln_g, ln_b, out_dtype):
    t = h2.shape[0]
    tm = min(512, t)
    slots = ys.shape[0] // (t // tm)
    row = lambda n: pl.BlockSpec((tm, n), lambda i: (i, 0))
    return pl.pallas_call(
        functools.partial(_combine_kernel, tm=tm, slots=slots),
        grid=(t // tm,),
        in_specs=[row(D_MODEL), row(LANES), row(LANES), pl.BlockSpec((slots, D_MODEL), lambda i: (i, 0)),
                  _full_spec(ln_g, 1), _full_spec(ln_b, 1)],
        out_specs=row(D_MODEL),
        out_shape=jax.ShapeDtypeStruct((t, D_MODEL), out_dtype),
        compiler_params=_cp("parallel"),
        name="moe_combine",
    )(h2, rs, rg, ys, ln_g, ln_b)


def _moe(h1, xs, rs, rg, n16_rows, w1, w3, w2, layer, ln_g, ln_b, out_dtype):
    t = h1.shape[0]
    tm = min(512, t)
    nt = t // tm
    slots = xs.shape[0] // nt
    n16 = n16_rows[::8, :N_EXPERTS]
    cpb = max(4, min(MOE_BLOCK_CHUNKS, (t * TOP_K) // (N_EXPERTS * MOE_CHUNK)))
    per_e = jnp.sum(n16, axis=0)
    blocks_e = (per_e + cpb - 1) // cpb
    blk_end = jnp.cumsum(blocks_e)
    blk_start = blk_end - blocks_e
    n_used = blk_end[-1:].astype(I32)
    max_chunks = (t * TOP_K) // MOE_CHUNK + nt * N_EXPERTS
    nblk = max_chunks // cpb + N_EXPERTS
    blk = jnp.arange(nblk, dtype=I32)
    block_e = jnp.minimum(jnp.sum(blk_end[None, :] <= blk[:, None], axis=1), N_EXPERTS - 1).astype(I32)
    of_e = block_e[:, None] == jnp.arange(N_EXPERTS, dtype=I32)[None, :]
    pick_e = lambda tab: jnp.sum(jnp.where(of_e, tab[None, :], 0), axis=1)
    pick_col = lambda tab: jnp.sum(jnp.where(of_e[:, None, :], tab[None, :, :], 0), axis=2)
    run_end = pick_col(jnp.cumsum(n16, axis=0))
    tile_off = pick_col(jnp.cumsum(n16, axis=1) - n16)
    k = ((blk - pick_e(blk_start)) * cpb)[:, None] + jnp.arange(cpb, dtype=I32)[None, :]
    real = (k < pick_e(per_e)[:, None]) & (blk < n_used[0])[:, None]
    done = run_end[:, None, :] <= k[:, :, None]
    tile = jnp.minimum(jnp.sum(done, axis=2), nt - 1)
    run_first = jnp.max(jnp.where(done, run_end[:, None, :], 0), axis=2)
    of_t = tile[:, :, None] == jnp.arange(nt, dtype=I32)[None, None, :]
    src = tile * (slots // MOE_CHUNK) + jnp.sum(jnp.where(of_t, tile_off[:, None, :], 0), axis=2) + (k - run_first)
    src = jnp.where(real, src, src[:, :1])
    src = jnp.where((blk < n_used[0])[:, None], src, 0).astype(I32).reshape(-1)
    nreal = jnp.sum(real, axis=1).astype(I32)
    ys = _experts(xs, src, nreal, block_e, n_used, w1, w3, w2, layer)
    return _combine(h1, rs, rg, ys, ln_g, ln_b, out_dtype)


def _swap_halves(x):
    lane = lax.broadcasted_iota(I32, x.shape, 1)
    first = (lane % MLA_ROPE) < (MLA_ROPE // 2)
    return jnp.where(first, pltpu.roll(x, LANES - MLA_ROPE // 2, 1), pltpu.roll(x, MLA_ROPE // 2, 1))


def _rope(x, cos, sin):
    parts = []
    for t in range(x.shape[1] // LANES):
        sl = slice(t * LANES, (t + 1) * LANES)
        parts.append(x[:, sl] * cos[:, sl] + _swap_halves(x[:, sl]) * sin[:, sl])
    return parts[0] if len(parts) == 1 else jnp.concatenate(parts, axis=1)


def _odd_in_kernel(h_ref, w_ref, qg_ref, wqn_ref, wqp_ref, wuk_ref, perm_ref, kvg_ref,
                   cq_ref, sq_ref, ck_ref, sk_ref, gg_ref, gb_ref, ws_ref, bs_ref,
                   q_ref, kc_ref, kt_ref, ckv_ref, kpe_ref, gated_ref, vn_ref, *, tl, cl):
    hb = h_ref[0].astype(BF16)
    z = _dot(hb, w_ref[...])
    o_ckv = MLA_Q_RANK
    o_u = o_ckv + MLA_KV_RANK
    o_v = o_u + GMLP_WIDTH
    o_k = o_v + GMLP_WIDTH
    cq = z[:, :MLA_Q_RANK]
    cqn = cq * lax.rsqrt(jnp.mean(cq * cq, axis=-1, keepdims=True) + LN_EPS) * qg_ref[...]
    cqb = cqn.astype(BF16)
    qn = _dot(cqb, wqn_ref[...])
    qp = _dot(cqb, wqp_ref[...])
    qp = _rope(qp, cq_ref[...], sq_ref[...])
    qpe = _dot((qp * MLA_SCALE).astype(BF16), perm_ref[...])
    for h in range(MLA_HEADS):
        sl = slice(h * LANES, (h + 1) * LANES)
        qa = _dot((qn[:, sl] * MLA_SCALE).astype(BF16), wuk_ref[h])
        q_ref[0, h, :, 0:LANES] = qa.astype(BF16)
        q_ref[0, h, :, LANES:] = qpe[:, sl].astype(BF16)

    ckv = z[:, o_ckv:o_u]
    ckvn = ckv * lax.rsqrt(jnp.mean(ckv * ckv, axis=-1, keepdims=True) + LN_EPS) * kvg_ref[...]
    kp = z[:, o_k:]
    kp = _rope(kp, ck_ref[...], sk_ref[...])
    ckv_ref[0] = ckvn
    kpe_ref[0] = kp[:, :MLA_ROPE]
    one = (lax.broadcasted_iota(I32, (1, LANES), 1) == MLA_ONE_LANE - LANES).astype(F32)
    kp1 = kp + one
    kc_ref[0, :, 0:LANES] = ckvn.astype(BF16)
    kc_ref[0, :, LANES:] = kp1.astype(BF16)
    kt_ref[0, 0:LANES, :] = ckvn.T.astype(BF16)
    kt_ref[0, LANES:, :] = kp1.T.astype(BF16)

    gu = _gelu(z[:, o_u:o_v])
    vn = _layernorm(_gelu(z[:, o_v:o_k]), gg_ref[...], gb_ref[...])
    vn_ref[0] = vn
    vnb = vn.astype(BF16)
    for n in range(tl // cl):
        rs = slice(n * cl, (n + 1) * cl)
        for g in range(GMLP_GROUPS):
            ls = slice(g * GMLP_CH, (g + 1) * GMLP_CH)
            sg = _dot(ws_ref[g], vnb[rs, ls]) + bs_ref[:, ls]
            gated_ref[0, rs, ls] = (gu[rs, ls] * sg).astype(BF16)


def _odd_in(h3, w_in, q_g, w_qn, w_qp, w_uk, perm, kv_g, cos_q, sin_q, cos_k, sin_k,
            gm_g, gm_b, ws, bs):
    bsz, l, _ = h3.shape
    tl = min(l, 512)
    cl = min(l, GMLP_CHUNK)
    rowb = lambda n: pl.BlockSpec((1, tl, n), lambda b, i: (b, i, 0))
    tab = lambda n: pl.BlockSpec((tl, n), lambda b, i: (i, 0))
    consts = (w_in, q_g, w_qn, w_qp, w_uk, perm, kv_g)
    consts2 = (gm_g, gm_b, ws, bs)
    return pl.pallas_call(
        functools.partial(_odd_in_kernel, tl=tl, cl=cl),
        grid=(bsz, l // tl),
        in_specs=[rowb(D_MODEL)] + [_full_spec(a, 2) for a in consts]
                 + [tab(MLA_HEADS * MLA_ROPE), tab(MLA_HEADS * MLA_ROPE), tab(LANES), tab(LANES)]
                 + [_full_spec(a, 2) for a in consts2],
        out_specs=[pl.BlockSpec((1, MLA_HEADS, tl, MLA_QW), lambda b, i: (b, 0, i, 0)),
                   rowb(MLA_QW), pl.BlockSpec((1, MLA_QW, tl), lambda b, i: (b, 0, i)),
                   rowb(MLA_KV_RANK), rowb(MLA_ROPE), rowb(GMLP_WIDTH), rowb(GMLP_WIDTH)],
        out_shape=[jax.ShapeDtypeStruct((bsz, MLA_HEADS, l, MLA_QW), BF16),
                   jax.ShapeDtypeStruct((bsz, l, MLA_QW), BF16),
                   jax.ShapeDtypeStruct((bsz, MLA_QW, l), BF16),
                   jax.ShapeDtypeStruct((bsz, l, MLA_KV_RANK), F32),
                   jax.ShapeDtypeStruct((bsz, l, MLA_ROPE), F32),
                   jax.ShapeDtypeStruct((bsz, l, GMLP_WIDTH), BF16),
                   jax.ShapeDtypeStruct((bsz, l, GMLP_WIDTH), F32)],
        compiler_params=_cp("parallel", "parallel"),
        name="odd_in",
    )(h3, *consts, cos_q, sin_q, cos_k, sin_k, *consts2)


def _attn_kernel(qi_ref, kj_ref, flag_ref, q_ref, k_ref, kt_ref, wuv_ref, o_ref,
                 m_scr, l_scr, acc_scr, *, tq, hpb, tk, nsub, pos0, n_keys):
    p = pl.program_id(1)
    flag = flag_ref[p]
    nblk = MLA_HEADS // hpb
    width = hpb * tq

    @pl.when((flag & 1) != 0)
    def _():
        m_scr[...] = jnp.full_like(m_scr, -jnp.inf)
        l_scr[...] = jnp.zeros_like(l_scr)
        acc_scr[...] = jnp.zeros_like(acc_scr)

    def step(sub, hidden):
        kt = kt_ref[0, :, sub * tk:(sub + 1) * tk]
        kk = k_ref[0, sub * tk:(sub + 1) * tk, :]
        if hidden:
            qpos = pos0 + qi_ref[p] * tq + lax.broadcasted_iota(I32, (1, width), 1) % tq
            kpos = (kj_ref[p] * nsub + sub) * tk + lax.broadcasted_iota(I32, (tk, 1), 0)
            visible = ((kpos // CHUNK) <= (qpos // CHUNK)) & (kpos < n_keys)
        half = tk // MLA_KEY_PARTS
        for h in range(nblk):
            s = _dot_nt(kk, q_ref[0, h])
            if hidden:
                s = jnp.where(visible, s, -jnp.inf)
            m = m_scr[h]
            acc = acc_scr[h]
            l = l_scr[h]
            for part in range(MLA_KEY_PARTS):
                ks = slice(part * half, (part + 1) * half)
                sp = s[ks]
                m_new = jnp.maximum(m, jnp.max(sp, axis=0, keepdims=True))
                alpha = jnp.exp(m - m_new)
                pr = jnp.exp(sp - m_new).astype(BF16)
                acc = alpha * acc + _dot(kt[:MLA_KV_RANK, ks], pr)
                l = alpha * l + _dot(kt[MLA_ONE_LANE:MLA_ONE_LANE + 16, ks], pr)
                m = m_new
            acc_scr[h] = acc
            l_scr[h] = l
            m_scr[h] = m

    for sub in range(nsub):
        seen = (flag & (4 << (2 * sub))) != 0
        some_hidden = (flag & (8 << (2 * sub))) != 0

        @pl.when(seen & some_hidden)
        def _():
            step(sub, True)

        @pl.when(seen & jnp.logical_not(some_hidden))
        def _():
            step(sub, False)

    @pl.when((flag & 2) != 0)
    def _():
        if hpb == 1:
            for pair in range(MLA_HEADS // 2):
                tile = jnp.zeros((tq, LANES), F32)
                for h in (2 * pair, 2 * pair + 1):
                    lat = (acc_scr[h] / l_scr[h, 0:1, :]).astype(BF16)
                    tile = tile + _dot_tn(lat, wuv_ref[h])
                o_ref[0, :, pair * LANES:(pair + 1) * LANES] = tile.astype(BF16)
        else:
            head_of_lane = lax.broadcasted_iota(I32, (1, MLA_HEADS * MLA_V), 1) // MLA_V
            out = jnp.zeros((tq, MLA_HEADS * MLA_V), F32)
            for h in range(nblk):
                lat = (acc_scr[h] / l_scr[h, 0:1, :]).astype(BF16)
                full = _dot_tn(lat, wuv_ref[...])
                for hh in range(hpb):
                    out = out + jnp.where(head_of_lane == h * hpb + hh, full[hh * tq:(hh + 1) * tq], 0.0)
            o_ref[0] = out.astype(BF16)


def _attn_pairs(l, tq, tk, nsub, pos0, n_keys):
    qi, kj, flag = [], [], []
    for i in range(l // tq):
        q_first = pos0 + i * tq
        q_last = q_first + tq - 1
        vis = min(CHUNK * (q_last // CHUNK) + CHUNK - 1, n_keys - 1)
        ntiles = vis // tk + 1
        nsteps = -(-ntiles // nsub)
        for j in range(nsteps):
            f = int(j == 0) + 2 * int(j == nsteps - 1)
            for s in range(nsub):
                t = j * nsub + s
                if t < ntiles:
                    hidden = ((t + 1) * tk - 1) // CHUNK > q_first // CHUNK or (t + 1) * tk > n_keys
                    f += (4 + 8 * int(hidden)) << (2 * s)
            qi.append(i); kj.append(j); flag.append(f)
    return [jnp.asarray(np.array(a, np.int32)) for a in (qi, kj, flag)]


def _attention(q4, kc3, kt3, w_uv, pos0, n_keys, tq, tk, nsub):
    bsz, _, l, _ = q4.shape
    pairs = _attn_pairs(l, tq, tk, nsub, pos0, n_keys)
    npairs = int(pairs[0].shape[0])
    hpb = max(1, min(MLA_HEADS, 256 // tq)) if l == tq else 1
    nblk = MLA_HEADS // hpb
    width = hpb * tq
    q4 = q4.reshape(bsz, nblk, hpb * l, MLA_QW)
    if hpb == 1:
        heads = w_uv.reshape(MLA_KV_RANK, MLA_HEADS, MLA_V).transpose(1, 0, 2)
        w_uv = jnp.stack([jnp.pad(heads[h], ((0, 0), ((h % 2) * MLA_V, (1 - h % 2) * MLA_V)))
                          for h in range(MLA_HEADS)])
    return pl.pallas_call(
        functools.partial(_attn_kernel, tq=tq, hpb=hpb, tk=tk, nsub=nsub, pos0=pos0, n_keys=n_keys),
        grid_spec=pltpu.PrefetchScalarGridSpec(
            num_scalar_prefetch=3,
            grid=(bsz, npairs),
            in_specs=[pl.BlockSpec((1, nblk, width, MLA_QW), lambda b, p, qi, kj, f: (b, 0, qi[p], 0)),
                      pl.BlockSpec((1, nsub * tk, MLA_QW), lambda b, p, qi, kj, f: (b, kj[p], 0)),
                      pl.BlockSpec((1, MLA_QW, nsub * tk), lambda b, p, qi, kj, f: (b, 0, kj[p])),
                      pl.BlockSpec(w_uv.shape, lambda b, p, qi, kj, f: (0,) * w_uv.ndim)],
            out_specs=pl.BlockSpec((1, tq, MLA_HEADS * MLA_V), lambda b, p, qi, kj, f: (b, qi[p], 0)),
            scratch_shapes=[pltpu.VMEM((nblk, 1, width), F32),
                            pltpu.VMEM((nblk, 16, width), F32),
                            pltpu.VMEM((nblk, MLA_KV_RANK, width), F32)]),
        out_shape=jax.ShapeDtypeStruct((bsz, l, MLA_HEADS * MLA_V), BF16),
        compiler_params=_cp("parallel", "arbitrary"),
        name="mla_attention",
    )(*pairs, q4, kc3, kt3, w_uv)


def _rope_tables(pos0, l, width):
    half = MLA_ROPE // 2
    inv = ROPE_THETA ** (-jnp.arange(half, dtype=F32) * 2.0 / MLA_ROPE)
    ang = (pos0 + jnp.arange(l)).astype(F32)[:, None] * inv[None, :]
    cos, sin = jnp.cos(ang), jnp.sin(ang)
    cos32 = jnp.concatenate([cos, cos], axis=1)
    sin32 = jnp.concatenate([-sin, sin], axis=1)
    reps = width // MLA_ROPE
    return jnp.tile(cos32, (1, reps)), jnp.tile(sin32, (1, reps))


def _prep_even(w_in, w_gate_up, b_gate, gla_norm_g, pool_w, pool_scale, w_out):
    o_r = 2 * GLA_QK + GLA_V
    o_g = o_r + GLA_V
    o_p = o_g + GLA_GATE_RANK
    w_main = jnp.concatenate([w_in[:, :o_g], w_in[:, o_p:]], axis=1).astype(BF16)
    w_g = jnp.pad(w_in[:, o_g:o_p], ((0, 0), (0, LANES - GLA_GATE_RANK))).astype(BF16)
    w_gu = jnp.pad(w_gate_up, ((0, LANES - GLA_GATE_RANK), (0, 0))).astype(BF16)
    return dict(w_main=w_main, w_g=w_g, w_gu=w_gu, b_g=b_gate.reshape(1, -1),
                gnorm=gla_norm_g.reshape(1, -1), pool_w=pool_w.astype(BF16),
                pool_scale=pool_scale.reshape(1, -1), w_out=w_out.astype(BF16))


def _prep_odd(w_in, q_norm_g, w_uq, kv_norm_g, w_uk, w_uv, gm_g, gm_b, gm_ws, gm_bs, w_out):
    o_ckv = MLA_Q_RANK
    o_kpe = o_ckv + MLA_KV_RANK
    o_u = o_kpe + MLA_ROPE
    w_in2 = jnp.concatenate([w_in[:, :o_kpe], w_in[:, o_u:], w_in[:, o_kpe:o_u],
                             jnp.zeros((D_MODEL, LANES - MLA_ROPE), F32)], axis=1).astype(BF16)
    uq = w_uq.reshape(MLA_Q_RANK, MLA_HEADS, MLA_NOPE + MLA_ROPE)
    w_qn = jnp.pad(uq[:, :, :MLA_NOPE], ((0, 0), (0, 0), (0, LANES - MLA_NOPE)))
    w_qn = w_qn.reshape(MLA_Q_RANK, MLA_HEADS * LANES).astype(BF16)
    w_qp = uq[:, :, MLA_NOPE:].reshape(MLA_Q_RANK, MLA_HEADS * MLA_ROPE).astype(BF16)
    uk = w_uk.reshape(MLA_KV_RANK, MLA_HEADS, MLA_NOPE).transpose(1, 2, 0)
    w_ukp = jnp.pad(uk, ((0, 0), (0, LANES - MLA_NOPE), (0, 0))).astype(BF16)
    src = np.arange(MLA_HEADS * MLA_ROPE)
    perm = np.zeros((MLA_HEADS * MLA_ROPE, MLA_HEADS * LANES), np.float32)
    perm[src, (src // MLA_ROPE) * LANES + src % MLA_ROPE] = 1.0
    return dict(w_in=w_in2, q_g=q_norm_g.reshape(1, -1), w_qn=w_qn, w_qp=w_qp, w_uk=w_ukp,
                perm=jnp.asarray(perm, BF16), kv_g=kv_norm_g.reshape(1, -1), w_uv=w_uv.astype(BF16),
                gm_g=gm_g.reshape(1, -1), gm_b=gm_b.reshape(1, -1), gm_ws=gm_ws, gm_bs=gm_bs,
                w_out=w_out.astype(BF16))


def _prep_route(wg, bg, we, be):
    w_r = jnp.pad(jnp.concatenate([wg, we], axis=1), ((0, 0), (0, LANES - N_GROUPS - N_EXPERTS)))
    w_hi = w_r.astype(BF16)
    w_lo = (w_r - w_hi.astype(F32)).astype(BF16)
    b_r = jnp.pad(jnp.concatenate([bg, be]), (0, LANES - N_GROUPS - N_EXPERTS)).reshape(1, LANES)
    return jnp.concatenate([w_hi, w_lo], axis=1), b_r


def _even_mixer(h3, st0, hist, pos0, pw):
    bsz, l, _ = h3.shape
    q, k, la, v, r, xp = _even_in(h3.reshape(bsz * l, D_MODEL), pw['w_main'], pw['w_g'], pw['w_gu'], pw['b_g'])
    to3 = lambda a: a.reshape(bsz, l, a.shape[-1])
    o, st = _gla(to3(q), to3(k), to3(la), to3(v), to3(r), st0, pw['gnorm'])
    xp3 = to3(xp)
    hist16 = jnp.pad(hist, ((0, 0), (POOL_HALO - POOL_HIST, 0), (0, 0)))
    pooled = _pool(xp3, hist16, pw['pool_w'], pw['pool_scale'], pos0)
    hist_new = jnp.concatenate([hist, xp3[:, -POOL_HIST:].astype(F32)], axis=1)[:, -POOL_HIST:]
    return o, pooled, st, hist_new


def _odd_mixer(h3, ckv_past, kpe_past, pw):
    bsz, l, _ = h3.shape
    n_past = ckv_past.shape[1]
    cos_q, sin_q = _rope_tables(n_past, l, MLA_HEADS * MLA_ROPE)
    cos_k, sin_k = _rope_tables(n_past, l, MLA_ROPE)
    padk = ((0, 0), (0, LANES - MLA_ROPE))
    cos_k, sin_k = jnp.pad(cos_k, padk), jnp.pad(sin_k, padk)
    cl = min(l, GMLP_CHUNK)
    ws = jnp.tril(pw['gm_ws'][:, :cl, :cl]).astype(BF16)
    bs = jnp.repeat(pw['gm_bs'][:, :cl].T, GMLP_CH, axis=1)
    q4, kc, kt, ckv, kpe, gated, vn = _odd_in(h3, pw['w_in'], pw['q_g'], pw['w_qn'], pw['w_qp'], pw['w_uk'],
                                              pw['perm'], pw['kv_g'], cos_q, sin_q, cos_k, sin_k,
                                              pw['gm_g'], pw['gm_b'], ws, bs)
    n_keys = n_past + l
    tq, tk, nsub = min(l, 256), 512, 2
    span = tk * nsub
    if n_past:
        past = jnp.concatenate([ckv_past, kpe_past, jnp.ones((bsz, n_past, 1), F32),
                                jnp.zeros((bsz, n_past, MLA_QW - MLA_ONE_LANE - 1), F32)], axis=2).astype(BF16)
        kc = jnp.concatenate([past, kc], axis=1)
        kt = jnp.concatenate([past.transpose(0, 2, 1), kt], axis=2)
    kc = jnp.pad(kc, ((0, 0), (0, -n_keys % span), (0, 0)))
    kt = jnp.pad(kt, ((0, 0), (0, 0), (0, -n_keys % span)))
    attn = _attention(q4, kc, kt, pw['w_uv'], n_past, n_keys, tq, tk, nsub)
    return attn, gated, ckv, kpe, vn


def _finish_layer(a3, b3, h3, w_out, lw):
    bsz, l, _ = h3.shape
    t = bsz * l
    h1, xs, rs, rg, n16 = _out_route(a3.reshape(t, -1), b3.reshape(t, -1), h3.reshape(t, D_MODEL), w_out,
                                     lw['ln_mix_g'], lw['ln_mix_b'], lw['w_r'], lw['b_r'])
    h2 = _moe(h1, xs, rs, rg, n16, lw['w1'], lw['w3'], lw['w2'], lw['layer'], lw['ln_ffn_g'], lw['ln_ffn_b'],
              lw['out_dtype'])
    return h2.reshape(bsz, l, D_MODEL)


def kernel(x_prompt, x_sample, state_gla, state_pool, cache_mla_ckv, cache_mla_kpe, w_in_even, w_gate_up, b_gate, gla_norm_g, pool_w, pool_scale, w_out_even, w_in_odd, mla_q_norm_g, mla_w_uq, mla_kv_norm_g, mla_w_uk, mla_w_uv, gmlp_norm_g, gmlp_norm_b, gmlp_ws, gmlp_bs, w_out_odd, ln_mix_g, ln_mix_b, router_group_w, router_group_b, router_expert_w, router_expert_b, expert_w1, expert_w3, expert_w2, ln_ffn_g, ln_ffn_b):
    hp, hs = x_prompt, x_sample
    bp = hp.shape[0]
    past_len = cache_mla_ckv.shape[2]
    gla_p, gla_s, pool_p, pool_s = [], [], [], []
    ckv_p, ckv_s, kpe_p, kpe_s, gv_s = [], [], [], [], []

    def state_to_t(s):
        return s.transpose(0, 3, 1, 2).reshape(s.shape[0], GLA_DV, GLA_QK)

    def state_from_t(st):
        return st.reshape(st.shape[0], GLA_DV, GLA_HEADS, GLA_DK).transpose(0, 2, 3, 1)

    for layer in range(DEPTH):
        i = layer // 2
        w_r, b_r = _prep_route(router_group_w[layer], router_group_b[layer],
                               router_expert_w[layer], router_expert_b[layer])
        lw = dict(ln_mix_g=ln_mix_g[layer].reshape(1, -1), ln_mix_b=ln_mix_b[layer].reshape(1, -1),
                  ln_ffn_g=ln_ffn_g[layer].reshape(1, -1), ln_ffn_b=ln_ffn_b[layer].reshape(1, -1),
                  w_r=w_r, b_r=b_r, w1=expert_w1, w3=expert_w3, w2=expert_w2, layer=layer,
                  out_dtype=F32 if layer == DEPTH - 1 else BF16)
        if layer % 2 == 0:
            pw = _prep_even(w_in_even[i], w_gate_up[i], b_gate[i], gla_norm_g[i], pool_w[i], pool_scale[i],
                            w_out_even[i])
            st0 = jnp.zeros((bp, GLA_DV, GLA_QK), F32)
            hist0 = jnp.zeros((bp, POOL_HIST, POOL_WIDTH), F32)
            op, pp, stp, histp = _even_mixer(hp, st0, hist0, 0, pw)
            os_, ps, sts, hists = _even_mixer(hs, state_to_t(state_gla[i]), state_pool[i], past_len, pw)
            gla_p.append(state_from_t(stp)); gla_s.append(state_from_t(sts))
            pool_p.append(histp); pool_s.append(hists)
            ap, bpj, as_, bsj = op, pp, os_, ps
        else:
            pw = _prep_odd(w_in_odd[i], mla_q_norm_g[i], mla_w_uq[i], mla_kv_norm_g[i], mla_w_uk[i], mla_w_uv[i],
                           gmlp_norm_g[i], gmlp_norm_b[i], gmlp_ws[i], gmlp_bs[i], w_out_odd[i])
            no_ckv = jnp.zeros((bp, 0, MLA_KV_RANK), F32)
            no_kpe = jnp.zeros((bp, 0, MLA_ROPE), F32)
            ap, bpj, cp, kp, _ = _odd_mixer(hp, no_ckv, no_kpe, pw)
            as_, bsj, cs, ks, vs = _odd_mixer(hs, cache_mla_ckv[i], cache_mla_kpe[i], pw)
            ckv_p.append(cp); ckv_s.append(cs); kpe_p.append(kp); kpe_s.append(ks); gv_s.append(vs)
        hp = _finish_layer(ap, bpj, hp, pw['w_out'], lw)
        hs = _finish_layer(as_, bsj, hs, pw['w_out'], lw)
    return (hp, hs, jnp.stack(gla_p), jnp.stack(gla_s), jnp.stack(pool_p), jnp.stack(pool_s),
            jnp.stack(ckv_p), jnp.stack(ckv_s), jnp.stack(kpe_p), jnp.stack(kpe_s), jnp.stack(gv_s))
```

```python
import functools

import numpy as np
import jax
import jax.numpy as jnp
from jax import lax
from jax.experimental import pallas as pl
from jax.experimental.pallas import tpu as pltpu

F32 = jnp.float32
BF16 = jnp.bfloat16
I32 = jnp.int32

D_MODEL = 1024
DEPTH = 2
CHUNK = 64
ALPHA = (2 * DEPTH) ** 0.25
LN_EPS = 1e-5

GLA_HEADS = 4
GLA_DV = 128
GLA_DK = 64
GLA_QK = GLA_HEADS * GLA_DK
GLA_V = GLA_HEADS * GLA_DV
GLA_GATE_RANK = 16
GLA_GATE_TAU = 16.0
GLA_SUB = 8

POOL_WIDTH = 512
POOL_CH = 128
POOL_WINDOWS = (2, 4, 8, 16)
POOL_HIST = 15
POOL_HALO = 16

MLA_HEADS = 8
MLA_NOPE = 64
MLA_ROPE = 32
MLA_V = 64
MLA_Q_RANK = 256
MLA_KV_RANK = 128
ROPE_THETA = 10000.0
MLA_SCALE = (MLA_NOPE + MLA_ROPE) ** -0.5
LOG2E = 1.4426950408889634
MLA_QSCALE = MLA_SCALE * LOG2E
MLA_QW = 256
MLA_ONE_LANE = MLA_KV_RANK + MLA_ROPE
MLA_KEY_PARTS = 2
GMLP_WIDTH = 512
GMLP_CH = 128
GMLP_GROUPS = 4
GMLP_CHUNK = 128

N_GROUPS = 4
EXPERTS_PER_GROUP = 8
N_EXPERTS = 32
TOP_K = 2
D_EXPERT = 256
MOE_ROWS = 512
MOE_CHUNK = 16
MOE_BLOCK_CHUNKS = MOE_ROWS // MOE_CHUNK

LANES = 128
VMEM_LIMIT = 48 * 1024 * 1024


def _cp(*sem):
    return pltpu.CompilerParams(dimension_semantics=sem, vmem_limit_bytes=VMEM_LIMIT)


def _dot(a, b):
    return jnp.dot(a, b, preferred_element_type=F32)


def _dot_nt(a, b):
    return lax.dot_general(a, b, (((1,), (1,)), ((), ())), preferred_element_type=F32)


def _dot_tn(a, b):
    return lax.dot_general(a, b, (((0,), (0,)), ((), ())), preferred_element_type=F32)


def _split3(x):
    hi = x.astype(BF16)
    r1 = x - hi.astype(F32)
    mid = r1.astype(BF16)
    lo = (r1 - mid.astype(F32)).astype(BF16)
    return hi, mid, lo


def _layernorm(x, g, b):
    mu = jnp.mean(x, axis=-1, keepdims=True)
    xc = x - mu
    var = jnp.mean(xc * xc, axis=-1, keepdims=True)
    return xc * lax.rsqrt(var + LN_EPS) * g + b


def _gelu(x):
    return 0.5 * x * (1.0 + jnp.tanh(0.7978845608028654 * (x + 0.044715 * (x * x * x))))


def _sigmoid(x):
    return 1.0 / (1.0 + jnp.exp(-x))


def _full_spec(a, nargs):
    nd = a.ndim
    if nargs == 1:
        return pl.BlockSpec(a.shape, lambda i: (0,) * nd)
    return pl.BlockSpec(a.shape, lambda i, j: (0,) * nd)


def _even_in_kernel(x_ref, w_ref, wg_ref, wgu_ref, bg_ref,
                    q_ref, k_ref, la_ref, v_ref, r_ref, xp_ref):
    xb = x_ref[...].astype(BF16)
    z = _dot(xb, w_ref[...])
    q_ref[...] = z[:, 0:GLA_QK] * (GLA_DK ** -0.5)
    k_ref[...] = z[:, GLA_QK:2 * GLA_QK]
    v_ref[...] = z[:, 2 * GLA_QK:2 * GLA_QK + GLA_V]
    r_ref[...] = z[:, 2 * GLA_QK + GLA_V:2 * GLA_QK + 2 * GLA_V]
    xp_ref[...] = z[:, 2 * GLA_QK + 2 * GLA_V:].astype(BF16)
    g = _dot(xb, wg_ref[...])
    pre = _dot(g.astype(BF16), wgu_ref[...]) + bg_ref[...]
    logsig = jnp.minimum(pre, 0.0) - jnp.log(1.0 + jnp.exp(-jnp.abs(pre)))
    la_ref[...] = logsig * (LOG2E / GLA_GATE_TAU)


def _even_in(x2, w_main, w_g, w_gu, b_g):
    t = x2.shape[0]
    tm = min(512, t)
    row = lambda n: pl.BlockSpec((tm, n), lambda i: (i, 0))
    widths = (GLA_QK, GLA_QK, GLA_QK, GLA_V, GLA_V, POOL_WIDTH)
    return pl.pallas_call(
        _even_in_kernel,
        grid=(t // tm,),
        in_specs=[row(D_MODEL)] + [_full_spec(a, 1) for a in (w_main, w_g, w_gu, b_g)],
        out_specs=[row(n) for n in widths],
        out_shape=[jax.ShapeDtypeStruct((t, n), BF16 if i == 5 else F32) for i, n in enumerate(widths)],
        compiler_params=_cp("parallel"),
        name="even_in",
    )(x2, w_main, w_g, w_gu, b_g)


def _gla_consts(c):
    n = c * GLA_SUB
    tri = np.tril(np.ones((c, c), np.float32))
    headsum = (np.arange(GLA_QK)[:, None] // GLA_DK == np.arange(GLA_V)[None, :] // GLA_DV).astype(np.float32)
    msel = (np.arange(n)[None, :] // GLA_SUB == np.arange(c)[:, None]).astype(np.float32)
    return [jnp.asarray(a, BF16) for a in (tri, headsum, msel)]


def _gla_chunk(q, k, la, v, st, c, tri, headsum, msel):
    lane = lax.broadcasted_iota(I32, (1, GLA_QK), 1)
    head_of_lane = lane // GLA_DK
    row = lax.broadcasted_iota(I32, (c, 1), 0)
    ii = lax.broadcasted_iota(I32, (c, c), 0)
    jj = lax.broadcasted_iota(I32, (c, c), 1)

    hi, mid, lo = _split3(la)
    b = _dot(tri, hi) + _dot(tri, mid) + _dot(tri, lo)

    head_masks = [head_of_lane == h for h in range(GLA_HEADS)]
    vb = v.astype(BF16)

    a_off = [jnp.zeros((c, c), F32) for _ in range(GLA_HEADS)]
    s = c // 2
    while s >= GLA_SUB:
        nblk = c // (2 * s)
        blk = row // (2 * s)
        right = ((row // s) % 2) == 1
        bref = jnp.zeros((c, GLA_QK), F32)
        for m in range(nblk):
            r0 = m * 2 * s + s - 1
            bref = jnp.where(blk == m, b[r0:r0 + 1, :], bref)
        qe = jnp.where(right, q * jnp.exp2(jnp.minimum(b - bref, 0.0)), 0.0)
        ke = jnp.where(right, 0.0, k * jnp.exp2(jnp.minimum(bref - b, 0.0))).astype(BF16)
        same = (ii // (2 * s)) == (jj // (2 * s))
        for h in range(GLA_HEADS):
            a = _dot_nt(jnp.where(head_masks[h], qe, 0.0).astype(BF16), ke)
            a_off[h] = a_off[h] + (a if nblk == 1 else jnp.where(same, a, 0.0))
        s //= 2

    nsb = c // GLA_SUB
    parts = []
    for i in range(nsb):
        sl = slice(i * GLA_SUB, (i + 1) * GLA_SUB)
        bi, qi, ki = b[sl], q[sl], k[sl]
        diff = bi[:, None, :] - bi[None, :, :]
        p = qi[:, None, :] * ki[None, :, :] * jnp.exp2(jnp.minimum(diff, 0.0))
        parts.append(p.reshape(GLA_SUB * GLA_SUB, GLA_QK))
    n = nsb * GLA_SUB * GLA_SUB
    idx = lax.broadcasted_iota(I32, (n, 1), 0)
    causal = (idx % GLA_SUB) <= ((idx // GLA_SUB) % GLA_SUB)
    pcat = jnp.where(causal, jnp.concatenate(parts, axis=0), 0.0).astype(BF16)
    rsum = _dot(pcat, headsum)

    qb = q * jnp.exp2(b)
    b_end = b[c - 1:c, :]
    kd = (k * jnp.exp2(b_end - b)).astype(BF16)
    stb = st.astype(BF16)
    st_new = st * jnp.exp2(b_end)

    outs = []
    for h in range(GLA_HEADS):
        vh = vb[:, h * GLA_DV:(h + 1) * GLA_DV]
        vf = v[:, h * GLA_DV:(h + 1) * GLA_DV]
        vt = jnp.concatenate(
            [jnp.broadcast_to(vf[i * GLA_SUB:(i + 1) * GLA_SUB][None], (GLA_SUB, GLA_SUB, GLA_DV))
             .reshape(GLA_SUB * GLA_SUB, GLA_DV) for i in range(nsb)], axis=0)
        xh = rsum[:, h * GLA_DV:(h + 1) * GLA_DV] * vt
        o = _dot(msel, xh.astype(BF16))
        o = o + _dot(a_off[h].astype(BF16), vh)
        o = o + _dot_nt(jnp.where(head_masks[h], qb, 0.0).astype(BF16), stb)
        outs.append(o)
        st_new = st_new + jnp.where(head_masks[h], _dot_tn(vh, kd), 0.0)
    return outs, st_new


def _gla_kernel(q_ref, k_ref, la_ref, v_ref, r_ref, st0_ref, g_ref, tri_ref, hs_ref, ms_ref,
                o_ref, st_ref, st_scr, *, c, nchunks):
    @pl.when(pl.program_id(1) == 0)
    def _():
        st_scr[...] = st0_ref[0]

    def body(ci, carry):
        r0 = pl.multiple_of(ci * c, c)
        rows = pl.ds(r0, c)
        outs, st_new = _gla_chunk(q_ref[0, rows, :], k_ref[0, rows, :], la_ref[0, rows, :], v_ref[0, rows, :],
                                  st_scr[...], c, tri_ref[...], hs_ref[...], ms_ref[...])
        st_scr[...] = st_new
        r = r_ref[0, rows, :]
        g = g_ref[...]
        for h in range(GLA_HEADS):
            o = outs[h]
            sl = slice(h * GLA_DV, (h + 1) * GLA_DV)
            on = o * lax.rsqrt(jnp.mean(o * o, axis=-1, keepdims=True) + LN_EPS) * g
            rh = r[:, sl]
            o_ref[0, rows, sl] = (on * (rh * _sigmoid(rh))).astype(BF16)
        return carry

    lax.fori_loop(0, nchunks, body, 0, unroll=8 if nchunks % 8 == 0 else 1)
    st_ref[0] = st_scr[...]


def _gla(q3, k3, la3, v3, r3, st0, gnorm):
    bsz, l, _ = q3.shape
    c = min(l, CHUNK)
    tl = min(l, 512)
    blk = lambda n: pl.BlockSpec((1, tl, n), lambda b, i: (b, i, 0))
    st_spec = pl.BlockSpec((1, GLA_DV, GLA_QK), lambda b, i: (b, 0, 0))
    consts = _gla_consts(c)
    return pl.pallas_call(
        functools.partial(_gla_kernel, c=c, nchunks=tl // c),
        grid=(bsz, l // tl),
        in_specs=[blk(GLA_QK), blk(GLA_QK), blk(GLA_QK), blk(GLA_V), blk(GLA_V), st_spec,
                  _full_spec(gnorm, 2)] + [_full_spec(a, 2) for a in consts],
        out_specs=[blk(GLA_V), st_spec],
        out_shape=[jax.ShapeDtypeStruct((bsz, l, GLA_V), BF16),
                   jax.ShapeDtypeStruct((bsz, GLA_DV, GLA_QK), F32)],
        scratch_shapes=[pltpu.VMEM((GLA_DV, GLA_QK), F32)],
        compiler_params=_cp("parallel", "arbitrary"),
        name="gla",
    )(q3, k3, la3, v3, r3, st0, gnorm, *consts)


def _pool_kernel(x_ref, halo_ref, hist_ref, w_ref, scale_ref, o_ref, *, tl, pos0):
    i = pl.program_id(1)
    x = x_ref[0].astype(F32)
    prev = jnp.where(i == 0, hist_ref[0], halo_ref[0].astype(F32))
    e = jnp.concatenate([prev, x], axis=0)
    t = i * tl + lax.broadcasted_iota(I32, (tl, 1), 0)
    pos = pos0 + t
    sums = []
    shift = 1
    for g, w in enumerate(POOL_WINDOWS):
        e = e[:, POOL_CH:] if g > 0 else e
        while shift < w:
            e = e[shift:] + e[:-shift]
            shift *= 2
        off = POOL_HALO - (w - 1)
        sums.append(e[off:off + tl, :POOL_CH])
    outs = []
    for g, w in enumerate(POOL_WINDOWS):
        cnt = jnp.minimum(pos + 1, w).astype(F32)
        mix = sums[g] / cnt - x[:, g * POOL_CH:(g + 1) * POOL_CH]
        outs.append(_dot(mix.astype(BF16), w_ref[g]))
    o_ref[0] = (jnp.concatenate(outs, axis=1) * scale_ref[...]).astype(BF16)


def _pool(xp3, hist16, pool_w, pool_scale, pos0):
    bsz, l, _ = xp3.shape
    tl = min(l, 512)
    per = tl // POOL_HALO
    return pl.pallas_call(
        functools.partial(_pool_kernel, tl=tl, pos0=pos0),
        grid=(bsz, l // tl),
        in_specs=[pl.BlockSpec((1, tl, POOL_WIDTH), lambda b, i: (b, i, 0)),
                  pl.BlockSpec((1, POOL_HALO, POOL_WIDTH), lambda b, i: (b, jnp.maximum(i * per - 1, 0), 0)),
                  pl.BlockSpec((1, POOL_HALO, POOL_WIDTH), lambda b, i: (b, 0, 0)),
                  _full_spec(pool_w, 2), _full_spec(pool_scale, 2)],
        out_specs=pl.BlockSpec((1, tl, POOL_WIDTH), lambda b, i: (b, i, 0)),
        out_shape=jax.ShapeDtypeStruct((bsz, l, POOL_WIDTH), BF16),
        compiler_params=_cp("parallel", "parallel"),
        name="pool",
    )(xp3, xp3, hist16, pool_w, pool_scale)


def _out_route_kernel(a_ref, b_ref, h_ref, w_ref, g_ref, bt_ref, wr_ref, br_ref,
                      h1_ref, xs_ref, rs_ref, rg_ref, n16_ref, *, tm, slots):
    half = w_ref.shape[0] // 2
    y = _dot(a_ref[...].astype(BF16), w_ref[0:half, :]) + _dot(b_ref[...].astype(BF16), w_ref[half:, :])
    x = _layernorm(ALPHA * h_ref[...].astype(F32) + y, g_ref[...], bt_ref[...])
    h1_ref[...] = x.astype(BF16)

    xh = x.astype(BF16)
    xl = (x - xh.astype(F32)).astype(BF16)
    hl = _dot(xh, wr_ref[...])
    logits = hl[:, :LANES] + hl[:, LANES:] + _dot(xl, wr_ref[:, 0:LANES]) + br_ref[...]

    lane = lax.broadcasted_iota(I32, (tm, LANES), 1)
    lanef = lane.astype(F32)
    neg = -jnp.inf
    big = jnp.float32(1 << 20)

    def first_lane(hit):
        return jnp.min(jnp.where(hit, lanef, big), axis=-1, keepdims=True).astype(I32)

    gl = jnp.where(lane < N_GROUPS, logits, neg)
    gmax = jnp.max(gl, axis=-1, keepdims=True)
    g_sel = first_lane(gl == gmax)
    g_prob = 1.0 / jnp.sum(jnp.exp(gl - gmax), axis=-1, keepdims=True)
    eidx = lane - N_GROUPS
    in_grp = (eidx >= 0) & (eidx < N_EXPERTS) & ((eidx // EXPERTS_PER_GROUP) == g_sel)
    el = jnp.where(in_grp, logits, neg)
    v1 = jnp.max(el, axis=-1, keepdims=True)
    i1 = first_lane(el == v1)
    el2 = jnp.where(lane == i1, neg, el)
    v2 = jnp.max(el2, axis=-1, keepdims=True)
    i2 = first_lane(el2 == v2)
    e21 = jnp.exp(v2 - v1)
    gate1 = g_prob / (1.0 + e21)
    gate2 = g_prob * e21 / (1.0 + e21)
    e1 = i1 - N_GROUPS
    e2 = i2 - N_GROUPS

    oh1 = lane == e1
    oh2 = lane == e2
    oh = oh1.astype(F32) + oh2.astype(F32)
    ti = lax.broadcasted_iota(I32, (tm, tm), 0)
    tj = lax.broadcasted_iota(I32, (tm, tm), 1)
    before = _dot((tj < ti).astype(BF16), oh.astype(BF16))
    cnt = jnp.sum(oh, axis=0, keepdims=True)
    n16 = jnp.floor((cnt + (MOE_CHUNK - 1)) * (1.0 / MOE_CHUNK))
    n16_8 = jnp.broadcast_to(n16, (8, LANES))
    ui = lax.broadcasted_iota(I32, (LANES, LANES), 0)
    uj = lax.broadcasted_iota(I32, (LANES, LANES), 1)
    run_start = _dot(n16_8.astype(BF16), (ui < uj).astype(BF16))[0:1]
    slot_of = MOE_CHUNK * run_start + before
    slot1 = jnp.sum(jnp.where(oh1, slot_of, 0.0), axis=-1, keepdims=True).astype(I32)
    slot2 = jnp.sum(jnp.where(oh2, slot_of, 0.0), axis=-1, keepdims=True).astype(I32)
    sl = lax.broadcasted_iota(I32, (tm, slots), 1)
    place = ((sl == slot1) | (sl == slot2)).astype(BF16)
    xs_ref[...] = _dot_tn(place, xh).astype(BF16)

    rs_ref[...] = jnp.where(lane == 0, slot1, jnp.where(lane == 1, slot2, 0))
    rg_ref[...] = jnp.where(lane == 0, gate1, jnp.where(lane == 1, gate2, 0.0))
    n16_ref[...] = n16_8.astype(I32)


def _moe_slots(tm):
    worst = tm * TOP_K + N_EXPERTS * (MOE_CHUNK - 1)
    return -(-worst // MOE_ROWS) * MOE_ROWS


def _out_route(a2, b2, h2, w_out, ln_g, ln_b, w_r, b_r):
    t = h2.shape[0]
    tm = min(512, t)
    nt = t // tm
    slots = _moe_slots(tm)
    row = lambda n: pl.BlockSpec((tm, n), lambda i: (i, 0))
    return pl.pallas_call(
        functools.partial(_out_route_kernel, tm=tm, slots=slots),
        grid=(nt,),
        in_specs=[row(a2.shape[1]), row(b2.shape[1]), row(D_MODEL)]
                 + [_full_spec(a, 1) for a in (w_out, ln_g, ln_b, w_r, b_r)],
        out_specs=[row(D_MODEL), pl.BlockSpec((slots, D_MODEL), lambda i: (i, 0)), row(LANES), row(LANES),
                   pl.BlockSpec((8, LANES), lambda i: (i, 0))],
        out_shape=[jax.ShapeDtypeStruct((t, D_MODEL), BF16), jax.ShapeDtypeStruct((nt * slots, D_MODEL), BF16),
                   jax.ShapeDtypeStruct((t, LANES), I32), jax.ShapeDtypeStruct((t, LANES), F32),
                   jax.ShapeDtypeStruct((nt * 8, LANES), I32)],
        compiler_params=_cp("parallel"),
        name="out_route",
    )(a2, b2, h2, w_out, ln_g, ln_b, w_r, b_r)


def _chunk_rows(chunk):
    return pl.ds(pl.multiple_of(chunk * MOE_CHUNK, MOE_CHUNK), MOE_CHUNK)


def _expert_kernel(src_ref, nreal_ref, be_ref, nu_ref, xs_ref, w1_ref, w3_ref, w2_ref, ys_ref,
                   xbuf, ybuf, gsem, ssem, *, cpb):
    del be_ref
    b = pl.program_id(0)
    nu = nu_ref[0]

    rows = cpb * MOE_CHUNK

    def gather(blk, slot, j):
        return pltpu.make_async_copy(xs_ref.at[_chunk_rows(src_ref[blk * cpb + j]), :],
                                     xbuf.at[slot, pl.ds(j * MOE_CHUNK, MOE_CHUNK), :], gsem.at[slot])

    def scatter(blk, slot, j):
        rows_j = pl.ds(j * MOE_CHUNK, MOE_CHUNK) if isinstance(j, int) else _chunk_rows(j)
        return pltpu.make_async_copy(ybuf.at[slot, rows_j, :],
                                     ys_ref.at[_chunk_rows(src_ref[blk * cpb + j]), :],
                                     ssem.at[slot])

    def start_gather(blk, slot):
        for j in range(cpb):
            gather(blk, slot, j).start()

    def wait_gather(slot):
        pltpu.make_async_copy(xs_ref.at[pl.ds(0, rows), :], xbuf.at[slot], gsem.at[slot]).wait()

    def for_real_chunks(blk, fn):
        def body(j, carry):
            fn(j)
            return carry
        lax.fori_loop(0, nreal_ref[blk], body, 0)

    def wait_scatter(blk, slot):
        full = nreal_ref[blk] == cpb

        @pl.when(full)
        def _():
            pltpu.make_async_copy(ybuf.at[slot], ys_ref.at[pl.ds(0, rows), :], ssem.at[slot]).wait()

        @pl.when(jnp.logical_not(full))
        def _():
            for_real_chunks(blk, lambda j: scatter(blk, slot, j).wait())

    @pl.when(b < nu)
    def _():
        slot = b % 2

        @pl.when(b == 0)
        def _():
            start_gather(b, slot)

        @pl.when(b + 1 < nu)
        def _():
            start_gather(b + 1, 1 - slot)

        wait_gather(slot)

        @pl.when(b >= 2)
        def _():
            wait_scatter(b - 2, slot)

        xb = xbuf[slot]
        a = _dot(xb, w1_ref[0, 0].astype(BF16))
        hid = a * _sigmoid(a) * _dot(xb, w3_ref[0, 0].astype(BF16))
        ybuf[slot] = _dot(hid.astype(BF16), w2_ref[0, 0].astype(BF16)).astype(BF16)
        full = nreal_ref[b] == cpb

        @pl.when(full)
        def _():
            for j in range(cpb):
                scatter(b, slot, j).start()

        @pl.when(jnp.logical_not(full))
        def _():
            for_real_chunks(b, lambda j: scatter(b, slot, j).start())

        @pl.when(b == nu - 1)
        def _():
            wait_scatter(b, slot)

            @pl.when(b >= 1)
            def _():
                wait_scatter(b - 1, 1 - slot)


def _experts(xs, src, nreal, block_e, n_used, w1, w3, w2, layer):
    nblk = nreal.shape[0]
    cpb = src.shape[0] // nblk
    rows = cpb * MOE_CHUNK
    wspec = lambda shape: pl.BlockSpec((1,) + shape, lambda i, src, nr, be, nu: (layer, be[i], 0, 0))
    return pl.pallas_call(
        functools.partial(_expert_kernel, cpb=cpb),
        grid_spec=pltpu.PrefetchScalarGridSpec(
            num_scalar_prefetch=4,
            grid=(nblk,),
            in_specs=[pl.BlockSpec(memory_space=pl.ANY),
                      wspec((1, D_MODEL, D_EXPERT)), wspec((1, D_MODEL, D_EXPERT)), wspec((1, D_EXPERT, D_MODEL))],
            out_specs=pl.BlockSpec(memory_space=pl.ANY),
            scratch_shapes=[pltpu.VMEM((2, rows, D_MODEL), BF16), pltpu.VMEM((2, rows, D_MODEL), BF16),
                            pltpu.SemaphoreType.DMA((2,)), pltpu.SemaphoreType.DMA((2,))]),
        out_shape=jax.ShapeDtypeStruct(xs.shape, xs.dtype),
        input_output_aliases={4: 0},
        compiler_params=_cp("arbitrary"),
        name="moe_experts",
    )(src, nreal, block_e, n_used, xs, w1, w3, w2)


def _combine_kernel(h_ref, rs_ref, rg_ref, ys_ref, g_ref, b_ref, o_ref, *, tm, slots):
    rs = rs_ref[...]
    rg = rg_ref[...]
    sl = lax.broadcasted_iota(I32, (tm, slots), 1).astype(jnp.int16)
    s1, s2 = rs[:, 0:1].astype(jnp.int16), rs[:, 1:2].astype(jnp.int16)
    g1, g2 = rg[:, 0:1].astype(BF16), rg[:, 1:2].astype(BF16)
    weight = jnp.where(sl == s1, g1, jnp.where(sl == s2, g2, jnp.zeros((), BF16)))
    y = _dot(weight, ys_ref[...])
    o_ref[...] = _layernorm(ALPHA * h_ref[...].astype(F32) + y, g_ref[...], b_ref[...]).astype(o_ref.dtype)


def _combine(h2, rs, rg, ys, ln_g, ln_b, out_dtype):
    t = h2.shape[0]
    tm = min(512, t)
    slots = ys.shape[0] // (t // tm)
    row = lambda n: pl.BlockSpec((tm, n), lambda i: (i, 0))
    return pl.pallas_call(
        functools.partial(_combine_kernel, tm=tm, slots=slots),
        grid=(t // tm,),
        in_specs=[row(D_MODEL), row(LANES), row(LANES), pl.BlockSpec((slots, D_MODEL), lambda i: (i, 0)),
                  _full_spec(ln_g, 1), _full_spec(ln_b, 1)],
        out_specs=row(D_MODEL),
        out_shape=jax.ShapeDtypeStruct((t, D_MODEL), out_dtype),
        compiler_params=_cp("parallel"),
        name="moe_combine",
    )(h2, rs, rg, ys, ln_g, ln_b)


def _moe(h1, xs, rs, rg, n16_rows, w1, w3, w2, layer, ln_g, ln_b, out_dtype):
    t = h1.shape[0]
    tm = min(512, t)
    nt = t // tm
    slots = xs.shape[0] // nt
    n16 = n16_rows[::8, :N_EXPERTS]
    cpb = max(4, min(MOE_BLOCK_CHUNKS, (t * TOP_K) // (N_EXPERTS * MOE_CHUNK)))
    per_e = jnp.sum(n16, axis=0)
    blocks_e = (per_e + cpb - 1) // cpb
    blk_end = jnp.cumsum(blocks_e)
    blk_start = blk_end - blocks_e
    n_used = blk_end[-1:].astype(I32)
    max_chunks = (t * TOP_K) // MOE_CHUNK + nt * N_EXPERTS
    nblk = max_chunks // cpb + N_EXPERTS
    blk = jnp.arange(nblk, dtype=I32)
    block_e = jnp.minimum(jnp.sum(blk_end[None, :] <= blk[:, None], axis=1), N_EXPERTS - 1).astype(I32)
    of_e = block_e[:, None] == jnp.arange(N_EXPERTS, dtype=I32)[None, :]
    pick_e = lambda tab: jnp.sum(jnp.where(of_e, tab[None, :], 0), axis=1)
    pick_col = lambda tab: jnp.sum(jnp.where(of_e[:, None, :], tab[None, :, :], 0), axis=2)
    run_end = pick_col(jnp.cumsum(n16, axis=0))
    tile_off = pick_col(jnp.cumsum(n16, axis=1) - n16)
    k = ((blk - pick_e(blk_start)) * cpb)[:, None] + jnp.arange(cpb, dtype=I32)[None, :]
    real = (k < pick_e(per_e)[:, None]) & (blk < n_used[0])[:, None]
    done = run_end[:, None, :] <= k[:, :, None]
    tile = jnp.minimum(jnp.sum(done, axis=2), nt - 1)
    run_first = jnp.max(jnp.where(done, run_end[:, None, :], 0), axis=2)
    of_t = tile[:, :, None] == jnp.arange(nt, dtype=I32)[None, None, :]
    src = tile * (slots // MOE_CHUNK) + jnp.sum(jnp.where(of_t, tile_off[:, None, :], 0), axis=2) + (k - run_first)
    src = jnp.where(real, src, src[:, :1])
    src = jnp.where((blk < n_used[0])[:, None], src, 0).astype(I32).reshape(-1)
    nreal = jnp.sum(real, axis=1).astype(I32)
    ys = _experts(xs, src, nreal, block_e, n_used, w1, w3, w2, layer)
    return _combine(h1, rs, rg, ys, ln_g, ln_b, out_dtype)


def _swap_halves(x):
    lane = lax.broadcasted_iota(I32, x.shape, 1)
    first = (lane % MLA_ROPE) < (MLA_ROPE // 2)
    return jnp.where(first, pltpu.roll(x, LANES - MLA_ROPE // 2, 1), pltpu.roll(x, MLA_ROPE // 2, 1))


def _rope(x, cos, sin):
    parts = []
    for t in range(x.shape[1] // LANES):
        sl = slice(t * LANES, (t + 1) * LANES)
        parts.append(x[:, sl] * cos[:, sl] + _swap_halves(x[:, sl]) * sin[:, sl])
    return parts[0] if len(parts) == 1 else jnp.concatenate(parts, axis=1)


def _odd_in_kernel(h_ref, w_ref, qg_ref, wqn_ref, wqp_ref, wuk_ref, perm_ref, kvg_ref,
                   cq_ref, sq_ref, ck_ref, sk_ref, gg_ref, gb_ref, ws_ref, bs_ref,
                   q_ref, kc_ref, kt_ref, ckv_ref, kpe_ref, gated_ref, vn_ref, *, tl, cl):
    hb = h_ref[0].astype(BF16)
    z = _dot(hb, w_ref[...])
    o_ckv = MLA_Q_RANK
    o_u = o_ckv + MLA_KV_RANK
    o_v = o_u + GMLP_WIDTH
    o_k = o_v + GMLP_WIDTH
    cq = z[:, :MLA_Q_RANK]
    cqn = cq * lax.rsqrt(jnp.mean(cq * cq, axis=-1, keepdims=True) + LN_EPS) * qg_ref[...]
    cqb = cqn.astype(BF16)
    qn = _dot(cqb, wqn_ref[...])
    qp = _dot(cqb, wqp_ref[...])
    qp = _rope(qp, cq_ref[...], sq_ref[...])
    qpe = _dot((qp * MLA_QSCALE).astype(BF16), perm_ref[...])
    for h in range(MLA_HEADS):
        sl = slice(h * LANES, (h + 1) * LANES)
        qa = _dot((qn[:, sl] * MLA_QSCALE).astype(BF16), wuk_ref[h])
        q_ref[0, h, :, 0:LANES] = qa.astype(BF16)
        q_ref[0, h, :, LANES:] = qpe[:, sl].astype(BF16)

    ckv = z[:, o_ckv:o_u]
    ckvn = ckv * lax.rsqrt(jnp.mean(ckv * ckv, axis=-1, keepdims=True) + LN_EPS) * kvg_ref[...]
    kp = z[:, o_k:]
    kp = _rope(kp, ck_ref[...], sk_ref[...])
    ckv_ref[0] = ckvn
    kpe_ref[0] = kp[:, :MLA_ROPE]
    one = (lax.broadcasted_iota(I32, (1, LANES), 1) == MLA_ONE_LANE - LANES).astype(F32)
    kp1 = kp + one
    kc_ref[0, :, 0:LANES] = ckvn.astype(BF16)
    kc_ref[0, :, LANES:] = kp1.astype(BF16)
    kt_ref[0, 0:LANES, :] = ckvn.T.astype(BF16)
    kt_ref[0, LANES:, :] = kp1.T.astype(BF16)

    gu = _gelu(z[:, o_u:o_v])
    vn = _layernorm(_gelu(z[:, o_v:o_k]), gg_ref[...], gb_ref[...])
    vn_ref[0] = vn
    vnb = vn.astype(BF16)
    for n in range(tl // cl):
        rs = slice(n * cl, (n + 1) * cl)
        for g in range(GMLP_GROUPS):
            ls = slice(g * GMLP_CH, (g + 1) * GMLP_CH)
            sg = _dot(ws_ref[g], vnb[rs, ls]) + bs_ref[:, ls]
            gated_ref[0, rs, ls] = (gu[rs, ls] * sg).astype(BF16)


def _odd_in(h3, w_in, q_g, w_qn, w_qp, w_uk, perm, kv_g, cos_q, sin_q, cos_k, sin_k,
            gm_g, gm_b, ws, bs):
    bsz, l, _ = h3.shape
    tl = min(l, 512)
    cl = min(l, GMLP_CHUNK)
    rowb = lambda n: pl.BlockSpec((1, tl, n), lambda b, i: (b, i, 0))
    tab = lambda n: pl.BlockSpec((tl, n), lambda b, i: (i, 0))
    consts = (w_in, q_g, w_qn, w_qp, w_uk, perm, kv_g)
    consts2 = (gm_g, gm_b, ws, bs)
    return pl.pallas_call(
        functools.partial(_odd_in_kernel, tl=tl, cl=cl),
        grid=(bsz, l // tl),
        in_specs=[rowb(D_MODEL)] + [_full_spec(a, 2) for a in consts]
                 + [tab(MLA_HEADS * MLA_ROPE), tab(MLA_HEADS * MLA_ROPE), tab(LANES), tab(LANES)]
                 + [_full_spec(a, 2) for a in consts2],
        out_specs=[pl.BlockSpec((1, MLA_HEADS, tl, MLA_QW), lambda b, i: (b, 0, i, 0)),
                   rowb(MLA_QW), pl.BlockSpec((1, MLA_QW, tl), lambda b, i: (b, 0, i)),
                   rowb(MLA_KV_RANK), rowb(MLA_ROPE), rowb(GMLP_WIDTH), rowb(GMLP_WIDTH)],
        out_shape=[jax.ShapeDtypeStruct((bsz, MLA_HEADS, l, MLA_QW), BF16),
                   jax.ShapeDtypeStruct((bsz, l, MLA_QW), BF16),
                   jax.ShapeDtypeStruct((bsz, MLA_QW, l), BF16),
                   jax.ShapeDtypeStruct((bsz, l, MLA_KV_RANK), F32),
                   jax.ShapeDtypeStruct((bsz, l, MLA_ROPE), F32),
                   jax.ShapeDtypeStruct((bsz, l, GMLP_WIDTH), BF16),
                   jax.ShapeDtypeStruct((bsz, l, GMLP_WIDTH), F32)],
        compiler_params=_cp("parallel", "parallel"),
        name="odd_in",
    )(h3, *consts, cos_q, sin_q, cos_k, sin_k, *consts2)


def _attn_kernel(qi_ref, kj_ref, flag_ref, q_ref, k_ref, kt_ref, wuv_ref, o_ref,
                 m_scr, l_scr, acc_scr, *, tq, hpb, tk, nsub, pos0, n_keys):
    p = pl.program_id(1)
    flag = flag_ref[p]
    nblk = MLA_HEADS // hpb
    width = hpb * tq

    @pl.when((flag & 1) != 0)
    def _():
        m_scr[...] = jnp.full_like(m_scr, -jnp.inf)
        l_scr[...] = jnp.zeros_like(l_scr)
        acc_scr[...] = jnp.zeros_like(acc_scr)

    def step(sub, hidden):
        kt = kt_ref[0, :, sub * tk:(sub + 1) * tk]
        kk = k_ref[0, sub * tk:(sub + 1) * tk, :]
        if hidden:
            qpos = pos0 + qi_ref[p] * tq + lax.broadcasted_iota(I32, (1, width), 1) % tq
            kpos = (kj_ref[p] * nsub + sub) * tk + lax.broadcasted_iota(I32, (tk, 1), 0)
            visible = ((kpos // CHUNK) <= (qpos // CHUNK)) & (kpos < n_keys)
        half = tk // MLA_KEY_PARTS
        for h in range(nblk):
            s = _dot_nt(kk, q_ref[0, h])
            if hidden:
                s = jnp.where(visible, s, -jnp.inf)
            m = m_scr[h]
            acc = acc_scr[h]
            l = l_scr[h]
            for part in range(MLA_KEY_PARTS):
                ks = slice(part * half, (part + 1) * half)
                sp = s[ks]
                m_new = jnp.maximum(m, jnp.max(sp, axis=0, keepdims=True))
                alpha = jnp.exp2(m - m_new)
                pr = jnp.exp2(sp - m_new).astype(BF16)
                acc = alpha * acc + _dot(kt[:MLA_KV_RANK, ks], pr)
                l = alpha * l + _dot(kt[MLA_ONE_LANE:MLA_ONE_LANE + 16, ks], pr)
                m = m_new
            acc_scr[h] = acc
            l_scr[h] = l
            m_scr[h] = m

    for sub in range(nsub):
        seen = (flag & (4 << (2 * sub))) != 0
        some_hidden = (flag & (8 << (2 * sub))) != 0

        @pl.when(seen & some_hidden)
        def _():
            step(sub, True)

        @pl.when(seen & jnp.logical_not(some_hidden))
        def _():
            step(sub, False)

    @pl.when((flag & 2) != 0)
    def _():
        if hpb == 1:
            for pair in range(MLA_HEADS // 2):
                tile = jnp.zeros((tq, LANES), F32)
                for h in (2 * pair, 2 * pair + 1):
                    lat = (acc_scr[h] / l_scr[h, 0:1, :]).astype(BF16)
                    tile = tile + _dot_tn(lat, wuv_ref[h])
                o_ref[0, :, pair * LANES:(pair + 1) * LANES] = tile.astype(BF16)
        else:
            head_of_lane = lax.broadcasted_iota(I32, (1, MLA_HEADS * MLA_V), 1) // MLA_V
            out = jnp.zeros((tq, MLA_HEADS * MLA_V), F32)
            for h in range(nblk):
                lat = (acc_scr[h] / l_scr[h, 0:1, :]).astype(BF16)
                full = _dot_tn(lat, wuv_ref[...])
                for hh in range(hpb):
                    out = out + jnp.where(head_of_lane == h * hpb + hh, full[hh * tq:(hh + 1) * tq], 0.0)
            o_ref[0] = out.astype(BF16)


def _attn_pairs(l, tq, tk, nsub, pos0, n_keys):
    qi, kj, flag = [], [], []
    for i in range(l // tq):
        q_first = pos0 + i * tq
        q_last = q_first + tq - 1
        vis = min(CHUNK * (q_last // CHUNK) + CHUNK - 1, n_keys - 1)
        ntiles = vis // tk + 1
        nsteps = -(-ntiles // nsub)
        for j in range(nsteps):
            f = int(j == 0) + 2 * int(j == nsteps - 1)
            for s in range(nsub):
                t = j * nsub + s
                if t < ntiles:
                    hidden = ((t + 1) * tk - 1) // CHUNK > q_first // CHUNK or (t + 1) * tk > n_keys
                    f += (4 + 8 * int(hidden)) << (2 * s)
            qi.append(i); kj.append(j); flag.append(f)
    return [jnp.asarray(np.array(a, np.int32)) for a in (qi, kj, flag)]


def _attention(q4, kc3, kt3, w_uv, pos0, n_keys, tq, tk, nsub):
    bsz, _, l, _ = q4.shape
    pairs = _attn_pairs(l, tq, tk, nsub, pos0, n_keys)
    npairs = int(pairs[0].shape[0])
    hpb = max(1, min(MLA_HEADS, 256 // tq)) if l == tq else 1
    nblk = MLA_HEADS // hpb
    width = hpb * tq
    q4 = q4.reshape(bsz, nblk, hpb * l, MLA_QW)
    if hpb == 1:
        heads = w_uv.reshape(MLA_KV_RANK, MLA_HEADS, MLA_V).transpose(1, 0, 2)
        w_uv = jnp.stack([jnp.pad(heads[h], ((0, 0), ((h % 2) * MLA_V, (1 - h % 2) * MLA_V)))
                          for h in range(MLA_HEADS)])
    return pl.pallas_call(
        functools.partial(_attn_kernel, tq=tq, hpb=hpb, tk=tk, nsub=nsub, pos0=pos0, n_keys=n_keys),
        grid_spec=pltpu.PrefetchScalarGridSpec(
            num_scalar_prefetch=3,
            grid=(bsz, npairs),
            in_specs=[pl.BlockSpec((1, nblk, width, MLA_QW), lambda b, p, qi, kj, f: (b, 0, qi[p], 0)),
                      pl.BlockSpec((1, nsub * tk, MLA_QW), lambda b, p, qi, kj, f: (b, kj[p], 0)),
                      pl.BlockSpec((1, MLA_QW, nsub * tk), lambda b, p, qi, kj, f: (b, 0, kj[p])),
                      pl.BlockSpec(w_uv.shape, lambda b, p, qi, kj, f: (0,) * w_uv.ndim)],
            out_specs=pl.BlockSpec((1, tq, MLA_HEADS * MLA_V), lambda b, p, qi, kj, f: (b, qi[p], 0)),
            scratch_shapes=[pltpu.VMEM((nblk, 1, width), F32),
                            pltpu.VMEM((nblk, 16, width), F32),
                            pltpu.VMEM((nblk, MLA_KV_RANK, width), F32)]),
        out_shape=jax.ShapeDtypeStruct((bsz, l, MLA_HEADS * MLA_V), BF16),
        compiler_params=_cp("parallel", "arbitrary"),
        name="mla_attention",
    )(*pairs, q4, kc3, kt3, w_uv)


def _rope_tables(pos0, l, width):
    half = MLA_ROPE // 2
    inv = ROPE_THETA ** (-jnp.arange(half, dtype=F32) * 2.0 / MLA_ROPE)
    ang = (pos0 + jnp.arange(l)).astype(F32)[:, None] * inv[None, :]
    cos, sin = jnp.cos(ang), jnp.sin(ang)
    cos32 = jnp.concatenate([cos, cos], axis=1)
    sin32 = jnp.concatenate([-sin, sin], axis=1)
    reps = width // MLA_ROPE
    return jnp.tile(cos32, (1, reps)), jnp.tile(sin32, (1, reps))


def _prep_even(w_in, w_gate_up, b_gate, gla_norm_g, pool_w, pool_scale, w_out):
    o_r = 2 * GLA_QK + GLA_V
    o_g = o_r + GLA_V
    o_p = o_g + GLA_GATE_RANK
    w_main = jnp.concatenate([w_in[:, :o_g], w_in[:, o_p:]], axis=1).astype(BF16)
    w_g = jnp.pad(w_in[:, o_g:o_p], ((0, 0), (0, LANES - GLA_GATE_RANK))).astype(BF16)
    w_gu = jnp.pad(w_gate_up, ((0, LANES - GLA_GATE_RANK), (0, 0))).astype(BF16)
    return dict(w_main=w_main, w_g=w_g, w_gu=w_gu, b_g=b_gate.reshape(1, -1),
                gnorm=gla_norm_g.reshape(1, -1), pool_w=pool_w.astype(BF16),
                pool_scale=pool_scale.reshape(1, -1), w_out=w_out.astype(BF16))


def _prep_odd(w_in, q_norm_g, w_uq, kv_norm_g, w_uk, w_uv, gm_g, gm_b, gm_ws, gm_bs, w_out):
    o_ckv = MLA_Q_RANK
    o_kpe = o_ckv + MLA_KV_RANK
    o_u = o_kpe + MLA_ROPE
    w_in2 = jnp.concatenate([w_in[:, :o_kpe], w_in[:, o_u:], w_in[:, o_kpe:o_u],
                             jnp.zeros((D_MODEL, LANES - MLA_ROPE), F32)], axis=1).astype(BF16)
    uq = w_uq.reshape(MLA_Q_RANK, MLA_HEADS, MLA_NOPE + MLA_ROPE)
    w_qn = jnp.pad(uq[:, :, :MLA_NOPE], ((0, 0), (0, 0), (0, LANES - MLA_NOPE)))
    w_qn = w_qn.reshape(MLA_Q_RANK, MLA_HEADS * LANES).astype(BF16)
    w_qp = uq[:, :, MLA_NOPE:].reshape(MLA_Q_RANK, MLA_HEADS * MLA_ROPE).astype(BF16)
    uk = w_uk.reshape(MLA_KV_RANK, MLA_HEADS, MLA_NOPE).transpose(1, 2, 0)
    w_ukp = jnp.pad(uk, ((0, 0), (0, LANES - MLA_NOPE), (0, 0))).astype(BF16)
    src = np.arange(MLA_HEADS * MLA_ROPE)
    perm = np.zeros((MLA_HEADS * MLA_ROPE, MLA_HEADS * LANES), np.float32)
    perm[src, (src // MLA_ROPE) * LANES + src % MLA_ROPE] = 1.0
    return dict(w_in=w_in2, q_g=q_norm_g.reshape(1, -1), w_qn=w_qn, w_qp=w_qp, w_uk=w_ukp,
                perm=jnp.asarray(perm, BF16), kv_g=kv_norm_g.reshape(1, -1), w_uv=w_uv.astype(BF16),
                gm_g=gm_g.reshape(1, -1), gm_b=gm_b.reshape(1, -1), gm_ws=gm_ws, gm_bs=gm_bs,
                w_out=w_out.astype(BF16))


def _prep_route(wg, bg, we, be):
    w_r = jnp.pad(jnp.concatenate([wg, we], axis=1), ((0, 0), (0, LANES - N_GROUPS - N_EXPERTS)))
    w_hi = w_r.astype(BF16)
    w_lo = (w_r - w_hi.astype(F32)).astype(BF16)
    b_r = jnp.pad(jnp.concatenate([bg, be]), (0, LANES - N_GROUPS - N_EXPERTS)).reshape(1, LANES)
    return jnp.concatenate([w_hi, w_lo], axis=1), b_r


def _even_mixer(h3, st0, hist, pos0, pw):
    bsz, l, _ = h3.shape
    q, k, la, v, r, xp = _even_in(h3.reshape(bsz * l, D_MODEL), pw['w_main'], pw['w_g'], pw['w_gu'], pw['b_g'])
    to3 = lambda a: a.reshape(bsz, l, a.shape[-1])
    o, st = _gla(to3(q), to3(k), to3(la), to3(v), to3(r), st0, pw['gnorm'])
    xp3 = to3(xp)
    hist16 = jnp.pad(hist, ((0, 0), (POOL_HALO - POOL_HIST, 0), (0, 0)))
    pooled = _pool(xp3, hist16, pw['pool_w'], pw['pool_scale'], pos0)
    hist_new = jnp.concatenate([hist, xp3[:, -POOL_HIST:].astype(F32)], axis=1)[:, -POOL_HIST:]
    return o, pooled, st, hist_new


def _odd_mixer(h3, ckv_past, kpe_past, pw):
    bsz, l, _ = h3.shape
    n_past = ckv_past.shape[1]
    cos_q, sin_q = _rope_tables(n_past, l, MLA_HEADS * MLA_ROPE)
    cos_k, sin_k = _rope_tables(n_past, l, MLA_ROPE)
    padk = ((0, 0), (0, LANES - MLA_ROPE))
    cos_k, sin_k = jnp.pad(cos_k, padk), jnp.pad(sin_k, padk)
    cl = min(l, GMLP_CHUNK)
    ws = jnp.tril(pw['gm_ws'][:, :cl, :cl]).astype(BF16)
    bs = jnp.repeat(pw['gm_bs'][:, :cl].T, GMLP_CH, axis=1)
    q4, kc, kt, ckv, kpe, gated, vn = _odd_in(h3, pw['w_in'], pw['q_g'], pw['w_qn'], pw['w_qp'], pw['w_uk'],
                                              pw['perm'], pw['kv_g'], cos_q, sin_q, cos_k, sin_k,
                                              pw['gm_g'], pw['gm_b'], ws, bs)
    n_keys = n_past + l
    tq, tk, nsub = min(l, 256), 512, 2
    span = tk * nsub
    if n_past:
        past = jnp.concatenate([ckv_past, kpe_past, jnp.ones((bsz, n_past, 1), F32),
                                jnp.zeros((bsz, n_past, MLA_QW - MLA_ONE_LANE - 1), F32)], axis=2).astype(BF16)
        kc = jnp.concatenate([past, kc], axis=1)
        kt = jnp.concatenate([past.transpose(0, 2, 1), kt], axis=2)
    kc = jnp.pad(kc, ((0, 0), (0, -n_keys % span), (0, 0)))
    kt = jnp.pad(kt, ((0, 0), (0, 0), (0, -n_keys % span)))
    attn = _attention(q4, kc, kt, pw['w_uv'], n_past, n_keys, tq, tk, nsub)
    return attn, gated, ckv, kpe, vn


def _finish_layer(a3, b3, h3, w_out, lw):
    bsz, l, _ = h3.shape
    t = bsz * l
    h1, xs, rs, rg, n16 = _out_route(a3.reshape(t, -1), b3.reshape(t, -1), h3.reshape(t, D_MODEL), w_out,
                                     lw['ln_mix_g'], lw['ln_mix_b'], lw['w_r'], lw['b_r'])
    h2 = _moe(h1, xs, rs, rg, n16, lw['w1'], lw['w3'], lw['w2'], lw['layer'], lw['ln_ffn_g'], lw['ln_ffn_b'],
              lw['out_dtype'])
    return h2.reshape(bsz, l, D_MODEL)


def kernel(x_prompt, x_sample, state_gla, state_pool, cache_mla_ckv, cache_mla_kpe, w_in_even, w_gate_up, b_gate, gla_norm_g, pool_w, pool_scale, w_out_even, w_in_odd, mla_q_norm_g, mla_w_uq, mla_kv_norm_g, mla_w_uk, mla_w_uv, gmlp_norm_g, gmlp_norm_b, gmlp_ws, gmlp_bs, w_out_odd, ln_mix_g, ln_mix_b, router_group_w, router_group_b, router_expert_w, router_expert_b, expert_w1, expert_w3, expert_w2, ln_ffn_g, ln_ffn_b):
    hp, hs = x_prompt, x_sample
    bp = hp.shape[0]
    past_len = cache_mla_ckv.shape[2]
    gla_p, gla_s, pool_p, pool_s = [], [], [], []
    ckv_p, ckv_s, kpe_p, kpe_s, gv_s = [], [], [], [], []

    def state_to_t(s):
        return s.transpose(0, 3, 1, 2).reshape(s.shape[0], GLA_DV, GLA_QK)

    def state_from_t(st):
        return st.reshape(st.shape[0], GLA_DV, GLA_HEADS, GLA_DK).transpose(0, 2, 3, 1)

    for layer in range(DEPTH):
        i = layer // 2
        w_r, b_r = _prep_route(router_group_w[layer], router_group_b[layer],
                               router_expert_w[layer], router_expert_b[layer])
        lw = dict(ln_mix_g=ln_mix_g[layer].reshape(1, -1), ln_mix_b=ln_mix_b[layer].reshape(1, -1),
                  ln_ffn_g=ln_ffn_g[layer].reshape(1, -1), ln_ffn_b=ln_ffn_b[layer].reshape(1, -1),
                  w_r=w_r, b_r=b_r, w1=expert_w1, w3=expert_w3, w2=expert_w2, layer=layer,
                  out_dtype=F32 if layer == DEPTH - 1 else BF16)
        if layer % 2 == 0:
            pw = _prep_even(w_in_even[i], w_gate_up[i], b_gate[i], gla_norm_g[i], pool_w[i], pool_scale[i],
                            w_out_even[i])
            st0 = jnp.zeros((bp, GLA_DV, GLA_QK), F32)
            hist0 = jnp.zeros((bp, POOL_HIST, POOL_WIDTH), F32)
            op, pp, stp, histp = _even_mixer(hp, st0, hist0, 0, pw)
            os_, ps, sts, hists = _even_mixer(hs, state_to_t(state_gla[i]), state_pool[i], past_len, pw)
            gla_p.append(state_from_t(stp)); gla_s.append(state_from_t(sts))
            pool_p.append(histp); pool_s.append(hists)
            ap, bpj, as_, bsj = op, pp, os_, ps
        else:
            pw = _prep_odd(w_in_odd[i], mla_q_norm_g[i], mla_w_uq[i], mla_kv_norm_g[i], mla_w_uk[i], mla_w_uv[i],
                           gmlp_norm_g[i], gmlp_norm_b[i], gmlp_ws[i], gmlp_bs[i], w_out_odd[i])
            no_ckv = jnp.zeros((bp, 0, MLA_KV_RANK), F32)
            no_kpe = jnp.zeros((bp, 0, MLA_ROPE), F32)
            ap, bpj, cp, kp, _ = _odd_mixer(hp, no_ckv, no_kpe, pw)
            as_, bsj, cs, ks, vs = _odd_mixer(hs, cache_mla_ckv[i], cache_mla_kpe[i], pw)
            ckv_p.append(cp); ckv_s.append(cs); kpe_p.append(kp); kpe_s.append(ks); gv_s.append(vs)
        hp = _finish_layer(ap, bpj, hp, pw['w_out'], lw)
        hs = _finish_layer(as_, bsj, hs, pw['w_out'], lw)
    return (hp, hs, jnp.stack(gla_p), jnp.stack(gla_s), jnp.stack(pool_p), jnp.stack(pool_s),
            jnp.stack(ckv_p), jnp.stack(ckv_s), jnp.stack(kpe_p), jnp.stack(kpe_s), jnp.stack(gv_s))
```

```python
import functools

import numpy as np
import jax
import jax.numpy as jnp
from jax import lax
from jax.experimental import pallas as pl
from jax.experimental.pallas import tpu as pltpu

F32 = jnp.float32
BF16 = jnp.bfloat16
I32 = jnp.int32

D_MODEL = 1024
DEPTH = 2
CHUNK = 64
ALPHA = (2 * DEPTH) ** 0.25
LN_EPS = 1e-5

GLA_HEADS = 4
GLA_DV = 128
GLA_DK = 64
GLA_QK = GLA_HEADS * GLA_DK
GLA_V = GLA_HEADS * GLA_DV
GLA_GATE_RANK = 16
GLA_GATE_TAU = 16.0
GLA_SUB = 8

POOL_WIDTH = 512
POOL_CH = 128
POOL_WINDOWS = (2, 4, 8, 16)
POOL_HIST = 15
POOL_HALO = 16

MLA_HEADS = 8
MLA_NOPE = 64
MLA_ROPE = 32
MLA_V = 64
MLA_Q_RANK = 256
MLA_KV_RANK = 128
ROPE_THETA = 10000.0
MLA_SCALE = (MLA_NOPE + MLA_ROPE) ** -0.5
MLA_QW = 256
MLA_ONE_LANE = MLA_KV_RANK + MLA_ROPE
MLA_KEY_PARTS = 2
GMLP_WIDTH = 512
GMLP_CH = 128
GMLP_GROUPS = 4
GMLP_CHUNK = 128

N_GROUPS = 4
EXPERTS_PER_GROUP = 8
N_EXPERTS = 32
TOP_K = 2
D_EXPERT = 256
MOE_ROWS = 512
MOE_CHUNK = 16
MOE_BLOCK_CHUNKS = MOE_ROWS // MOE_CHUNK

LANES = 128
VMEM_LIMIT = 48 * 1024 * 1024


def _cp(*sem):
    return pltpu.CompilerParams(dimension_semantics=sem, vmem_limit_bytes=VMEM_LIMIT)


def _dot(a, b):
    return jnp.dot(a, b, preferred_element_type=F32)


def _dot_nt(a, b):
    return lax.dot_general(a, b, (((1,), (1,)), ((), ())), preferred_element_type=F32)


def _dot_tn(a, b):
    return lax.dot_general(a, b, (((0,), (0,)), ((), ())), preferred_element_type=F32)


def _split3(x):
    hi = x.astype(BF16)
    r1 = x - hi.astype(F32)
    mid = r1.astype(BF16)
    lo = (r1 - mid.astype(F32)).astype(BF16)
    return hi, mid, lo


def _layernorm(x, g, b):
    mu = jnp.mean(x, axis=-1, keepdims=True)
    xc = x - mu
    var = jnp.mean(xc * xc, axis=-1, keepdims=True)
    return xc * lax.rsqrt(var + LN_EPS) * g + b


def _gelu(x):
    return 0.5 * x * (1.0 + jnp.tanh(0.7978845608028654 * (x + 0.044715 * (x * x * x))))


def _sigmoid(x):
    return 1.0 / (1.0 + jnp.exp(-x))


def _full_spec(a, nargs):
    nd = a.ndim
    if nargs == 1:
        return pl.BlockSpec(a.shape, lambda i: (0,) * nd)
    return pl.BlockSpec(a.shape, lambda i, j: (0,) * nd)


def _even_in_kernel(x_ref, w_ref, wg_ref, wgu_ref, bg_ref,
                    q_ref, k_ref, la_ref, v_ref, r_ref, xp_ref):
    xb = x_ref[...].astype(BF16)
    z = _dot(xb, w_ref[...])
    q_ref[...] = z[:, 0:GLA_QK] * (GLA_DK ** -0.5)
    k_ref[...] = z[:, GLA_QK:2 * GLA_QK]
    v_ref[...] = z[:, 2 * GLA_QK:2 * GLA_QK + GLA_V]
    r_ref[...] = z[:, 2 * GLA_QK + GLA_V:2 * GLA_QK + 2 * GLA_V]
    xp_ref[...] = z[:, 2 * GLA_QK + 2 * GLA_V:].astype(BF16)
    g = _dot(xb, wg_ref[...])
    pre = _dot(g.astype(BF16), wgu_ref[...]) + bg_ref[...]
    logsig = jnp.minimum(pre, 0.0) - jnp.log(1.0 + jnp.exp(-jnp.abs(pre)))
    la_ref[...] = logsig * (1.0 / GLA_GATE_TAU)


def _even_in(x2, w_main, w_g, w_gu, b_g):
    t = x2.shape[0]
    tm = min(512, t)
    row = lambda n: pl.BlockSpec((tm, n), lambda i: (i, 0))
    widths = (GLA_QK, GLA_QK, GLA_QK, GLA_V, GLA_V, POOL_WIDTH)
    return pl.pallas_call(
        _even_in_kernel,
        grid=(t // tm,),
        in_specs=[row(D_MODEL)] + [_full_spec(a, 1) for a in (w_main, w_g, w_gu, b_g)],
        out_specs=[row(n) for n in widths],
        out_shape=[jax.ShapeDtypeStruct((t, n), BF16 if i == 5 else F32) for i, n in enumerate(widths)],
        compiler_params=_cp("parallel"),
        name="even_in",
    )(x2, w_main, w_g, w_gu, b_g)


def _gla_consts(c):
    n = c * GLA_SUB
    tri = np.tril(np.ones((c, c), np.float32))
    headsum = (np.arange(GLA_QK)[:, None] // GLA_DK == np.arange(GLA_V)[None, :] // GLA_DV).astype(np.float32)
    msel = (np.arange(n)[None, :] // GLA_SUB == np.arange(c)[:, None]).astype(np.float32)
    return [jnp.asarray(a, BF16) for a in (tri, headsum, msel)]


def _gla_chunk(q, k, la, v, st, c, tri, headsum, msel):
    lane = lax.broadcasted_iota(I32, (1, GLA_QK), 1)
    head_of_lane = lane // GLA_DK
    row = lax.broadcasted_iota(I32, (c, 1), 0)
    ii = lax.broadcasted_iota(I32, (c, c), 0)
    jj = lax.broadcasted_iota(I32, (c, c), 1)

    hi, mid, lo = _split3(la)
    b = _dot(tri, hi) + _dot(tri, mid) + _dot(tri, lo)

    head_masks = [head_of_lane == h for h in range(GLA_HEADS)]
    vb = v.astype(BF16)

    a_off = [jnp.zeros((c, c), F32) for _ in range(GLA_HEADS)]
    s = c // 2
    while s >= GLA_SUB:
        nblk = c // (2 * s)
        blk = row // (2 * s)
        right = ((row // s) % 2) == 1
        bref = jnp.zeros((c, GLA_QK), F32)
        for m in range(nblk):
            r0 = m * 2 * s + s - 1
            bref = jnp.where(blk == m, b[r0:r0 + 1, :], bref)
        qe = jnp.where(right, q * jnp.exp(jnp.minimum(b - bref, 0.0)), 0.0)
        ke = jnp.where(right, 0.0, k * jnp.exp(jnp.minimum(bref - b, 0.0))).astype(BF16)
        same = (ii // (2 * s)) == (jj // (2 * s))
        for h in range(GLA_HEADS):
            a = _dot_nt(jnp.where(head_masks[h], qe, 0.0).astype(BF16), ke)
            a_off[h] = a_off[h] + (a if nblk == 1 else jnp.where(same, a, 0.0))
        s //= 2

    nsb = c // GLA_SUB
    parts = []
    for i in range(nsb):
        sl = slice(i * GLA_SUB, (i + 1) * GLA_SUB)
        bi, qi, ki = b[sl], q[sl], k[sl]
        diff = bi[:, None, :] - bi[None, :, :]
        p = qi[:, None, :] * ki[None, :, :] * jnp.exp(jnp.minimum(diff, 0.0))
        parts.append(p.reshape(GLA_SUB * GLA_SUB, GLA_QK))
    n = nsb * GLA_SUB * GLA_SUB
    idx = lax.broadcasted_iota(I32, (n, 1), 0)
    causal = (idx % GLA_SUB) <= ((idx // GLA_SUB) % GLA_SUB)
    pcat = jnp.where(causal, jnp.concatenate(parts, axis=0), 0.0).astype(BF16)
    rsum = _dot(pcat, headsum)

    qb = q * jnp.exp(b)
    b_end = b[c - 1:c, :]
    kd = (k * jnp.exp(b_end - b)).astype(BF16)
    stb = st.astype(BF16)
    st_new = st * jnp.exp(b_end)

    outs = []
    for h in range(GLA_HEADS):
        vh = vb[:, h * GLA_DV:(h + 1) * GLA_DV]
        vf = v[:, h * GLA_DV:(h + 1) * GLA_DV]
        vt = jnp.concatenate(
            [jnp.broadcast_to(vf[i * GLA_SUB:(i + 1) * GLA_SUB][None], (GLA_SUB, GLA_SUB, GLA_DV))
             .reshape(GLA_SUB * GLA_SUB, GLA_DV) for i in range(nsb)], axis=0)
        xh = rsum[:, h * GLA_DV:(h + 1) * GLA_DV] * vt
        o = _dot(msel, xh.astype(BF16))
        o = o + _dot(a_off[h].astype(BF16), vh)
        o = o + _dot_nt(jnp.where(head_masks[h], qb, 0.0).astype(BF16), stb)
        outs.append(o)
        st_new = st_new + jnp.where(head_masks[h], _dot_tn(vh, kd), 0.0)
    return outs, st_new


def _gla_kernel(q_ref, k_ref, la_ref, v_ref, r_ref, st0_ref, g_ref, tri_ref, hs_ref, ms_ref,
                o_ref, st_ref, st_scr, *, c, nchunks):
    @pl.when(pl.program_id(1) == 0)
    def _():
        st_scr[...] = st0_ref[0]

    def body(ci, carry):
        r0 = pl.multiple_of(ci * c, c)
        rows = pl.ds(r0, c)
        outs, st_new = _gla_chunk(q_ref[0, rows, :], k_ref[0, rows, :], la_ref[0, rows, :], v_ref[0, rows, :],
                                  st_scr[...], c, tri_ref[...], hs_ref[...], ms_ref[...])
        st_scr[...] = st_new
        r = r_ref[0, rows, :]
        g = g_ref[...]
        for h in range(GLA_HEADS):
            o = outs[h]
            sl = slice(h * GLA_DV, (h + 1) * GLA_DV)
            on = o * lax.rsqrt(jnp.mean(o * o, axis=-1, keepdims=True) + LN_EPS) * g
            rh = r[:, sl]
            o_ref[0, rows, sl] = (on * (rh * _sigmoid(rh))).astype(BF16)
        return carry

    lax.fori_loop(0, nchunks, body, 0, unroll=8 if nchunks % 8 == 0 else 1)
    st_ref[0] = st_scr[...]


def _gla(q3, k3, la3, v3, r3, st0, gnorm):
    bsz, l, _ = q3.shape
    c = min(l, CHUNK)
    tl = min(l, 512)
    blk = lambda n: pl.BlockSpec((1, tl, n), lambda b, i: (b, i, 0))
    st_spec = pl.BlockSpec((1, GLA_DV, GLA_QK), lambda b, i: (b, 0, 0))
    consts = _gla_consts(c)
    return pl.pallas_call(
        functools.partial(_gla_kernel, c=c, nchunks=tl // c),
        grid=(bsz, l // tl),
        in_specs=[blk(GLA_QK), blk(GLA_QK), blk(GLA_QK), blk(GLA_V), blk(GLA_V), st_spec,
                  _full_spec(gnorm, 2)] + [_full_spec(a, 2) for a in consts],
        out_specs=[blk(GLA_V), st_spec],
        out_shape=[jax.ShapeDtypeStruct((bsz, l, GLA_V), BF16),
                   jax.ShapeDtypeStruct((bsz, GLA_DV, GLA_QK), F32)],
        scratch_shapes=[pltpu.VMEM((GLA_DV, GLA_QK), F32)],
        compiler_params=_cp("parallel", "arbitrary"),
        name="gla",
    )(q3, k3, la3, v3, r3, st0, gnorm, *consts)


def _pool_kernel(x_ref, halo_ref, hist_ref, w_ref, scale_ref, o_ref, *, tl, pos0):
    i = pl.program_id(1)
    x = x_ref[0].astype(F32)
    prev = jnp.where(i == 0, hist_ref[0], halo_ref[0].astype(F32))
    e = jnp.concatenate([prev, x], axis=0)
    t = i * tl + lax.broadcasted_iota(I32, (tl, 1), 0)
    pos = pos0 + t
    sums = []
    shift = 1
    for g, w in enumerate(POOL_WINDOWS):
        e = e[:, POOL_CH:] if g > 0 else e
        while shift < w:
            e = e[shift:] + e[:-shift]
            shift *= 2
        off = POOL_HALO - (w - 1)
        sums.append(e[off:off + tl, :POOL_CH])
    outs = []
    for g, w in enumerate(POOL_WINDOWS):
        cnt = jnp.minimum(pos + 1, w).astype(F32)
        mix = sums[g] / cnt - x[:, g * POOL_CH:(g + 1) * POOL_CH]
        outs.append(_dot(mix.astype(BF16), w_ref[g]))
    o_ref[0] = (jnp.concatenate(outs, axis=1) * scale_ref[...]).astype(BF16)


def _pool(xp3, hist16, pool_w, pool_scale, pos0):
    bsz, l, _ = xp3.shape
    tl = min(l, 512)
    per = tl // POOL_HALO
    return pl.pallas_call(
        functools.partial(_pool_kernel, tl=tl, pos0=pos0),
        grid=(bsz, l // tl),
        in_specs=[pl.BlockSpec((1, tl, POOL_WIDTH), lambda b, i: (b, i, 0)),
                  pl.BlockSpec((1, POOL_HALO, POOL_WIDTH), lambda b, i: (b, jnp.maximum(i * per - 1, 0), 0)),
                  pl.BlockSpec((1, POOL_HALO, POOL_WIDTH), lambda b, i: (b, 0, 0)),
                  _full_spec(pool_w, 2), _full_spec(pool_scale, 2)],
        out_specs=pl.BlockSpec((1, tl, POOL_WIDTH), lambda b, i: (b, i, 0)),
        out_shape=jax.ShapeDtypeStruct((bsz, l, POOL_WIDTH), BF16),
        compiler_params=_cp("parallel", "parallel"),
        name="pool",
    )(xp3, xp3, hist16, pool_w, pool_scale)


def _out_route_kernel(a_ref, b_ref, h_ref, w_ref, g_ref, bt_ref, wr_ref, br_ref,
                      h1_ref, xs_ref, rs_ref, rg_ref, n16_ref, *, tm, slots):
    half = w_ref.shape[0] // 2
    y = _dot(a_ref[...].astype(BF16), w_ref[0:half, :]) + _dot(b_ref[...].astype(BF16), w_ref[half:, :])
    x = _layernorm(ALPHA * h_ref[...].astype(F32) + y, g_ref[...], bt_ref[...])
    h1_ref[...] = x.astype(BF16)

    xh = x.astype(BF16)
    xl = (x - xh.astype(F32)).astype(BF16)
    hl = _dot(xh, wr_ref[...])
    logits = hl[:, :LANES] + hl[:, LANES:] + _dot(xl, wr_ref[:, 0:LANES]) + br_ref[...]

    lane = lax.broadcasted_iota(I32, (tm, LANES), 1)
    lanef = lane.astype(F32)
    neg = -jnp.inf
    big = jnp.float32(1 << 20)

    def first_lane(hit):
        return jnp.min(jnp.where(hit, lanef, big), axis=-1, keepdims=True).astype(I32)

    gl = jnp.where(lane < N_GROUPS, logits, neg)
    gmax = jnp.max(gl, axis=-1, keepdims=True)
    g_sel = first_lane(gl == gmax)
    g_prob = 1.0 / jnp.sum(jnp.exp(gl - gmax), axis=-1, keepdims=True)
    eidx = lane - N_GROUPS
    in_grp = (eidx >= 0) & (eidx < N_EXPERTS) & ((eidx // EXPERTS_PER_GROUP) == g_sel)
    el = jnp.where(in_grp, logits, neg)
    v1 = jnp.max(el, axis=-1, keepdims=True)
    i1 = first_lane(el == v1)
    el2 = jnp.where(lane == i1, neg, el)
    v2 = jnp.max(el2, axis=-1, keepdims=True)
    i2 = first_lane(el2 == v2)
    e21 = jnp.exp(v2 - v1)
    gate1 = g_prob / (1.0 + e21)
    gate2 = g_prob * e21 / (1.0 + e21)
    e1 = i1 - N_GROUPS
    e2 = i2 - N_GROUPS

    oh1 = lane == e1
    oh2 = lane == e2
    oh = oh1.astype(F32) + oh2.astype(F32)
    ti = lax.broadcasted_iota(I32, (tm, tm), 0)
    tj = lax.broadcasted_iota(I32, (tm, tm), 1)
    before = _dot((tj < ti).astype(BF16), oh.astype(BF16))
    cnt = jnp.sum(oh, axis=0, keepdims=True)
    n16 = jnp.floor((cnt + (MOE_CHUNK - 1)) * (1.0 / MOE_CHUNK))
    n16_8 = jnp.broadcast_to(n16, (8, LANES))
    ui = lax.broadcasted_iota(I32, (LANES, LANES), 0)
    uj = lax.broadcasted_iota(I32, (LANES, LANES), 1)
    run_start = _dot(n16_8.astype(BF16), (ui < uj).astype(BF16))[0:1]
    slot_of = MOE_CHUNK * run_start + before
    slot1 = jnp.sum(jnp.where(oh1, slot_of, 0.0), axis=-1, keepdims=True).astype(I32)
    slot2 = jnp.sum(jnp.where(oh2, slot_of, 0.0), axis=-1, keepdims=True).astype(I32)
    sl = lax.broadcasted_iota(I32, (tm, slots), 1)
    place = ((sl == slot1) | (sl == slot2)).astype(BF16)
    xs_ref[...] = _dot_tn(place, xh).astype(BF16)

    rs_ref[...] = jnp.where(lane == 0, slot1, jnp.where(lane == 1, slot2, 0))
    rg_ref[...] = jnp.where(lane == 0, gate1, jnp.where(lane == 1, gate2, 0.0))
    n16_ref[...] = n16_8.astype(I32)


def _moe_slots(tm):
    worst = tm * TOP_K + N_EXPERTS * (MOE_CHUNK - 1)
    return -(-worst // MOE_ROWS) * MOE_ROWS


def _out_route(a2, b2, h2, w_out, ln_g, ln_b, w_r, b_r):
    t = h2.shape[0]
    tm = min(512, t)
    nt = t // tm
    slots = _moe_slots(tm)
    row = lambda n: pl.BlockSpec((tm, n), lambda i: (i, 0))
    return pl.pallas_call(
        functools.partial(_out_route_kernel, tm=tm, slots=slots),
        grid=(nt,),
        in_specs=[row(a2.shape[1]), row(b2.shape[1]), row(D_MODEL)]
                 + [_full_spec(a, 1) for a in (w_out, ln_g, ln_b, w_r, b_r)],
        out_specs=[row(D_MODEL), pl.BlockSpec((slots, D_MODEL), lambda i: (i, 0)), row(LANES), row(LANES),
                   pl.BlockSpec((8, LANES), lambda i: (i, 0))],
        out_shape=[jax.ShapeDtypeStruct((t, D_MODEL), BF16), jax.ShapeDtypeStruct((nt * slots, D_MODEL), BF16),
                   jax.ShapeDtypeStruct((t, LANES), I32), jax.ShapeDtypeStruct((t, LANES), F32),
                   jax.ShapeDtypeStruct((nt * 8, LANES), I32)],
        compiler_params=_cp("parallel"),
        name="out_route",
    )(a2, b2, h2, w_out, ln_g, ln_b, w_r, b_r)


def _chunk_rows(chunk):
    return pl.ds(pl.multiple_of(chunk * MOE_CHUNK, MOE_CHUNK), MOE_CHUNK)


def _expert_kernel(src_ref, nreal_ref, be_ref, nu_ref, xs_ref, w1_ref, w3_ref, w2_ref, ys_ref,
                   xbuf, ybuf, gsem, ssem, *, cpb):
    del be_ref
    b = pl.program_id(0)
    nu = nu_ref[0]

    rows = cpb * MOE_CHUNK

    def gather(blk, slot, j):
        return pltpu.make_async_copy(xs_ref.at[_chunk_rows(src_ref[blk * cpb + j]), :],
                                     xbuf.at[slot, pl.ds(j * MOE_CHUNK, MOE_CHUNK), :], gsem.at[slot])

    def scatter(blk, slot, j):
        rows_j = pl.ds(j * MOE_CHUNK, MOE_CHUNK) if isinstance(j, int) else _chunk_rows(j)
        return pltpu.make_async_copy(ybuf.at[slot, rows_j, :],
                                     ys_ref.at[_chunk_rows(src_ref[blk * cpb + j]), :],
                                     ssem.at[slot])

    def start_gather(blk, slot):
        for j in range(cpb):
            gather(blk, slot, j).start()

    def wait_gather(slot):
        pltpu.make_async_copy(xs_ref.at[pl.ds(0, rows), :], xbuf.at[slot], gsem.at[slot]).wait()

    def for_real_chunks(blk, fn):
        def body(j, carry):
            fn(j)
            return carry
        lax.fori_loop(0, nreal_ref[blk], body, 0)

    def wait_scatter(blk, slot):
        full = nreal_ref[blk] == cpb

        @pl.when(full)
        def _():
            pltpu.make_async_copy(ybuf.at[slot], ys_ref.at[pl.ds(0, rows), :], ssem.at[slot]).wait()

        @pl.when(jnp.logical_not(full))
        def _():
            for_real_chunks(blk, lambda j: scatter(blk, slot, j).wait())

    @pl.when(b < nu)
    def _():
        slot = b % 2

        @pl.when(b == 0)
        def _():
            start_gather(b, slot)

        @pl.when(b + 1 < nu)
        def _():
            start_gather(b + 1, 1 - slot)

        wait_gather(slot)

        @pl.when(b >= 2)
        def _():
            wait_scatter(b - 2, slot)

        xb = xbuf[slot]
        a = _dot(xb, w1_ref[0, 0].astype(BF16))
        hid = a * _sigmoid(a) * _dot(xb, w3_ref[0, 0].astype(BF16))
        ybuf[slot] = _dot(hid.astype(BF16), w2_ref[0, 0].astype(BF16)).astype(BF16)
        full = nreal_ref[b] == cpb

        @pl.when(full)
        def _():
            for j in range(cpb):
                scatter(b, slot, j).start()

        @pl.when(jnp.logical_not(full))
        def _():
            for_real_chunks(b, lambda j: scatter(b, slot, j).start())

        @pl.when(b == nu - 1)
        def _():
            wait_scatter(b, slot)

            @pl.when(b >= 1)
            def _():
                wait_scatter(b - 1, 1 - slot)


def _experts(xs, src, nreal, block_e, n_used, w1, w3, w2, layer):
    nblk = nreal.shape[0]
    cpb = src.shape[0] // nblk
    rows = cpb * MOE_CHUNK
    wspec = lambda shape: pl.BlockSpec((1,) + shape, lambda i, src, nr, be, nu: (layer, be[i], 0, 0))
    return pl.pallas_call(
        functools.partial(_expert_kernel, cpb=cpb),
        grid_spec=pltpu.PrefetchScalarGridSpec(
            num_scalar_prefetch=4,
            grid=(nblk,),
            in_specs=[pl.BlockSpec(memory_space=pl.ANY),
                      wspec((1, D_MODEL, D_EXPERT)), wspec((1, D_MODEL, D_EXPERT)), wspec((1, D_EXPERT, D_MODEL))],
            out_specs=pl.BlockSpec(memory_space=pl.ANY),
            scratch_shapes=[pltpu.VMEM((2, rows, D_MODEL), BF16), pltpu.VMEM((2, rows, D_MODEL), BF16),
                            pltpu.SemaphoreType.DMA((2,)), pltpu.SemaphoreType.DMA((2,))]),
        out_shape=jax.ShapeDtypeStruct(xs.shape, xs.dtype),
        input_output_aliases={4: 0},
        compiler_params=_cp("arbitrary"),
        name="moe_experts",
    )(src, nreal, block_e, n_used, xs, w1, w3, w2)


def _combine_kernel(h_ref, rs_ref, rg_ref, ys_ref, g_ref, b_ref, o_ref, *, tm, slots):
    rs = rs_ref[...]
    rg = rg_ref[...]
    sl = lax.broadcasted_iota(I32, (tm, slots), 1).astype(jnp.int16)
    s1, s2 = rs[:, 0:1].astype(jnp.int16), rs[:, 1:2].astype(jnp.int16)
    g1, g2 = rg[:, 0:1].astype(BF16), rg[:, 1:2].astype(BF16)
    weight = jnp.where(sl == s1, g1, jnp.where(sl == s2, g2, jnp.zeros((), BF16)))
    y = _dot(weight, ys_ref[...])
    o_ref[...] = _layernorm(ALPHA * h_ref[...].astype(F32) + y, g_ref[...], b_ref[...]).astype(o_ref.dtype)


def _combine(h2, rs, rg, ys, ln_g, ln_b, out_dtype):
    t = h2.shape[0]
    tm = min(512, t)
    slots = ys.shape[0] // (t // tm)
    row = lambda n: pl.BlockSpec((tm, n), lambda i: (i, 0))
    return pl.pallas_call(
        functools.partial(_combine_kernel, tm=tm, slots=slots),
        grid=(t // tm,),
        in_specs=[row(D_MODEL), row(LANES), row(LANES), pl.BlockSpec((slots, D_MODEL), lambda i: (i, 0)),
                  _full_spec(ln_g, 1), _full_spec(ln_b, 1)],
        out_specs=row(D_MODEL),
        out_shape=jax.ShapeDtypeStruct((t, D_MODEL), out_dtype),
        compiler_params=_cp("parallel"),
        name="moe_combine",
    )(h2, rs, rg, ys, ln_g, ln_b)


def _moe(h1, xs, rs, rg, n16_rows, w1, w3, w2, layer, ln_g, ln_b, out_dtype):
    t = h1.shape[0]
    tm = min(512, t)
    nt = t // tm
    slots = xs.shape[0] // nt
    n16 = n16_rows[::8, :N_EXPERTS]
    cpb = max(4, min(MOE_BLOCK_CHUNKS, (t * TOP_K) // (N_EXPERTS * MOE_CHUNK)))
    per_e = jnp.sum(n16, axis=0)
    blocks_e = (per_e + cpb - 1) // cpb
    blk_end = jnp.cumsum(blocks_e)
    blk_start = blk_end - blocks_e
    n_used = blk_end[-1:].astype(I32)
    max_chunks = (t * TOP_K) // MOE_CHUNK + nt * N_EXPERTS
    nblk = max_chunks // cpb + N_EXPERTS
    blk = jnp.arange(nblk, dtype=I32)
    block_e = jnp.minimum(jnp.sum(blk_end[None, :] <= blk[:, None], axis=1), N_EXPERTS - 1).astype(I32)
    of_e = block_e[:, None] == jnp.arange(N_EXPERTS, dtype=I32)[None, :]
    pick_e = lambda tab: jnp.sum(jnp.where(of_e, tab[None, :], 0), axis=1)
    pick_col = lambda tab: jnp.sum(jnp.where(of_e[:, None, :], tab[None, :, :], 0), axis=2)
    run_end = pick_col(jnp.cumsum(n16, axis=0))
    tile_off = pick_col(jnp.cumsum(n16, axis=1) - n16)
    k = ((blk - pick_e(blk_start)) * cpb)[:, None] + jnp.arange(cpb, dtype=I32)[None, :]
    real = (k < pick_e(per_e)[:, None]) & (blk < n_used[0])[:, None]
    done = run_end[:, None, :] <= k[:, :, None]
    tile = jnp.minimum(jnp.sum(done, axis=2), nt - 1)
    run_first = jnp.max(jnp.where(done, run_end[:, None, :], 0), axis=2)
    of_t = tile[:, :, None] == jnp.arange(nt, dtype=I32)[None, None, :]
    src = tile * (slots // MOE_CHUNK) + jnp.sum(jnp.where(of_t, tile_off[:, None, :], 0), axis=2) + (k - run_first)
    src = jnp.where(real, src, src[:, :1])
    src = jnp.where((blk < n_used[0])[:, None], src, 0).astype(I32).reshape(-1)
    nreal = jnp.sum(real, axis=1).astype(I32)
    ys = _experts(xs, src, nreal, block_e, n_used, w1, w3, w2, layer)
    return _combine(h1, rs, rg, ys, ln_g, ln_b, out_dtype)


def _swap_halves(x):
    lane = lax.broadcasted_iota(I32, x.shape, 1)
    first = (lane % MLA_ROPE) < (MLA_ROPE // 2)
    return jnp.where(first, pltpu.roll(x, LANES - MLA_ROPE // 2, 1), pltpu.roll(x, MLA_ROPE // 2, 1))


def _rope(x, cos, sin):
    parts = []
    for t in range(x.shape[1] // LANES):
        sl = slice(t * LANES, (t + 1) * LANES)
        parts.append(x[:, sl] * cos[:, sl] + _swap_halves(x[:, sl]) * sin[:, sl])
    return parts[0] if len(parts) == 1 else jnp.concatenate(parts, axis=1)


def _odd_in_kernel(h_ref, w_ref, qg_ref, wqn_ref, wqp_ref, wuk_ref, perm_ref, kvg_ref,
                   cq_ref, sq_ref, ck_ref, sk_ref, gg_ref, gb_ref, ws_ref, bs_ref,
                   q_ref, kc_ref, kt_ref, ckv_ref, kpe_ref, gated_ref, vn_ref, *, tl, cl):
    hb = h_ref[0].astype(BF16)
    z = _dot(hb, w_ref[...])
    o_ckv = MLA_Q_RANK
    o_u = o_ckv + MLA_KV_RANK
    o_v = o_u + GMLP_WIDTH
    o_k = o_v + GMLP_WIDTH
    cq = z[:, :MLA_Q_RANK]
    cqn = cq * lax.rsqrt(jnp.mean(cq * cq, axis=-1, keepdims=True) + LN_EPS) * qg_ref[...]
    cqb = cqn.astype(BF16)
    qn = _dot(cqb, wqn_ref[...])
    qp = _dot(cqb, wqp_ref[...])
    qp = _rope(qp, cq_ref[...], sq_ref[...])
    qpe = _dot((qp * MLA_SCALE).astype(BF16), perm_ref[...])
    for h in range(MLA_HEADS):
        sl = slice(h * LANES, (h + 1) * LANES)
        qa = _dot((qn[:, sl] * MLA_SCALE).astype(BF16), wuk_ref[h])
        q_ref[0, h, :, 0:LANES] = qa.astype(BF16)
        q_ref[0, h, :, LANES:] = qpe[:, sl].astype(BF16)

    ckv = z[:, o_ckv:o_u]
    ckvn = ckv * lax.rsqrt(jnp.mean(ckv * ckv, axis=-1, keepdims=True) + LN_EPS) * kvg_ref[...]
    kp = z[:, o_k:]
    kp = _rope(kp, ck_ref[...], sk_ref[...])
    ckv_ref[0] = ckvn
    kpe_ref[0] = kp[:, :MLA_ROPE]
    one = (lax.broadcasted_iota(I32, (1, LANES), 1) == MLA_ONE_LANE - LANES).astype(F32)
    kp1 = kp + one
    kc_ref[0, :, 0:LANES] = ckvn.astype(BF16)
    kc_ref[0, :, LANES:] = kp1.astype(BF16)
    kt_ref[0, 0:LANES, :] = ckvn.T.astype(BF16)
    kt_ref[0, LANES:, :] = kp1.T.astype(BF16)

    gu = _gelu(z[:, o_u:o_v])
    vn = _layernorm(_gelu(z[:, o_v:o_k]), gg_ref[...], gb_ref[...])
    vn_ref[0] = vn
    vnb = vn.astype(BF16)
    for n in range(tl // cl):
        rs = slice(n * cl, (n + 1) * cl)
        for g in range(GMLP_GROUPS):
            ls = slice(g * GMLP_CH, (g + 1) * GMLP_CH)
            sg = _dot(ws_ref[g], vnb[rs, ls]) + bs_ref[:, ls]
            gated_ref[0, rs, ls] = (gu[rs, ls] * sg).astype(BF16)


def _odd_in(h3, w_in, q_g, w_qn, w_qp, w_uk, perm, kv_g, cos_q, sin_q, cos_k, sin_k,
            gm_g, gm_b, ws, bs):
    bsz, l, _ = h3.shape
    tl = min(l, 512)
    cl = min(l, GMLP_CHUNK)
    rowb = lambda n: pl.BlockSpec((1, tl, n), lambda b, i: (b, i, 0))
    tab = lambda n: pl.BlockSpec((tl, n), lambda b, i: (i, 0))
    consts = (w_in, q_g, w_qn, w_qp, w_uk, perm, kv_g)
    consts2 = (gm_g, gm_b, ws, bs)
    return pl.pallas_call(
        functools.partial(_odd_in_kernel, tl=tl, cl=cl),
        grid=(bsz, l // tl),
        in_specs=[rowb(D_MODEL)] + [_full_spec(a, 2) for a in consts]
                 + [tab(MLA_HEADS * MLA_ROPE), tab(MLA_HEADS * MLA_ROPE), tab(LANES), tab(LANES)]
                 + [_full_spec(a, 2) for a in consts2],
        out_specs=[pl.BlockSpec((1, MLA_HEADS, tl, MLA_QW), lambda b, i: (b, 0, i, 0)),
                   rowb(MLA_QW), pl.BlockSpec((1, MLA_QW, tl), lambda b, i: (b, 0, i)),
                   rowb(MLA_KV_RANK), rowb(MLA_ROPE), rowb(GMLP_WIDTH), rowb(GMLP_WIDTH)],
        out_shape=[jax.ShapeDtypeStruct((bsz, MLA_HEADS, l, MLA_QW), BF16),
                   jax.ShapeDtypeStruct((bsz, l, MLA_QW), BF16),
                   jax.ShapeDtypeStruct((bsz, MLA_QW, l), BF16),
                   jax.ShapeDtypeStruct((bsz, l, MLA_KV_RANK), F32),
                   jax.ShapeDtypeStruct((bsz, l, MLA_ROPE), F32),
                   jax.ShapeDtypeStruct((bsz, l, GMLP_WIDTH), BF16),
                   jax.ShapeDtypeStruct((bsz, l, GMLP_WIDTH), F32)],
        compiler_params=_cp("parallel", "parallel"),
        name="odd_in",
    )(h3, *consts, cos_q, sin_q, cos_k, sin_k, *consts2)


def _attn_kernel(qi_ref, kj_ref, flag_ref, q_ref, k_ref, kt_ref, wuv_ref, o_ref,
                 m_scr, l_scr, acc_scr, *, tq, hpb, tk, nsub, pos0, n_keys):
    p = pl.program_id(1)
    flag = flag_ref[p]
    nblk = MLA_HEADS // hpb
    width = hpb * tq

    @pl.when((flag & 1) != 0)
    def _():
        m_scr[...] = jnp.full_like(m_scr, -jnp.inf)
        l_scr[...] = jnp.zeros_like(l_scr)
        acc_scr[...] = jnp.zeros_like(acc_scr)

    def step(sub, hidden):
        kt = kt_ref[0, :, sub * tk:(sub + 1) * tk]
        kk = k_ref[0, sub * tk:(sub + 1) * tk, :]
        if hidden:
            qpos = pos0 + qi_ref[p] * tq + lax.broadcasted_iota(I32, (1, width), 1) % tq
            kpos = (kj_ref[p] * nsub + sub) * tk + lax.broadcasted_iota(I32, (tk, 1), 0)
            visible = ((kpos // CHUNK) <= (qpos // CHUNK)) & (kpos < n_keys)
        half = tk // MLA_KEY_PARTS
        for h in range(nblk):
            s = _dot_nt(kk, q_ref[0, h])
            if hidden:
                s = jnp.where(visible, s, -jnp.inf)
            m = m_scr[h]
            acc = acc_scr[h]
            l = l_scr[h]
            for part in range(MLA_KEY_PARTS):
                ks = slice(part * half, (part + 1) * half)
                sp = s[ks]
                m_new = jnp.maximum(m, jnp.max(sp, axis=0, keepdims=True))
                alpha = jnp.exp(m - m_new)
                pr = jnp.exp(sp - m_new).astype(BF16)
                acc = alpha * acc + _dot(kt[:MLA_KV_RANK, ks], pr)
                l = alpha * l + _dot(kt[MLA_ONE_LANE:MLA_ONE_LANE + 16, ks], pr)
                m = m_new
            acc_scr[h] = acc
            l_scr[h] = l
            m_scr[h] = m

    all_plain_bits = sum(4 << (2 * sub) for sub in range(nsub))
    tile_bits = sum(12 << (2 * sub) for sub in range(nsub))
    all_plain = (flag & tile_bits) == all_plain_bits

    @pl.when(all_plain)
    def _():
        for sub in range(nsub):
            step(sub, False)

    for sub in range(nsub):
        seen = (flag & (4 << (2 * sub))) != 0
        some_hidden = (flag & (8 << (2 * sub))) != 0

        @pl.when(jnp.logical_not(all_plain) & seen & some_hidden)
        def _():
            step(sub, True)

        @pl.when(jnp.logical_not(all_plain) & seen & jnp.logical_not(some_hidden))
        def _():
            step(sub, False)

    @pl.when((flag & 2) != 0)
    def _():
        if hpb == 1:
            for pair in range(MLA_HEADS // 2):
                tile = jnp.zeros((tq, LANES), F32)
                for h in (2 * pair, 2 * pair + 1):
                    lat = (acc_scr[h] / l_scr[h, 0:1, :]).astype(BF16)
                    tile = tile + _dot_tn(lat, wuv_ref[h])
                o_ref[0, :, pair * LANES:(pair + 1) * LANES] = tile.astype(BF16)
        else:
            head_of_lane = lax.broadcasted_iota(I32, (1, MLA_HEADS * MLA_V), 1) // MLA_V
            out = jnp.zeros((tq, MLA_HEADS * MLA_V), F32)
            for h in range(nblk):
                lat = (acc_scr[h] / l_scr[h, 0:1, :]).astype(BF16)
                full = _dot_tn(lat, wuv_ref[...])
                for hh in range(hpb):
                    out = out + jnp.where(head_of_lane == h * hpb + hh, full[hh * tq:(hh + 1) * tq], 0.0)
            o_ref[0] = out.astype(BF16)


def _attn_pairs(l, tq, tk, nsub, pos0, n_keys):
    qi, kj, flag = [], [], []
    for i in range(l // tq):
        q_first = pos0 + i * tq
        q_last = q_first + tq - 1
        vis = min(CHUNK * (q_last // CHUNK) + CHUNK - 1, n_keys - 1)
        ntiles = vis // tk + 1
        nsteps = -(-ntiles // nsub)
        for j in range(nsteps):
            f = int(j == 0) + 2 * int(j == nsteps - 1)
            for s in range(nsub):
                t = j * nsub + s
                if t < ntiles:
                    hidden = ((t + 1) * tk - 1) // CHUNK > q_first // CHUNK or (t + 1) * tk > n_keys
                    f += (4 + 8 * int(hidden)) << (2 * s)
            qi.append(i); kj.append(j); flag.append(f)
    return [jnp.asarray(np.array(a, np.int32)) for a in (qi, kj, flag)]


def _attention(q4, kc3, kt3, w_uv, pos0, n_keys, tq, tk, nsub):
    bsz, _, l, _ = q4.shape
    pairs = _attn_pairs(l, tq, tk, nsub, pos0, n_keys)
    npairs = int(pairs[0].shape[0])
    hpb = max(1, min(MLA_HEADS, 256 // tq)) if l == tq else 1
    nblk = MLA_HEADS // hpb
    width = hpb * tq
    q4 = q4.reshape(bsz, nblk, hpb * l, MLA_QW)
    if hpb == 1:
        heads = w_uv.reshape(MLA_KV_RANK, MLA_HEADS, MLA_V).transpose(1, 0, 2)
        w_uv = jnp.stack([jnp.pad(heads[h], ((0, 0), ((h % 2) * MLA_V, (1 - h % 2) * MLA_V)))
                          for h in range(MLA_HEADS)])
    return pl.pallas_call(
        functools.partial(_attn_kernel, tq=tq, hpb=hpb, tk=tk, nsub=nsub, pos0=pos0, n_keys=n_keys),
        grid_spec=pltpu.PrefetchScalarGridSpec(
            num_scalar_prefetch=3,
            grid=(bsz, npairs),
            in_specs=[pl.BlockSpec((1, nblk, width, MLA_QW), lambda b, p, qi, kj, f: (b, 0, qi[p], 0)),
                      pl.BlockSpec((1, nsub * tk, MLA_QW), lambda b, p, qi, kj, f: (b, kj[p], 0)),
                      pl.BlockSpec((1, MLA_QW, nsub * tk), lambda b, p, qi, kj, f: (b, 0, kj[p])),
                      pl.BlockSpec(w_uv.shape, lambda b, p, qi, kj, f: (0,) * w_uv.ndim)],
            out_specs=pl.BlockSpec((1, tq, MLA_HEADS * MLA_V), lambda b, p, qi, kj, f: (b, qi[p], 0)),
            scratch_shapes=[pltpu.VMEM((nblk, 1, width), F32),
                            pltpu.VMEM((nblk, 16, width), F32),
                            pltpu.VMEM((nblk, MLA_KV_RANK, width), F32)]),
        out_shape=jax.ShapeDtypeStruct((bsz, l, MLA_HEADS * MLA_V), BF16),
        compiler_params=_cp("parallel", "arbitrary"),
        name="mla_attention",
    )(*pairs, q4, kc3, kt3, w_uv)


def _rope_tables(pos0, l, width):
    half = MLA_ROPE // 2
    inv = ROPE_THETA ** (-jnp.arange(half, dtype=F32) * 2.0 / MLA_ROPE)
    ang = (pos0 + jnp.arange(l)).astype(F32)[:, None] * inv[None, :]
    cos, sin = jnp.cos(ang), jnp.sin(ang)
    cos32 = jnp.concatenate([cos, cos], axis=1)
    sin32 = jnp.concatenate([-sin, sin], axis=1)
    reps = width // MLA_ROPE
    return jnp.tile(cos32, (1, reps)), jnp.tile(sin32, (1, reps))


def _prep_even(w_in, w_gate_up, b_gate, gla_norm_g, pool_w, pool_scale, w_out):
    o_r = 2 * GLA_QK + GLA_V
    o_g = o_r + GLA_V
    o_p = o_g + GLA_GATE_RANK
    w_main = jnp.concatenate([w_in[:, :o_g], w_in[:, o_p:]], axis=1).astype(BF16)
    w_g = jnp.pad(w_in[:, o_g:o_p], ((0, 0), (0, LANES - GLA_GATE_RANK))).astype(BF16)
    w_gu = jnp.pad(w_gate_up, ((0, LANES - GLA_GATE_RANK), (0, 0))).astype(BF16)
    return dict(w_main=w_main, w_g=w_g, w_gu=w_gu, b_g=b_gate.reshape(1, -1),
                gnorm=gla_norm_g.reshape(1, -1), pool_w=pool_w.astype(BF16),
                pool_scale=pool_scale.reshape(1, -1), w_out=w_out.astype(BF16))


def _prep_odd(w_in, q_norm_g, w_uq, kv_norm_g, w_uk, w_uv, gm_g, gm_b, gm_ws, gm_bs, w_out):
    o_ckv = MLA_Q_RANK
    o_kpe = o_ckv + MLA_KV_RANK
    o_u = o_kpe + MLA_ROPE
    w_in2 = jnp.concatenate([w_in[:, :o_kpe], w_in[:, o_u:], w_in[:, o_kpe:o_u],
                             jnp.zeros((D_MODEL, LANES - MLA_ROPE), F32)], axis=1).astype(BF16)
    uq = w_uq.reshape(MLA_Q_RANK, MLA_HEADS, MLA_NOPE + MLA_ROPE)
    w_qn = jnp.pad(uq[:, :, :MLA_NOPE], ((0, 0), (0, 0), (0, LANES - MLA_NOPE)))
    w_qn = w_qn.reshape(MLA_Q_RANK, MLA_HEADS * LANES).astype(BF16)
    w_qp = uq[:, :, MLA_NOPE:].reshape(MLA_Q_RANK, MLA_HEADS * MLA_ROPE).astype(BF16)
    uk = w_uk.reshape(MLA_KV_RANK, MLA_HEADS, MLA_NOPE).transpose(1, 2, 0)
    w_ukp = jnp.pad(uk, ((0, 0), (0, LANES - MLA_NOPE), (0, 0))).astype(BF16)
    src = np.arange(MLA_HEADS * MLA_ROPE)
    perm = np.zeros((MLA_HEADS * MLA_ROPE, MLA_HEADS * LANES), np.float32)
    perm[src, (src // MLA_ROPE) * LANES + src % MLA_ROPE] = 1.0
    return dict(w_in=w_in2, q_g=q_norm_g.reshape(1, -1), w_qn=w_qn, w_qp=w_qp, w_uk=w_ukp,
                perm=jnp.asarray(perm, BF16), kv_g=kv_norm_g.reshape(1, -1), w_uv=w_uv.astype(BF16),
                gm_g=gm_g.reshape(1, -1), gm_b=gm_b.reshape(1, -1), gm_ws=gm_ws, gm_bs=gm_bs,
                w_out=w_out.astype(BF16))


def _prep_route(wg, bg, we, be):
    w_r = jnp.pad(jnp.concatenate([wg, we], axis=1), ((0, 0), (0, LANES - N_GROUPS - N_EXPERTS)))
    w_hi = w_r.astype(BF16)
    w_lo = (w_r - w_hi.astype(F32)).astype(BF16)
    b_r = jnp.pad(jnp.concatenate([bg, be]), (0, LANES - N_GROUPS - N_EXPERTS)).reshape(1, LANES)
    return jnp.concatenate([w_hi, w_lo], axis=1), b_r


def _even_mixer(h3, st0, hist, pos0, pw):
    bsz, l, _ = h3.shape
    q, k, la, v, r, xp = _even_in(h3.reshape(bsz * l, D_MODEL), pw['w_main'], pw['w_g'], pw['w_gu'], pw['b_g'])
    to3 = lambda a: a.reshape(bsz, l, a.shape[-1])
    o, st = _gla(to3(q), to3(k), to3(la), to3(v), to3(r), st0, pw['gnorm'])
    xp3 = to3(xp)
    hist16 = jnp.pad(hist, ((0, 0), (POOL_HALO - POOL_HIST, 0), (0, 0)))
    pooled = _pool(xp3, hist16, pw['pool_w'], pw['pool_scale'], pos0)
    hist_new = jnp.concatenate([hist, xp3[:, -POOL_HIST:].astype(F32)], axis=1)[:, -POOL_HIST:]
    return o, pooled, st, hist_new


def _odd_mixer(h3, ckv_past, kpe_past, pw):
    bsz, l, _ = h3.shape
    n_past = ckv_past.shape[1]
    cos_q, sin_q = _rope_tables(n_past, l, MLA_HEADS * MLA_ROPE)
    cos_k, sin_k = _rope_tables(n_past, l, MLA_ROPE)
    padk = ((0, 0), (0, LANES - MLA_ROPE))
    cos_k, sin_k = jnp.pad(cos_k, padk), jnp.pad(sin_k, padk)
    cl = min(l, GMLP_CHUNK)
    ws = jnp.tril(pw['gm_ws'][:, :cl, :cl]).astype(BF16)
    bs = jnp.repeat(pw['gm_bs'][:, :cl].T, GMLP_CH, axis=1)
    q4, kc, kt, ckv, kpe, gated, vn = _odd_in(h3, pw['w_in'], pw['q_g'], pw['w_qn'], pw['w_qp'], pw['w_uk'],
                                              pw['perm'], pw['kv_g'], cos_q, sin_q, cos_k, sin_k,
                                              pw['gm_g'], pw['gm_b'], ws, bs)
    n_keys = n_past + l
    tq, tk, nsub = min(l, 256), 512, 2
    span = tk * nsub
    if n_past:
        past = jnp.concatenate([ckv_past, kpe_past, jnp.ones((bsz, n_past, 1), F32),
                                jnp.zeros((bsz, n_past, MLA_QW - MLA_ONE_LANE - 1), F32)], axis=2).astype(BF16)
        kc = jnp.concatenate([past, kc], axis=1)
        kt = jnp.concatenate([past.transpose(0, 2, 1), kt], axis=2)
    kc = jnp.pad(kc, ((0, 0), (0, -n_keys % span), (0, 0)))
    kt = jnp.pad(kt, ((0, 0), (0, 0), (0, -n_keys % span)))
    attn = _attention(q4, kc, kt, pw['w_uv'], n_past, n_keys, tq, tk, nsub)
    return attn, gated, ckv, kpe, vn


def _finish_layer(a3, b3, h3, w_out, lw):
    bsz, l, _ = h3.shape
    t = bsz * l
    h1, xs, rs, rg, n16 = _out_route(a3.reshape(t, -1), b3.reshape(t, -1), h3.reshape(t, D_MODEL), w_out,
                                     lw['ln_mix_g'], lw['ln_mix_b'], lw['w_r'], lw['b_r'])
    h2 = _moe(h1, xs, rs, rg, n16, lw['w1'], lw['w3'], lw['w2'], lw['layer'], lw['ln_ffn_g'], lw['ln_ffn_b'],
              lw['out_dtype'])
    return h2.reshape(bsz, l, D_MODEL)


def kernel(x_prompt, x_sample, state_gla, state_pool, cache_mla_ckv, cache_mla_kpe, w_in_even, w_gate_up, b_gate, gla_norm_g, pool_w, pool_scale, w_out_even, w_in_odd, mla_q_norm_g, mla_w_uq, mla_kv_norm_g, mla_w_uk, mla_w_uv, gmlp_norm_g, gmlp_norm_b, gmlp_ws, gmlp_bs, w_out_odd, ln_mix_g, ln_mix_b, router_group_w, router_group_b, router_expert_w, router_expert_b, expert_w1, expert_w3, expert_w2, ln_ffn_g, ln_ffn_b):
    hp, hs = x_prompt, x_sample
    bp = hp.shape[0]
    past_len = cache_mla_ckv.shape[2]
    gla_p, gla_s, pool_p, pool_s = [], [], [], []
    ckv_p, ckv_s, kpe_p, kpe_s, gv_s = [], [], [], [], []

    def state_to_t(s):
        return s.transpose(0, 3, 1, 2).reshape(s.shape[0], GLA_DV, GLA_QK)

    def state_from_t(st):
        return st.reshape(st.shape[0], GLA_DV, GLA_HEADS, GLA_DK).transpose(0, 2, 3, 1)

    for layer in range(DEPTH):
        i = layer // 2
        w_r, b_r = _prep_route(router_group_w[layer], router_group_b[layer],
                               router_expert_w[layer], router_expert_b[layer])
        lw = dict(ln_mix_g=ln_mix_g[layer].reshape(1, -1), ln_mix_b=ln_mix_b[layer].reshape(1, -1),
                  ln_ffn_g=ln_ffn_g[layer].reshape(1, -1), ln_ffn_b=ln_ffn_b[layer].reshape(1, -1),
                  w_r=w_r, b_r=b_r, w1=expert_w1, w3=expert_w3, w2=expert_w2, layer=layer,
                  out_dtype=F32 if layer == DEPTH - 1 else BF16)
        if layer % 2 == 0:
            pw = _prep_even(w_in_even[i], w_gate_up[i], b_gate[i], gla_norm_g[i], pool_w[i], pool_scale[i],
                            w_out_even[i])
            st0 = jnp.zeros((bp, GLA_DV, GLA_QK), F32)
            hist0 = jnp.zeros((bp, POOL_HIST, POOL_WIDTH), F32)
            op, pp, stp, histp = _even_mixer(hp, st0, hist0, 0, pw)
            os_, ps, sts, hists = _even_mixer(hs, state_to_t(state_gla[i]), state_pool[i], past_len, pw)
            gla_p.append(state_from_t(stp)); gla_s.append(state_from_t(sts))
            pool_p.append(histp); pool_s.append(hists)
            ap, bpj, as_, bsj = op, pp, os_, ps
        else:
            pw = _prep_odd(w_in_odd[i], mla_q_norm_g[i], mla_w_uq[i], mla_kv_norm_g[i], mla_w_uk[i], mla_w_uv[i],
                           gmlp_norm_g[i], gmlp_norm_b[i], gmlp_ws[i], gmlp_bs[i], w_out_odd[i])
            no_ckv = jnp.zeros((bp, 0, MLA_KV_RANK), F32)
            no_kpe = jnp.zeros((bp, 0, MLA_ROPE), F32)
            ap, bpj, cp, kp, _ = _odd_mixer(hp, no_ckv, no_kpe, pw)
            as_, bsj, cs, ks, vs = _odd_mixer(hs, cache_mla_ckv[i], cache_mla_kpe[i], pw)
            ckv_p.append(cp); ckv_s.append(cs); kpe_p.append(kp); kpe_s.append(ks); gv_s.append(vs)
        hp = _finish_layer(ap, bpj, hp, pw['w_out'], lw)
        hs = _finish_layer(as_, bsj, hs, pw['w_out'], lw)
    return (hp, hs, jnp.stack(gla_p), jnp.stack(gla_s), jnp.stack(pool_p), jnp.stack(pool_s),
            jnp.stack(ckv_p), jnp.stack(ckv_s), jnp.stack(kpe_p), jnp.stack(kpe_s), jnp.stack(gv_s))
```

```python
import functools

import numpy as np
import jax
import jax.numpy as jnp
from jax import lax
from jax.experimental import pallas as pl
from jax.experimental.pallas import tpu as pltpu

F32 = jnp.float32
BF16 = jnp.bfloat16
I32 = jnp.int32

D_MODEL = 1024
DEPTH = 2
CHUNK = 64
ALPHA = (2 * DEPTH) ** 0.25
LN_EPS = 1e-5

GLA_HEADS = 4
GLA_DV = 128
GLA_DK = 64
GLA_QK = GLA_HEADS * GLA_DK
GLA_V = GLA_HEADS * GLA_DV
GLA_GATE_RANK = 16
GLA_GATE_TAU = 16.0
GLA_SUB = 8

POOL_WIDTH = 512
POOL_CH = 128
POOL_WINDOWS = (2, 4, 8, 16)
POOL_HIST = 15
POOL_HALO = 16

MLA_HEADS = 8
MLA_NOPE = 64
MLA_ROPE = 32
MLA_V = 64
MLA_Q_RANK = 256
MLA_KV_RANK = 128
ROPE_THETA = 10000.0
MLA_SCALE = (MLA_NOPE + MLA_ROPE) ** -0.5
MLA_QW = 256
MLA_ONE_LANE = MLA_KV_RANK + MLA_ROPE
MLA_KEY_PARTS = 2
GMLP_WIDTH = 512
GMLP_CH = 128
GMLP_GROUPS = 4
GMLP_CHUNK = 128

N_GROUPS = 4
EXPERTS_PER_GROUP = 8
N_EXPERTS = 32
TOP_K = 2
D_EXPERT = 256
MOE_ROWS = 512
MOE_CHUNK = 16
MOE_BLOCK_CHUNKS = MOE_ROWS // MOE_CHUNK

LANES = 128
VMEM_LIMIT = 48 * 1024 * 1024


def _cp(*sem):
    return pltpu.CompilerParams(dimension_semantics=sem, vmem_limit_bytes=VMEM_LIMIT)


def _dot(a, b):
    return jnp.dot(a, b, preferred_element_type=F32)


def _dot_nt(a, b):
    return lax.dot_general(a, b, (((1,), (1,)), ((), ())), preferred_element_type=F32)


def _dot_tn(a, b):
    return lax.dot_general(a, b, (((0,), (0,)), ((), ())), preferred_element_type=F32)


def _split3(x):
    hi = x.astype(BF16)
    r1 = x - hi.astype(F32)
    mid = r1.astype(BF16)
    lo = (r1 - mid.astype(F32)).astype(BF16)
    return hi, mid, lo


def _layernorm(x, g, b):
    mu = jnp.mean(x, axis=-1, keepdims=True)
    xc = x - mu
    var = jnp.mean(xc * xc, axis=-1, keepdims=True)
    return xc * lax.rsqrt(var + LN_EPS) * g + b


def _gelu(x):
    return 0.5 * x * (1.0 + jnp.tanh(0.7978845608028654 * (x + 0.044715 * (x * x * x))))


def _sigmoid(x):
    return 1.0 / (1.0 + jnp.exp(-x))


def _full_spec(a, nargs):
    nd = a.ndim
    if nargs == 1:
        return pl.BlockSpec(a.shape, lambda i: (0,) * nd)
    return pl.BlockSpec(a.shape, lambda i, j: (0,) * nd)


def _even_in_kernel(x_ref, w_ref, wg_ref, wgu_ref, bg_ref,
                    q_ref, k_ref, la_ref, v_ref, r_ref, xp_ref):
    xb = x_ref[...].astype(BF16)
    z = _dot(xb, w_ref[...])
    q_ref[...] = z[:, 0:GLA_QK] * (GLA_DK ** -0.5)
    k_ref[...] = z[:, GLA_QK:2 * GLA_QK]
    v_ref[...] = z[:, 2 * GLA_QK:2 * GLA_QK + GLA_V]
    r_ref[...] = z[:, 2 * GLA_QK + GLA_V:2 * GLA_QK + 2 * GLA_V]
    xp_ref[...] = z[:, 2 * GLA_QK + 2 * GLA_V:].astype(BF16)
    g = _dot(xb, wg_ref[...])
    pre = _dot(g.astype(BF16), wgu_ref[...]) + bg_ref[...]
    logsig = jnp.minimum(pre, 0.0) - jnp.log(1.0 + jnp.exp(-jnp.abs(pre)))
    la_ref[...] = logsig * (1.0 / GLA_GATE_TAU)


def _even_in(x2, w_main, w_g, w_gu, b_g):
    t = x2.shape[0]
    tm = min(512, t)
    row = lambda n: pl.BlockSpec((tm, n), lambda i: (i, 0))
    widths = (GLA_QK, GLA_QK, GLA_QK, GLA_V, GLA_V, POOL_WIDTH)
    return pl.pallas_call(
        _even_in_kernel,
        grid=(t // tm,),
        in_specs=[row(D_MODEL)] + [_full_spec(a, 1) for a in (w_main, w_g, w_gu, b_g)],
        out_specs=[row(n) for n in widths],
        out_shape=[jax.ShapeDtypeStruct((t, n), BF16 if i == 5 else F32) for i, n in enumerate(widths)],
        compiler_params=_cp("parallel"),
        name="even_in",
    )(x2, w_main, w_g, w_gu, b_g)


def _gla_consts(c):
    n = c * GLA_SUB
    tri = np.tril(np.ones((c, c), np.float32))
    headsum = (np.arange(GLA_QK)[:, None] // GLA_DK == np.arange(GLA_V)[None, :] // GLA_DV).astype(np.float32)
    msel = (np.arange(n)[None, :] // GLA_SUB == np.arange(c)[:, None]).astype(np.float32)
    return [jnp.asarray(a, BF16) for a in (tri, headsum, msel)]


def _gla_chunk(q, k, la, v, st, c, tri, headsum, msel):
    lane = lax.broadcasted_iota(I32, (1, GLA_QK), 1)
    head_of_lane = lane // GLA_DK
    row = lax.broadcasted_iota(I32, (c, 1), 0)
    ii = lax.broadcasted_iota(I32, (c, c), 0)
    jj = lax.broadcasted_iota(I32, (c, c), 1)

    hi, mid, lo = _split3(la)
    b = _dot(tri, hi) + _dot(tri, mid) + _dot(tri, lo)

    head_masks = [head_of_lane == h for h in range(GLA_HEADS)]
    vb = v.astype(BF16)

    a_off = [jnp.zeros((c, c), F32) for _ in range(GLA_HEADS)]
    s = c // 2
    while s >= GLA_SUB:
        nblk = c // (2 * s)
        blk = row // (2 * s)
        right = ((row // s) % 2) == 1
        bref = jnp.zeros((c, GLA_QK), F32)
        for m in range(nblk):
            r0 = m * 2 * s + s - 1
            bref = jnp.where(blk == m, b[r0:r0 + 1, :], bref)
        qe = jnp.where(right, q * jnp.exp(jnp.minimum(b - bref, 0.0)), 0.0)
        ke = jnp.where(right, 0.0, k * jnp.exp(jnp.minimum(bref - b, 0.0))).astype(BF16)
        same = (ii // (2 * s)) == (jj // (2 * s))
        for h in range(GLA_HEADS):
            a = _dot_nt(jnp.where(head_masks[h], qe, 0.0).astype(BF16), ke)
            a_off[h] = a_off[h] + (a if nblk == 1 else jnp.where(same, a, 0.0))
        s //= 2

    nsb = c // GLA_SUB
    parts = []
    for i in range(nsb):
        sl = slice(i * GLA_SUB, (i + 1) * GLA_SUB)
        bi, qi, ki = b[sl], q[sl], k[sl]
        diff = bi[:, None, :] - bi[None, :, :]
        p = qi[:, None, :] * ki[None, :, :] * jnp.exp(jnp.minimum(diff, 0.0))
        parts.append(p.reshape(GLA_SUB * GLA_SUB, GLA_QK))
    n = nsb * GLA_SUB * GLA_SUB
    idx = lax.broadcasted_iota(I32, (n, 1), 0)
    causal = (idx % GLA_SUB) <= ((idx // GLA_SUB) % GLA_SUB)
    pcat = jnp.where(causal, jnp.concatenate(parts, axis=0), 0.0).astype(BF16)
    rsum = _dot(pcat, headsum)

    qb = q * jnp.exp(b)
    b_end = b[c - 1:c, :]
    kd = (k * jnp.exp(b_end - b)).astype(BF16)
    stb = st.astype(BF16)
    st_new = st * jnp.exp(b_end)

    outs = []
    for h in range(GLA_HEADS):
        vh = vb[:, h * GLA_DV:(h + 1) * GLA_DV]
        vf = v[:, h * GLA_DV:(h + 1) * GLA_DV]
        vt = jnp.concatenate(
            [jnp.broadcast_to(vf[i * GLA_SUB:(i + 1) * GLA_SUB][None], (GLA_SUB, GLA_SUB, GLA_DV))
             .reshape(GLA_SUB * GLA_SUB, GLA_DV) for i in range(nsb)], axis=0)
        xh = rsum[:, h * GLA_DV:(h + 1) * GLA_DV] * vt
        o = _dot(msel, xh.astype(BF16))
        o = o + _dot(a_off[h].astype(BF16), vh)
        o = o + _dot_nt(jnp.where(head_masks[h], qb, 0.0).astype(BF16), stb)
        outs.append(o)
        st_new = st_new + jnp.where(head_masks[h], _dot_tn(vh, kd), 0.0)
    return outs, st_new


def _gla_kernel(q_ref, k_ref, la_ref, v_ref, r_ref, st0_ref, g_ref, tri_ref, hs_ref, ms_ref,
                o_ref, st_ref, st_scr, *, c, nchunks):
    @pl.when(pl.program_id(1) == 0)
    def _():
        st_scr[...] = st0_ref[0]

    def body(ci, carry):
        r0 = pl.multiple_of(ci * c, c)
        rows = pl.ds(r0, c)
        outs, st_new = _gla_chunk(q_ref[0, rows, :], k_ref[0, rows, :], la_ref[0, rows, :], v_ref[0, rows, :],
                                  st_scr[...], c, tri_ref[...], hs_ref[...], ms_ref[...])
        st_scr[...] = st_new
        r = r_ref[0, rows, :]
        g = g_ref[...]
        for h in range(GLA_HEADS):
            o = outs[h]
            sl = slice(h * GLA_DV, (h + 1) * GLA_DV)
            on = o * lax.rsqrt(jnp.mean(o * o, axis=-1, keepdims=True) + LN_EPS) * g
            rh = r[:, sl]
            o_ref[0, rows, sl] = (on * (rh * _sigmoid(rh))).astype(BF16)
        return carry

    lax.fori_loop(0, nchunks, body, 0, unroll=8 if nchunks % 8 == 0 else 1)
    st_ref[0] = st_scr[...]


def _gla(q3, k3, la3, v3, r3, st0, gnorm):
    bsz, l, _ = q3.shape
    c = min(l, CHUNK)
    tl = min(l, 512)
    blk = lambda n: pl.BlockSpec((1, tl, n), lambda b, i: (b, i, 0))
    st_spec = pl.BlockSpec((1, GLA_DV, GLA_QK), lambda b, i: (b, 0, 0))
    consts = _gla_consts(c)
    return pl.pallas_call(
        functools.partial(_gla_kernel, c=c, nchunks=tl // c),
        grid=(bsz, l // tl),
        in_specs=[blk(GLA_QK), blk(GLA_QK), blk(GLA_QK), blk(GLA_V), blk(GLA_V), st_spec,
                  _full_spec(gnorm, 2)] + [_full_spec(a, 2) for a in consts],
        out_specs=[blk(GLA_V), st_spec],
        out_shape=[jax.ShapeDtypeStruct((bsz, l, GLA_V), BF16),
                   jax.ShapeDtypeStruct((bsz, GLA_DV, GLA_QK), F32)],
        scratch_shapes=[pltpu.VMEM((GLA_DV, GLA_QK), F32)],
        compiler_params=_cp("parallel", "arbitrary"),
        name="gla",
    )(q3, k3, la3, v3, r3, st0, gnorm, *consts)


def _pool_kernel(x_ref, halo_ref, hist_ref, w_ref, scale_ref, o_ref, *, tl, pos0):
    i = pl.program_id(1)
    x = x_ref[0].astype(F32)
    prev = jnp.where(i == 0, hist_ref[0], halo_ref[0].astype(F32))
    e = jnp.concatenate([prev, x], axis=0)
    t = i * tl + lax.broadcasted_iota(I32, (tl, 1), 0)
    pos = pos0 + t
    sums = []
    shift = 1
    for g, w in enumerate(POOL_WINDOWS):
        e = e[:, POOL_CH:] if g > 0 else e
        while shift < w:
            e = e[shift:] + e[:-shift]
            shift *= 2
        off = POOL_HALO - (w - 1)
        sums.append(e[off:off + tl, :POOL_CH])
    outs = []
    for g, w in enumerate(POOL_WINDOWS):
        cnt = jnp.minimum(pos + 1, w).astype(F32)
        mix = sums[g] / cnt - x[:, g * POOL_CH:(g + 1) * POOL_CH]
        outs.append(_dot(mix.astype(BF16), w_ref[g]))
    o_ref[0] = (jnp.concatenate(outs, axis=1) * scale_ref[...]).astype(BF16)


def _pool(xp3, hist16, pool_w, pool_scale, pos0):
    bsz, l, _ = xp3.shape
    tl = min(l, 512)
    per = tl // POOL_HALO
    return pl.pallas_call(
        functools.partial(_pool_kernel, tl=tl, pos0=pos0),
        grid=(bsz, l // tl),
        in_specs=[pl.BlockSpec((1, tl, POOL_WIDTH), lambda b, i: (b, i, 0)),
                  pl.BlockSpec((1, POOL_HALO, POOL_WIDTH), lambda b, i: (b, jnp.maximum(i * per - 1, 0), 0)),
                  pl.BlockSpec((1, POOL_HALO, POOL_WIDTH), lambda b, i: (b, 0, 0)),
                  _full_spec(pool_w, 2), _full_spec(pool_scale, 2)],
        out_specs=pl.BlockSpec((1, tl, POOL_WIDTH), lambda b, i: (b, i, 0)),
        out_shape=jax.ShapeDtypeStruct((bsz, l, POOL_WIDTH), BF16),
        compiler_params=_cp("parallel", "parallel"),
        name="pool",
    )(xp3, xp3, hist16, pool_w, pool_scale)


def _out_route_kernel(a_ref, b_ref, h_ref, w_ref, g_ref, bt_ref, wr_ref, br_ref,
                      h1_ref, xs_ref, rs_ref, rg_ref, n16_ref, *, tm, slots):
    half = w_ref.shape[0] // 2
    y = _dot(a_ref[...].astype(BF16), w_ref[0:half, :]) + _dot(b_ref[...].astype(BF16), w_ref[half:, :])
    x = _layernorm(ALPHA * h_ref[...].astype(F32) + y, g_ref[...], bt_ref[...])
    h1_ref[...] = x.astype(BF16)

    xh = x.astype(BF16)
    xl = (x - xh.astype(F32)).astype(BF16)
    hl = _dot(xh, wr_ref[...])
    logits = hl[:, :LANES] + hl[:, LANES:] + _dot(xl, wr_ref[:, 0:LANES]) + br_ref[...]

    lane = lax.broadcasted_iota(I32, (tm, LANES), 1)
    lanef = lane.astype(F32)
    neg = -jnp.inf
    big = jnp.float32(1 << 20)

    def first_lane(hit):
        return jnp.min(jnp.where(hit, lanef, big), axis=-1, keepdims=True).astype(I32)

    gl = jnp.where(lane < N_GROUPS, logits, neg)
    gmax = jnp.max(gl, axis=-1, keepdims=True)
    g_sel = first_lane(gl == gmax)
    g_prob = 1.0 / jnp.sum(jnp.exp(gl - gmax), axis=-1, keepdims=True)
    eidx = lane - N_GROUPS
    in_grp = (eidx >= 0) & (eidx < N_EXPERTS) & ((eidx // EXPERTS_PER_GROUP) == g_sel)
    el = jnp.where(in_grp, logits, neg)
    v1 = jnp.max(el, axis=-1, keepdims=True)
    i1 = first_lane(el == v1)
    el2 = jnp.where(lane == i1, neg, el)
    v2 = jnp.max(el2, axis=-1, keepdims=True)
    i2 = first_lane(el2 == v2)
    e21 = jnp.exp(v2 - v1)
    gate1 = g_prob / (1.0 + e21)
    gate2 = g_prob * e21 / (1.0 + e21)
    e1 = i1 - N_GROUPS
    e2 = i2 - N_GROUPS

    oh1 = lane == e1
    oh2 = lane == e2
    oh = oh1.astype(F32) + oh2.astype(F32)
    ti = lax.broadcasted_iota(I32, (tm, tm), 0)
    tj = lax.broadcasted_iota(I32, (tm, tm), 1)
    before = _dot((tj < ti).astype(BF16), oh.astype(BF16))
    cnt = jnp.sum(oh, axis=0, keepdims=True)
    n16 = jnp.floor((cnt + (MOE_CHUNK - 1)) * (1.0 / MOE_CHUNK))
    n16_8 = jnp.broadcast_to(n16, (8, LANES))
    ui = lax.broadcasted_iota(I32, (LANES, LANES), 0)
    uj = lax.broadcasted_iota(I32, (LANES, LANES), 1)
    run_start = _dot(n16_8.astype(BF16), (ui < uj).astype(BF16))[0:1]
    slot_of = MOE_CHUNK * run_start + before
    slot1 = jnp.sum(jnp.where(oh1, slot_of, 0.0), axis=-1, keepdims=True).astype(I32)
    slot2 = jnp.sum(jnp.where(oh2, slot_of, 0.0), axis=-1, keepdims=True).astype(I32)
    sl = lax.broadcasted_iota(I32, (tm, slots), 1)
    place = ((sl == slot1) | (sl == slot2)).astype(BF16)
    xs_ref[...] = _dot_tn(place, xh).astype(BF16)

    rs_ref[...] = jnp.where(lane == 0, slot1, jnp.where(lane == 1, slot2, 0))
    rg_ref[...] = jnp.where(lane == 0, gate1, jnp.where(lane == 1, gate2, 0.0))
    n16_ref[...] = n16_8.astype(I32)


def _moe_slots(tm):
    worst = tm * TOP_K + N_EXPERTS * (MOE_CHUNK - 1)
    return -(-worst // MOE_ROWS) * MOE_ROWS


def _out_route(a2, b2, h2, w_out, ln_g, ln_b, w_r, b_r):
    t = h2.shape[0]
    tm = min(512, t)
    nt = t // tm
    slots = _moe_slots(tm)
    row = lambda n: pl.BlockSpec((tm, n), lambda i: (i, 0))
    return pl.pallas_call(
        functools.partial(_out_route_kernel, tm=tm, slots=slots),
        grid=(nt,),
        in_specs=[row(a2.shape[1]), row(b2.shape[1]), row(D_MODEL)]
                 + [_full_spec(a, 1) for a in (w_out, ln_g, ln_b, w_r, b_r)],
        out_specs=[row(D_MODEL), pl.BlockSpec((slots, D_MODEL), lambda i: (i, 0)), row(LANES), row(LANES),
                   pl.BlockSpec((8, LANES), lambda i: (i, 0))],
        out_shape=[jax.ShapeDtypeStruct((t, D_MODEL), BF16), jax.ShapeDtypeStruct((nt * slots, D_MODEL), BF16),
                   jax.ShapeDtypeStruct((t, LANES), I32), jax.ShapeDtypeStruct((t, LANES), F32),
                   jax.ShapeDtypeStruct((nt * 8, LANES), I32)],
        compiler_params=_cp("parallel"),
        name="out_route",
    )(a2, b2, h2, w_out, ln_g, ln_b, w_r, b_r)


def _chunk_rows(chunk):
    return pl.ds(pl.multiple_of(chunk * MOE_CHUNK, MOE_CHUNK), MOE_CHUNK)


def _expert_kernel(src_ref, nreal_ref, be_ref, nu_ref, xs_ref, w1_ref, w3_ref, w2_ref, ys_ref,
                   xbuf, ybuf, gsem, ssem, *, cpb):
    del be_ref
    b = pl.program_id(0)
    nu = nu_ref[0]

    rows = cpb * MOE_CHUNK

    def gather(blk, slot, j):
        return pltpu.make_async_copy(xs_ref.at[_chunk_rows(src_ref[blk * cpb + j]), :],
                                     xbuf.at[slot, pl.ds(j * MOE_CHUNK, MOE_CHUNK), :], gsem.at[slot])

    def scatter(blk, slot, j):
        rows_j = pl.ds(j * MOE_CHUNK, MOE_CHUNK) if isinstance(j, int) else _chunk_rows(j)
        return pltpu.make_async_copy(ybuf.at[slot, rows_j, :],
                                     ys_ref.at[_chunk_rows(src_ref[blk * cpb + j]), :],
                                     ssem.at[slot])

    def start_gather(blk, slot):
        for j in range(cpb):
            gather(blk, slot, j).start()

    def wait_gather(slot):
        pltpu.make_async_copy(xs_ref.at[pl.ds(0, rows), :], xbuf.at[slot], gsem.at[slot]).wait()

    def for_real_chunks(blk, fn):
        def body(j, carry):
            fn(j)
            return carry
        lax.fori_loop(0, nreal_ref[blk], body, 0)

    def wait_scatter(blk, slot):
        full = nreal_ref[blk] == cpb

        @pl.when(full)
        def _():
            pltpu.make_async_copy(ybuf.at[slot], ys_ref.at[pl.ds(0, rows), :], ssem.at[slot]).wait()

        @pl.when(jnp.logical_not(full))
        def _():
            for_real_chunks(blk, lambda j: scatter(blk, slot, j).wait())

    @pl.when(b < nu)
    def _():
        slot = b % 2

        @pl.when(b == 0)
        def _():
            start_gather(b, slot)

        @pl.when(b + 1 < nu)
        def _():
            start_gather(b + 1, 1 - slot)

        wait_gather(slot)

        @pl.when(b >= 2)
        def _():
            wait_scatter(b - 2, slot)

        xb = xbuf[slot]
        a = _dot(xb, w1_ref[0, 0].astype(BF16))
        hid = a * _sigmoid(a) * _dot(xb, w3_ref[0, 0].astype(BF16))
        ybuf[slot] = _dot(hid.astype(BF16), w2_ref[0, 0].astype(BF16)).astype(BF16)
        full = nreal_ref[b] == cpb

        @pl.when(full)
        def _():
            for j in range(cpb):
                scatter(b, slot, j).start()

        @pl.when(jnp.logical_not(full))
        def _():
            for_real_chunks(b, lambda j: scatter(b, slot, j).start())

        @pl.when(b == nu - 1)
        def _():
            wait_scatter(b, slot)

            @pl.when(b >= 1)
            def _():
                wait_scatter(b - 1, 1 - slot)


def _experts(xs, src, nreal, block_e, n_used, w1, w3, w2, layer):
    nblk = nreal.shape[0]
    cpb = src.shape[0] // nblk
    rows = cpb * MOE_CHUNK
    wspec = lambda shape: pl.BlockSpec((1,) + shape, lambda i, src, nr, be, nu: (layer, be[i], 0, 0))
    return pl.pallas_call(
        functools.partial(_expert_kernel, cpb=cpb),
        grid_spec=pltpu.PrefetchScalarGridSpec(
            num_scalar_prefetch=4,
            grid=(nblk,),
            in_specs=[pl.BlockSpec(memory_space=pl.ANY),
                      wspec((1, D_MODEL, D_EXPERT)), wspec((1, D_MODEL, D_EXPERT)), wspec((1, D_EXPERT, D_MODEL))],
            out_specs=pl.BlockSpec(memory_space=pl.ANY),
            scratch_shapes=[pltpu.VMEM((2, rows, D_MODEL), BF16), pltpu.VMEM((2, rows, D_MODEL), BF16),
                            pltpu.SemaphoreType.DMA((2,)), pltpu.SemaphoreType.DMA((2,))]),
        out_shape=jax.ShapeDtypeStruct(xs.shape, xs.dtype),
        input_output_aliases={4: 0},
        compiler_params=_cp("arbitrary"),
        name="moe_experts",
    )(src, nreal, block_e, n_used, xs, w1, w3, w2)


def _combine_kernel(h_ref, rs_ref, rg_ref, ys_ref, g_ref, b_ref, o_ref, *, tm, slots):
    rs = rs_ref[...]
    rg = rg_ref[...]
    sl = lax.broadcasted_iota(I32, (tm, slots), 1).astype(jnp.int16)
    s1, s2 = rs[:, 0:1].astype(jnp.int16), rs[:, 1:2].astype(jnp.int16)
    g1, g2 = rg[:, 0:1].astype(BF16), rg[:, 1:2].astype(BF16)
    weight = jnp.where(sl == s1, g1, jnp.where(sl == s2, g2, jnp.zeros((), BF16)))
    y = _dot(weight, ys_ref[...])
    o_ref[...] = _layernorm(ALPHA * h_ref[...].astype(F32) + y, g_ref[...], b_ref[...]).astype(o_ref.dtype)


def _combine(h2, rs, rg, ys, ln_g, ln_b, out_dtype):
    t = h2.shape[0]
    tm = min(512, t)
    slots = ys.shape[0] // (t // tm)
    row = lambda n: pl.BlockSpec((tm, n), lambda i: (i, 0))
    return pl.pallas_call(
        functools.partial(_combine_kernel, tm=tm, slots=slots),
        grid=(t // tm,),
        in_specs=[row(D_MODEL), row(LANES), row(LANES), pl.BlockSpec((slots, D_MODEL), lambda i: (i, 0)),
                  _full_spec(ln_g, 1), _full_spec(ln_b, 1)],
        out_specs=row(D_MODEL),
        out_shape=jax.ShapeDtypeStruct((t, D_MODEL), out_dtype),
        compiler_params=_cp("parallel"),
        name="moe_combine",
    )(h2, rs, rg, ys, ln_g, ln_b)


def _moe(h1, xs, rs, rg, n16_rows, w1, w3, w2, layer, ln_g, ln_b, out_dtype):
    t = h1.shape[0]
    tm = min(512, t)
    nt = t // tm
    slots = xs.shape[0] // nt
    n16 = n16_rows[::8, :N_EXPERTS]
    cpb = max(4, min(MOE_BLOCK_CHUNKS, (t * TOP_K) // (N_EXPERTS * MOE_CHUNK)))
    per_e = jnp.sum(n16, axis=0)
    blocks_e = (per_e + cpb - 1) // cpb
    blk_end = jnp.cumsum(blocks_e)
    blk_start = blk_end - blocks_e
    n_used = blk_end[-1:].astype(I32)
    max_chunks = (t * TOP_K) // MOE_CHUNK + nt * N_EXPERTS
    nblk = max_chunks // cpb + N_EXPERTS
    blk = jnp.arange(nblk, dtype=I32)
    block_e = jnp.minimum(jnp.sum(blk_end[None, :] <= blk[:, None], axis=1), N_EXPERTS - 1).astype(I32)
    of_e = block_e[:, None] == jnp.arange(N_EXPERTS, dtype=I32)[None, :]
    pick_e = lambda tab: jnp.sum(jnp.where(of_e, tab[None, :], 0), axis=1)
    pick_col = lambda tab: jnp.sum(jnp.where(of_e[:, None, :], tab[None, :, :], 0), axis=2)
    run_end = pick_col(jnp.cumsum(n16, axis=0))
    tile_off = pick_col(jnp.cumsum(n16, axis=1) - n16)
    k = ((blk - pick_e(blk_start)) * cpb)[:, None] + jnp.arange(cpb, dtype=I32)[None, :]
    real = (k < pick_e(per_e)[:, None]) & (blk < n_used[0])[:, None]
    done = run_end[:, None, :] <= k[:, :, None]
    tile = jnp.minimum(jnp.sum(done, axis=2), nt - 1)
    run_first = jnp.max(jnp.where(done, run_end[:, None, :], 0), axis=2)
    of_t = tile[:, :, None] == jnp.arange(nt, dtype=I32)[None, None, :]
    src = tile * (slots // MOE_CHUNK) + jnp.sum(jnp.where(of_t, tile_off[:, None, :], 0), axis=2) + (k - run_first)
    src = jnp.where(real, src, src[:, :1])
    src = jnp.where((blk < n_used[0])[:, None], src, 0).astype(I32).reshape(-1)
    nreal = jnp.sum(real, axis=1).astype(I32)
    ys = _experts(xs, src, nreal, block_e, n_used, w1, w3, w2, layer)
    return _combine(h1, rs, rg, ys, ln_g, ln_b, out_dtype)


def _swap_halves(x):
    lane = lax.broadcasted_iota(I32, x.shape, 1)
    first = (lane % MLA_ROPE) < (MLA_ROPE // 2)
    return jnp.where(first, pltpu.roll(x, LANES - MLA_ROPE // 2, 1), pltpu.roll(x, MLA_ROPE // 2, 1))


def _rope(x, cos, sin):
    parts = []
    for t in range(x.shape[1] // LANES):
        sl = slice(t * LANES, (t + 1) * LANES)
        parts.append(x[:, sl] * cos[:, sl] + _swap_halves(x[:, sl]) * sin[:, sl])
    return parts[0] if len(parts) == 1 else jnp.concatenate(parts, axis=1)


def _odd_in_kernel(h_ref, w_ref, qg_ref, wqn_ref, wqp_ref, wuk_ref, perm_ref, kvg_ref,
                   cq_ref, sq_ref, ck_ref, sk_ref, gg_ref, gb_ref, ws_ref, bs_ref,
                   q_ref, kc_ref, kt_ref, ckv_ref, kpe_ref, gated_ref, vn_ref, *, tl, cl):
    hb = h_ref[0].astype(BF16)
    z = _dot(hb, w_ref[...])
    o_ckv = MLA_Q_RANK
    o_u = o_ckv + MLA_KV_RANK
    o_v = o_u + GMLP_WIDTH
    o_k = o_v + GMLP_WIDTH
    cq = z[:, :MLA_Q_RANK]
    cqn = cq * lax.rsqrt(jnp.mean(cq * cq, axis=-1, keepdims=True) + LN_EPS) * qg_ref[...]
    cqb = cqn.astype(BF16)
    qn = _dot(cqb, wqn_ref[...])
    qp = _dot(cqb, wqp_ref[...])
    qp = _rope(qp, cq_ref[...], sq_ref[...])
    qpe = _dot((qp * MLA_SCALE).astype(BF16), perm_ref[...])
    for h in range(MLA_HEADS):
        sl = slice(h * LANES, (h + 1) * LANES)
        qa = _dot((qn[:, sl] * MLA_SCALE).astype(BF16), wuk_ref[h])
        q_ref[0, h, :, 0:LANES] = qa.astype(BF16)
        q_ref[0, h, :, LANES:] = qpe[:, sl].astype(BF16)

    ckv = z[:, o_ckv:o_u]
    ckvn = ckv * lax.rsqrt(jnp.mean(ckv * ckv, axis=-1, keepdims=True) + LN_EPS) * kvg_ref[...]
    kp = z[:, o_k:]
    kp = _rope(kp, ck_ref[...], sk_ref[...])
    ckv_ref[0] = ckvn
    kpe_ref[0] = kp[:, :MLA_ROPE]
    one = (lax.broadcasted_iota(I32, (1, LANES), 1) == MLA_ONE_LANE - LANES).astype(F32)
    kp1 = kp + one
    kc_ref[0, :, 0:LANES] = ckvn.astype(BF16)
    kc_ref[0, :, LANES:] = kp1.astype(BF16)
    kt_ref[0, 0:LANES, :] = ckvn.T.astype(BF16)
    kt_ref[0, LANES:, :] = kp1.T.astype(BF16)

    gu = _gelu(z[:, o_u:o_v])
    vn = _layernorm(_gelu(z[:, o_v:o_k]), gg_ref[...], gb_ref[...])
    vn_ref[0] = vn
    vnb = vn.astype(BF16)
    for n in range(tl // cl):
        rs = slice(n * cl, (n + 1) * cl)
        for g in range(GMLP_GROUPS):
            ls = slice(g * GMLP_CH, (g + 1) * GMLP_CH)
            sg = _dot(ws_ref[g], vnb[rs, ls]) + bs_ref[:, ls]
            gated_ref[0, rs, ls] = (gu[rs, ls] * sg).astype(BF16)


def _odd_in(h3, w_in, q_g, w_qn, w_qp, w_uk, perm, kv_g, cos_q, sin_q, cos_k, sin_k,
            gm_g, gm_b, ws, bs):
    bsz, l, _ = h3.shape
    tl = min(l, 512)
    cl = min(l, GMLP_CHUNK)
    rowb = lambda n: pl.BlockSpec((1, tl, n), lambda b, i: (b, i, 0))
    tab = lambda n: pl.BlockSpec((tl, n), lambda b, i: (i, 0))
    consts = (w_in, q_g, w_qn, w_qp, w_uk, perm, kv_g)
    consts2 = (gm_g, gm_b, ws, bs)
    return pl.pallas_call(
        functools.partial(_odd_in_kernel, tl=tl, cl=cl),
        grid=(bsz, l // tl),
        in_specs=[rowb(D_MODEL)] + [_full_spec(a, 2) for a in consts]
                 + [tab(MLA_HEADS * MLA_ROPE), tab(MLA_HEADS * MLA_ROPE), tab(LANES), tab(LANES)]
                 + [_full_spec(a, 2) for a in consts2],
        out_specs=[pl.BlockSpec((1, MLA_HEADS, tl, MLA_QW), lambda b, i: (b, 0, i, 0)),
                   rowb(MLA_QW), pl.BlockSpec((1, MLA_QW, tl), lambda b, i: (b, 0, i)),
                   rowb(MLA_KV_RANK), rowb(MLA_ROPE), rowb(GMLP_WIDTH), rowb(GMLP_WIDTH)],
        out_shape=[jax.ShapeDtypeStruct((bsz, MLA_HEADS, l, MLA_QW), BF16),
                   jax.ShapeDtypeStruct((bsz, l, MLA_QW), BF16),
                   jax.ShapeDtypeStruct((bsz, MLA_QW, l), BF16),
                   jax.ShapeDtypeStruct((bsz, l, MLA_KV_RANK), F32),
                   jax.ShapeDtypeStruct((bsz, l, MLA_ROPE), F32),
                   jax.ShapeDtypeStruct((bsz, l, GMLP_WIDTH), BF16),
                   jax.ShapeDtypeStruct((bsz, l, GMLP_WIDTH), F32)],
        compiler_params=_cp("parallel", "parallel"),
        name="odd_in",
    )(h3, *consts, cos_q, sin_q, cos_k, sin_k, *consts2)


def _attn_kernel(qi_ref, kj_ref, flag_ref, q_ref, k_ref, kt_ref, wuv_ref, o_ref,
                 m_scr, l_scr, acc_scr, *, tq, hpb, tk, nsub, pos0, n_keys):
    p = pl.program_id(1)
    flag = flag_ref[p]
    nblk = MLA_HEADS // hpb
    width = hpb * tq

    @pl.when((flag & 1) != 0)
    def _():
        m_scr[...] = jnp.full_like(m_scr, -jnp.inf)
        l_scr[...] = jnp.zeros_like(l_scr)
        acc_scr[...] = jnp.zeros_like(acc_scr)

    def step(sub, hidden):
        kt = kt_ref[0, :, sub * tk:(sub + 1) * tk]
        kk = k_ref[0, sub * tk:(sub + 1) * tk, :]
        if hidden:
            qpos = pos0 + qi_ref[p] * tq + lax.broadcasted_iota(I32, (1, width), 1) % tq
            kpos = (kj_ref[p] * nsub + sub) * tk + lax.broadcasted_iota(I32, (tk, 1), 0)
            visible = ((kpos // CHUNK) <= (qpos // CHUNK)) & (kpos < n_keys)
        half = tk // MLA_KEY_PARTS
        for h in range(nblk):
            s = _dot_nt(kk, q_ref[0, h])
            if hidden:
                s = jnp.where(visible, s, -jnp.inf)
            m = m_scr[h]
            acc = acc_scr[h]
            l = l_scr[h]
            for part in range(MLA_KEY_PARTS):
                ks = slice(part * half, (part + 1) * half)
                sp = s[ks]
                m_new = jnp.maximum(m, jnp.max(sp, axis=0, keepdims=True))
                alpha = jnp.exp(m - m_new)
                pr = jnp.exp(sp - m_new).astype(BF16)
                acc = alpha * acc + _dot(kt[:MLA_KV_RANK, ks], pr)
                l = alpha * l + _dot(kt[MLA_ONE_LANE:MLA_ONE_LANE + 16, ks], pr)
                m = m_new
            acc_scr[h] = acc
            l_scr[h] = l
            m_scr[h] = m

    tile_bits = sum(12 << (2 * sub) for sub in range(nsub))
    all_plain_bits = sum(4 << (2 * sub) for sub in range(nsub))
    last_masked_bits = all_plain_bits + (8 << (2 * (nsub - 1)))
    all_plain = (flag & tile_bits) == all_plain_bits
    last_masked = (flag & tile_bits) == last_masked_bits
    fused = all_plain | last_masked

    @pl.when(all_plain)
    def _():
        for sub in range(nsub):
            step(sub, False)

    @pl.when(last_masked)
    def _():
        for sub in range(nsub):
            step(sub, sub == nsub - 1)

    for sub in range(nsub):
        seen = (flag & (4 << (2 * sub))) != 0
        some_hidden = (flag & (8 << (2 * sub))) != 0

        @pl.when(jnp.logical_not(fused) & seen & some_hidden)
        def _():
            step(sub, True)

        @pl.when(jnp.logical_not(fused) & seen & jnp.logical_not(some_hidden))
        def _():
            step(sub, False)

    @pl.when((flag & 2) != 0)
    def _():
        if hpb == 1:
            for pair in range(MLA_HEADS // 2):
                tile = jnp.zeros((tq, LANES), F32)
                for h in (2 * pair, 2 * pair + 1):
                    lat = (acc_scr[h] / l_scr[h, 0:1, :]).astype(BF16)
                    tile = tile + _dot_tn(lat, wuv_ref[h])
                o_ref[0, :, pair * LANES:(pair + 1) * LANES] = tile.astype(BF16)
        else:
            head_of_lane = lax.broadcasted_iota(I32, (1, MLA_HEADS * MLA_V), 1) // MLA_V
            out = jnp.zeros((tq, MLA_HEADS * MLA_V), F32)
            for h in range(nblk):
                lat = (acc_scr[h] / l_scr[h, 0:1, :]).astype(BF16)
                full = _dot_tn(lat, wuv_ref[...])
                for hh in range(hpb):
                    out = out + jnp.where(head_of_lane == h * hpb + hh, full[hh * tq:(hh + 1) * tq], 0.0)
            o_ref[0] = out.astype(BF16)


def _attn_pairs(l, tq, tk, nsub, pos0, n_keys):
    qi, kj, flag = [], [], []
    for i in range(l // tq):
        q_first = pos0 + i * tq
        q_last = q_first + tq - 1
        vis = min(CHUNK * (q_last // CHUNK) + CHUNK - 1, n_keys - 1)
        ntiles = vis // tk + 1
        nsteps = -(-ntiles // nsub)
        for j in range(nsteps):
            f = int(j == 0) + 2 * int(j == nsteps - 1)
            for s in range(nsub):
                t = j * nsub + s
                if t < ntiles:
                    hidden = ((t + 1) * tk - 1) // CHUNK > q_first // CHUNK or (t + 1) * tk > n_keys
                    f += (4 + 8 * int(hidden)) << (2 * s)
            qi.append(i); kj.append(j); flag.append(f)
    return [jnp.asarray(np.array(a, np.int32)) for a in (qi, kj, flag)]


def _attention(q4, kc3, kt3, w_uv, pos0, n_keys, tq, tk, nsub):
    bsz, _, l, _ = q4.shape
    pairs = _attn_pairs(l, tq, tk, nsub, pos0, n_keys)
    npairs = int(pairs[0].shape[0])
    hpb = max(1, min(MLA_HEADS, 256 // tq)) if l == tq else 1
    nblk = MLA_HEADS // hpb
    width = hpb * tq
    q4 = q4.reshape(bsz, nblk, hpb * l, MLA_QW)
    if hpb == 1:
        heads = w_uv.reshape(MLA_KV_RANK, MLA_HEADS, MLA_V).transpose(1, 0, 2)
        w_uv = jnp.stack([jnp.pad(heads[h], ((0, 0), ((h % 2) * MLA_V, (1 - h % 2) * MLA_V)))
                          for h in range(MLA_HEADS)])
    return pl.pallas_call(
        functools.partial(_attn_kernel, tq=tq, hpb=hpb, tk=tk, nsub=nsub, pos0=pos0, n_keys=n_keys),
        grid_spec=pltpu.PrefetchScalarGridSpec(
            num_scalar_prefetch=3,
            grid=(bsz, npairs),
            in_specs=[pl.BlockSpec((1, nblk, width, MLA_QW), lambda b, p, qi, kj, f: (b, 0, qi[p], 0)),
                      pl.BlockSpec((1, nsub * tk, MLA_QW), lambda b, p, qi, kj, f: (b, kj[p], 0)),
                      pl.BlockSpec((1, MLA_QW, nsub * tk), lambda b, p, qi, kj, f: (b, 0, kj[p])),
                      pl.BlockSpec(w_uv.shape, lambda b, p, qi, kj, f: (0,) * w_uv.ndim)],
            out_specs=pl.BlockSpec((1, tq, MLA_HEADS * MLA_V), lambda b, p, qi, kj, f: (b, qi[p], 0)),
            scratch_shapes=[pltpu.VMEM((nblk, 1, width), F32),
                            pltpu.VMEM((nblk, 16, width), F32),
                            pltpu.VMEM((nblk, MLA_KV_RANK, width), F32)]),
        out_shape=jax.ShapeDtypeStruct((bsz, l, MLA_HEADS * MLA_V), BF16),
        compiler_params=_cp("parallel", "arbitrary"),
        name="mla_attention",
    )(*pairs, q4, kc3, kt3, w_uv)


def _rope_tables(pos0, l, width):
    half = MLA_ROPE // 2
    inv = ROPE_THETA ** (-jnp.arange(half, dtype=F32) * 2.0 / MLA_ROPE)
    ang = (pos0 + jnp.arange(l)).astype(F32)[:, None] * inv[None, :]
    cos, sin = jnp.cos(ang), jnp.sin(ang)
    cos32 = jnp.concatenate([cos, cos], axis=1)
    sin32 = jnp.concatenate([-sin, sin], axis=1)
    reps = width // MLA_ROPE
    return jnp.tile(cos32, (1, reps)), jnp.tile(sin32, (1, reps))


def _prep_even(w_in, w_gate_up, b_gate, gla_norm_g, pool_w, pool_scale, w_out):
    o_r = 2 * GLA_QK + GLA_V
    o_g = o_r + GLA_V
    o_p = o_g + GLA_GATE_RANK
    w_main = jnp.concatenate([w_in[:, :o_g], w_in[:, o_p:]], axis=1).astype(BF16)
    w_g = jnp.pad(w_in[:, o_g:o_p], ((0, 0), (0, LANES - GLA_GATE_RANK))).astype(BF16)
    w_gu = jnp.pad(w_gate_up, ((0, LANES - GLA_GATE_RANK), (0, 0))).astype(BF16)
    return dict(w_main=w_main, w_g=w_g, w_gu=w_gu, b_g=b_gate.reshape(1, -1),
                gnorm=gla_norm_g.reshape(1, -1), pool_w=pool_w.astype(BF16),
                pool_scale=pool_scale.reshape(1, -1), w_out=w_out.astype(BF16))


def _prep_odd(w_in, q_norm_g, w_uq, kv_norm_g, w_uk, w_uv, gm_g, gm_b, gm_ws, gm_bs, w_out):
    o_ckv = MLA_Q_RANK
    o_kpe = o_ckv + MLA_KV_RANK
    o_u = o_kpe + MLA_ROPE
    w_in2 = jnp.concatenate([w_in[:, :o_kpe], w_in[:, o_u:], w_in[:, o_kpe:o_u],
                             jnp.zeros((D_MODEL, LANES - MLA_ROPE), F32)], axis=1).astype(BF16)
    uq = w_uq.reshape(MLA_Q_RANK, MLA_HEADS, MLA_NOPE + MLA_ROPE)
    w_qn = jnp.pad(uq[:, :, :MLA_NOPE], ((0, 0), (0, 0), (0, LANES - MLA_NOPE)))
    w_qn = w_qn.reshape(MLA_Q_RANK, MLA_HEADS * LANES).astype(BF16)
    w_qp = uq[:, :, MLA_NOPE:].reshape(MLA_Q_RANK, MLA_HEADS * MLA_ROPE).astype(BF16)
    uk = w_uk.reshape(MLA_KV_RANK, MLA_HEADS, MLA_NOPE).transpose(1, 2, 0)
    w_ukp = jnp.pad(uk, ((0, 0), (0, LANES - MLA_NOPE), (0, 0))).astype(BF16)
    src = np.arange(MLA_HEADS * MLA_ROPE)
    perm = np.zeros((MLA_HEADS * MLA_ROPE, MLA_HEADS * LANES), np.float32)
    perm[src, (src // MLA_ROPE) * LANES + src % MLA_ROPE] = 1.0
    return dict(w_in=w_in2, q_g=q_norm_g.reshape(1, -1), w_qn=w_qn, w_qp=w_qp, w_uk=w_ukp,
                perm=jnp.asarray(perm, BF16), kv_g=kv_norm_g.reshape(1, -1), w_uv=w_uv.astype(BF16),
                gm_g=gm_g.reshape(1, -1), gm_b=gm_b.reshape(1, -1), gm_ws=gm_ws, gm_bs=gm_bs,
                w_out=w_out.astype(BF16))


def _prep_route(wg, bg, we, be):
    w_r = jnp.pad(jnp.concatenate([wg, we], axis=1), ((0, 0), (0, LANES - N_GROUPS - N_EXPERTS)))
    w_hi = w_r.astype(BF16)
    w_lo = (w_r - w_hi.astype(F32)).astype(BF16)
    b_r = jnp.pad(jnp.concatenate([bg, be]), (0, LANES - N_GROUPS - N_EXPERTS)).reshape(1, LANES)
    return jnp.concatenate([w_hi, w_lo], axis=1), b_r


def _even_mixer(h3, st0, hist, pos0, pw):
    bsz, l, _ = h3.shape
    q, k, la, v, r, xp = _even_in(h3.reshape(bsz * l, D_MODEL), pw['w_main'], pw['w_g'], pw['w_gu'], pw['b_g'])
    to3 = lambda a: a.reshape(bsz, l, a.shape[-1])
    o, st = _gla(to3(q), to3(k), to3(la), to3(v), to3(r), st0, pw['gnorm'])
    xp3 = to3(xp)
    hist16 = jnp.pad(hist, ((0, 0), (POOL_HALO - POOL_HIST, 0), (0, 0)))
    pooled = _pool(xp3, hist16, pw['pool_w'], pw['pool_scale'], pos0)
    hist_new = jnp.concatenate([hist, xp3[:, -POOL_HIST:].astype(F32)], axis=1)[:, -POOL_HIST:]
    return o, pooled, st, hist_new


def _odd_mixer(h3, ckv_past, kpe_past, pw):
    bsz, l, _ = h3.shape
    n_past = ckv_past.shape[1]
    cos_q, sin_q = _rope_tables(n_past, l, MLA_HEADS * MLA_ROPE)
    cos_k, sin_k = _rope_tables(n_past, l, MLA_ROPE)
    padk = ((0, 0), (0, LANES - MLA_ROPE))
    cos_k, sin_k = jnp.pad(cos_k, padk), jnp.pad(sin_k, padk)
    cl = min(l, GMLP_CHUNK)
    ws = jnp.tril(pw['gm_ws'][:, :cl, :cl]).astype(BF16)
    bs = jnp.repeat(pw['gm_bs'][:, :cl].T, GMLP_CH, axis=1)
    q4, kc, kt, ckv, kpe, gated, vn = _odd_in(h3, pw['w_in'], pw['q_g'], pw['w_qn'], pw['w_qp'], pw['w_uk'],
                                              pw['perm'], pw['kv_g'], cos_q, sin_q, cos_k, sin_k,
                                              pw['gm_g'], pw['gm_b'], ws, bs)
    n_keys = n_past + l
    tq, tk, nsub = min(l, 256), 512, 2
    span = tk * nsub
    if n_past:
        past = jnp.concatenate([ckv_past, kpe_past, jnp.ones((bsz, n_past, 1), F32),
                                jnp.zeros((bsz, n_past, MLA_QW - MLA_ONE_LANE - 1), F32)], axis=2).astype(BF16)
        kc = jnp.concatenate([past, kc], axis=1)
        kt = jnp.concatenate([past.transpose(0, 2, 1), kt], axis=2)
    kc = jnp.pad(kc, ((0, 0), (0, -n_keys % span), (0, 0)))
    kt = jnp.pad(kt, ((0, 0), (0, 0), (0, -n_keys % span)))
    attn = _attention(q4, kc, kt, pw['w_uv'], n_past, n_keys, tq, tk, nsub)
    return attn, gated, ckv, kpe, vn


def _finish_layer(a3, b3, h3, w_out, lw):
    bsz, l, _ = h3.shape
    t = bsz * l
    h1, xs, rs, rg, n16 = _out_route(a3.reshape(t, -1), b3.reshape(t, -1), h3.reshape(t, D_MODEL), w_out,
                                     lw['ln_mix_g'], lw['ln_mix_b'], lw['w_r'], lw['b_r'])
    h2 = _moe(h1, xs, rs, rg, n16, lw['w1'], lw['w3'], lw['w2'], lw['layer'], lw['ln_ffn_g'], lw['ln_ffn_b'],
              lw['out_dtype'])
    return h2.reshape(bsz, l, D_MODEL)


def kernel(x_prompt, x_sample, state_gla, state_pool, cache_mla_ckv, cache_mla_kpe, w_in_even, w_gate_up, b_gate, gla_norm_g, pool_w, pool_scale, w_out_even, w_in_odd, mla_q_norm_g, mla_w_uq, mla_kv_norm_g, mla_w_uk, mla_w_uv, gmlp_norm_g, gmlp_norm_b, gmlp_ws, gmlp_bs, w_out_odd, ln_mix_g, ln_mix_b, router_group_w, router_group_b, router_expert_w, router_expert_b, expert_w1, expert_w3, expert_w2, ln_ffn_g, ln_ffn_b):
    hp, hs = x_prompt, x_sample
    bp = hp.shape[0]
    past_len = cache_mla_ckv.shape[2]
    gla_p, gla_s, pool_p, pool_s = [], [], [], []
    ckv_p, ckv_s, kpe_p, kpe_s, gv_s = [], [], [], [], []

    def state_to_t(s):
        return s.transpose(0, 3, 1, 2).reshape(s.shape[0], GLA_DV, GLA_QK)

    def state_from_t(st):
        return st.reshape(st.shape[0], GLA_DV, GLA_HEADS, GLA_DK).transpose(0, 2, 3, 1)

    for layer in range(DEPTH):
        i = layer // 2
        w_r, b_r = _prep_route(router_group_w[layer], router_group_b[layer],
                               router_expert_w[layer], router_expert_b[layer])
        lw = dict(ln_mix_g=ln_mix_g[layer].reshape(1, -1), ln_mix_b=ln_mix_b[layer].reshape(1, -1),
                  ln_ffn_g=ln_ffn_g[layer].reshape(1, -1), ln_ffn_b=ln_ffn_b[layer].reshape(1, -1),
                  w_r=w_r, b_r=b_r, w1=expert_w1, w3=expert_w3, w2=expert_w2, layer=layer,
                  out_dtype=F32 if layer == DEPTH - 1 else BF16)
        if layer % 2 == 0:
            pw = _prep_even(w_in_even[i], w_gate_up[i], b_gate[i], gla_norm_g[i], pool_w[i], pool_scale[i],
                            w_out_even[i])
            st0 = jnp.zeros((bp, GLA_DV, GLA_QK), F32)
            hist0 = jnp.zeros((bp, POOL_HIST, POOL_WIDTH), F32)
            op, pp, stp, histp = _even_mixer(hp, st0, hist0, 0, pw)
            os_, ps, sts, hists = _even_mixer(hs, state_to_t(state_gla[i]), state_pool[i], past_len, pw)
            gla_p.append(state_from_t(stp)); gla_s.append(state_from_t(sts))
            pool_p.append(histp); pool_s.append(hists)
            ap, bpj, as_, bsj = op, pp, os_, ps
        else:
            pw = _prep_odd(w_in_odd[i], mla_q_norm_g[i], mla_w_uq[i], mla_kv_norm_g[i], mla_w_uk[i], mla_w_uv[i],
                           gmlp_norm_g[i], gmlp_norm_b[i], gmlp_ws[i], gmlp_bs[i], w_out_odd[i])
            no_ckv = jnp.zeros((bp, 0, MLA_KV_RANK), F32)
            no_kpe = jnp.zeros((bp, 0, MLA_ROPE), F32)
            ap, bpj, cp, kp, _ = _odd_mixer(hp, no_ckv, no_kpe, pw)
            as_, bsj, cs, ks, vs = _odd_mixer(hs, cache_mla_ckv[i], cache_mla_kpe[i], pw)
            ckv_p.append(cp); ckv_s.append(cs); kpe_p.append(kp); kpe_s.append(ks); gv_s.append(vs)
        hp = _finish_layer(ap, bpj, hp, pw['w_out'], lw)
        hs = _finish_layer(as_, bsj, hs, pw['w_out'], lw)
    return (hp, hs, jnp.stack(gla_p), jnp.stack(gla_s), jnp.stack(pool_p), jnp.stack(pool_s),
            jnp.stack(ckv_p), jnp.stack(ckv_s), jnp.stack(kpe_p), jnp.stack(kpe_s), jnp.stack(gv_s))
```

```python
import functools

import numpy as np
import jax
import jax.numpy as jnp
from jax import lax
from jax.experimental import pallas as pl
from jax.experimental.pallas import tpu as pltpu

F32 = jnp.float32
BF16 = jnp.bfloat16
I32 = jnp.int32

D_MODEL = 1024
DEPTH = 2
CHUNK = 64
ALPHA = (2 * DEPTH) ** 0.25
LN_EPS = 1e-5

GLA_HEADS = 4
GLA_DV = 128
GLA_DK = 64
GLA_QK = GLA_HEADS * GLA_DK
GLA_V = GLA_HEADS * GLA_DV
GLA_GATE_RANK = 16
GLA_GATE_TAU = 16.0
GLA_SUB = 8

POOL_WIDTH = 512
POOL_CH = 128
POOL_WINDOWS = (2, 4, 8, 16)
POOL_HIST = 15
POOL_HALO = 16

MLA_HEADS = 8
MLA_NOPE = 64
MLA_ROPE = 32
MLA_V = 64
MLA_Q_RANK = 256
MLA_KV_RANK = 128
ROPE_THETA = 10000.0
MLA_SCALE = (MLA_NOPE + MLA_ROPE) ** -0.5
MLA_QW = 256
MLA_ONE_LANE = MLA_KV_RANK + MLA_ROPE
MLA_KEY_PARTS = 2
GMLP_WIDTH = 512
GMLP_CH = 128
GMLP_GROUPS = 4
GMLP_CHUNK = 128

N_GROUPS = 4
EXPERTS_PER_GROUP = 8
N_EXPERTS = 32
TOP_K = 2
D_EXPERT = 256
MOE_ROWS = 512
MOE_CHUNK = 16
MOE_BLOCK_CHUNKS = MOE_ROWS // MOE_CHUNK

LANES = 128
VMEM_LIMIT = 48 * 1024 * 1024


def _cp(*sem):
    return pltpu.CompilerParams(dimension_semantics=sem, vmem_limit_bytes=VMEM_LIMIT)


def _dot(a, b):
    return jnp.dot(a, b, preferred_element_type=F32)


def _dot_nt(a, b):
    return lax.dot_general(a, b, (((1,), (1,)), ((), ())), preferred_element_type=F32)


def _dot_tn(a, b):
    return lax.dot_general(a, b, (((0,), (0,)), ((), ())), preferred_element_type=F32)


def _split3(x):
    hi = x.astype(BF16)
    r1 = x - hi.astype(F32)
    mid = r1.astype(BF16)
    lo = (r1 - mid.astype(F32)).astype(BF16)
    return hi, mid, lo


def _layernorm(x, g, b):
    mu = jnp.mean(x, axis=-1, keepdims=True)
    xc = x - mu
    var = jnp.mean(xc * xc, axis=-1, keepdims=True)
    return xc * lax.rsqrt(var + LN_EPS) * g + b


def _gelu(x):
    return 0.5 * x * (1.0 + jnp.tanh(0.7978845608028654 * (x + 0.044715 * (x * x * x))))


def _sigmoid(x):
    return 1.0 / (1.0 + jnp.exp(-x))


def _full_spec(a, nargs):
    nd = a.ndim
    if nargs == 1:
        return pl.BlockSpec(a.shape, lambda i: (0,) * nd)
    return pl.BlockSpec(a.shape, lambda i, j: (0,) * nd)


def _even_in_kernel(x_ref, w_ref, wg_ref, wgu_ref, bg_ref, hist_ref, pw_ref, ps_ref,
                    q_ref, k_ref, la_ref, v_ref, r_ref, xp_ref, pool_ref, prev_scr, *, tm, tps, pos0):
    xb = x_ref[...].astype(BF16)
    z = _dot(xb, w_ref[...])
    q_ref[...] = z[:, 0:GLA_QK] * (GLA_DK ** -0.5)
    k_ref[...] = z[:, GLA_QK:2 * GLA_QK]
    v_ref[...] = z[:, 2 * GLA_QK:2 * GLA_QK + GLA_V]
    r_ref[...] = z[:, 2 * GLA_QK + GLA_V:2 * GLA_QK + 2 * GLA_V]
    xpb = z[:, 2 * GLA_QK + 2 * GLA_V:].astype(BF16)
    xp_ref[...] = xpb
    g = _dot(xb, wg_ref[...])
    pre = _dot(g.astype(BF16), wgu_ref[...]) + bg_ref[...]
    logsig = jnp.minimum(pre, 0.0) - jnp.log(1.0 + jnp.exp(-jnp.abs(pre)))
    la_ref[...] = logsig * (1.0 / GLA_GATE_TAU)

    seq_tile = pl.program_id(0) % tps
    xp = xpb.astype(F32)
    prev = jnp.where(seq_tile == 0, hist_ref[0], prev_scr[...])
    pool_ref[...] = _pool_tile(xp, prev, pos0 + seq_tile * tm, pw_ref, ps_ref, tm).astype(BF16)
    prev_scr[...] = xp[tm - POOL_HALO:, :]


def _even_in(x2, seq_len, hist16, pos0, w_main, w_g, w_gu, b_g, pool_w, pool_scale):
    t = x2.shape[0]
    tm = min(512, seq_len)
    tps = seq_len // tm
    row = lambda n: pl.BlockSpec((tm, n), lambda i: (i, 0))
    widths = (GLA_QK, GLA_QK, GLA_QK, GLA_V, GLA_V, POOL_WIDTH, POOL_WIDTH)
    return pl.pallas_call(
        functools.partial(_even_in_kernel, tm=tm, tps=tps, pos0=pos0),
        grid=(t // tm,),
        in_specs=[row(D_MODEL)] + [_full_spec(a, 1) for a in (w_main, w_g, w_gu, b_g)]
                 + [pl.BlockSpec((1, POOL_HALO, POOL_WIDTH), lambda i: (i // tps, 0, 0)),
                    _full_spec(pool_w, 1), _full_spec(pool_scale, 1)],
        out_specs=[row(n) for n in widths],
        out_shape=[jax.ShapeDtypeStruct((t, n), BF16 if i >= 5 else F32) for i, n in enumerate(widths)],
        scratch_shapes=[pltpu.VMEM((POOL_HALO, POOL_WIDTH), F32)],
        compiler_params=_cp("arbitrary"),
        name="even_in",
    )(x2, w_main, w_g, w_gu, b_g, hist16, pool_w, pool_scale)


def _gla_consts(c):
    n = c * GLA_SUB
    tri = np.tril(np.ones((c, c), np.float32))
    headsum = (np.arange(GLA_QK)[:, None] // GLA_DK == np.arange(GLA_V)[None, :] // GLA_DV).astype(np.float32)
    msel = (np.arange(n)[None, :] // GLA_SUB == np.arange(c)[:, None]).astype(np.float32)
    return [jnp.asarray(a, BF16) for a in (tri, headsum, msel)]


def _gla_chunk(q, k, la, v, st, c, tri, headsum, msel):
    lane = lax.broadcasted_iota(I32, (1, GLA_QK), 1)
    head_of_lane = lane // GLA_DK
    row = lax.broadcasted_iota(I32, (c, 1), 0)
    ii = lax.broadcasted_iota(I32, (c, c), 0)
    jj = lax.broadcasted_iota(I32, (c, c), 1)

    hi, mid, lo = _split3(la)
    b = _dot(tri, hi) + _dot(tri, mid) + _dot(tri, lo)

    head_masks = [head_of_lane == h for h in range(GLA_HEADS)]
    vb = v.astype(BF16)

    a_off = [jnp.zeros((c, c), F32) for _ in range(GLA_HEADS)]
    s = c // 2
    while s >= GLA_SUB:
        nblk = c // (2 * s)
        blk = row // (2 * s)
        right = ((row // s) % 2) == 1
        bref = jnp.zeros((c, GLA_QK), F32)
        for m in range(nblk):
            r0 = m * 2 * s + s - 1
            bref = jnp.where(blk == m, b[r0:r0 + 1, :], bref)
        qe = jnp.where(right, q * jnp.exp(jnp.minimum(b - bref, 0.0)), 0.0)
        ke = jnp.where(right, 0.0, k * jnp.exp(jnp.minimum(bref - b, 0.0))).astype(BF16)
        same = (ii // (2 * s)) == (jj // (2 * s))
        for h in range(GLA_HEADS):
            a = _dot_nt(jnp.where(head_masks[h], qe, 0.0).astype(BF16), ke)
            a_off[h] = a_off[h] + (a if nblk == 1 else jnp.where(same, a, 0.0))
        s //= 2

    nsb = c // GLA_SUB
    parts = []
    for i in range(nsb):
        sl = slice(i * GLA_SUB, (i + 1) * GLA_SUB)
        bi, qi, ki = b[sl], q[sl], k[sl]
        diff = bi[:, None, :] - bi[None, :, :]
        p = qi[:, None, :] * ki[None, :, :] * jnp.exp(jnp.minimum(diff, 0.0))
        parts.append(p.reshape(GLA_SUB * GLA_SUB, GLA_QK))
    n = nsb * GLA_SUB * GLA_SUB
    idx = lax.broadcasted_iota(I32, (n, 1), 0)
    causal = (idx % GLA_SUB) <= ((idx // GLA_SUB) % GLA_SUB)
    pcat = jnp.where(causal, jnp.concatenate(parts, axis=0), 0.0).astype(BF16)
    rsum = _dot(pcat, headsum)

    qb = q * jnp.exp(b)
    b_end = b[c - 1:c, :]
    kd = (k * jnp.exp(b_end - b)).astype(BF16)
    stb = st.astype(BF16)
    st_new = st * jnp.exp(b_end)

    outs = []
    for h in range(GLA_HEADS):
        vh = vb[:, h * GLA_DV:(h + 1) * GLA_DV]
        vf = v[:, h * GLA_DV:(h + 1) * GLA_DV]
        vt = jnp.concatenate(
            [jnp.broadcast_to(vf[i * GLA_SUB:(i + 1) * GLA_SUB][None], (GLA_SUB, GLA_SUB, GLA_DV))
             .reshape(GLA_SUB * GLA_SUB, GLA_DV) for i in range(nsb)], axis=0)
        xh = rsum[:, h * GLA_DV:(h + 1) * GLA_DV] * vt
        o = _dot(msel, xh.astype(BF16))
        o = o + _dot(a_off[h].astype(BF16), vh)
        o = o + _dot_nt(jnp.where(head_masks[h], qb, 0.0).astype(BF16), stb)
        outs.append(o)
        st_new = st_new + jnp.where(head_masks[h], _dot_tn(vh, kd), 0.0)
    return outs, st_new


def _gla_kernel(q_ref, k_ref, la_ref, v_ref, r_ref, st0_ref, g_ref, tri_ref, hs_ref, ms_ref,
                o_ref, st_ref, st_scr, *, c, nchunks):
    @pl.when(pl.program_id(1) == 0)
    def _():
        st_scr[...] = st0_ref[0]

    def body(ci, carry):
        r0 = pl.multiple_of(ci * c, c)
        rows = pl.ds(r0, c)
        outs, st_new = _gla_chunk(q_ref[0, rows, :], k_ref[0, rows, :], la_ref[0, rows, :], v_ref[0, rows, :],
                                  st_scr[...], c, tri_ref[...], hs_ref[...], ms_ref[...])
        st_scr[...] = st_new
        r = r_ref[0, rows, :]
        g = g_ref[...]
        for h in range(GLA_HEADS):
            o = outs[h]
            sl = slice(h * GLA_DV, (h + 1) * GLA_DV)
            on = o * lax.rsqrt(jnp.mean(o * o, axis=-1, keepdims=True) + LN_EPS) * g
            rh = r[:, sl]
            o_ref[0, rows, sl] = (on * (rh * _sigmoid(rh))).astype(BF16)
        return carry

    lax.fori_loop(0, nchunks, body, 0, unroll=8 if nchunks % 8 == 0 else 1)
    st_ref[0] = st_scr[...]


def _gla(q3, k3, la3, v3, r3, st0, gnorm):
    bsz, l, _ = q3.shape
    c = min(l, CHUNK)
    tl = min(l, 512)
    blk = lambda n: pl.BlockSpec((1, tl, n), lambda b, i: (b, i, 0))
    st_spec = pl.BlockSpec((1, GLA_DV, GLA_QK), lambda b, i: (b, 0, 0))
    consts = _gla_consts(c)
    return pl.pallas_call(
        functools.partial(_gla_kernel, c=c, nchunks=tl // c),
        grid=(bsz, l // tl),
        in_specs=[blk(GLA_QK), blk(GLA_QK), blk(GLA_QK), blk(GLA_V), blk(GLA_V), st_spec,
                  _full_spec(gnorm, 2)] + [_full_spec(a, 2) for a in consts],
        out_specs=[blk(GLA_V), st_spec],
        out_shape=[jax.ShapeDtypeStruct((bsz, l, GLA_V), BF16),
                   jax.ShapeDtypeStruct((bsz, GLA_DV, GLA_QK), F32)],
        scratch_shapes=[pltpu.VMEM((GLA_DV, GLA_QK), F32)],
        compiler_params=_cp("parallel", "arbitrary"),
        name="gla",
    )(q3, k3, la3, v3, r3, st0, gnorm, *consts)


def _pool_tile(x, prev, pos_first, w_ref, scale_ref, tl):
    e = jnp.concatenate([prev, x], axis=0)
    pos = pos_first + lax.broadcasted_iota(I32, (tl, 1), 0)
    sums = []
    shift = 1
    for g, w in enumerate(POOL_WINDOWS):
        e = e[:, POOL_CH:] if g > 0 else e
        while shift < w:
            e = e[shift:] + e[:-shift]
            shift *= 2
        off = POOL_HALO - (w - 1)
        sums.append(e[off:off + tl, :POOL_CH])
    outs = []
    for g, w in enumerate(POOL_WINDOWS):
        cnt = jnp.minimum(pos + 1, w).astype(F32)
        mix = sums[g] / cnt - x[:, g * POOL_CH:(g + 1) * POOL_CH]
        outs.append(_dot(mix.astype(BF16), w_ref[g]))
    return jnp.concatenate(outs, axis=1) * scale_ref[...]


def _out_route_kernel(a_ref, b_ref, h_ref, w_ref, g_ref, bt_ref, wr_ref, br_ref,
                      h1_ref, xs_ref, rs_ref, rg_ref, n16_ref, *, tm, slots):
    half = w_ref.shape[0] // 2
    y = _dot(a_ref[...].astype(BF16), w_ref[0:half, :]) + _dot(b_ref[...].astype(BF16), w_ref[half:, :])
    x = _layernorm(ALPHA * h_ref[...].astype(F32) + y, g_ref[...], bt_ref[...])
    h1_ref[...] = x.astype(BF16)

    xh = x.astype(BF16)
    xl = (x - xh.astype(F32)).astype(BF16)
    hl = _dot(xh, wr_ref[...])
    logits = hl[:, :LANES] + hl[:, LANES:] + _dot(xl, wr_ref[:, 0:LANES]) + br_ref[...]

    lane = lax.broadcasted_iota(I32, (tm, LANES), 1)
    lanef = lane.astype(F32)
    neg = -jnp.inf
    big = jnp.float32(1 << 20)

    def first_lane(hit):
        return jnp.min(jnp.where(hit, lanef, big), axis=-1, keepdims=True).astype(I32)

    gl = jnp.where(lane < N_GROUPS, logits, neg)
    gmax = jnp.max(gl, axis=-1, keepdims=True)
    g_sel = first_lane(gl == gmax)
    g_prob = 1.0 / jnp.sum(jnp.exp(gl - gmax), axis=-1, keepdims=True)
    eidx = lane - N_GROUPS
    in_grp = (eidx >= 0) & (eidx < N_EXPERTS) & ((eidx // EXPERTS_PER_GROUP) == g_sel)
    el = jnp.where(in_grp, logits, neg)
    v1 = jnp.max(el, axis=-1, keepdims=True)
    i1 = first_lane(el == v1)
    el2 = jnp.where(lane == i1, neg, el)
    v2 = jnp.max(el2, axis=-1, keepdims=True)
    i2 = first_lane(el2 == v2)
    e21 = jnp.exp(v2 - v1)
    gate1 = g_prob / (1.0 + e21)
    gate2 = g_prob * e21 / (1.0 + e21)
    e1 = i1 - N_GROUPS
    e2 = i2 - N_GROUPS

    oh1 = lane == e1
    oh2 = lane == e2
    oh = oh1.astype(F32) + oh2.astype(F32)
    ti = lax.broadcasted_iota(I32, (tm, tm), 0)
    tj = lax.broadcasted_iota(I32, (tm, tm), 1)
    before = _dot((tj < ti).astype(BF16), oh.astype(BF16))
    cnt = jnp.sum(oh, axis=0, keepdims=True)
    n16 = jnp.floor((cnt + (MOE_CHUNK - 1)) * (1.0 / MOE_CHUNK))
    n16_8 = jnp.broadcast_to(n16, (8, LANES))
    ui = lax.broadcasted_iota(I32, (LANES, LANES), 0)
    uj = lax.broadcasted_iota(I32, (LANES, LANES), 1)
    run_start = _dot(n16_8.astype(BF16), (ui < uj).astype(BF16))[0:1]
    slot_of = MOE_CHUNK * run_start + before
    slot1 = jnp.sum(jnp.where(oh1, slot_of, 0.0), axis=-1, keepdims=True).astype(I32)
    slot2 = jnp.sum(jnp.where(oh2, slot_of, 0.0), axis=-1, keepdims=True).astype(I32)
    sl = lax.broadcasted_iota(I32, (tm, slots), 1)
    place = ((sl == slot1) | (sl == slot2)).astype(BF16)
    xs_ref[...] = _dot_tn(place, xh).astype(BF16)

    rs_ref[...] = jnp.where(lane == 0, slot1, jnp.where(lane == 1, slot2, 0))
    rg_ref[...] = jnp.where(lane == 0, gate1, jnp.where(lane == 1, gate2, 0.0))
    n16_ref[...] = n16_8.astype(I32)


def _moe_slots(tm):
    worst = tm * TOP_K + N_EXPERTS * (MOE_CHUNK - 1)
    return -(-worst // MOE_ROWS) * MOE_ROWS


def _out_route(a2, b2, h2, w_out, ln_g, ln_b, w_r, b_r):
    t = h2.shape[0]
    tm = min(512, t)
    nt = t // tm
    slots = _moe_slots(tm)
    row = lambda n: pl.BlockSpec((tm, n), lambda i: (i, 0))
    return pl.pallas_call(
        functools.partial(_out_route_kernel, tm=tm, slots=slots),
        grid=(nt,),
        in_specs=[row(a2.shape[1]), row(b2.shape[1]), row(D_MODEL)]
                 + [_full_spec(a, 1) for a in (w_out, ln_g, ln_b, w_r, b_r)],
        out_specs=[row(D_MODEL), pl.BlockSpec((slots, D_MODEL), lambda i: (i, 0)), row(LANES), row(LANES),
                   pl.BlockSpec((8, LANES), lambda i: (i, 0))],
        out_shape=[jax.ShapeDtypeStruct((t, D_MODEL), BF16), jax.ShapeDtypeStruct((nt * slots, D_MODEL), BF16),
                   jax.ShapeDtypeStruct((t, LANES), I32), jax.ShapeDtypeStruct((t, LANES), F32),
                   jax.ShapeDtypeStruct((nt * 8, LANES), I32)],
        compiler_params=_cp("parallel"),
        name="out_route",
    )(a2, b2, h2, w_out, ln_g, ln_b, w_r, b_r)


def _chunk_rows(chunk):
    return pl.ds(pl.multiple_of(chunk * MOE_CHUNK, MOE_CHUNK), MOE_CHUNK)


def _expert_kernel(src_ref, nreal_ref, be_ref, nu_ref, xs_ref, w1_ref, w3_ref, w2_ref, ys_ref,
                   xbuf, ybuf, gsem, ssem, *, cpb):
    del be_ref
    b = pl.program_id(0)
    nu = nu_ref[0]

    rows = cpb * MOE_CHUNK

    def gather(blk, slot, j):
        return pltpu.make_async_copy(xs_ref.at[_chunk_rows(src_ref[blk * cpb + j]), :],
                                     xbuf.at[slot, pl.ds(j * MOE_CHUNK, MOE_CHUNK), :], gsem.at[slot])

    def scatter(blk, slot, j):
        rows_j = pl.ds(j * MOE_CHUNK, MOE_CHUNK) if isinstance(j, int) else _chunk_rows(j)
        return pltpu.make_async_copy(ybuf.at[slot, rows_j, :],
                                     ys_ref.at[_chunk_rows(src_ref[blk * cpb + j]), :],
                                     ssem.at[slot])

    def start_gather(blk, slot):
        for j in range(cpb):
            gather(blk, slot, j).start()

    def wait_gather(slot):
        pltpu.make_async_copy(xs_ref.at[pl.ds(0, rows), :], xbuf.at[slot], gsem.at[slot]).wait()

    def for_real_chunks(blk, fn):
        def body(j, carry):
            fn(j)
            return carry
        lax.fori_loop(0, nreal_ref[blk], body, 0)

    def wait_scatter(blk, slot):
        full = nreal_ref[blk] == cpb

        @pl.when(full)
        def _():
            pltpu.make_async_copy(ybuf.at[slot], ys_ref.at[pl.ds(0, rows), :], ssem.at[slot]).wait()

        @pl.when(jnp.logical_not(full))
        def _():
            for_real_chunks(blk, lambda j: scatter(blk, slot, j).wait())

    @pl.when(b < nu)
    def _():
        slot = b % 2

        @pl.when(b == 0)
        def _():
            start_gather(b, slot)

        @pl.when(b + 1 < nu)
        def _():
            start_gather(b + 1, 1 - slot)

        wait_gather(slot)

        @pl.when(b >= 2)
        def _():
            wait_scatter(b - 2, slot)

        xb = xbuf[slot]
        a = _dot(xb, w1_ref[0, 0].astype(BF16))
        hid = a * _sigmoid(a) * _dot(xb, w3_ref[0, 0].astype(BF16))
        ybuf[slot] = _dot(hid.astype(BF16), w2_ref[0, 0].astype(BF16)).astype(BF16)
        full = nreal_ref[b] == cpb

        @pl.when(full)
        def _():
            for j in range(cpb):
                scatter(b, slot, j).start()

        @pl.when(jnp.logical_not(full))
        def _():
            for_real_chunks(b, lambda j: scatter(b, slot, j).start())

        @pl.when(b == nu - 1)
        def _():
            wait_scatter(b, slot)

            @pl.when(b >= 1)
            def _():
                wait_scatter(b - 1, 1 - slot)


def _experts(xs, src, nreal, block_e, n_used, w1, w3, w2, layer):
    nblk = nreal.shape[0]
    cpb = src.shape[0] // nblk
    rows = cpb * MOE_CHUNK
    wspec = lambda shape: pl.BlockSpec((1,) + shape, lambda i, src, nr, be, nu: (layer, be[i], 0, 0))
    return pl.pallas_call(
        functools.partial(_expert_kernel, cpb=cpb),
        grid_spec=pltpu.PrefetchScalarGridSpec(
            num_scalar_prefetch=4,
            grid=(nblk,),
            in_specs=[pl.BlockSpec(memory_space=pl.ANY),
                      wspec((1, D_MODEL, D_EXPERT)), wspec((1, D_MODEL, D_EXPERT)), wspec((1, D_EXPERT, D_MODEL))],
            out_specs=pl.BlockSpec(memory_space=pl.ANY),
            scratch_shapes=[pltpu.VMEM((2, rows, D_MODEL), BF16), pltpu.VMEM((2, rows, D_MODEL), BF16),
                            pltpu.SemaphoreType.DMA((2,)), pltpu.SemaphoreType.DMA((2,))]),
        out_shape=jax.ShapeDtypeStruct(xs.shape, xs.dtype),
        input_output_aliases={4: 0},
        compiler_params=_cp("arbitrary"),
        name="moe_experts",
    )(src, nreal, block_e, n_used, xs, w1, w3, w2)


def _combine_kernel(h_ref, rs_ref, rg_ref, ys_ref, g_ref, b_ref, o_ref, *, tm, slots):
    rs = rs_ref[...]
    rg = rg_ref[...]
    sl = lax.broadcasted_iota(I32, (tm, slots), 1).astype(jnp.int16)
    s1, s2 = rs[:, 0:1].astype(jnp.int16), rs[:, 1:2].astype(jnp.int16)
    g1, g2 = rg[:, 0:1].astype(BF16), rg[:, 1:2].astype(BF16)
    weight = jnp.where(sl == s1, g1, jnp.where(sl == s2, g2, jnp.zeros((), BF16)))
    y = _dot(weight, ys_ref[...])
    o_ref[...] = _layernorm(ALPHA * h_ref[...].astype(F32) + y, g_ref[...], b_ref[...]).astype(o_ref.dtype)


def _combine(h2, rs, rg, ys, ln_g, ln_b, out_dtype):
    t = h2.shape[0]
    tm = min(512, t)
    slots = ys.shape[0] // (t // tm)
    row = lambda n: pl.BlockSpec((tm, n), lambda i: (i, 0))
    return pl.pallas_call(
        functools.partial(_combine_kernel, tm=tm, slots=slots),
        grid=(t // tm,),
        in_specs=[row(D_MODEL), row(LANES), row(LANES), pl.BlockSpec((slots, D_MODEL), lambda i: (i, 0)),
                  _full_spec(ln_g, 1), _full_spec(ln_b, 1)],
        out_specs=row(D_MODEL),
        out_shape=jax.ShapeDtypeStruct((t, D_MODEL), out_dtype),
        compiler_params=_cp("parallel"),
        name="moe_combine",
    )(h2, rs, rg, ys, ln_g, ln_b)


def _moe(h1, xs, rs, rg, n16_rows, w1, w3, w2, layer, ln_g, ln_b, out_dtype):
    t = h1.shape[0]
    tm = min(512, t)
    nt = t // tm
    slots = xs.shape[0] // nt
    n16 = n16_rows[::8, :N_EXPERTS]
    cpb = max(4, min(MOE_BLOCK_CHUNKS, (t * TOP_K) // (N_EXPERTS * MOE_CHUNK)))
    per_e = jnp.sum(n16, axis=0)
    blocks_e = (per_e + cpb - 1) // cpb
    blk_end = jnp.cumsum(blocks_e)
    blk_start = blk_end - blocks_e
    n_used = blk_end[-1:].astype(I32)
    max_chunks = (t * TOP_K) // MOE_CHUNK + nt * N_EXPERTS
    nblk = max_chunks // cpb + N_EXPERTS
    blk = jnp.arange(nblk, dtype=I32)
    block_e = jnp.minimum(jnp.sum(blk_end[None, :] <= blk[:, None], axis=1), N_EXPERTS - 1).astype(I32)
    of_e = block_e[:, None] == jnp.arange(N_EXPERTS, dtype=I32)[None, :]
    pick_e = lambda tab: jnp.sum(jnp.where(of_e, tab[None, :], 0), axis=1)
    pick_col = lambda tab: jnp.sum(jnp.where(of_e[:, None, :], tab[None, :, :], 0), axis=2)
    run_end = pick_col(jnp.cumsum(n16, axis=0))
    tile_off = pick_col(jnp.cumsum(n16, axis=1) - n16)
    k = ((blk - pick_e(blk_start)) * cpb)[:, None] + jnp.arange(cpb, dtype=I32)[None, :]
    real = (k < pick_e(per_e)[:, None]) & (blk < n_used[0])[:, None]
    done = run_end[:, None, :] <= k[:, :, None]
    tile = jnp.minimum(jnp.sum(done, axis=2), nt - 1)
    run_first = jnp.max(jnp.where(done, run_end[:, None, :], 0), axis=2)
    of_t = tile[:, :, None] == jnp.arange(nt, dtype=I32)[None, None, :]
    src = tile * (slots // MOE_CHUNK) + jnp.sum(jnp.where(of_t, tile_off[:, None, :], 0), axis=2) + (k - run_first)
    src = jnp.where(real, src, src[:, :1])
    src = jnp.where((blk < n_used[0])[:, None], src, 0).astype(I32).reshape(-1)
    nreal = jnp.sum(real, axis=1).astype(I32)
    ys = _experts(xs, src, nreal, block_e, n_used, w1, w3, w2, layer)
    return _combine(h1, rs, rg, ys, ln_g, ln_b, out_dtype)


def _swap_halves(x):
    lane = lax.broadcasted_iota(I32, x.shape, 1)
    first = (lane % MLA_ROPE) < (MLA_ROPE // 2)
    return jnp.where(first, pltpu.roll(x, LANES - MLA_ROPE // 2, 1), pltpu.roll(x, MLA_ROPE // 2, 1))


def _rope(x, cos, sin):
    parts = []
    for t in range(x.shape[1] // LANES):
        sl = slice(t * LANES, (t + 1) * LANES)
        parts.append(x[:, sl] * cos[:, sl] + _swap_halves(x[:, sl]) * sin[:, sl])
    return parts[0] if len(parts) == 1 else jnp.concatenate(parts, axis=1)


def _odd_in_kernel(h_ref, w_ref, qg_ref, wqn_ref, wqp_ref, wuk_ref, perm_ref, kvg_ref,
                   cq_ref, sq_ref, ck_ref, sk_ref, gg_ref, gb_ref, ws_ref, bs_ref,
                   q_ref, kc_ref, kt_ref, ckv_ref, kpe_ref, gated_ref, vn_ref, *, tl, cl):
    hb = h_ref[0].astype(BF16)
    z = _dot(hb, w_ref[...])
    o_ckv = MLA_Q_RANK
    o_u = o_ckv + MLA_KV_RANK
    o_v = o_u + GMLP_WIDTH
    o_k = o_v + GMLP_WIDTH
    cq = z[:, :MLA_Q_RANK]
    cqn = cq * lax.rsqrt(jnp.mean(cq * cq, axis=-1, keepdims=True) + LN_EPS) * qg_ref[...]
    cqb = cqn.astype(BF16)
    qn = _dot(cqb, wqn_ref[...])
    qp = _dot(cqb, wqp_ref[...])
    qp = _rope(qp, cq_ref[...], sq_ref[...])
    qpe = _dot((qp * MLA_SCALE).astype(BF16), perm_ref[...])
    for h in range(MLA_HEADS):
        sl = slice(h * LANES, (h + 1) * LANES)
        qa = _dot((qn[:, sl] * MLA_SCALE).astype(BF16), wuk_ref[h])
        q_ref[0, h, :, 0:LANES] = qa.astype(BF16)
        q_ref[0, h, :, LANES:] = qpe[:, sl].astype(BF16)

    ckv = z[:, o_ckv:o_u]
    ckvn = ckv * lax.rsqrt(jnp.mean(ckv * ckv, axis=-1, keepdims=True) + LN_EPS) * kvg_ref[...]
    kp = z[:, o_k:]
    kp = _rope(kp, ck_ref[...], sk_ref[...])
    ckv_ref[0] = ckvn
    kpe_ref[0] = kp[:, :MLA_ROPE]
    one = (lax.broadcasted_iota(I32, (1, LANES), 1) == MLA_ONE_LANE - LANES).astype(F32)
    kp1 = kp + one
    kc_ref[0, :, 0:LANES] = ckvn.astype(BF16)
    kc_ref[0, :, LANES:] = kp1.astype(BF16)
    kt_ref[0, 0:LANES, :] = ckvn.T.astype(BF16)
    kt_ref[0, LANES:, :] = kp1.T.astype(BF16)

    gu = _gelu(z[:, o_u:o_v])
    vn = _layernorm(_gelu(z[:, o_v:o_k]), gg_ref[...], gb_ref[...])
    vn_ref[0] = vn
    vnb = vn.astype(BF16)
    for n in range(tl // cl):
        rs = slice(n * cl, (n + 1) * cl)
        for g in range(GMLP_GROUPS):
            ls = slice(g * GMLP_CH, (g + 1) * GMLP_CH)
            sg = _dot(ws_ref[g], vnb[rs, ls]) + bs_ref[:, ls]
            gated_ref[0, rs, ls] = (gu[rs, ls] * sg).astype(BF16)


def _odd_in(h3, w_in, q_g, w_qn, w_qp, w_uk, perm, kv_g, cos_q, sin_q, cos_k, sin_k,
            gm_g, gm_b, ws, bs):
    bsz, l, _ = h3.shape
    tl = min(l, 512)
    cl = min(l, GMLP_CHUNK)
    rowb = lambda n: pl.BlockSpec((1, tl, n), lambda b, i: (b, i, 0))
    tab = lambda n: pl.BlockSpec((tl, n), lambda b, i: (i, 0))
    consts = (w_in, q_g, w_qn, w_qp, w_uk, perm, kv_g)
    consts2 = (gm_g, gm_b, ws, bs)
    return pl.pallas_call(
        functools.partial(_odd_in_kernel, tl=tl, cl=cl),
        grid=(bsz, l // tl),
        in_specs=[rowb(D_MODEL)] + [_full_spec(a, 2) for a in consts]
                 + [tab(MLA_HEADS * MLA_ROPE), tab(MLA_HEADS * MLA_ROPE), tab(LANES), tab(LANES)]
                 + [_full_spec(a, 2) for a in consts2],
        out_specs=[pl.BlockSpec((1, MLA_HEADS, tl, MLA_QW), lambda b, i: (b, 0, i, 0)),
                   rowb(MLA_QW), pl.BlockSpec((1, MLA_QW, tl), lambda b, i: (b, 0, i)),
                   rowb(MLA_KV_RANK), rowb(MLA_ROPE), rowb(GMLP_WIDTH), rowb(GMLP_WIDTH)],
        out_shape=[jax.ShapeDtypeStruct((bsz, MLA_HEADS, l, MLA_QW), BF16),
                   jax.ShapeDtypeStruct((bsz, l, MLA_QW), BF16),
                   jax.ShapeDtypeStruct((bsz, MLA_QW, l), BF16),
                   jax.ShapeDtypeStruct((bsz, l, MLA_KV_RANK), F32),
                   jax.ShapeDtypeStruct((bsz, l, MLA_ROPE), F32),
                   jax.ShapeDtypeStruct((bsz, l, GMLP_WIDTH), BF16),
                   jax.ShapeDtypeStruct((bsz, l, GMLP_WIDTH), F32)],
        compiler_params=_cp("parallel", "parallel"),
        name="odd_in",
    )(h3, *consts, cos_q, sin_q, cos_k, sin_k, *consts2)


def _attn_kernel(qi_ref, kj_ref, flag_ref, q_ref, k_ref, kt_ref, wuv_ref, o_ref,
                 m_scr, l_scr, acc_scr, *, tq, hpb, tk, nsub, pos0, n_keys):
    p = pl.program_id(1)
    flag = flag_ref[p]
    nblk = MLA_HEADS // hpb
    width = hpb * tq

    @pl.when((flag & 1) != 0)
    def _():
        m_scr[...] = jnp.full_like(m_scr, -jnp.inf)
        l_scr[...] = jnp.zeros_like(l_scr)
        acc_scr[...] = jnp.zeros_like(acc_scr)

    def step(sub, hidden):
        kt = kt_ref[0, :, sub * tk:(sub + 1) * tk]
        kk = k_ref[0, sub * tk:(sub + 1) * tk, :]
        if hidden:
            qpos = pos0 + qi_ref[p] * tq + lax.broadcasted_iota(I32, (1, width), 1) % tq
            kpos = (kj_ref[p] * nsub + sub) * tk + lax.broadcasted_iota(I32, (tk, 1), 0)
            visible = ((kpos // CHUNK) <= (qpos // CHUNK)) & (kpos < n_keys)
        half = tk // MLA_KEY_PARTS
        for h in range(nblk):
            s = _dot_nt(kk, q_ref[0, h])
            if hidden:
                s = jnp.where(visible, s, -jnp.inf)
            m = m_scr[h]
            acc = acc_scr[h]
            l = l_scr[h]
            for part in range(MLA_KEY_PARTS):
                ks = slice(part * half, (part + 1) * half)
                sp = s[ks]
                m_new = jnp.maximum(m, jnp.max(sp, axis=0, keepdims=True))
                alpha = jnp.exp(m - m_new)
                pr = jnp.exp(sp - m_new).astype(BF16)
                acc = alpha * acc + _dot(kt[:MLA_KV_RANK, ks], pr)
                l = alpha * l + _dot(kt[MLA_ONE_LANE:MLA_ONE_LANE + 16, ks], pr)
                m = m_new
            acc_scr[h] = acc
            l_scr[h] = l
            m_scr[h] = m

    tile_bits = sum(12 << (2 * sub) for sub in range(nsub))
    all_plain_bits = sum(4 << (2 * sub) for sub in range(nsub))
    last_masked_bits = all_plain_bits + (8 << (2 * (nsub - 1)))
    all_plain = (flag & tile_bits) == all_plain_bits
    last_masked = (flag & tile_bits) == last_masked_bits
    fused = all_plain | last_masked

    @pl.when(all_plain)
    def _():
        for sub in range(nsub):
            step(sub, False)

    @pl.when(last_masked)
    def _():
        for sub in range(nsub):
            step(sub, sub == nsub - 1)

    for sub in range(nsub):
        seen = (flag & (4 << (2 * sub))) != 0
        some_hidden = (flag & (8 << (2 * sub))) != 0

        @pl.when(jnp.logical_not(fused) & seen & some_hidden)
        def _():
            step(sub, True)

        @pl.when(jnp.logical_not(fused) & seen & jnp.logical_not(some_hidden))
        def _():
            step(sub, False)

    @pl.when((flag & 2) != 0)
    def _():
        if hpb == 1:
            for pair in range(MLA_HEADS // 2):
                tile = jnp.zeros((tq, LANES), F32)
                for h in (2 * pair, 2 * pair + 1):
                    lat = (acc_scr[h] / l_scr[h, 0:1, :]).astype(BF16)
                    tile = tile + _dot_tn(lat, wuv_ref[h])
                o_ref[0, :, pair * LANES:(pair + 1) * LANES] = tile.astype(BF16)
        else:
            head_of_lane = lax.broadcasted_iota(I32, (1, MLA_HEADS * MLA_V), 1) // MLA_V
            out = jnp.zeros((tq, MLA_HEADS * MLA_V), F32)
            for h in range(nblk):
                lat = (acc_scr[h] / l_scr[h, 0:1, :]).astype(BF16)
                full = _dot_tn(lat, wuv_ref[...])
                for hh in range(hpb):
                    out = out + jnp.where(head_of_lane == h * hpb + hh, full[hh * tq:(hh + 1) * tq], 0.0)
            o_ref[0] = out.astype(BF16)


def _attn_pairs(l, tq, tk, nsub, pos0, n_keys):
    qi, kj, flag = [], [], []
    for i in range(l // tq):
        q_first = pos0 + i * tq
        q_last = q_first + tq - 1
        vis = min(CHUNK * (q_last // CHUNK) + CHUNK - 1, n_keys - 1)
        ntiles = vis // tk + 1
        nsteps = -(-ntiles // nsub)
        for j in range(nsteps):
            f = int(j == 0) + 2 * int(j == nsteps - 1)
            for s in range(nsub):
                t = j * nsub + s
                if t < ntiles:
                    hidden = ((t + 1) * tk - 1) // CHUNK > q_first // CHUNK or (t + 1) * tk > n_keys
                    f += (4 + 8 * int(hidden)) << (2 * s)
            qi.append(i); kj.append(j); flag.append(f)
    return [jnp.asarray(np.array(a, np.int32)) for a in (qi, kj, flag)]


def _attention(q4, kc3, kt3, w_uv, pos0, n_keys, tq, tk, nsub):
    bsz, _, l, _ = q4.shape
    pairs = _attn_pairs(l, tq, tk, nsub, pos0, n_keys)
    npairs = int(pairs[0].shape[0])
    hpb = max(1, min(MLA_HEADS, 256 // tq)) if l == tq else 1
    nblk = MLA_HEADS // hpb
    width = hpb * tq
    q4 = q4.reshape(bsz, nblk, hpb * l, MLA_QW)
    if hpb == 1:
        heads = w_uv.reshape(MLA_KV_RANK, MLA_HEADS, MLA_V).transpose(1, 0, 2)
        w_uv = jnp.stack([jnp.pad(heads[h], ((0, 0), ((h % 2) * MLA_V, (1 - h % 2) * MLA_V)))
                          for h in range(MLA_HEADS)])
    return pl.pallas_call(
        functools.partial(_attn_kernel, tq=tq, hpb=hpb, tk=tk, nsub=nsub, pos0=pos0, n_keys=n_keys),
        grid_spec=pltpu.PrefetchScalarGridSpec(
            num_scalar_prefetch=3,
            grid=(bsz, npairs),
            in_specs=[pl.BlockSpec((1, nblk, width, MLA_QW), lambda b, p, qi, kj, f: (b, 0, qi[p], 0)),
                      pl.BlockSpec((1, nsub * tk, MLA_QW), lambda b, p, qi, kj, f: (b, kj[p], 0)),
                      pl.BlockSpec((1, MLA_QW, nsub * tk), lambda b, p, qi, kj, f: (b, 0, kj[p])),
                      pl.BlockSpec(w_uv.shape, lambda b, p, qi, kj, f: (0,) * w_uv.ndim)],
            out_specs=pl.BlockSpec((1, tq, MLA_HEADS * MLA_V), lambda b, p, qi, kj, f: (b, qi[p], 0)),
            scratch_shapes=[pltpu.VMEM((nblk, 1, width), F32),
                            pltpu.VMEM((nblk, 16, width), F32),
                            pltpu.VMEM((nblk, MLA_KV_RANK, width), F32)]),
        out_shape=jax.ShapeDtypeStruct((bsz, l, MLA_HEADS * MLA_V), BF16),
        compiler_params=_cp("parallel", "arbitrary"),
        name="mla_attention",
    )(*pairs, q4, kc3, kt3, w_uv)


def _rope_tables(pos0, l, width):
    half = MLA_ROPE // 2
    inv = ROPE_THETA ** (-jnp.arange(half, dtype=F32) * 2.0 / MLA_ROPE)
    ang = (pos0 + jnp.arange(l)).astype(F32)[:, None] * inv[None, :]
    cos, sin = jnp.cos(ang), jnp.sin(ang)
    cos32 = jnp.concatenate([cos, cos], axis=1)
    sin32 = jnp.concatenate([-sin, sin], axis=1)
    reps = width // MLA_ROPE
    return jnp.tile(cos32, (1, reps)), jnp.tile(sin32, (1, reps))


def _prep_even(w_in, w_gate_up, b_gate, gla_norm_g, pool_w, pool_scale, w_out):
    o_r = 2 * GLA_QK + GLA_V
    o_g = o_r + GLA_V
    o_p = o_g + GLA_GATE_RANK
    w_main = jnp.concatenate([w_in[:, :o_g], w_in[:, o_p:]], axis=1).astype(BF16)
    w_g = jnp.pad(w_in[:, o_g:o_p], ((0, 0), (0, LANES - GLA_GATE_RANK))).astype(BF16)
    w_gu = jnp.pad(w_gate_up, ((0, LANES - GLA_GATE_RANK), (0, 0))).astype(BF16)
    return dict(w_main=w_main, w_g=w_g, w_gu=w_gu, b_g=b_gate.reshape(1, -1),
                gnorm=gla_norm_g.reshape(1, -1), pool_w=pool_w.astype(BF16),
                pool_scale=pool_scale.reshape(1, -1), w_out=w_out.astype(BF16))


def _prep_odd(w_in, q_norm_g, w_uq, kv_norm_g, w_uk, w_uv, gm_g, gm_b, gm_ws, gm_bs, w_out):
    o_ckv = MLA_Q_RANK
    o_kpe = o_ckv + MLA_KV_RANK
    o_u = o_kpe + MLA_ROPE
    w_in2 = jnp.concatenate([w_in[:, :o_kpe], w_in[:, o_u:], w_in[:, o_kpe:o_u],
                             jnp.zeros((D_MODEL, LANES - MLA_ROPE), F32)], axis=1).astype(BF16)
    uq = w_uq.reshape(MLA_Q_RANK, MLA_HEADS, MLA_NOPE + MLA_ROPE)
    w_qn = jnp.pad(uq[:, :, :MLA_NOPE], ((0, 0), (0, 0), (0, LANES - MLA_NOPE)))
    w_qn = w_qn.reshape(MLA_Q_RANK, MLA_HEADS * LANES).astype(BF16)
    w_qp = uq[:, :, MLA_NOPE:].reshape(MLA_Q_RANK, MLA_HEADS * MLA_ROPE).astype(BF16)
    uk = w_uk.reshape(MLA_KV_RANK, MLA_HEADS, MLA_NOPE).transpose(1, 2, 0)
    w_ukp = jnp.pad(uk, ((0, 0), (0, LANES - MLA_NOPE), (0, 0))).astype(BF16)
    src = np.arange(MLA_HEADS * MLA_ROPE)
    perm = np.zeros((MLA_HEADS * MLA_ROPE, MLA_HEADS * LANES), np.float32)
    perm[src, (src // MLA_ROPE) * LANES + src % MLA_ROPE] = 1.0
    return dict(w_in=w_in2, q_g=q_norm_g.reshape(1, -1), w_qn=w_qn, w_qp=w_qp, w_uk=w_ukp,
                perm=jnp.asarray(perm, BF16), kv_g=kv_norm_g.reshape(1, -1), w_uv=w_uv.astype(BF16),
                gm_g=gm_g.reshape(1, -1), gm_b=gm_b.reshape(1, -1), gm_ws=gm_ws, gm_bs=gm_bs,
                w_out=w_out.astype(BF16))


def _prep_route(wg, bg, we, be):
    w_r = jnp.pad(jnp.concatenate([wg, we], axis=1), ((0, 0), (0, LANES - N_GROUPS - N_EXPERTS)))
    w_hi = w_r.astype(BF16)
    w_lo = (w_r - w_hi.astype(F32)).astype(BF16)
    b_r = jnp.pad(jnp.concatenate([bg, be]), (0, LANES - N_GROUPS - N_EXPERTS)).reshape(1, LANES)
    return jnp.concatenate([w_hi, w_lo], axis=1), b_r


def _even_mixer(h3, st0, hist, pos0, pw):
    bsz, l, _ = h3.shape
    hist16 = jnp.pad(hist, ((0, 0), (POOL_HALO - POOL_HIST, 0), (0, 0)))
    q, k, la, v, r, xp, pooled = _even_in(h3.reshape(bsz * l, D_MODEL), l, hist16, pos0, pw['w_main'], pw['w_g'],
                                          pw['w_gu'], pw['b_g'], pw['pool_w'], pw['pool_scale'])
    to3 = lambda a: a.reshape(bsz, l, a.shape[-1])
    o, st = _gla(to3(q), to3(k), to3(la), to3(v), to3(r), st0, pw['gnorm'])
    xp3 = to3(xp)
    hist_new = jnp.concatenate([hist, xp3[:, -POOL_HIST:].astype(F32)], axis=1)[:, -POOL_HIST:]
    return o, pooled, st, hist_new


def _odd_mixer(h3, ckv_past, kpe_past, pw):
    bsz, l, _ = h3.shape
    n_past = ckv_past.shape[1]
    cos_q, sin_q = _rope_tables(n_past, l, MLA_HEADS * MLA_ROPE)
    cos_k, sin_k = _rope_tables(n_past, l, MLA_ROPE)
    padk = ((0, 0), (0, LANES - MLA_ROPE))
    cos_k, sin_k = jnp.pad(cos_k, padk), jnp.pad(sin_k, padk)
    cl = min(l, GMLP_CHUNK)
    ws = jnp.tril(pw['gm_ws'][:, :cl, :cl]).astype(BF16)
    bs = jnp.repeat(pw['gm_bs'][:, :cl].T, GMLP_CH, axis=1)
    q4, kc, kt, ckv, kpe, gated, vn = _odd_in(h3, pw['w_in'], pw['q_g'], pw['w_qn'], pw['w_qp'], pw['w_uk'],
                                              pw['perm'], pw['kv_g'], cos_q, sin_q, cos_k, sin_k,
                                              pw['gm_g'], pw['gm_b'], ws, bs)
    n_keys = n_past + l
    tq, tk, nsub = min(l, 256), 512, 2
    span = tk * nsub
    if n_past:
        past = jnp.concatenate([ckv_past, kpe_past, jnp.ones((bsz, n_past, 1), F32),
                                jnp.zeros((bsz, n_past, MLA_QW - MLA_ONE_LANE - 1), F32)], axis=2).astype(BF16)
        kc = jnp.concatenate([past, kc], axis=1)
        kt = jnp.concatenate([past.transpose(0, 2, 1), kt], axis=2)
    kc = jnp.pad(kc, ((0, 0), (0, -n_keys % span), (0, 0)))
    kt = jnp.pad(kt, ((0, 0), (0, 0), (0, -n_keys % span)))
    attn = _attention(q4, kc, kt, pw['w_uv'], n_past, n_keys, tq, tk, nsub)
    return attn, gated, ckv, kpe, vn


def _finish_layer(a3, b3, h3, w_out, lw):
    bsz, l, _ = h3.shape
    t = bsz * l
    h1, xs, rs, rg, n16 = _out_route(a3.reshape(t, -1), b3.reshape(t, -1), h3.reshape(t, D_MODEL), w_out,
                                     lw['ln_mix_g'], lw['ln_mix_b'], lw['w_r'], lw['b_r'])
    h2 = _moe(h1, xs, rs, rg, n16, lw['w1'], lw['w3'], lw['w2'], lw['layer'], lw['ln_ffn_g'], lw['ln_ffn_b'],
              lw['out_dtype'])
    return h2.reshape(bsz, l, D_MODEL)


def kernel(x_prompt, x_sample, state_gla, state_pool, cache_mla_ckv, cache_mla_kpe, w_in_even, w_gate_up, b_gate, gla_norm_g, pool_w, pool_scale, w_out_even, w_in_odd, mla_q_norm_g, mla_w_uq, mla_kv_norm_g, mla_w_uk, mla_w_uv, gmlp_norm_g, gmlp_norm_b, gmlp_ws, gmlp_bs, w_out_odd, ln_mix_g, ln_mix_b, router_group_w, router_group_b, router_expert_w, router_expert_b, expert_w1, expert_w3, expert_w2, ln_ffn_g, ln_ffn_b):
    hp, hs = x_prompt, x_sample
    bp = hp.shape[0]
    past_len = cache_mla_ckv.shape[2]
    gla_p, gla_s, pool_p, pool_s = [], [], [], []
    ckv_p, ckv_s, kpe_p, kpe_s, gv_s = [], [], [], [], []

    def state_to_t(s):
        return s.transpose(0, 3, 1, 2).reshape(s.shape[0], GLA_DV, GLA_QK)

    def state_from_t(st):
        return st.reshape(st.shape[0], GLA_DV, GLA_HEADS, GLA_DK).transpose(0, 2, 3, 1)

    for layer in range(DEPTH):
        i = layer // 2
        w_r, b_r = _prep_route(router_group_w[layer], router_group_b[layer],
                               router_expert_w[layer], router_expert_b[layer])
        lw = dict(ln_mix_g=ln_mix_g[layer].reshape(1, -1), ln_mix_b=ln_mix_b[layer].reshape(1, -1),
                  ln_ffn_g=ln_ffn_g[layer].reshape(1, -1), ln_ffn_b=ln_ffn_b[layer].reshape(1, -1),
                  w_r=w_r, b_r=b_r, w1=expert_w1, w3=expert_w3, w2=expert_w2, layer=layer,
                  out_dtype=F32 if layer == DEPTH - 1 else BF16)
        if layer % 2 == 0:
            pw = _prep_even(w_in_even[i], w_gate_up[i], b_gate[i], gla_norm_g[i], pool_w[i], pool_scale[i],
                            w_out_even[i])
            st0 = jnp.zeros((bp, GLA_DV, GLA_QK), F32)
            hist0 = jnp.zeros((bp, POOL_HIST, POOL_WIDTH), F32)
            op, pp, stp, histp = _even_mixer(hp, st0, hist0, 0, pw)
            os_, ps, sts, hists = _even_mixer(hs, state_to_t(state_gla[i]), state_pool[i], past_len, pw)
            gla_p.append(state_from_t(stp)); gla_s.append(state_from_t(sts))
            pool_p.append(histp); pool_s.append(hists)
            ap, bpj, as_, bsj = op, pp, os_, ps
        else:
            pw = _prep_odd(w_in_odd[i], mla_q_norm_g[i], mla_w_uq[i], mla_kv_norm_g[i], mla_w_uk[i], mla_w_uv[i],
                           gmlp_norm_g[i], gmlp_norm_b[i], gmlp_ws[i], gmlp_bs[i], w_out_odd[i])
            no_ckv = jnp.zeros((bp, 0, MLA_KV_RANK), F32)
            no_kpe = jnp.zeros((bp, 0, MLA_ROPE), F32)
            ap, bpj, cp, kp, _ = _odd_mixer(hp, no_ckv, no_kpe, pw)
            as_, bsj, cs, ks, vs = _odd_mixer(hs, cache_mla_ckv[i], cache_mla_kpe[i], pw)
            ckv_p.append(cp); ckv_s.append(cs); kpe_p.append(kp); kpe_s.append(ks); gv_s.append(vs)
        hp = _finish_layer(ap, bpj, hp, pw['w_out'], lw)
        hs = _finish_layer(as_, bsj, hs, pw['w_out'], lw)
    return (hp, hs, jnp.stack(gla_p), jnp.stack(gla_s), jnp.stack(pool_p), jnp.stack(pool_s),
            jnp.stack(ckv_p), jnp.stack(ckv_s), jnp.stack(kpe_p), jnp.stack(kpe_s), jnp.stack(gv_s))
```
